```python
import math
import jax, jax.numpy as jnp
from jax import lax
import numpy as np

D_MODEL = 1024
BATCH = 8
SEQ = 2048
DEPTH = 1

D_MIX = D_MODEL
D_A = D_MIX // 2
D_B = D_MIX - D_A
N_GROUPS_A = 8
GROUP_DIM_A = D_A // N_GROUPS_A
HEAD_DIM = 64
N_HEADS_B = D_B // HEAD_DIM
CHUNK = 128
Q_BLOCK = 128
D_IN_PROJ = 3 * D_A + 4 * D_B
LN_EPS = 1e-5
DEEPNORM_ALPHA = (2.0 * DEPTH) ** 0.25
DEEPNORM_BETA = (8.0 * DEPTH) ** -0.25

kernel_name = "hymba_gmlp_stickbreaking_deepnorm_adaln"


def layer_norm(x, g, b):
    xf = x.astype(jnp.float32)
    mu = jnp.mean(xf, axis=-1, keepdims=True)
    var = jnp.mean(jnp.square(xf - mu), axis=-1, keepdims=True)
    y = (xf - mu) * lax.rsqrt(var + LN_EPS) * g.astype(jnp.float32) + b.astype(jnp.float32)
    return y.astype(x.dtype)


def chunked_sgu(u, v, ln_g, ln_b, w_s, b_s):
    bsz, seq, _ = u.shape
    n_chunks = seq // CHUNK
    v = layer_norm(v, ln_g, ln_b)
    v = v.reshape(bsz, n_chunks, CHUNK, N_GROUPS_A, GROUP_DIM_A)
    causal = jnp.tril(jnp.ones((CHUNK, CHUNK), dtype=bool))
    w = jnp.where(causal[None], w_s, 0.0).astype(v.dtype)
    mixed = jnp.einsum('gts,bnsgc->bntgc', w, v) + b_s.T[None, None, :, :, None]
    return u * mixed.reshape(bsz, seq, D_A)


def stick_breaking_attention(q, k, v):
    bsz, n_heads, seq, dh = q.shape
    n_blocks = seq // Q_BLOCK
    scale = 1.0 / math.sqrt(dh)
    q_blocks = q.reshape(bsz, n_heads, n_blocks, Q_BLOCK, dh).transpose(2, 0, 1, 3, 4)
    s_pos = jnp.arange(seq)

    def one_block(args):
        qb, blk = args
        z = jnp.einsum('bhtd,bhsd->bhts', qb, k).astype(jnp.float32) * scale
        t_pos = blk * Q_BLOCK + jnp.arange(Q_BLOCK)
        causal = s_pos[None, :] < t_pos[:, None]
        log_beta = jax.nn.log_sigmoid(z)
        log_1m_beta = jnp.where(causal, -jax.nn.softplus(z), 0.0)
        log_stick = lax.cumsum(log_1m_beta, axis=3, reverse=True) - log_1m_beta
        a = jnp.where(causal, jnp.exp(log_beta + log_stick), 0.0).astype(v.dtype)
        return jnp.einsum('bhts,bhsd->bhtd', a, v)

    out = lax.map(one_block, (q_blocks, jnp.arange(n_blocks)))
    return out.transpose(1, 0, 3, 2, 4).reshape(bsz, seq, n_heads * dh)


def setup_inputs(seed: int = 0) -> dict:
    key = jax.random.key(seed)
    ks = jax.random.split(key, 12)
    f32 = jnp.float32
    x = jax.random.normal(ks[0], (BATCH, SEQ, D_MODEL), f32)
    c = jax.random.normal(ks[1], (BATCH, D_MODEL), f32)
    w_ada = jax.random.normal(ks[2], (DEPTH, D_MODEL, 3 * D_MODEL), f32) * (0.5 * D_MODEL ** -0.5)
    b_ada = jax.random.normal(ks[3], (DEPTH, 3 * D_MODEL), f32) * 0.02
    w_in = jax.random.normal(ks[4], (DEPTH, D_MODEL, D_IN_PROJ), f32) * D_MODEL ** -0.5
    sgu_ln_g = 1.0 + 0.05 * jax.random.normal(ks[5], (DEPTH, D_A), f32)
    sgu_ln_b = 0.02 * jax.random.normal(ks[6], (DEPTH, D_A), f32)
    w_spatial = jax.random.normal(ks[7], (DEPTH, N_GROUPS_A, CHUNK, CHUNK), f32) * CHUNK ** -0.5
    b_spatial = 1.0 + 0.1 * jax.random.normal(ks[8], (DEPTH, N_GROUPS_A, CHUNK), f32)
    w_out = jax.random.normal(ks[9], (DEPTH, D_MIX, D_MODEL), f32) * (DEEPNORM_BETA * D_MIX ** -0.5)
    ln_g = 1.0 + 0.05 * jax.random.normal(ks[10], (DEPTH, D_MODEL), f32)
    ln_b = 0.02 * jax.random.normal(ks[11], (DEPTH, D_MODEL), f32)
    return {"x": x, "c": c, "w_ada": w_ada, "b_ada": b_ada, "w_in": w_in,
            "sgu_ln_g": sgu_ln_g, "sgu_ln_b": sgu_ln_b, "w_spatial": w_spatial,
            "b_spatial": b_spatial, "w_out": w_out, "ln_g": ln_g, "ln_b": ln_b}


def reference(x, c, w_ada, b_ada, w_in, sgu_ln_g, sgu_ln_b, w_spatial, b_spatial, w_out, ln_g, ln_b):
    bsz, seq, _ = x.shape
    split_points = np.cumsum([D_A, D_A, D_A, D_B, D_B, D_B])
    silu_c = jax.nn.silu(c)
    for layer in range(DEPTH):
        mod = silu_c @ w_ada[layer] + b_ada[layer]
        shift, scale, gate = jnp.split(mod, 3, axis=-1)
        h = x * (1.0 + scale[:, None, :]) + shift[:, None, :]

        proj = h @ w_in[layer]
        u_a, v_a, z_a, q, k, v_b, z_b = jnp.split(proj, split_points, axis=-1)

        y_a = chunked_sgu(jax.nn.gelu(u_a), jax.nn.gelu(v_a), sgu_ln_g[layer], sgu_ln_b[layer],
                          w_spatial[layer], b_spatial[layer])

        def heads(t):
            return t.reshape(bsz, seq, N_HEADS_B, HEAD_DIM).transpose(0, 2, 1, 3)
        y_b = stick_breaking_attention(heads(q), heads(k), heads(v_b))

        y = jnp.concatenate([jax.nn.silu(z_a) * y_a, jax.nn.silu(z_b) * y_b], axis=-1)
        y = y @ w_out[layer]

        x = layer_norm(DEEPNORM_ALPHA * x + gate[:, None, :] * y, ln_g[layer], ln_b[layer])
    return x
```

```python
import functools
import math

import jax
import jax.numpy as jnp
from jax import lax
from jax.experimental import pallas as pl
from jax.experimental.pallas import tpu as pltpu

F32 = jnp.float32
BF16 = jnp.bfloat16

N_GROUPS = 8
GROUP_DIM = 64
CHUNK = 128
LN_EPS = 1e-5
LANES = 128

TM = 512
TQ = 256
TK = 256
VMEM_LIMIT = 48 * 1024 * 1024


def _dot(a, b):
    return jnp.dot(a, b, preferred_element_type=F32)


def _dot_nt(a, b):
    return lax.dot_general(a, b, (((1,), (1,)), ((), ())), preferred_element_type=F32)


def _gelu_tanh(x):
    c = math.sqrt(2.0 / math.pi)
    return 0.5 * x * (1.0 + jnp.tanh(c * (x + 0.044715 * (x * x * x))))


def _silu(x):
    return x / (1.0 + jnp.exp(-x))


def _mod_kernel(c_ref, w_ref, b_ref, o_ref):
    c = c_ref[...]
    sc = _silu(c).astype(BF16)
    o_ref[...] = _dot(sc, w_ref[...].astype(BF16)) + b_ref[...]


def _modulation(c, w_ada, b_ada):
    bsz, d = c.shape
    n = w_ada.shape[1]
    tn = 1024
    return pl.pallas_call(
        _mod_kernel,
        grid=(n // tn,),
        in_specs=[
            pl.BlockSpec((bsz, d), lambda j: (0, 0)),
            pl.BlockSpec((d, tn), lambda j: (0, j)),
            pl.BlockSpec((1, tn), lambda j: (0, j)),
        ],
        out_specs=pl.BlockSpec((bsz, tn), lambda j: (0, j)),
        out_shape=jax.ShapeDtypeStruct((bsz, n), F32),
        compiler_params=pltpu.CompilerParams(
            dimension_semantics=("arbitrary",), vmem_limit_bytes=VMEM_LIMIT),
        name="adaln_mod",
    )(c, w_ada, b_ada.reshape(1, n))


def _inproj_kernel(x_ref, mod_ref, wnn_ref, wnt_ref, wsp_ref, bsp_ref, lng_ref, lnb_ref,
                   yag_ref, k_ref, qT_ref, vT_ref, gzT_ref, ug_scr, vn_scr, ya_scr):
    d_a = yag_ref.shape[-1]
    tm = x_ref.shape[1]
    shift = mod_ref[0, 0:1, :]
    scale = mod_ref[0, 1:2, :]
    h = (x_ref[0] * (1.0 + scale) + shift).astype(BF16)

    ug_scr[...] = _gelu_tanh(_dot(h, wnn_ref[:, 0:d_a]))
    vg = _gelu_tanh(_dot(h, wnn_ref[:, d_a:2 * d_a]))
    mu = jnp.mean(vg, axis=-1, keepdims=True)
    vc = vg - mu
    var = jnp.mean(vc * vc, axis=-1, keepdims=True)
    vn_scr[...] = (vc * lax.rsqrt(var + LN_EPS) * lng_ref[...] + lnb_ref[...]).astype(BF16)

    t_idx = lax.broadcasted_iota(jnp.int32, (CHUNK, CHUNK), 0)
    s_idx = lax.broadcasted_iota(jnp.int32, (CHUNK, CHUNK), 1)
    causal = t_idx >= s_idx
    w_sp = [jnp.where(causal, wsp_ref[g], 0.0).astype(BF16) for g in range(N_GROUPS)]
    first_group = lax.broadcasted_iota(jnp.int32, (CHUNK, LANES), 1) < GROUP_DIM
    for c in range(tm // CHUNK):
        rows = slice(c * CHUNK, (c + 1) * CHUNK)
        for p in range(d_a // LANES):
            cols = slice(p * LANES, (p + 1) * LANES)
            vp = vn_scr[rows, cols]
            mixed = jnp.where(first_group, _dot(w_sp[2 * p], vp), _dot(w_sp[2 * p + 1], vp))
            ya_scr[rows, cols] = ug_scr[rows, cols] * (mixed + bsp_ref[:, cols])
    za = _dot(h, wnn_ref[:, 2 * d_a:3 * d_a])
    yag_ref[0] = (_silu(za) * ya_scr[...]).astype(BF16)

    k_ref[0] = _dot(h, wnn_ref[:, 3 * d_a:4 * d_a]).astype(BF16)
    d_b = qT_ref.shape[1]
    scale_q = 1.0 / math.sqrt(GROUP_DIM)
    qT_ref[0] = (_dot_nt(wnt_ref[0:d_b, :], h) * scale_q).astype(BF16)
    vT = _dot_nt(wnt_ref[d_b:2 * d_b, :], h).astype(BF16)
    for j in range(tm // TK):
        vT_ref[0, j] = vT[:, j * TK:(j + 1) * TK]
    gzT_ref[0] = _silu(_dot_nt(wnt_ref[2 * d_b:3 * d_b, :], h))


def _inproj(x, mod3, w_nn, w_nt, w_sp, b_sp_full, ln_g, ln_b):
    bsz, seq, d = x.shape
    d_a = ln_g.shape[-1]
    d_b = w_nt.shape[0] // 3
    grid = (bsz, seq // TM)
    const2 = lambda b, m: (0, 0)
    return pl.pallas_call(
        _inproj_kernel,
        grid=grid,
        in_specs=[
            pl.BlockSpec((1, TM, d), lambda b, m: (b, m, 0)),
            pl.BlockSpec((1, 3, d), lambda b, m: (b, 0, 0)),
            pl.BlockSpec(w_nn.shape, const2),
            pl.BlockSpec(w_nt.shape, const2),
            pl.BlockSpec(w_sp.shape, lambda b, m: (0, 0, 0)),
            pl.BlockSpec(b_sp_full.shape, const2),
            pl.BlockSpec((1, d_a), const2),
            pl.BlockSpec((1, d_a), const2),
        ],
        out_specs=[
            pl.BlockSpec((1, TM, d_a), lambda b, m: (b, m, 0)),
            pl.BlockSpec((1, TM, d_b), lambda b, m: (b, m, 0)),
            pl.BlockSpec((1, d_b, TM), lambda b, m: (b, 0, m)),
            pl.BlockSpec((1, TM // TK, d_b, TK), lambda b, m: (b, m, 0, 0)),
            pl.BlockSpec((1, d_b, TM), lambda b, m: (b, 0, m)),
        ],
        out_shape=[
            jax.ShapeDtypeStruct((bsz, seq, d_a), BF16),
            jax.ShapeDtypeStruct((bsz, seq, d_b), BF16),
            jax.ShapeDtypeStruct((bsz, d_b, seq), BF16),
            jax.ShapeDtypeStruct((bsz, seq // TK, d_b, TK), BF16),
            jax.ShapeDtypeStruct((bsz, d_b, seq), F32),
        ],
        scratch_shapes=[
            pltpu.VMEM((TM, d_a), F32),
            pltpu.VMEM((TM, d_a), BF16),
            pltpu.VMEM((TM, d_a), F32),
        ],
        compiler_params=pltpu.CompilerParams(
            dimension_semantics=("arbitrary", "arbitrary"), vmem_limit_bytes=VMEM_LIMIT),
        name="inproj_gmlp",
    )(x, mod3, w_nn, w_nt, w_sp, b_sp_full, ln_g, ln_b)


def _attn_kernel(qT_ref, k_ref, vT_ref, gzT_ref, o_ref):
    i = pl.program_id(2)
    qT = qT_ref[0]
    row = lax.broadcasted_iota(jnp.int32, qT.shape, 0)
    s_idx = lax.broadcasted_iota(jnp.int32, (TK, TQ), 0)
    t_idx = lax.broadcasted_iota(jnp.int32, (TK, TQ), 1)
    strictly_earlier = s_idx < t_idx
    j_idx = lax.broadcasted_iota(jnp.int32, (TK, TK), 1)
    r_idx = lax.broadcasted_iota(jnp.int32, (TK, TK), 0)
    later_or_same = jnp.where(j_idx >= r_idx, 1.0, 0.0).astype(BF16)

    def tile(kb, hh, q_masked, carry, diag):
        run, acc = carry
        start = pl.multiple_of(kb * TK, TK)
        zT = _dot(k_ref[0, pl.ds(start, TK), :], q_masked)
        sp = jnp.maximum(zT, 0.0) + jnp.log(1.0 + jnp.exp(-jnp.abs(zT)))
        if diag:
            sp = jnp.where(strictly_earlier, sp, 0.0)
        hi = sp.astype(BF16)
        lo = (sp - hi.astype(F32)).astype(BF16)
        csum = _dot(later_or_same, hi) + _dot(later_or_same, lo)
        a = jnp.exp(zT - (run + csum))
        if diag:
            a = jnp.where(strictly_earlier, a, 0.0)
        v_blk = vT_ref[0, kb, hh * GROUP_DIM:(hh + 1) * GROUP_DIM, :]
        acc = acc + _dot(v_blk, a.astype(BF16))
        return run + csum[0:1, :], acc

    outs = []
    for hh in range(LANES // GROUP_DIM):
        in_head = (row >= hh * GROUP_DIM) & (row < (hh + 1) * GROUP_DIM)
        q_masked = jnp.where(in_head, qT, jnp.zeros_like(qT))
        carry = (jnp.zeros((1, TQ), F32), jnp.zeros((GROUP_DIM, TQ), F32))
        carry = tile(i, hh, q_masked, carry, True)
        carry = lax.fori_loop(
            0, i, lambda j, cr: tile(i - 1 - j, hh, q_masked, cr, False), carry)
        outs.append(carry[1])
    yT = jnp.concatenate(outs, axis=0) * gzT_ref[0]
    o_ref[0] = yT.T.astype(BF16)


def _attention(qT, k, vT, gzT):
    bsz, d_b, seq = qT.shape
    n_pairs = d_b // LANES
    grid = (bsz, n_pairs, seq // TQ)
    return pl.pallas_call(
        _attn_kernel,
        grid=grid,
        in_specs=[
            pl.BlockSpec((1, LANES, TQ), lambda b, p, i: (b, p, i)),
            pl.BlockSpec((1, seq, LANES), lambda b, p, i: (b, 0, p)),
            pl.BlockSpec((1, seq // TK, LANES, TK), lambda b, p, i: (b, 0, p, 0)),
            pl.BlockSpec((1, LANES, TQ), lambda b, p, i: (b, p, i)),
        ],
        out_specs=pl.BlockSpec((1, TQ, LANES), lambda b, p, i: (b, i, p)),
        out_shape=jax.ShapeDtypeStruct((bsz, seq, d_b), BF16),
        compiler_params=pltpu.CompilerParams(
            dimension_semantics=("arbitrary", "arbitrary", "arbitrary"),
            vmem_limit_bytes=VMEM_LIMIT),
        name="stickbreak_attn",
    )(qT, k, vT, gzT)


def _out_kernel(alpha, ya_ref, yb_ref, wo_ref, x_ref, mod_ref, g_ref, b_ref, o_ref):
    d_a = ya_ref.shape[-1]
    y = _dot(ya_ref[0], wo_ref[0:d_a, :]) + _dot(yb_ref[0], wo_ref[d_a:, :])
    gate = mod_ref[0, 2:3, :]
    r = alpha * x_ref[0] + gate * y
    mu = jnp.mean(r, axis=-1, keepdims=True)
    rc = r - mu
    var = jnp.mean(rc * rc, axis=-1, keepdims=True)
    o_ref[0] = rc * lax.rsqrt(var + LN_EPS) * g_ref[...] + b_ref[...]


def _outproj(ya, yb, w_out, x, mod3, ln_g, ln_b, alpha):
    bsz, seq, d = x.shape
    d_a = ya.shape[-1]
    d_b = yb.shape[-1]
    grid = (bsz, seq // TM)
    const2 = lambda b, m: (0, 0)
    return pl.pallas_call(
        functools.partial(_out_kernel, alpha),
        grid=grid,
        in_specs=[
            pl.BlockSpec((1, TM, d_a), lambda b, m: (b, m, 0)),
            pl.BlockSpec((1, TM, d_b), lambda b, m: (b, m, 0)),
            pl.BlockSpec(w_out.shape, const2),
            pl.BlockSpec((1, TM, d), lambda b, m: (b, m, 0)),
            pl.BlockSpec((1, 3, d), lambda b, m: (b, 0, 0)),
            pl.BlockSpec((1, d), const2),
            pl.BlockSpec((1, d), const2),
        ],
        out_specs=pl.BlockSpec((1, TM, d), lambda b, m: (b, m, 0)),
        out_shape=jax.ShapeDtypeStruct((bsz, seq, d), x.dtype),
        compiler_params=pltpu.CompilerParams(
            dimension_semantics=("arbitrary", "arbitrary"), vmem_limit_bytes=VMEM_LIMIT),
        name="outproj_deepnorm",
    )(ya, yb, w_out, x, mod3, ln_g, ln_b)


def kernel(x, c, w_ada, b_ada, w_in, sgu_ln_g, sgu_ln_b, w_spatial, b_spatial, w_out, ln_g, ln_b):
    depth = w_ada.shape[0]
    bsz, seq, d = x.shape
    d_a = sgu_ln_g.shape[-1]
    alpha = (2.0 * depth) ** 0.25
    for layer in range(depth):
        mod3 = _modulation(c, w_ada[layer], b_ada[layer]).reshape(bsz, 3, d)
        w = w_in[layer].astype(BF16)
        d_b = (w.shape[1] - 3 * d_a) // 4
        o = 3 * d_a
        w_nn = jnp.concatenate([w[:, :3 * d_a], w[:, o + d_b:o + 2 * d_b]], axis=1)
        w_nt = jnp.concatenate(
            [w[:, o:o + d_b], w[:, o + 2 * d_b:o + 3 * d_b], w[:, o + 3 * d_b:]], axis=1).T
        b_sp_full = jnp.repeat(b_spatial[layer].T, GROUP_DIM, axis=1)
        ya, k, qT, vT, gzT = _inproj(
            x, mod3, w_nn, w_nt, w_spatial[layer], b_sp_full,
            sgu_ln_g[layer].reshape(1, d_a), sgu_ln_b[layer].reshape(1, d_a))
        yb = _attention(qT, k, vT, gzT)
        x = _outproj(ya, yb, w_out[layer].astype(BF16), x, mod3,
                     ln_g[layer].reshape(1, d), ln_b[layer].reshape(1, d), alpha)
    return x
```

```python
import functools
import math

import jax
import jax.numpy as jnp
from jax import lax
from jax.experimental import pallas as pl
from jax.experimental.pallas import tpu as pltpu

F32 = jnp.float32
BF16 = jnp.bfloat16

N_GROUPS = 8
GROUP_DIM = 64
CHUNK = 128
LN_EPS = 1e-5
LANES = 128

TM = 512
TQ = 256
TK = 256
ATTN_HEADS_PER_STEP = 8
VMEM_LIMIT = 48 * 1024 * 1024


def _dot(a, b):
    return jnp.dot(a, b, preferred_element_type=F32)


def _dot_nt(a, b):
    return lax.dot_general(a, b, (((1,), (1,)), ((), ())), preferred_element_type=F32)


def _gelu_tanh(x):
    c = math.sqrt(2.0 / math.pi)
    return 0.5 * x * (1.0 + jnp.tanh(c * (x + 0.044715 * (x * x * x))))


def _silu(x):
    return x / (1.0 + jnp.exp(-x))


def _mod_kernel(c_ref, w_ref, b_ref, o_ref):
    c = c_ref[...]
    sc = _silu(c).astype(BF16)
    o_ref[...] = _dot(sc, w_ref[...].astype(BF16)) + b_ref[...]


def _modulation(c, w_ada, b_ada):
    bsz, d = c.shape
    n = w_ada.shape[1]
    tn = 1024
    return pl.pallas_call(
        _mod_kernel,
        grid=(n // tn,),
        in_specs=[
            pl.BlockSpec((bsz, d), lambda j: (0, 0)),
            pl.BlockSpec((d, tn), lambda j: (0, j)),
            pl.BlockSpec((1, tn), lambda j: (0, j)),
        ],
        out_specs=pl.BlockSpec((bsz, tn), lambda j: (0, j)),
        out_shape=jax.ShapeDtypeStruct((bsz, n), F32),
        compiler_params=pltpu.CompilerParams(
            dimension_semantics=("arbitrary",), vmem_limit_bytes=VMEM_LIMIT),
        name="adaln_mod",
    )(c, w_ada, b_ada.reshape(1, n))


def _inproj_kernel(x_ref, mod_ref, wnn_ref, wnt_ref, wsp_ref, bsp_ref, lng_ref, lnb_ref,
                   yag_ref, k_ref, qT_ref, vT_ref, gzT_ref, ug_scr, vn_scr, ya_scr):
    d_a = yag_ref.shape[-1]
    tm = x_ref.shape[1]
    shift = mod_ref[0, 0:1, :]
    scale = mod_ref[0, 1:2, :]
    h = (x_ref[0] * (1.0 + scale) + shift).astype(BF16)

    ug_scr[...] = _gelu_tanh(_dot(h, wnn_ref[:, 0:d_a]))
    vg = _gelu_tanh(_dot(h, wnn_ref[:, d_a:2 * d_a]))
    mu = jnp.mean(vg, axis=-1, keepdims=True)
    vc = vg - mu
    var = jnp.mean(vc * vc, axis=-1, keepdims=True)
    vn_scr[...] = (vc * lax.rsqrt(var + LN_EPS) * lng_ref[...] + lnb_ref[...]).astype(BF16)

    t_idx = lax.broadcasted_iota(jnp.int32, (CHUNK, CHUNK), 0)
    s_idx = lax.broadcasted_iota(jnp.int32, (CHUNK, CHUNK), 1)
    causal = t_idx >= s_idx
    w_sp = [jnp.where(causal, wsp_ref[g], 0.0).astype(BF16) for g in range(N_GROUPS)]
    first_group = lax.broadcasted_iota(jnp.int32, (CHUNK, LANES), 1) < GROUP_DIM
    for c in range(tm // CHUNK):
        rows = slice(c * CHUNK, (c + 1) * CHUNK)
        for p in range(d_a // LANES):
            cols = slice(p * LANES, (p + 1) * LANES)
            vp = vn_scr[rows, cols]
            mixed = jnp.where(first_group, _dot(w_sp[2 * p], vp), _dot(w_sp[2 * p + 1], vp))
            ya_scr[rows, cols] = ug_scr[rows, cols] * (mixed + bsp_ref[:, cols])
    za = _dot(h, wnn_ref[:, 2 * d_a:3 * d_a])
    yag_ref[0] = (_silu(za) * ya_scr[...]).astype(BF16)

    k_ref[0] = _dot(h, wnn_ref[:, 3 * d_a:4 * d_a]).astype(BF16)
    d_b = qT_ref.shape[1]
    scale_q = 1.0 / math.sqrt(GROUP_DIM)
    qT_ref[0] = (_dot_nt(wnt_ref[0:d_b, :], h) * scale_q).astype(BF16)
    vT = _dot_nt(wnt_ref[d_b:2 * d_b, :], h).astype(BF16)
    for j in range(tm // TK):
        vT_ref[0, j] = vT[:, j * TK:(j + 1) * TK]
    gzT_ref[0] = _silu(_dot_nt(wnt_ref[2 * d_b:3 * d_b, :], h))


def _inproj(x, mod3, w_nn, w_nt, w_sp, b_sp_full, ln_g, ln_b):
    bsz, seq, d = x.shape
    d_a = ln_g.shape[-1]
    d_b = w_nt.shape[0] // 3
    grid = (bsz, seq // TM)
    const2 = lambda b, m: (0, 0)
    return pl.pallas_call(
        _inproj_kernel,
        grid=grid,
        in_specs=[
            pl.BlockSpec((1, TM, d), lambda b, m: (b, m, 0)),
            pl.BlockSpec((1, 3, d), lambda b, m: (b, 0, 0)),
            pl.BlockSpec(w_nn.shape, const2),
            pl.BlockSpec(w_nt.shape, const2),
            pl.BlockSpec(w_sp.shape, lambda b, m: (0, 0, 0)),
            pl.BlockSpec(b_sp_full.shape, const2),
            pl.BlockSpec((1, d_a), const2),
            pl.BlockSpec((1, d_a), const2),
        ],
        out_specs=[
            pl.BlockSpec((1, TM, d_a), lambda b, m: (b, m, 0)),
            pl.BlockSpec((1, TM, d_b), lambda b, m: (b, m, 0)),
            pl.BlockSpec((1, d_b, TM), lambda b, m: (b, 0, m)),
            pl.BlockSpec((1, TM // TK, d_b, TK), lambda b, m: (b, m, 0, 0)),
            pl.BlockSpec((1, d_b, TM), lambda b, m: (b, 0, m)),
        ],
        out_shape=[
            jax.ShapeDtypeStruct((bsz, seq, d_a), BF16),
            jax.ShapeDtypeStruct((bsz, seq, d_b), BF16),
            jax.ShapeDtypeStruct((bsz, d_b, seq), BF16),
            jax.ShapeDtypeStruct((bsz, seq // TK, d_b, TK), BF16),
            jax.ShapeDtypeStruct((bsz, d_b, seq), F32),
        ],
        scratch_shapes=[
            pltpu.VMEM((TM, d_a), F32),
            pltpu.VMEM((TM, d_a), BF16),
            pltpu.VMEM((TM, d_a), F32),
        ],
        compiler_params=pltpu.CompilerParams(
            dimension_semantics=("arbitrary", "arbitrary"), vmem_limit_bytes=VMEM_LIMIT),
        name="inproj_gmlp",
    )(x, mod3, w_nn, w_nt, w_sp, b_sp_full, ln_g, ln_b)


def _attn_kernel(qT_ref, k_ref, vT_ref, gzT_ref, o_ref):
    i = pl.program_id(2)
    n_heads = qT_ref.shape[1] // GROUP_DIM
    heads_per_tile = LANES // GROUP_DIM
    row = lax.broadcasted_iota(jnp.int32, (LANES, TQ), 0)
    s_idx = lax.broadcasted_iota(jnp.int32, (TK, TQ), 0)
    t_idx = lax.broadcasted_iota(jnp.int32, (TK, TQ), 1)
    strictly_earlier = s_idx < t_idx
    j_idx = lax.broadcasted_iota(jnp.int32, (TK, TK), 1)
    r_idx = lax.broadcasted_iota(jnp.int32, (TK, TK), 0)
    later_or_same = jnp.where(j_idx >= r_idx, 1.0, 0.0).astype(BF16)

    def tile(kb, h, q_masked, carry, diag):
        run, acc = carry
        start = pl.multiple_of(kb * TK, TK)
        lanes = slice((h // heads_per_tile) * LANES, (h // heads_per_tile + 1) * LANES)
        zT = _dot(k_ref[0, pl.ds(start, TK), lanes], q_masked)
        sp = jnp.maximum(zT, 0.0) + jnp.log(1.0 + jnp.exp(-jnp.abs(zT)))
        if diag:
            sp = jnp.where(strictly_earlier, sp, 0.0)
        hi = sp.astype(BF16)
        lo = (sp - hi.astype(F32)).astype(BF16)
        csum = _dot(later_or_same, hi) + _dot(later_or_same, lo)
        a = jnp.exp(zT - (run + csum))
        if diag:
            a = jnp.where(strictly_earlier, a, 0.0)
        v_blk = vT_ref[0, kb, h * GROUP_DIM:(h + 1) * GROUP_DIM, :]
        acc = acc + _dot(v_blk, a.astype(BF16))
        return run + csum[0:1, :], acc

    q_masked = []
    for h in range(n_heads):
        p, hh = divmod(h, heads_per_tile)
        q_pair = qT_ref[0, p * LANES:(p + 1) * LANES, :]
        in_head = (row >= hh * GROUP_DIM) & (row < (hh + 1) * GROUP_DIM)
        q_masked.append(jnp.where(in_head, q_pair, jnp.zeros_like(q_pair)))

    def all_heads(kb, carries, diag):
        start = pl.multiple_of(kb * TK, TK)
        heads = range(n_heads)
        zs = []
        for h in heads:
            lanes = slice((h // heads_per_tile) * LANES, (h // heads_per_tile + 1) * LANES)
            zs.append(_dot(k_ref[0, pl.ds(start, TK), lanes], q_masked[h]))
        his, los = [], []
        for h in heads:
            sp = jnp.maximum(zs[h], 0.0) + jnp.log(1.0 + jnp.exp(-jnp.abs(zs[h])))
            if diag:
                sp = jnp.where(strictly_earlier, sp, 0.0)
            hi = sp.astype(BF16)
            his.append(hi)
            los.append((sp - hi.astype(F32)).astype(BF16))
        csums = [_dot(later_or_same, his[h]) + _dot(later_or_same, los[h]) for h in heads]
        probs = []
        for h in heads:
            a = jnp.exp(zs[h] - (carries[h][0] + csums[h]))
            if diag:
                a = jnp.where(strictly_earlier, a, 0.0)
            probs.append(a.astype(BF16))
        out = []
        for h in heads:
            v_blk = vT_ref[0, kb, h * GROUP_DIM:(h + 1) * GROUP_DIM, :]
            out.append((carries[h][0] + csums[h][0:1, :], carries[h][1] + _dot(v_blk, probs[h])))
        return tuple(out)

    carries = tuple((jnp.zeros((1, TQ), F32), jnp.zeros((GROUP_DIM, TQ), F32))
                    for _ in range(n_heads))
    carries = all_heads(i, carries, True)
    carries = lax.fori_loop(0, i, lambda j, cr: all_heads(i - 1 - j, cr, False), carries)
    yT = jnp.concatenate([acc for _, acc in carries], axis=0) * gzT_ref[0]
    o_ref[0] = yT.T.astype(BF16)


def _attention(qT, k, vT, gzT):
    bsz, d_b, seq = qT.shape
    w = ATTN_HEADS_PER_STEP * GROUP_DIM
    grid = (bsz, d_b // w, seq // TQ)
    return pl.pallas_call(
        _attn_kernel,
        grid=grid,
        in_specs=[
            pl.BlockSpec((1, w, TQ), lambda b, p, i: (b, p, i)),
            pl.BlockSpec((1, seq, w), lambda b, p, i: (b, 0, p)),
            pl.BlockSpec((1, seq // TK, w, TK), lambda b, p, i: (b, 0, p, 0)),
            pl.BlockSpec((1, w, TQ), lambda b, p, i: (b, p, i)),
        ],
        out_specs=pl.BlockSpec((1, TQ, w), lambda b, p, i: (b, i, p)),
        out_shape=jax.ShapeDtypeStruct((bsz, seq, d_b), BF16),
        compiler_params=pltpu.CompilerParams(
            dimension_semantics=("arbitrary", "arbitrary", "arbitrary"),
            vmem_limit_bytes=VMEM_LIMIT),
        name="stickbreak_attn",
    )(qT, k, vT, gzT)


def _out_kernel(alpha, ya_ref, yb_ref, wo_ref, x_ref, mod_ref, g_ref, b_ref, o_ref):
    d_a = ya_ref.shape[-1]
    y = _dot(ya_ref[0], wo_ref[0:d_a, :]) + _dot(yb_ref[0], wo_ref[d_a:, :])
    gate = mod_ref[0, 2:3, :]
    r = alpha * x_ref[0] + gate * y
    mu = jnp.mean(r, axis=-1, keepdims=True)
    rc = r - mu
    var = jnp.mean(rc * rc, axis=-1, keepdims=True)
    o_ref[0] = rc * lax.rsqrt(var + LN_EPS) * g_ref[...] + b_ref[...]


def _outproj(ya, yb, w_out, x, mod3, ln_g, ln_b, alpha):
    bsz, seq, d = x.shape
    d_a = ya.shape[-1]
    d_b = yb.shape[-1]
    grid = (bsz, seq // TM)
    const2 = lambda b, m: (0, 0)
    return pl.pallas_call(
        functools.partial(_out_kernel, alpha),
        grid=grid,
        in_specs=[
            pl.BlockSpec((1, TM, d_a), lambda b, m: (b, m, 0)),
            pl.BlockSpec((1, TM, d_b), lambda b, m: (b, m, 0)),
            pl.BlockSpec(w_out.shape, const2),
            pl.BlockSpec((1, TM, d), lambda b, m: (b, m, 0)),
            pl.BlockSpec((1, 3, d), lambda b, m: (b, 0, 0)),
            pl.BlockSpec((1, d), const2),
            pl.BlockSpec((1, d), const2),
        ],
        out_specs=pl.BlockSpec((1, TM, d), lambda b, m: (b, m, 0)),
        out_shape=jax.ShapeDtypeStruct((bsz, seq, d), x.dtype),
        compiler_params=pltpu.CompilerParams(
            dimension_semantics=("arbitrary", "arbitrary"), vmem_limit_bytes=VMEM_LIMIT),
        name="outproj_deepnorm",
    )(ya, yb, w_out, x, mod3, ln_g, ln_b)


def kernel(x, c, w_ada, b_ada, w_in, sgu_ln_g, sgu_ln_b, w_spatial, b_spatial, w_out, ln_g, ln_b):
    depth = w_ada.shape[0]
    bsz, seq, d = x.shape
    d_a = sgu_ln_g.shape[-1]
    alpha = (2.0 * depth) ** 0.25
    for layer in range(depth):
        mod3 = _modulation(c, w_ada[layer], b_ada[layer]).reshape(bsz, 3, d)
        w = w_in[layer].astype(BF16)
        d_b = (w.shape[1] - 3 * d_a) // 4
        o = 3 * d_a
        w_nn = jnp.concatenate([w[:, :3 * d_a], w[:, o + d_b:o + 2 * d_b]], axis=1)
        w_nt = jnp.concatenate(
            [w[:, o:o + d_b], w[:, o + 2 * d_b:o + 3 * d_b], w[:, o + 3 * d_b:]], axis=1).T
        b_sp_full = jnp.repeat(b_spatial[layer].T, GROUP_DIM, axis=1)
        ya, k, qT, vT, gzT = _inproj(
            x, mod3, w_nn, w_nt, w_spatial[layer], b_sp_full,
            sgu_ln_g[layer].reshape(1, d_a), sgu_ln_b[layer].reshape(1, d_a))
        yb = _attention(qT, k, vT, gzT)
        x = _outproj(ya, yb, w_out[layer].astype(BF16), x, mod3,
                     ln_g[layer].reshape(1, d), ln_b[layer].reshape(1, d), alpha)
    return x
```

```python
import functools
import math

import jax
import jax.numpy as jnp
from jax import lax
from jax.experimental import pallas as pl
from jax.experimental.pallas import tpu as pltpu

F32 = jnp.float32
BF16 = jnp.bfloat16

N_GROUPS = 8
GROUP_DIM = 64
CHUNK = 128
LN_EPS = 1e-5
LANES = 128

TM = 512
TQ = 256
TK = 256
ATTN_HEADS_PER_STEP = 8
UNDERFLOW_LOG2 = 152.0
VMEM_LIMIT = 48 * 1024 * 1024


def _dot(a, b):
    return jnp.dot(a, b, preferred_element_type=F32)


def _dot_nt(a, b):
    return lax.dot_general(a, b, (((1,), (1,)), ((), ())), preferred_element_type=F32)


def _gelu_tanh(x):
    c = math.sqrt(2.0 / math.pi)
    return 0.5 * x * (1.0 + jnp.tanh(c * (x + 0.044715 * (x * x * x))))


def _silu(x):
    return x / (1.0 + jnp.exp(-x))


def _mod_kernel(c_ref, w_ref, b_ref, o_ref):
    c = c_ref[...]
    sc = _silu(c).astype(BF16)
    o_ref[...] = _dot(sc, w_ref[...].astype(BF16)) + b_ref[...]


def _modulation(c, w_ada, b_ada):
    bsz, d = c.shape
    n = w_ada.shape[1]
    tn = 1024
    return pl.pallas_call(
        _mod_kernel,
        grid=(n // tn,),
        in_specs=[
            pl.BlockSpec((bsz, d), lambda j: (0, 0)),
            pl.BlockSpec((d, tn), lambda j: (0, j)),
            pl.BlockSpec((1, tn), lambda j: (0, j)),
        ],
        out_specs=pl.BlockSpec((bsz, tn), lambda j: (0, j)),
        out_shape=jax.ShapeDtypeStruct((bsz, n), F32),
        compiler_params=pltpu.CompilerParams(
            dimension_semantics=("arbitrary",), vmem_limit_bytes=VMEM_LIMIT),
        name="adaln_mod",
    )(c, w_ada, b_ada.reshape(1, n))


def _inproj_kernel(x_ref, mod_ref, wnn_ref, wnt_ref, wsp_ref, bsp_ref, lng_ref, lnb_ref,
                   yag_ref, k_ref, qT_ref, vT_ref, gzT_ref, ug_scr, vn_scr, ya_scr):
    d_a = yag_ref.shape[-1]
    tm = x_ref.shape[1]
    shift = mod_ref[0, 0:1, :]
    scale = mod_ref[0, 1:2, :]
    h = (x_ref[0] * (1.0 + scale) + shift).astype(BF16)

    ug_scr[...] = _gelu_tanh(_dot(h, wnn_ref[:, 0:d_a]))
    vg = _gelu_tanh(_dot(h, wnn_ref[:, d_a:2 * d_a]))
    mu = jnp.mean(vg, axis=-1, keepdims=True)
    vc = vg - mu
    var = jnp.mean(vc * vc, axis=-1, keepdims=True)
    vn_scr[...] = (vc * lax.rsqrt(var + LN_EPS) * lng_ref[...] + lnb_ref[...]).astype(BF16)

    t_idx = lax.broadcasted_iota(jnp.int32, (CHUNK, CHUNK), 0)
    s_idx = lax.broadcasted_iota(jnp.int32, (CHUNK, CHUNK), 1)
    causal = t_idx >= s_idx
    w_sp = [jnp.where(causal, wsp_ref[g], 0.0).astype(BF16) for g in range(N_GROUPS)]
    first_group = lax.broadcasted_iota(jnp.int32, (CHUNK, LANES), 1) < GROUP_DIM
    for c in range(tm // CHUNK):
        rows = slice(c * CHUNK, (c + 1) * CHUNK)
        for p in range(d_a // LANES):
            cols = slice(p * LANES, (p + 1) * LANES)
            vp = vn_scr[rows, cols]
            mixed = jnp.where(first_group, _dot(w_sp[2 * p], vp), _dot(w_sp[2 * p + 1], vp))
            ya_scr[rows, cols] = ug_scr[rows, cols] * (mixed + bsp_ref[:, cols])
    za = _dot(h, wnn_ref[:, 2 * d_a:3 * d_a])
    yag_ref[0] = (_silu(za) * ya_scr[...]).astype(BF16)

    k_ref[0] = _dot(h, wnn_ref[:, 3 * d_a:4 * d_a]).astype(BF16)
    d_b = qT_ref.shape[1]
    scale_q = math.log2(math.e) / math.sqrt(GROUP_DIM)
    qT_ref[0] = (_dot_nt(wnt_ref[0:d_b, :], h) * scale_q).astype(BF16)
    vT = _dot_nt(wnt_ref[d_b:2 * d_b, :], h).astype(BF16)
    for j in range(tm // TK):
        vT_ref[0, j] = vT[:, j * TK:(j + 1) * TK]
    gzT_ref[0] = _silu(_dot_nt(wnt_ref[2 * d_b:3 * d_b, :], h))


def _inproj(x, mod3, w_nn, w_nt, w_sp, b_sp_full, ln_g, ln_b):
    bsz, seq, d = x.shape
    d_a = ln_g.shape[-1]
    d_b = w_nt.shape[0] // 3
    grid = (bsz, seq // TM)
    const2 = lambda b, m: (0, 0)
    return pl.pallas_call(
        _inproj_kernel,
        grid=grid,
        in_specs=[
            pl.BlockSpec((1, TM, d), lambda b, m: (b, m, 0)),
            pl.BlockSpec((1, 3, d), lambda b, m: (b, 0, 0)),
            pl.BlockSpec(w_nn.shape, const2),
            pl.BlockSpec(w_nt.shape, const2),
            pl.BlockSpec(w_sp.shape, lambda b, m: (0, 0, 0)),
            pl.BlockSpec(b_sp_full.shape, const2),
            pl.BlockSpec((1, d_a), const2),
            pl.BlockSpec((1, d_a), const2),
        ],
        out_specs=[
            pl.BlockSpec((1, TM, d_a), lambda b, m: (b, m, 0)),
            pl.BlockSpec((1, TM, d_b), lambda b, m: (b, m, 0)),
            pl.BlockSpec((1, d_b, TM), lambda b, m: (b, 0, m)),
            pl.BlockSpec((1, TM // TK, d_b, TK), lambda b, m: (b, m, 0, 0)),
            pl.BlockSpec((1, d_b, TM), lambda b, m: (b, 0, m)),
        ],
        out_shape=[
            jax.ShapeDtypeStruct((bsz, seq, d_a), BF16),
            jax.ShapeDtypeStruct((bsz, seq, d_b), BF16),
            jax.ShapeDtypeStruct((bsz, d_b, seq), BF16),
            jax.ShapeDtypeStruct((bsz, seq // TK, d_b, TK), BF16),
            jax.ShapeDtypeStruct((bsz, d_b, seq), F32),
        ],
        scratch_shapes=[
            pltpu.VMEM((TM, d_a), F32),
            pltpu.VMEM((TM, d_a), BF16),
            pltpu.VMEM((TM, d_a), F32),
        ],
        compiler_params=pltpu.CompilerParams(
            dimension_semantics=("arbitrary", "arbitrary"), vmem_limit_bytes=VMEM_LIMIT),
        name="inproj_gmlp",
    )(x, mod3, w_nn, w_nt, w_sp, b_sp_full, ln_g, ln_b)


def _attn_kernel(qT_ref, k_ref, vT_ref, gzT_ref, o_ref):
    i = pl.program_id(2)
    n_heads = qT_ref.shape[1] // GROUP_DIM
    heads_per_tile = LANES // GROUP_DIM
    row = lax.broadcasted_iota(jnp.int32, (LANES, TQ), 0)
    s_idx = lax.broadcasted_iota(jnp.int32, (TK, TQ), 0)
    t_idx = lax.broadcasted_iota(jnp.int32, (TK, TQ), 1)
    strictly_earlier = s_idx < t_idx
    j_idx = lax.broadcasted_iota(jnp.int32, (TK, TK), 1)
    r_idx = lax.broadcasted_iota(jnp.int32, (TK, TK), 0)
    later_or_same = jnp.where(j_idx >= r_idx, 1.0, 0.0).astype(BF16)
    later_or_same_x2 = jnp.concatenate([later_or_same, later_or_same], axis=1)

    q_masked = []
    for h in range(n_heads):
        p, hh = divmod(h, heads_per_tile)
        q_pair = qT_ref[0, p * LANES:(p + 1) * LANES, :]
        in_head = (row >= hh * GROUP_DIM) & (row < (hh + 1) * GROUP_DIM)
        q_masked.append(jnp.where(in_head, q_pair, jnp.zeros_like(q_pair)))

    def key_blocks(blocks, runs, accs):
        tiles = [(kb, diag, h) for kb, diag in blocks for h in range(n_heads)]
        zs = []
        for kb, _, h in tiles:
            start = pl.multiple_of(kb * TK, TK)
            lanes = slice((h // heads_per_tile) * LANES, (h // heads_per_tile + 1) * LANES)
            zs.append(_dot(k_ref[0, pl.ds(start, TK), lanes], q_masked[h]))
        hilos = []
        for (_, diag, _), z in zip(tiles, zs):
            sp = jnp.maximum(z, 0.0) + jnp.log2(1.0 + jnp.exp2(-jnp.abs(z)))
            if diag:
                sp = jnp.where(strictly_earlier, sp, 0.0)
            hi = sp.astype(BF16)
            lo = (sp - hi.astype(F32)).astype(BF16)
            hilos.append(jnp.concatenate([hi, lo], axis=0))
        csums = [_dot(later_or_same_x2, hl) for hl in hilos]
        runs = list(runs)
        probs = []
        for (_, diag, h), z, csum in zip(tiles, zs, csums):
            a = jnp.exp2(z - (runs[h] + csum))
            if diag:
                a = jnp.where(strictly_earlier, a, 0.0)
            probs.append(a.astype(BF16))
            runs[h] = runs[h] + csum[0:1, :]
        accs = list(accs)
        for (kb, _, h), p in zip(tiles, probs):
            v_blk = vT_ref[0, kb, h * GROUP_DIM:(h + 1) * GROUP_DIM, :]
            accs[h] = accs[h] + _dot(v_blk, p)
        return tuple(runs), tuple(accs)

    runs = tuple(jnp.zeros((1, TQ), F32) for _ in range(n_heads))
    accs = tuple(jnp.zeros((GROUP_DIM, TQ), F32) for _ in range(n_heads))
    runs, accs = lax.cond(
        i > 0,
        lambda: key_blocks([(i, True), (i - 1, False)], runs, accs),
        lambda: key_blocks([(i, True)], runs, accs))

    def sticks_left(state):
        kb, runs, _ = state
        least = functools.reduce(jnp.minimum, runs)
        return (kb >= 0) & (jnp.min(least) < UNDERFLOW_LOG2)

    def earlier_block(state):
        kb, runs, accs = state
        runs, accs = key_blocks([(kb, False)], runs, accs)
        return kb - 1, runs, accs

    _, _, accs = lax.while_loop(sticks_left, earlier_block, (i - 2, runs, accs))
    yT = jnp.concatenate(accs, axis=0) * gzT_ref[0]
    o_ref[0] = yT.T.astype(BF16)


def _attention(qT, k, vT, gzT):
    bsz, d_b, seq = qT.shape
    w = ATTN_HEADS_PER_STEP * GROUP_DIM
    grid = (bsz, d_b // w, seq // TQ)
    return pl.pallas_call(
        _attn_kernel,
        grid=grid,
        in_specs=[
            pl.BlockSpec((1, w, TQ), lambda b, p, i: (b, p, i)),
            pl.BlockSpec((1, seq, w), lambda b, p, i: (b, 0, p)),
            pl.BlockSpec((1, seq // TK, w, TK), lambda b, p, i: (b, 0, p, 0)),
            pl.BlockSpec((1, w, TQ), lambda b, p, i: (b, p, i)),
        ],
        out_specs=pl.BlockSpec((1, TQ, w), lambda b, p, i: (b, i, p)),
        out_shape=jax.ShapeDtypeStruct((bsz, seq, d_b), BF16),
        compiler_params=pltpu.CompilerParams(
            dimension_semantics=("arbitrary", "arbitrary", "arbitrary"),
            vmem_limit_bytes=VMEM_LIMIT),
        name="stickbreak_attn",
    )(qT, k, vT, gzT)


def _out_kernel(alpha, ya_ref, yb_ref, wo_ref, x_ref, mod_ref, g_ref, b_ref, o_ref):
    d_a = ya_ref.shape[-1]
    y = _dot(ya_ref[0], wo_ref[0:d_a, :]) + _dot(yb_ref[0], wo_ref[d_a:, :])
    gate = mod_ref[0, 2:3, :]
    r = alpha * x_ref[0] + gate * y
    mu = jnp.mean(r, axis=-1, keepdims=True)
    rc = r - mu
    var = jnp.mean(rc * rc, axis=-1, keepdims=True)
    o_ref[0] = rc * lax.rsqrt(var + LN_EPS) * g_ref[...] + b_ref[...]


def _outproj(ya, yb, w_out, x, mod3, ln_g, ln_b, alpha):
    bsz, seq, d = x.shape
    d_a = ya.shape[-1]
    d_b = yb.shape[-1]
    grid = (bsz, seq // TM)
    const2 = lambda b, m: (0, 0)
    return pl.pallas_call(
        functools.partial(_out_kernel, alpha),
        grid=grid,
        in_specs=[
            pl.BlockSpec((1, TM, d_a), lambda b, m: (b, m, 0)),
            pl.BlockSpec((1, TM, d_b), lambda b, m: (b, m, 0)),
            pl.BlockSpec(w_out.shape, const2),
            pl.BlockSpec((1, TM, d), lambda b, m: (b, m, 0)),
            pl.BlockSpec((1, 3, d), lambda b, m: (b, 0, 0)),
            pl.BlockSpec((1, d), const2),
            pl.BlockSpec((1, d), const2),
        ],
        out_specs=pl.BlockSpec((1, TM, d), lambda b, m: (b, m, 0)),
        out_shape=jax.ShapeDtypeStruct((bsz, seq, d), x.dtype),
        compiler_params=pltpu.CompilerParams(
            dimension_semantics=("arbitrary", "arbitrary"), vmem_limit_bytes=VMEM_LIMIT),
        name="outproj_deepnorm",
    )(ya, yb, w_out, x, mod3, ln_g, ln_b)


def kernel(x, c, w_ada, b_ada, w_in, sgu_ln_g, sgu_ln_b, w_spatial, b_spatial, w_out, ln_g, ln_b):
    depth = w_ada.shape[0]
    bsz, seq, d = x.shape
    d_a = sgu_ln_g.shape[-1]
    alpha = (2.0 * depth) ** 0.25
    for layer in range(depth):
        mod3 = _modulation(c, w_ada[layer], b_ada[layer]).reshape(bsz, 3, d)
        w = w_in[layer].astype(BF16)
        d_b = (w.shape[1] - 3 * d_a) // 4
        o = 3 * d_a
        w_nn = jnp.concatenate([w[:, :3 * d_a], w[:, o + d_b:o + 2 * d_b]], axis=1)
        w_nt = jnp.concatenate(
            [w[:, o:o + d_b], w[:, o + 2 * d_b:o + 3 * d_b], w[:, o + 3 * d_b:]], axis=1).T
        b_sp_full = jnp.repeat(b_spatial[layer].T, GROUP_DIM, axis=1)
        ya, k, qT, vT, gzT = _inproj(
            x, mod3, w_nn, w_nt, w_spatial[layer], b_sp_full,
            sgu_ln_g[layer].reshape(1, d_a), sgu_ln_b[layer].reshape(1, d_a))
        yb = _attention(qT, k, vT, gzT)
        x = _outproj(ya, yb, w_out[layer].astype(BF16), x, mod3,
                     ln_g[layer].reshape(1, d), ln_b[layer].reshape(1, d), alpha)
    return x
```

```python
import functools
import math

import jax
import jax.numpy as jnp
from jax import lax
from jax.experimental import pallas as pl
from jax.experimental.pallas import tpu as pltpu

F32 = jnp.float32
BF16 = jnp.bfloat16

N_GROUPS = 8
GROUP_DIM = 64
CHUNK = 128
LN_EPS = 1e-5
LANES = 128

TM = 512
TQ = 256
TK = 256
ATTN_HEADS_PER_STEP = 8
SKEW = (3, 4)
UNDERFLOW_LOG2 = 152.0
VMEM_LIMIT = 48 * 1024 * 1024


def _dot(a, b):
    return jnp.dot(a, b, preferred_element_type=F32)


def _dot_nt(a, b):
    return lax.dot_general(a, b, (((1,), (1,)), ((), ())), preferred_element_type=F32)


def _gelu_tanh(x):
    c = math.sqrt(2.0 / math.pi)
    return 0.5 * x * (1.0 + jnp.tanh(c * (x + 0.044715 * (x * x * x))))


def _silu(x):
    return x / (1.0 + jnp.exp(-x))


def _mod_kernel(c_ref, w_ref, b_ref, o_ref):
    c = c_ref[...]
    sc = _silu(c).astype(BF16)
    o_ref[...] = _dot(sc, w_ref[...].astype(BF16)) + b_ref[...]


def _modulation(c, w_ada, b_ada):
    bsz, d = c.shape
    n = w_ada.shape[1]
    tn = 1024
    return pl.pallas_call(
        _mod_kernel,
        grid=(n // tn,),
        in_specs=[
            pl.BlockSpec((bsz, d), lambda j: (0, 0)),
            pl.BlockSpec((d, tn), lambda j: (0, j)),
            pl.BlockSpec((1, tn), lambda j: (0, j)),
        ],
        out_specs=pl.BlockSpec((bsz, tn), lambda j: (0, j)),
        out_shape=jax.ShapeDtypeStruct((bsz, n), F32),
        compiler_params=pltpu.CompilerParams(
            dimension_semantics=("arbitrary",), vmem_limit_bytes=VMEM_LIMIT),
        name="adaln_mod",
    )(c, w_ada, b_ada.reshape(1, n))


def _inproj_kernel(x_ref, mod_ref, wnn_ref, wnt_ref, wsp_ref, bsp_ref, lng_ref, lnb_ref,
                   yag_ref, k_ref, qT_ref, vT_ref, gzT_ref, ug_scr, vn_scr, ya_scr):
    d_a = yag_ref.shape[-1]
    tm = x_ref.shape[1]
    shift = mod_ref[0, 0:1, :]
    scale = mod_ref[0, 1:2, :]
    h = (x_ref[0] * (1.0 + scale) + shift).astype(BF16)

    ug_scr[...] = _gelu_tanh(_dot(h, wnn_ref[:, 0:d_a]))
    vg = _gelu_tanh(_dot(h, wnn_ref[:, d_a:2 * d_a]))
    mu = jnp.mean(vg, axis=-1, keepdims=True)
    vc = vg - mu
    var = jnp.mean(vc * vc, axis=-1, keepdims=True)
    vn_scr[...] = (vc * lax.rsqrt(var + LN_EPS) * lng_ref[...] + lnb_ref[...]).astype(BF16)

    t_idx = lax.broadcasted_iota(jnp.int32, (CHUNK, CHUNK), 0)
    s_idx = lax.broadcasted_iota(jnp.int32, (CHUNK, CHUNK), 1)
    causal = t_idx >= s_idx
    w_sp = [jnp.where(causal, wsp_ref[g], 0.0).astype(BF16) for g in range(N_GROUPS)]
    first_group = lax.broadcasted_iota(jnp.int32, (CHUNK, LANES), 1) < GROUP_DIM
    for c in range(tm // CHUNK):
        rows = slice(c * CHUNK, (c + 1) * CHUNK)
        for p in range(d_a // LANES):
            cols = slice(p * LANES, (p + 1) * LANES)
            vp = vn_scr[rows, cols]
            mixed = jnp.where(first_group, _dot(w_sp[2 * p], vp), _dot(w_sp[2 * p + 1], vp))
            ya_scr[rows, cols] = ug_scr[rows, cols] * (mixed + bsp_ref[:, cols])
    za = _dot(h, wnn_ref[:, 2 * d_a:3 * d_a])
    yag_ref[0] = (_silu(za) * ya_scr[...]).astype(BF16)

    k_ref[0] = _dot(h, wnn_ref[:, 3 * d_a:4 * d_a]).astype(BF16)
    d_b = qT_ref.shape[1]
    scale_q = math.log2(math.e) / math.sqrt(GROUP_DIM)
    qT_ref[0] = (_dot_nt(wnt_ref[0:d_b, :], h) * scale_q).astype(BF16)
    vT = _dot_nt(wnt_ref[d_b:2 * d_b, :], h).astype(BF16)
    for j in range(tm // TK):
        vT_ref[0, j] = vT[:, j * TK:(j + 1) * TK]
    gzT_ref[0] = _silu(_dot_nt(wnt_ref[2 * d_b:3 * d_b, :], h))


def _inproj(x, mod3, w_nn, w_nt, w_sp, b_sp_full, ln_g, ln_b):
    bsz, seq, d = x.shape
    d_a = ln_g.shape[-1]
    d_b = w_nt.shape[0] // 3
    grid = (bsz, seq // TM)
    const2 = lambda b, m: (0, 0)
    return pl.pallas_call(
        _inproj_kernel,
        grid=grid,
        in_specs=[
            pl.BlockSpec((1, TM, d), lambda b, m: (b, m, 0)),
            pl.BlockSpec((1, 3, d), lambda b, m: (b, 0, 0)),
            pl.BlockSpec(w_nn.shape, const2),
            pl.BlockSpec(w_nt.shape, const2),
            pl.BlockSpec(w_sp.shape, lambda b, m: (0, 0, 0)),
            pl.BlockSpec(b_sp_full.shape, const2),
            pl.BlockSpec((1, d_a), const2),
            pl.BlockSpec((1, d_a), const2),
        ],
        out_specs=[
            pl.BlockSpec((1, TM, d_a), lambda b, m: (b, m, 0)),
            pl.BlockSpec((1, TM, d_b), lambda b, m: (b, m, 0)),
            pl.BlockSpec((1, d_b, TM), lambda b, m: (b, 0, m)),
            pl.BlockSpec((1, TM // TK, d_b, TK), lambda b, m: (b, m, 0, 0)),
            pl.BlockSpec((1, d_b, TM), lambda b, m: (b, 0, m)),
        ],
        out_shape=[
            jax.ShapeDtypeStruct((bsz, seq, d_a), BF16),
            jax.ShapeDtypeStruct((bsz, seq, d_b), BF16),
            jax.ShapeDtypeStruct((bsz, d_b, seq), BF16),
            jax.ShapeDtypeStruct((bsz, seq // TK, d_b, TK), BF16),
            jax.ShapeDtypeStruct((bsz, d_b, seq), F32),
        ],
        scratch_shapes=[
            pltpu.VMEM((TM, d_a), F32),
            pltpu.VMEM((TM, d_a), BF16),
            pltpu.VMEM((TM, d_a), F32),
        ],
        compiler_params=pltpu.CompilerParams(
            dimension_semantics=("arbitrary", "arbitrary"), vmem_limit_bytes=VMEM_LIMIT),
        name="inproj_gmlp",
    )(x, mod3, w_nn, w_nt, w_sp, b_sp_full, ln_g, ln_b)


def _attn_kernel(qT_ref, k_ref, vT_ref, gzT_ref, o_ref):
    i = pl.program_id(2)
    n_heads = qT_ref.shape[1] // GROUP_DIM
    heads_per_tile = LANES // GROUP_DIM
    row = lax.broadcasted_iota(jnp.int32, (LANES, TQ), 0)
    half = TK // 2
    s_idx = lax.broadcasted_iota(jnp.int32, (half, half), 0)
    t_idx = lax.broadcasted_iota(jnp.int32, (half, half), 1)
    strictly_earlier_half = s_idx < t_idx
    j_idx = lax.broadcasted_iota(jnp.int32, (TK, TK), 1)
    r_idx = lax.broadcasted_iota(jnp.int32, (TK, TK), 0)
    strictly_later = jnp.where(j_idx > r_idx, 1.0, 0.0).astype(BF16)

    q_masked = []
    for h in range(n_heads):
        p, hh = divmod(h, heads_per_tile)
        q_pair = qT_ref[0, p * LANES:(p + 1) * LANES, :]
        in_head = (row >= hh * GROUP_DIM) & (row < (hh + 1) * GROUP_DIM)
        q_masked.append(jnp.where(in_head, q_pair, jnp.zeros_like(q_pair)))

    lo_half, hi_half = slice(0, half), slice(half, TK)
    diag_parts = [(lo_half, lo_half, "tri"), (lo_half, hi_half, "full"),
                  (hi_half, lo_half, "empty"), (hi_half, hi_half, "tri")]
    full_parts = [(slice(0, TK), slice(0, TQ), "full")]

    def assemble(parts, pieces):
        if len(parts) == 1:
            return pieces[0]
        return jnp.concatenate([jnp.concatenate(pieces[0:2], axis=1),
                                jnp.concatenate(pieces[2:4], axis=1)], axis=0)

    def key_blocks(blocks, runs, accs):
        runs, accs = list(runs), list(accs)
        tiles = [dict(kb=kb, h=h, parts=diag_parts if diag else full_parts)
                 for kb, diag in blocks for h in range(n_heads)]

        def scores(t):
            start = pl.multiple_of(t["kb"] * TK, TK)
            p = t["h"] // heads_per_tile
            t["z"] = _dot(k_ref[0, pl.ds(start, TK), p * LANES:(p + 1) * LANES], q_masked[t["h"]])

        def softplus(t):
            sps, t["logb"] = [], []
            for rows, cols, kind in t["parts"]:
                if kind == "empty":
                    sps.append(jnp.zeros((half, half), BF16))
                    t["logb"].append(None)
                    continue
                z = t["z"][rows, cols]
                sp = jnp.maximum(z, 0.0) + jnp.log2(1.0 + jnp.exp2(-jnp.abs(z)))
                t["logb"].append(z - sp)
                if kind == "tri":
                    sp = jnp.where(strictly_earlier_half, sp, 0.0)
                sps.append(sp.astype(BF16))
            t["sp"] = assemble(t["parts"], sps)

        def later_sum(t):
            t["csum"] = _dot(strictly_later, t["sp"])

        def weights(t):
            run = runs[t["h"]]
            probs = []
            for (rows, cols, kind), logb in zip(t["parts"], t["logb"]):
                if kind == "empty":
                    probs.append(jnp.zeros((half, half), BF16))
                    continue
                a = jnp.exp2(logb - (run[:, cols] + t["csum"][rows, cols]))
                if kind == "tri":
                    a = jnp.where(strictly_earlier_half, a, 0.0)
                probs.append(a.astype(BF16))
            t["prob"] = assemble(t["parts"], probs)
            runs[t["h"]] = run + (t["csum"][0:1, :] + t["sp"][0:1, :].astype(F32))

        def values(t):
            h = t["h"]
            v_blk = vT_ref[0, t["kb"], h * GROUP_DIM:(h + 1) * GROUP_DIM, :]
            accs[h] = accs[h] + _dot(v_blk, t["prob"])

        n = len(tiles)
        lead, lag = SKEW
        for step in range(-lead, n + lag):
            if 0 <= step + lead < n:
                scores(tiles[step + lead])
            if 0 <= step < n:
                softplus(tiles[step])
                later_sum(tiles[step])
            if 0 <= step - lag < n:
                weights(tiles[step - lag])
                values(tiles[step - lag])
        return tuple(runs), tuple(accs)

    runs = tuple(jnp.zeros((1, TQ), F32) for _ in range(n_heads))
    accs = tuple(jnp.zeros((GROUP_DIM, TQ), F32) for _ in range(n_heads))
    runs, accs = lax.cond(
        i > 0,
        lambda: key_blocks([(i, True), (i - 1, False)], runs, accs),
        lambda: key_blocks([(i, True)], runs, accs))

    def sticks_left(state):
        kb, runs, _ = state
        least = functools.reduce(jnp.minimum, runs)
        return (kb >= 0) & (jnp.min(least) < UNDERFLOW_LOG2)

    def earlier_block(state):
        kb, runs, accs = state
        runs, accs = key_blocks([(kb, False)], runs, accs)
        return kb - 1, runs, accs

    _, _, accs = lax.while_loop(sticks_left, earlier_block, (i - 2, runs, accs))
    yT = jnp.concatenate(accs, axis=0) * gzT_ref[0]
    o_ref[0] = yT.T.astype(BF16)


def _attention(qT, k, vT, gzT):
    bsz, d_b, seq = qT.shape
    w = ATTN_HEADS_PER_STEP * GROUP_DIM
    grid = (bsz, d_b // w, seq // TQ)
    return pl.pallas_call(
        _attn_kernel,
        grid=grid,
        in_specs=[
            pl.BlockSpec((1, w, TQ), lambda b, p, i: (b, p, i)),
            pl.BlockSpec((1, seq, w), lambda b, p, i: (b, 0, p)),
            pl.BlockSpec((1, seq // TK, w, TK), lambda b, p, i: (b, 0, p, 0)),
            pl.BlockSpec((1, w, TQ), lambda b, p, i: (b, p, i)),
        ],
        out_specs=pl.BlockSpec((1, TQ, w), lambda b, p, i: (b, i, p)),
        out_shape=jax.ShapeDtypeStruct((bsz, seq, d_b), BF16),
        compiler_params=pltpu.CompilerParams(
            dimension_semantics=("arbitrary", "arbitrary", "arbitrary"),
            vmem_limit_bytes=VMEM_LIMIT),
        name="stickbreak_attn",
    )(qT, k, vT, gzT)


def _out_kernel(alpha, ya_ref, yb_ref, wo_ref, x_ref, mod_ref, g_ref, b_ref, o_ref):
    d_a = ya_ref.shape[-1]
    y = _dot(ya_ref[0], wo_ref[0:d_a, :]) + _dot(yb_ref[0], wo_ref[d_a:, :])
    gate = mod_ref[0, 2:3, :]
    r = alpha * x_ref[0] + gate * y
    mu = jnp.mean(r, axis=-1, keepdims=True)
    rc = r - mu
    var = jnp.mean(rc * rc, axis=-1, keepdims=True)
    o_ref[0] = rc * lax.rsqrt(var + LN_EPS) * g_ref[...] + b_ref[...]


def _outproj(ya, yb, w_out, x, mod3, ln_g, ln_b, alpha):
    bsz, seq, d = x.shape
    d_a = ya.shape[-1]
    d_b = yb.shape[-1]
    grid = (bsz, seq // TM)
    const2 = lambda b, m: (0, 0)
    return pl.pallas_call(
        functools.partial(_out_kernel, alpha),
        grid=grid,
        in_specs=[
            pl.BlockSpec((1, TM, d_a), lambda b, m: (b, m, 0)),
            pl.BlockSpec((1, TM, d_b), lambda b, m: (b, m, 0)),
            pl.BlockSpec(w_out.shape, const2),
            pl.BlockSpec((1, TM, d), lambda b, m: (b, m, 0)),
            pl.BlockSpec((1, 3, d), lambda b, m: (b, 0, 0)),
            pl.BlockSpec((1, d), const2),
            pl.BlockSpec((1, d), const2),
        ],
        out_specs=pl.BlockSpec((1, TM, d), lambda b, m: (b, m, 0)),
        out_shape=jax.ShapeDtypeStruct((bsz, seq, d), x.dtype),
        compiler_params=pltpu.CompilerParams(
            dimension_semantics=("arbitrary", "arbitrary"), vmem_limit_bytes=VMEM_LIMIT),
        name="outproj_deepnorm",
    )(ya, yb, w_out, x, mod3, ln_g, ln_b)


def kernel(x, c, w_ada, b_ada, w_in, sgu_ln_g, sgu_ln_b, w_spatial, b_spatial, w_out, ln_g, ln_b):
    depth = w_ada.shape[0]
    bsz, seq, d = x.shape
    d_a = sgu_ln_g.shape[-1]
    alpha = (2.0 * depth) ** 0.25
    for layer in range(depth):
        mod3 = _modulation(c, w_ada[layer], b_ada[layer]).reshape(bsz, 3, d)
        w = w_in[layer].astype(BF16)
        d_b = (w.shape[1] - 3 * d_a) // 4
        o = 3 * d_a
        w_nn = jnp.concatenate([w[:, :3 * d_a], w[:, o + d_b:o + 2 * d_b]], axis=1)
        w_nt = jnp.concatenate(
            [w[:, o:o + d_b], w[:, o + 2 * d_b:o + 3 * d_b], w[:, o + 3 * d_b:]], axis=1).T
        b_sp_full = jnp.repeat(b_spatial[layer].T, GROUP_DIM, axis=1)
        ya, k, qT, vT, gzT = _inproj(
            x, mod3, w_nn, w_nt, w_spatial[layer], b_sp_full,
            sgu_ln_g[layer].reshape(1, d_a), sgu_ln_b[layer].reshape(1, d_a))
        yb = _attention(qT, k, vT, gzT)
        x = _outproj(ya, yb, w_out[layer].astype(BF16), x, mod3,
                     ln_g[layer].reshape(1, d), ln_b[layer].reshape(1, d), alpha)
    return x
```

```python
import functools
import math

import jax
import jax.numpy as jnp
from jax import lax
from jax.experimental import pallas as pl
from jax.experimental.pallas import tpu as pltpu

F32 = jnp.float32
BF16 = jnp.bfloat16

N_GROUPS = 8
GROUP_DIM = 64
CHUNK = 128
LN_EPS = 1e-5
LANES = 128

TM = 1024
OUT_ROWS = 256
TQ = 256
TK = 256
ATTN_HEADS_PER_STEP = 8
SKEW = (3, 4)
UNDERFLOW_LOG2 = 152.0
VMEM_LIMIT = 48 * 1024 * 1024


def _dot(a, b):
    return jnp.dot(a, b, preferred_element_type=F32)


def _dot_nt(a, b):
    return lax.dot_general(a, b, (((1,), (1,)), ((), ())), preferred_element_type=F32)


def _gelu_tanh(x):
    c = math.sqrt(2.0 / math.pi)
    return 0.5 * x * (1.0 + jnp.tanh(c * (x + 0.044715 * (x * x * x))))


def _silu(x):
    return x / (1.0 + jnp.exp(-x))


def _mod_kernel(c_ref, w_ref, b_ref, o_ref):
    c = c_ref[...]
    sc = _silu(c).astype(BF16)
    o_ref[...] = _dot(sc, w_ref[...].astype(BF16)) + b_ref[...]


def _modulation(c, w_ada, b_ada):
    bsz, d = c.shape
    n = w_ada.shape[1]
    tn = 1024
    return pl.pallas_call(
        _mod_kernel,
        grid=(n // tn,),
        in_specs=[
            pl.BlockSpec((bsz, d), lambda j: (0, 0)),
            pl.BlockSpec((d, tn), lambda j: (0, j)),
            pl.BlockSpec((1, tn), lambda j: (0, j)),
        ],
        out_specs=pl.BlockSpec((bsz, tn), lambda j: (0, j)),
        out_shape=jax.ShapeDtypeStruct((bsz, n), F32),
        compiler_params=pltpu.CompilerParams(
            dimension_semantics=("arbitrary",), vmem_limit_bytes=VMEM_LIMIT),
        name="adaln_mod",
    )(c, w_ada, b_ada.reshape(1, n))


def _inproj_kernel(x_ref, mod_ref, wnn_ref, wnt_ref, wsp_ref, bsp_ref, lng_ref, lnb_ref,
                   yag_ref, k_ref, qT_ref, vT_ref, gzT_ref, ug_scr, vn_scr, ya_scr):
    d_a = yag_ref.shape[-1]
    tm = x_ref.shape[1]
    shift = mod_ref[0, 0:1, :]
    scale = mod_ref[0, 1:2, :]
    h = (x_ref[0] * (1.0 + scale) + shift).astype(BF16)

    ug_scr[...] = _gelu_tanh(_dot(h, wnn_ref[:, 0:d_a]))
    vg = _gelu_tanh(_dot(h, wnn_ref[:, d_a:2 * d_a]))
    mu = jnp.mean(vg, axis=-1, keepdims=True)
    vc = vg - mu
    var = jnp.mean(vc * vc, axis=-1, keepdims=True)
    vn_scr[...] = (vc * lax.rsqrt(var + LN_EPS) * lng_ref[...] + lnb_ref[...]).astype(BF16)

    t_idx = lax.broadcasted_iota(jnp.int32, (CHUNK, CHUNK), 0)
    s_idx = lax.broadcasted_iota(jnp.int32, (CHUNK, CHUNK), 1)
    causal = t_idx >= s_idx
    w_sp = [jnp.where(causal, wsp_ref[g], 0.0).astype(BF16) for g in range(N_GROUPS)]
    first_group = lax.broadcasted_iota(jnp.int32, (CHUNK, LANES), 1) < GROUP_DIM
    for c in range(tm // CHUNK):
        rows = slice(c * CHUNK, (c + 1) * CHUNK)
        for p in range(d_a // LANES):
            cols = slice(p * LANES, (p + 1) * LANES)
            vp = vn_scr[rows, cols]
            mixed = jnp.where(first_group, _dot(w_sp[2 * p], vp), _dot(w_sp[2 * p + 1], vp))
            ya_scr[rows, cols] = ug_scr[rows, cols] * (mixed + bsp_ref[:, cols])
    za = _dot(h, wnn_ref[:, 2 * d_a:3 * d_a])
    yag_ref[0] = (_silu(za) * ya_scr[...]).astype(BF16)

    k_ref[0] = _dot(h, wnn_ref[:, 3 * d_a:4 * d_a]).astype(BF16)
    d_b = qT_ref.shape[1]
    scale_q = math.log2(math.e) / math.sqrt(GROUP_DIM)
    qvzT = _dot_nt(wnt_ref[...], h)
    qT_ref[0] = (qvzT[0:d_b] * scale_q).astype(BF16)
    vT = qvzT[d_b:2 * d_b].astype(BF16)
    for j in range(tm // TK):
        vT_ref[0, j] = vT[:, j * TK:(j + 1) * TK]
    gzT_ref[0] = _silu(qvzT[2 * d_b:3 * d_b])


def _inproj(x, mod3, w_nn, w_nt, w_sp, b_sp_full, ln_g, ln_b):
    bsz, seq, d = x.shape
    d_a = ln_g.shape[-1]
    d_b = w_nt.shape[0] // 3
    grid = (bsz, seq // TM)
    const2 = lambda b, m: (0, 0)
    return pl.pallas_call(
        _inproj_kernel,
        grid=grid,
        in_specs=[
            pl.BlockSpec((1, TM, d), lambda b, m: (b, m, 0)),
            pl.BlockSpec((1, 3, d), lambda b, m: (b, 0, 0)),
            pl.BlockSpec(w_nn.shape, const2),
            pl.BlockSpec(w_nt.shape, const2),
            pl.BlockSpec(w_sp.shape, lambda b, m: (0, 0, 0)),
            pl.BlockSpec(b_sp_full.shape, const2),
            pl.BlockSpec((1, d_a), const2),
            pl.BlockSpec((1, d_a), const2),
        ],
        out_specs=[
            pl.BlockSpec((1, TM, d_a), lambda b, m: (b, m, 0)),
            pl.BlockSpec((1, TM, d_b), lambda b, m: (b, m, 0)),
            pl.BlockSpec((1, d_b, TM), lambda b, m: (b, 0, m)),
            pl.BlockSpec((1, TM // TK, d_b, TK), lambda b, m: (b, m, 0, 0)),
            pl.BlockSpec((1, d_b, TM), lambda b, m: (b, 0, m)),
        ],
        out_shape=[
            jax.ShapeDtypeStruct((bsz, seq, d_a), BF16),
            jax.ShapeDtypeStruct((bsz, seq, d_b), BF16),
            jax.ShapeDtypeStruct((bsz, d_b, seq), BF16),
            jax.ShapeDtypeStruct((bsz, seq // TK, d_b, TK), BF16),
            jax.ShapeDtypeStruct((bsz, d_b, seq), F32),
        ],
        scratch_shapes=[
            pltpu.VMEM((TM, d_a), F32),
            pltpu.VMEM((TM, d_a), BF16),
            pltpu.VMEM((TM, d_a), F32),
        ],
        compiler_params=pltpu.CompilerParams(
            dimension_semantics=("arbitrary", "arbitrary"), vmem_limit_bytes=VMEM_LIMIT),
        name="inproj_gmlp",
    )(x, mod3, w_nn, w_nt, w_sp, b_sp_full, ln_g, ln_b)


def _attn_kernel(qT_ref, k_ref, vT_ref, gzT_ref, o_ref):
    i = pl.program_id(2)
    n_heads = qT_ref.shape[1] // GROUP_DIM
    heads_per_tile = LANES // GROUP_DIM
    row = lax.broadcasted_iota(jnp.int32, (LANES, TQ), 0)
    half = TK // 2
    s_idx = lax.broadcasted_iota(jnp.int32, (half, half), 0)
    t_idx = lax.broadcasted_iota(jnp.int32, (half, half), 1)
    strictly_earlier_half = s_idx < t_idx
    j_idx = lax.broadcasted_iota(jnp.int32, (TK, TK), 1)
    r_idx = lax.broadcasted_iota(jnp.int32, (TK, TK), 0)
    strictly_later = jnp.where(j_idx > r_idx, 1.0, 0.0).astype(BF16)

    q_masked = []
    for h in range(n_heads):
        p, hh = divmod(h, heads_per_tile)
        q_pair = qT_ref[0, p * LANES:(p + 1) * LANES, :]
        in_head = (row >= hh * GROUP_DIM) & (row < (hh + 1) * GROUP_DIM)
        q_masked.append(jnp.where(in_head, q_pair, jnp.zeros_like(q_pair)))

    lo_half, hi_half = slice(0, half), slice(half, TK)
    diag_parts = [(lo_half, lo_half, "tri"), (lo_half, hi_half, "full"),
                  (hi_half, lo_half, "empty"), (hi_half, hi_half, "tri")]
    full_parts = [(slice(0, TK), slice(0, TQ), "full")]

    def assemble(parts, pieces):
        if len(parts) == 1:
            return pieces[0]
        return jnp.concatenate([jnp.concatenate(pieces[0:2], axis=1),
                                jnp.concatenate(pieces[2:4], axis=1)], axis=0)

    def key_blocks(blocks, runs, accs):
        runs, accs = list(runs), list(accs)
        tiles = [dict(kb=kb, h=h, parts=diag_parts if diag else full_parts)
                 for kb, diag in blocks for h in range(n_heads)]

        def scores(t):
            start = pl.multiple_of(t["kb"] * TK, TK)
            p = t["h"] // heads_per_tile
            t["z"] = _dot(k_ref[0, pl.ds(start, TK), p * LANES:(p + 1) * LANES], q_masked[t["h"]])

        def softplus(t):
            sps, t["logb"] = [], []
            for rows, cols, kind in t["parts"]:
                if kind == "empty":
                    sps.append(jnp.zeros((half, half), BF16))
                    t["logb"].append(None)
                    continue
                z = t["z"][rows, cols]
                sp = jnp.maximum(z, 0.0) + jnp.log2(1.0 + jnp.exp2(-jnp.abs(z)))
                t["logb"].append(z - sp)
                if kind == "tri":
                    sp = jnp.where(strictly_earlier_half, sp, 0.0)
                sps.append(sp.astype(BF16))
            t["sp"] = assemble(t["parts"], sps)

        def later_sum(t):
            t["csum"] = _dot(strictly_later, t["sp"])

        def weights(t):
            run = runs[t["h"]]
            probs = []
            for (rows, cols, kind), logb in zip(t["parts"], t["logb"]):
                if kind == "empty":
                    probs.append(jnp.zeros((half, half), BF16))
                    continue
                a = jnp.exp2(logb - (run[:, cols] + t["csum"][rows, cols]))
                if kind == "tri":
                    a = jnp.where(strictly_earlier_half, a, 0.0)
                probs.append(a.astype(BF16))
            t["prob"] = assemble(t["parts"], probs)
            runs[t["h"]] = run + (t["csum"][0:1, :] + t["sp"][0:1, :].astype(F32))

        def values(t):
            h = t["h"]
            v_blk = vT_ref[0, t["kb"], h * GROUP_DIM:(h + 1) * GROUP_DIM, :]
            accs[h] = accs[h] + _dot(v_blk, t["prob"])

        n = len(tiles)
        lead, lag = SKEW
        for step in range(-lead, n + lag):
            if 0 <= step + lead < n:
                scores(tiles[step + lead])
            if 0 <= step < n:
                softplus(tiles[step])
                later_sum(tiles[step])
            if 0 <= step - lag < n:
                weights(tiles[step - lag])
                values(tiles[step - lag])
        return tuple(runs), tuple(accs)

    runs = tuple(jnp.zeros((1, TQ), F32) for _ in range(n_heads))
    accs = tuple(jnp.zeros((GROUP_DIM, TQ), F32) for _ in range(n_heads))
    runs, accs = lax.cond(
        i > 0,
        lambda: key_blocks([(i, True), (i - 1, False)], runs, accs),
        lambda: key_blocks([(i, True)], runs, accs))

    def sticks_left(state):
        kb, runs, _ = state
        least = functools.reduce(jnp.minimum, runs)
        return (kb >= 0) & (jnp.min(least) < UNDERFLOW_LOG2)

    def earlier_block(state):
        kb, runs, accs = state
        runs, accs = key_blocks([(kb, False)], runs, accs)
        return kb - 1, runs, accs

    _, _, accs = lax.while_loop(sticks_left, earlier_block, (i - 2, runs, accs))
    yT = jnp.concatenate(accs, axis=0) * gzT_ref[0]
    o_ref[0] = yT.T.astype(BF16)


def _attention(qT, k, vT, gzT):
    bsz, d_b, seq = qT.shape
    w = ATTN_HEADS_PER_STEP * GROUP_DIM
    grid = (bsz, d_b // w, seq // TQ)
    return pl.pallas_call(
        _attn_kernel,
        grid=grid,
        in_specs=[
            pl.BlockSpec((1, w, TQ), lambda b, p, i: (b, p, i)),
            pl.BlockSpec((1, seq, w), lambda b, p, i: (b, 0, p)),
            pl.BlockSpec((1, seq // TK, w, TK), lambda b, p, i: (b, 0, p, 0)),
            pl.BlockSpec((1, w, TQ), lambda b, p, i: (b, p, i)),
        ],
        out_specs=pl.BlockSpec((1, TQ, w), lambda b, p, i: (b, i, p)),
        out_shape=jax.ShapeDtypeStruct((bsz, seq, d_b), BF16),
        compiler_params=pltpu.CompilerParams(
            dimension_semantics=("arbitrary", "arbitrary", "arbitrary"),
            vmem_limit_bytes=VMEM_LIMIT),
        name="stickbreak_attn",
    )(qT, k, vT, gzT)


def _out_kernel(alpha, ya_ref, yb_ref, wo_ref, x_ref, mod_ref, g_ref, b_ref, o_ref):
    gate = mod_ref[0, 2:3, :]
    tm = x_ref.shape[1]

    def project(c):
        rows = slice(c * OUT_ROWS, (c + 1) * OUT_ROWS)
        return _dot(jnp.concatenate([ya_ref[0, rows, :], yb_ref[0, rows, :]], axis=1), wo_ref[...])

    def normalise(c, y):
        rows = slice(c * OUT_ROWS, (c + 1) * OUT_ROWS)
        r = alpha * x_ref[0, rows, :] + gate * y
        mu = jnp.mean(r, axis=-1, keepdims=True)
        rc = r - mu
        var = jnp.mean(rc * rc, axis=-1, keepdims=True)
        o_ref[0, rows, :] = rc * lax.rsqrt(var + LN_EPS) * g_ref[...] + b_ref[...]

    n_chunks = tm // OUT_ROWS
    y = project(0)
    for c in range(n_chunks):
        y_next = project(c + 1) if c + 1 < n_chunks else None
        normalise(c, y)
        y = y_next


def _outproj(ya, yb, w_out, x, mod3, ln_g, ln_b, alpha):
    bsz, seq, d = x.shape
    d_a = ya.shape[-1]
    d_b = yb.shape[-1]
    grid = (bsz, seq // TM)
    const2 = lambda b, m: (0, 0)
    return pl.pallas_call(
        functools.partial(_out_kernel, alpha),
        grid=grid,
        in_specs=[
            pl.BlockSpec((1, TM, d_a), lambda b, m: (b, m, 0)),
            pl.BlockSpec((1, TM, d_b), lambda b, m: (b, m, 0)),
            pl.BlockSpec(w_out.shape, const2),
            pl.BlockSpec((1, TM, d), lambda b, m: (b, m, 0)),
            pl.BlockSpec((1, 3, d), lambda b, m: (b, 0, 0)),
            pl.BlockSpec((1, d), const2),
            pl.BlockSpec((1, d), const2),
        ],
        out_specs=pl.BlockSpec((1, TM, d), lambda b, m: (b, m, 0)),
        out_shape=jax.ShapeDtypeStruct((bsz, seq, d), x.dtype),
        compiler_params=pltpu.CompilerParams(
            dimension_semantics=("arbitrary", "arbitrary"), vmem_limit_bytes=VMEM_LIMIT),
        name="outproj_deepnorm",
    )(ya, yb, w_out, x, mod3, ln_g, ln_b)


def kernel(x, c, w_ada, b_ada, w_in, sgu_ln_g, sgu_ln_b, w_spatial, b_spatial, w_out, ln_g, ln_b):
    depth = w_ada.shape[0]
    bsz, seq, d = x.shape
    d_a = sgu_ln_g.shape[-1]
    alpha = (2.0 * depth) ** 0.25
    for layer in range(depth):
        mod3 = _modulation(c, w_ada[layer], b_ada[layer]).reshape(bsz, 3, d)
        w = w_in[layer].astype(BF16)
        d_b = (w.shape[1] - 3 * d_a) // 4
        o = 3 * d_a
        w_nn = jnp.concatenate([w[:, :3 * d_a], w[:, o + d_b:o + 2 * d_b]], axis=1)
        w_nt = jnp.concatenate(
            [w[:, o:o + d_b], w[:, o + 2 * d_b:o + 3 * d_b], w[:, o + 3 * d_b:]], axis=1).T
        b_sp_full = jnp.repeat(b_spatial[layer].T, GROUP_DIM, axis=1)
        ya, k, qT, vT, gzT = _inproj(
            x, mod3, w_nn, w_nt, w_spatial[layer], b_sp_full,
            sgu_ln_g[layer].reshape(1, d_a), sgu_ln_b[layer].reshape(1, d_a))
        yb = _attention(qT, k, vT, gzT)
        x = _outproj(ya, yb, w_out[layer].astype(BF16), x, mod3,
                     ln_g[layer].reshape(1, d), ln_b[layer].reshape(1, d), alpha)
    return x
```

```python
import functools
import math

import jax
import jax.numpy as jnp
from jax import lax
from jax.experimental import pallas as pl
from jax.experimental.pallas import tpu as pltpu

F32 = jnp.float32
BF16 = jnp.bfloat16

N_GROUPS = 8
GROUP_DIM = 64
CHUNK = 128
LN_EPS = 1e-5
LANES = 128

TM = 1024
OUT_ROWS = 256
TQ = 256
TK = 256
SKEW = (3, 4)
UNDERFLOW_LOG2 = 152.0
VMEM_LIMIT = 48 * 1024 * 1024


def _dot(a, b):
    return jnp.dot(a, b, preferred_element_type=F32)


def _dot_nt(a, b):
    return lax.dot_general(a, b, (((1,), (1,)), ((), ())), preferred_element_type=F32)


def _gelu_tanh(x):
    c = math.sqrt(2.0 / math.pi)
    return 0.5 * x * (1.0 + jnp.tanh(c * (x + 0.044715 * (x * x * x))))


def _silu(x):
    return x / (1.0 + jnp.exp(-x))


def _mod_kernel(c_ref, w_ref, b_ref, o_ref):
    c = c_ref[...]
    sc = _silu(c).astype(BF16)
    o_ref[...] = _dot(sc, w_ref[...].astype(BF16)) + b_ref[...]


def _modulation(c, w_ada, b_ada):
    bsz, d = c.shape
    n = w_ada.shape[1]
    tn = 1024
    return pl.pallas_call(
        _mod_kernel,
        grid=(n // tn,),
        in_specs=[
            pl.BlockSpec((bsz, d), lambda j: (0, 0)),
            pl.BlockSpec((d, tn), lambda j: (0, j)),
            pl.BlockSpec((1, tn), lambda j: (0, j)),
        ],
        out_specs=pl.BlockSpec((bsz, tn), lambda j: (0, j)),
        out_shape=jax.ShapeDtypeStruct((bsz, n), F32),
        compiler_params=pltpu.CompilerParams(
            dimension_semantics=("arbitrary",), vmem_limit_bytes=VMEM_LIMIT),
        name="adaln_mod",
    )(c, w_ada, b_ada.reshape(1, n))


def _inproj_kernel(x_ref, mod_ref, w_ref, wsp_ref, bsp_ref, lng_ref, lnb_ref,
                   yag_ref, q_ref, k_ref, v_ref, gz_ref, ug_scr, vn_scr, ya_scr):
    d_a = yag_ref.shape[-1]
    d_b = q_ref.shape[-1]
    tm = x_ref.shape[1]
    shift = mod_ref[0, 0:1, :]
    scale = mod_ref[0, 1:2, :]
    h = (x_ref[0] * (1.0 + scale) + shift).astype(BF16)

    def proj(lo, width):
        return _dot(h, w_ref[:, lo:lo + width])

    ug_scr[...] = _gelu_tanh(proj(0, d_a))
    vg = _gelu_tanh(proj(d_a, d_a))
    mu = jnp.mean(vg, axis=-1, keepdims=True)
    vc = vg - mu
    var = jnp.mean(vc * vc, axis=-1, keepdims=True)
    vn_scr[...] = (vc * lax.rsqrt(var + LN_EPS) * lng_ref[...] + lnb_ref[...]).astype(BF16)

    t_idx = lax.broadcasted_iota(jnp.int32, (CHUNK, CHUNK), 0)
    s_idx = lax.broadcasted_iota(jnp.int32, (CHUNK, CHUNK), 1)
    causal = t_idx >= s_idx
    w_sp = [jnp.where(causal, wsp_ref[g], 0.0).astype(BF16) for g in range(N_GROUPS)]
    first_group = lax.broadcasted_iota(jnp.int32, (CHUNK, LANES), 1) < GROUP_DIM
    for c in range(tm // CHUNK):
        rows = slice(c * CHUNK, (c + 1) * CHUNK)
        for p in range(d_a // LANES):
            cols = slice(p * LANES, (p + 1) * LANES)
            vp = vn_scr[rows, cols]
            mixed = jnp.where(first_group, _dot(w_sp[2 * p], vp), _dot(w_sp[2 * p + 1], vp))
            ya_scr[rows, cols] = ug_scr[rows, cols] * (mixed + bsp_ref[:, cols])
    yag_ref[0] = (_silu(proj(2 * d_a, d_a)) * ya_scr[...]).astype(BF16)

    o = 3 * d_a
    scale_q = math.log2(math.e) / math.sqrt(GROUP_DIM)
    q_ref[0] = (proj(o, d_b) * scale_q).astype(BF16)
    k_ref[0] = proj(o + d_b, d_b).astype(BF16)
    v_ref[0] = proj(o + 2 * d_b, d_b).astype(BF16)
    gz_ref[0] = _silu(proj(o + 3 * d_b, d_b))


def _inproj(x, mod3, w, w_sp, b_sp_full, ln_g, ln_b):
    bsz, seq, d = x.shape
    d_a = ln_g.shape[-1]
    d_b = (w.shape[1] - 3 * d_a) // 4
    grid = (bsz, seq // TM)
    const2 = lambda b, m: (0, 0)
    rows = lambda width: pl.BlockSpec((1, TM, width), lambda b, m: (b, m, 0))
    return pl.pallas_call(
        _inproj_kernel,
        grid=grid,
        in_specs=[
            rows(d),
            pl.BlockSpec((1, 3, d), lambda b, m: (b, 0, 0)),
            pl.BlockSpec(w.shape, const2),
            pl.BlockSpec(w_sp.shape, lambda b, m: (0, 0, 0)),
            pl.BlockSpec(b_sp_full.shape, const2),
            pl.BlockSpec((1, d_a), const2),
            pl.BlockSpec((1, d_a), const2),
        ],
        out_specs=[rows(d_a), rows(d_b), rows(d_b), rows(d_b), rows(d_b)],
        out_shape=[
            jax.ShapeDtypeStruct((bsz, seq, d_a), BF16),
            jax.ShapeDtypeStruct((bsz, seq, d_b), BF16),
            jax.ShapeDtypeStruct((bsz, seq, d_b), BF16),
            jax.ShapeDtypeStruct((bsz, seq, d_b), BF16),
            jax.ShapeDtypeStruct((bsz, seq, d_b), F32),
        ],
        scratch_shapes=[
            pltpu.VMEM((TM, d_a), F32),
            pltpu.VMEM((TM, d_a), BF16),
            pltpu.VMEM((TM, d_a), F32),
        ],
        compiler_params=pltpu.CompilerParams(
            dimension_semantics=("arbitrary", "arbitrary"), vmem_limit_bytes=VMEM_LIMIT),
        name="inproj_gmlp",
    )(x, mod3, w, w_sp, b_sp_full, ln_g, ln_b)


def _attn_kernel(q_ref, k_ref, v_ref, gz_ref, o_ref):
    i = pl.program_id(1)
    n_heads = q_ref.shape[-1] // GROUP_DIM
    heads_per_tile = LANES // GROUP_DIM
    lane = lax.broadcasted_iota(jnp.int32, (TQ, LANES), 1)
    half = TK // 2
    s_idx = lax.broadcasted_iota(jnp.int32, (half, half), 0)
    t_idx = lax.broadcasted_iota(jnp.int32, (half, half), 1)
    strictly_earlier_half = s_idx < t_idx
    j_idx = lax.broadcasted_iota(jnp.int32, (TK, TK), 1)
    r_idx = lax.broadcasted_iota(jnp.int32, (TK, TK), 0)
    strictly_later = jnp.where(j_idx > r_idx, 1.0, 0.0).astype(BF16)

    q_masked = []
    for h in range(n_heads):
        p, hh = divmod(h, heads_per_tile)
        q_pair = q_ref[0, :, p * LANES:(p + 1) * LANES]
        in_head = (lane >= hh * GROUP_DIM) & (lane < (hh + 1) * GROUP_DIM)
        q_masked.append(jnp.where(in_head, q_pair, jnp.zeros_like(q_pair)))

    lo_half, hi_half = slice(0, half), slice(half, TK)
    diag_parts = [(lo_half, lo_half, "tri"), (lo_half, hi_half, "full"),
                  (hi_half, lo_half, "empty"), (hi_half, hi_half, "tri")]
    full_parts = [(slice(0, TK), slice(0, TQ), "full")]

    def assemble(parts, pieces):
        if len(parts) == 1:
            return pieces[0]
        return jnp.concatenate([jnp.concatenate(pieces[0:2], axis=1),
                                jnp.concatenate(pieces[2:4], axis=1)], axis=0)

    def key_blocks(blocks, runs, accs):
        runs, accs = list(runs), list(accs)
        tiles = [dict(blk=n, kb=kb, h=h, parts=diag_parts if diag else full_parts)
                 for n, (kb, diag) in enumerate(blocks) for h in range(n_heads)]
        v_t = {}

        def scores(t):
            start = pl.multiple_of(t["kb"] * TK, TK)
            p = t["h"] // heads_per_tile
            lanes = slice(p * LANES, (p + 1) * LANES)
            t["z"] = _dot_nt(k_ref[0, pl.ds(start, TK), lanes], q_masked[t["h"]])
            if (t["blk"], p) not in v_t:
                v_pair = v_ref[0, pl.ds(start, TK), lanes]
                v_t[(t["blk"], p)] = v_pair.astype(F32).T.astype(BF16)

        def softplus(t):
            sps, t["logb"] = [], []
            for rows, cols, kind in t["parts"]:
                if kind == "empty":
                    sps.append(jnp.zeros((half, half), BF16))
                    t["logb"].append(None)
                    continue
                z = t["z"][rows, cols]
                sp = jnp.maximum(z, 0.0) + jnp.log2(1.0 + jnp.exp2(-jnp.abs(z)))
                t["logb"].append(z - sp)
                if kind == "tri":
                    sp = jnp.where(strictly_earlier_half, sp, 0.0)
                sps.append(sp.astype(BF16))
            t["sp"] = assemble(t["parts"], sps)

        def later_sum(t):
            t["csum"] = _dot(strictly_later, t["sp"])

        def weights(t):
            run = runs[t["h"]]
            probs = []
            for (rows, cols, kind), logb in zip(t["parts"], t["logb"]):
                if kind == "empty":
                    probs.append(jnp.zeros((half, half), BF16))
                    continue
                a = jnp.exp2(logb - (run[:, cols] + t["csum"][rows, cols]))
                if kind == "tri":
                    a = jnp.where(strictly_earlier_half, a, 0.0)
                probs.append(a.astype(BF16))
            t["prob"] = assemble(t["parts"], probs)
            runs[t["h"]] = run + (t["csum"][0:1, :] + t["sp"][0:1, :].astype(F32))

        def values(t):
            h = t["h"]
            p, hh = divmod(h, heads_per_tile)
            v_blk = v_t[(t["blk"], p)][hh * GROUP_DIM:(hh + 1) * GROUP_DIM, :]
            accs[h] = accs[h] + _dot(v_blk, t["prob"])

        n = len(tiles)
        lead, lag = SKEW
        for step in range(-lead, n + lag):
            if 0 <= step + lead < n:
                scores(tiles[step + lead])
            if 0 <= step < n:
                softplus(tiles[step])
                later_sum(tiles[step])
            if 0 <= step - lag < n:
                weights(tiles[step - lag])
                values(tiles[step - lag])
        return tuple(runs), tuple(accs)

    runs = tuple(jnp.zeros((1, TQ), F32) for _ in range(n_heads))
    accs = tuple(jnp.zeros((GROUP_DIM, TQ), F32) for _ in range(n_heads))
    runs, accs = lax.cond(
        i > 0,
        lambda: key_blocks([(i, True), (i - 1, False)], runs, accs),
        lambda: key_blocks([(i, True)], runs, accs))

    def sticks_left(state):
        kb, runs, _ = state
        least = functools.reduce(jnp.minimum, runs)
        return (kb >= 0) & (jnp.min(least) < UNDERFLOW_LOG2)

    def earlier_block(state):
        kb, runs, accs = state
        runs, accs = key_blocks([(kb, False)], runs, accs)
        return kb - 1, runs, accs

    _, _, accs = lax.while_loop(sticks_left, earlier_block, (i - 2, runs, accs))
    yT = jnp.concatenate(accs, axis=0)
    o_ref[0] = (yT.T * gz_ref[0]).astype(BF16)


def _attention(q, k, v, gz):
    bsz, seq, d_b = q.shape
    grid = (bsz, seq // TQ)
    q_rows = pl.BlockSpec((1, TQ, d_b), lambda b, i: (b, i, 0))
    all_rows = pl.BlockSpec((1, seq, d_b), lambda b, i: (b, 0, 0))
    return pl.pallas_call(
        _attn_kernel,
        grid=grid,
        in_specs=[q_rows, all_rows, all_rows, q_rows],
        out_specs=q_rows,
        out_shape=jax.ShapeDtypeStruct((bsz, seq, d_b), BF16),
        compiler_params=pltpu.CompilerParams(
            dimension_semantics=("arbitrary", "arbitrary"), vmem_limit_bytes=VMEM_LIMIT),
        name="stickbreak_attn",
    )(q, k, v, gz)


def _out_kernel(alpha, ya_ref, yb_ref, wo_ref, x_ref, mod_ref, g_ref, b_ref, o_ref):
    gate = mod_ref[0, 2:3, :]
    tm = x_ref.shape[1]

    def project(c):
        rows = slice(c * OUT_ROWS, (c + 1) * OUT_ROWS)
        return _dot(jnp.concatenate([ya_ref[0, rows, :], yb_ref[0, rows, :]], axis=1), wo_ref[...])

    def normalise(c, y):
        rows = slice(c * OUT_ROWS, (c + 1) * OUT_ROWS)
        r = alpha * x_ref[0, rows, :] + gate * y
        mu = jnp.mean(r, axis=-1, keepdims=True)
        rc = r - mu
        var = jnp.mean(rc * rc, axis=-1, keepdims=True)
        o_ref[0, rows, :] = rc * lax.rsqrt(var + LN_EPS) * g_ref[...] + b_ref[...]

    n_chunks = tm // OUT_ROWS
    y = project(0)
    for c in range(n_chunks):
        y_next = project(c + 1) if c + 1 < n_chunks else None
        normalise(c, y)
        y = y_next


def _outproj(ya, yb, w_out, x, mod3, ln_g, ln_b, alpha):
    bsz, seq, d = x.shape
    d_a = ya.shape[-1]
    d_b = yb.shape[-1]
    grid = (bsz, seq // TM)
    const2 = lambda b, m: (0, 0)
    return pl.pallas_call(
        functools.partial(_out_kernel, alpha),
        grid=grid,
        in_specs=[
            pl.BlockSpec((1, TM, d_a), lambda b, m: (b, m, 0)),
            pl.BlockSpec((1, TM, d_b), lambda b, m: (b, m, 0)),
            pl.BlockSpec(w_out.shape, const2),
            pl.BlockSpec((1, TM, d), lambda b, m: (b, m, 0)),
            pl.BlockSpec((1, 3, d), lambda b, m: (b, 0, 0)),
            pl.BlockSpec((1, d), const2),
            pl.BlockSpec((1, d), const2),
        ],
        out_specs=pl.BlockSpec((1, TM, d), lambda b, m: (b, m, 0)),
        out_shape=jax.ShapeDtypeStruct((bsz, seq, d), x.dtype),
        compiler_params=pltpu.CompilerParams(
            dimension_semantics=("arbitrary", "arbitrary"), vmem_limit_bytes=VMEM_LIMIT),
        name="outproj_deepnorm",
    )(ya, yb, w_out, x, mod3, ln_g, ln_b)


def kernel(x, c, w_ada, b_ada, w_in, sgu_ln_g, sgu_ln_b, w_spatial, b_spatial, w_out, ln_g, ln_b):
    depth = w_ada.shape[0]
    bsz, seq, d = x.shape
    d_a = sgu_ln_g.shape[-1]
    alpha = (2.0 * depth) ** 0.25
    for layer in range(depth):
        mod3 = _modulation(c, w_ada[layer], b_ada[layer]).reshape(bsz, 3, d)
        b_sp_full = jnp.repeat(b_spatial[layer].T, GROUP_DIM, axis=1)
        ya, q, k, v, gz = _inproj(
            x, mod3, w_in[layer].astype(BF16), w_spatial[layer], b_sp_full,
            sgu_ln_g[layer].reshape(1, d_a), sgu_ln_b[layer].reshape(1, d_a))
        yb = _attention(q, k, v, gz)
        x = _outproj(ya, yb, w_out[layer].astype(BF16), x, mod3,
                     ln_g[layer].reshape(1, d), ln_b[layer].reshape(1, d), alpha)
    return x
```

```python
import functools
import math

import jax
import jax.numpy as jnp
from jax import lax
from jax.experimental import pallas as pl
from jax.experimental.pallas import tpu as pltpu

F32 = jnp.float32
BF16 = jnp.bfloat16

N_GROUPS = 8
GROUP_DIM = 64
CHUNK = 128
LN_EPS = 1e-5
LANES = 128

TM = 1024
OUT_ROWS = 256
TQ = 256
TK = 256
SKEW = (3, 4)
UNDERFLOW_LOG2 = 152.0
VMEM_LIMIT = 48 * 1024 * 1024


def _dot(a, b):
    return jnp.dot(a, b, preferred_element_type=F32)


def _dot_nt(a, b):
    return lax.dot_general(a, b, (((1,), (1,)), ((), ())), preferred_element_type=F32)


def _gelu_tanh(x):
    c = math.sqrt(2.0 / math.pi)
    return 0.5 * x * (1.0 + jnp.tanh(c * (x + 0.044715 * (x * x * x))))


def _silu(x):
    return x / (1.0 + jnp.exp(-x))


def _mod_kernel(c_ref, w_ref, b_ref, o_ref):
    c = c_ref[...]
    sc = _silu(c).astype(BF16)
    o_ref[...] = _dot(sc, w_ref[...].astype(BF16)) + b_ref[...]


def _modulation(c, w_ada, b_ada):
    bsz, d = c.shape
    n = w_ada.shape[1]
    tn = 1024
    return pl.pallas_call(
        _mod_kernel,
        grid=(n // tn,),
        in_specs=[
            pl.BlockSpec((bsz, d), lambda j: (0, 0)),
            pl.BlockSpec((d, tn), lambda j: (0, j)),
            pl.BlockSpec((1, tn), lambda j: (0, j)),
        ],
        out_specs=pl.BlockSpec((bsz, tn), lambda j: (0, j)),
        out_shape=jax.ShapeDtypeStruct((bsz, n), F32),
        compiler_params=pltpu.CompilerParams(
            dimension_semantics=("arbitrary",), vmem_limit_bytes=VMEM_LIMIT),
        name="adaln_mod",
    )(c, w_ada, b_ada.reshape(1, n))


def _inproj_kernel(x_ref, mod_ref, w_ref, wsp_ref, bsp_ref, lng_ref, lnb_ref,
                   yag_ref, q_ref, k_ref, vT_ref, gz_ref, ug_scr, vn_scr, ya_scr):
    d_a = yag_ref.shape[-1]
    d_b = q_ref.shape[-1]
    tm = x_ref.shape[1]
    shift = mod_ref[0, 0:1, :]
    scale = mod_ref[0, 1:2, :]
    h = (x_ref[0] * (1.0 + scale) + shift).astype(BF16)

    def proj(lo, width):
        return _dot(h, w_ref[:, lo:lo + width])

    ug_scr[...] = _gelu_tanh(proj(0, d_a))
    vg = _gelu_tanh(proj(d_a, d_a))
    mu = jnp.mean(vg, axis=-1, keepdims=True)
    vc = vg - mu
    var = jnp.mean(vc * vc, axis=-1, keepdims=True)
    vn_scr[...] = (vc * lax.rsqrt(var + LN_EPS) * lng_ref[...] + lnb_ref[...]).astype(BF16)

    t_idx = lax.broadcasted_iota(jnp.int32, (CHUNK, CHUNK), 0)
    s_idx = lax.broadcasted_iota(jnp.int32, (CHUNK, CHUNK), 1)
    causal = t_idx >= s_idx
    w_sp = [jnp.where(causal, wsp_ref[g], 0.0).astype(BF16) for g in range(N_GROUPS)]
    first_group = lax.broadcasted_iota(jnp.int32, (CHUNK, LANES), 1) < GROUP_DIM
    for c in range(tm // CHUNK):
        rows = slice(c * CHUNK, (c + 1) * CHUNK)
        for p in range(d_a // LANES):
            cols = slice(p * LANES, (p + 1) * LANES)
            vp = vn_scr[rows, cols]
            mixed = jnp.where(first_group, _dot(w_sp[2 * p], vp), _dot(w_sp[2 * p + 1], vp))
            ya_scr[rows, cols] = ug_scr[rows, cols] * (mixed + bsp_ref[:, cols])
    yag_ref[0] = (_silu(proj(2 * d_a, d_a)) * ya_scr[...]).astype(BF16)

    o = 3 * d_a
    scale_q = math.log2(math.e) / math.sqrt(GROUP_DIM)
    q_ref[0] = (proj(o, d_b) * scale_q).astype(BF16)
    k_ref[0] = proj(o + d_b, d_b).astype(BF16)
    v = proj(o + 2 * d_b, d_b)
    for j in range(tm // TK):
        vT_ref[0, j] = v[j * TK:(j + 1) * TK, :].T.astype(BF16)
    gz_ref[0] = _silu(proj(o + 3 * d_b, d_b))


def _inproj(x, mod3, w, w_sp, b_sp_full, ln_g, ln_b):
    bsz, seq, d = x.shape
    d_a = ln_g.shape[-1]
    d_b = (w.shape[1] - 3 * d_a) // 4
    grid = (bsz, seq // TM)
    const2 = lambda b, m: (0, 0)
    rows = lambda width: pl.BlockSpec((1, TM, width), lambda b, m: (b, m, 0))
    return pl.pallas_call(
        _inproj_kernel,
        grid=grid,
        in_specs=[
            rows(d),
            pl.BlockSpec((1, 3, d), lambda b, m: (b, 0, 0)),
            pl.BlockSpec(w.shape, const2),
            pl.BlockSpec(w_sp.shape, lambda b, m: (0, 0, 0)),
            pl.BlockSpec(b_sp_full.shape, const2),
            pl.BlockSpec((1, d_a), const2),
            pl.BlockSpec((1, d_a), const2),
        ],
        out_specs=[rows(d_a), rows(d_b), rows(d_b),
                   pl.BlockSpec((1, TM // TK, d_b, TK), lambda b, m: (b, m, 0, 0)),
                   rows(d_b)],
        out_shape=[
            jax.ShapeDtypeStruct((bsz, seq, d_a), BF16),
            jax.ShapeDtypeStruct((bsz, seq, d_b), BF16),
            jax.ShapeDtypeStruct((bsz, seq, d_b), BF16),
            jax.ShapeDtypeStruct((bsz, seq // TK, d_b, TK), BF16),
            jax.ShapeDtypeStruct((bsz, seq, d_b), F32),
        ],
        scratch_shapes=[
            pltpu.VMEM((TM, d_a), F32),
            pltpu.VMEM((TM, d_a), BF16),
            pltpu.VMEM((TM, d_a), F32),
        ],
        compiler_params=pltpu.CompilerParams(
            dimension_semantics=("arbitrary", "arbitrary"), vmem_limit_bytes=VMEM_LIMIT),
        name="inproj_gmlp",
    )(x, mod3, w, w_sp, b_sp_full, ln_g, ln_b)


def _attn_kernel(q_ref, k_ref, vT_ref, gz_ref, o_ref):
    i = pl.program_id(1)
    n_heads = q_ref.shape[-1] // GROUP_DIM
    heads_per_tile = LANES // GROUP_DIM
    lane = lax.broadcasted_iota(jnp.int32, (TQ, LANES), 1)
    half = TK // 2
    s_idx = lax.broadcasted_iota(jnp.int32, (half, half), 0)
    t_idx = lax.broadcasted_iota(jnp.int32, (half, half), 1)
    strictly_earlier_half = s_idx < t_idx
    j_idx = lax.broadcasted_iota(jnp.int32, (TK, TK), 1)
    r_idx = lax.broadcasted_iota(jnp.int32, (TK, TK), 0)
    strictly_later = jnp.where(j_idx > r_idx, 1.0, 0.0).astype(BF16)

    q_masked = []
    for h in range(n_heads):
        p, hh = divmod(h, heads_per_tile)
        q_pair = q_ref[0, :, p * LANES:(p + 1) * LANES]
        in_head = (lane >= hh * GROUP_DIM) & (lane < (hh + 1) * GROUP_DIM)
        q_masked.append(jnp.where(in_head, q_pair, jnp.zeros_like(q_pair)))

    lo_half, hi_half = slice(0, half), slice(half, TK)
    diag_parts = [(lo_half, lo_half, "tri"), (lo_half, hi_half, "full"),
                  (hi_half, lo_half, "empty"), (hi_half, hi_half, "tri")]
    full_parts = [(slice(0, TK), slice(0, TQ), "full")]

    def assemble(parts, pieces):
        if len(parts) == 1:
            return pieces[0]
        return jnp.concatenate([jnp.concatenate(pieces[0:2], axis=1),
                                jnp.concatenate(pieces[2:4], axis=1)], axis=0)

    def key_blocks(blocks, runs, accs):
        runs, accs = list(runs), list(accs)
        tiles = [dict(blk=n, kb=kb, h=h, parts=diag_parts if diag else full_parts)
                 for n, (kb, diag) in enumerate(blocks) for h in range(n_heads)]

        def scores(t):
            start = pl.multiple_of(t["kb"] * TK, TK)
            p = t["h"] // heads_per_tile
            lanes = slice(p * LANES, (p + 1) * LANES)
            t["z"] = _dot_nt(k_ref[0, pl.ds(start, TK), lanes], q_masked[t["h"]])

        def softplus(t):
            sps, t["logb"] = [], []
            for rows, cols, kind in t["parts"]:
                if kind == "empty":
                    sps.append(jnp.zeros((half, half), BF16))
                    t["logb"].append(None)
                    continue
                z = t["z"][rows, cols]
                sp = jnp.maximum(z, 0.0) + jnp.log2(1.0 + jnp.exp2(-jnp.abs(z)))
                t["logb"].append(z - sp)
                if kind == "tri":
                    sp = jnp.where(strictly_earlier_half, sp, 0.0)
                sps.append(sp.astype(BF16))
            t["sp"] = assemble(t["parts"], sps)

        def later_sum(t):
            t["csum"] = _dot(strictly_later, t["sp"])

        def weights(t):
            run = runs[t["h"]]
            probs = []
            for (rows, cols, kind), logb in zip(t["parts"], t["logb"]):
                if kind == "empty":
                    probs.append(jnp.zeros((half, half), BF16))
                    continue
                a = jnp.exp2(logb - (run[:, cols] + t["csum"][rows, cols]))
                if kind == "tri":
                    a = jnp.where(strictly_earlier_half, a, 0.0)
                probs.append(a.astype(BF16))
            t["prob"] = assemble(t["parts"], probs)
            runs[t["h"]] = run + (t["csum"][0:1, :] + t["sp"][0:1, :].astype(F32))

        def values(t):
            h = t["h"]
            v_blk = vT_ref[0, t["kb"], h * GROUP_DIM:(h + 1) * GROUP_DIM, :]
            accs[h] = accs[h] + _dot(v_blk, t["prob"])

        n = len(tiles)
        lead, lag = SKEW
        for step in range(-lead, n + lag):
            if 0 <= step + lead < n:
                scores(tiles[step + lead])
            if 0 <= step < n:
                softplus(tiles[step])
                later_sum(tiles[step])
            if 0 <= step - lag < n:
                weights(tiles[step - lag])
                values(tiles[step - lag])
        return tuple(runs), tuple(accs)

    runs = tuple(jnp.zeros((1, TQ), F32) for _ in range(n_heads))
    accs = tuple(jnp.zeros((GROUP_DIM, TQ), F32) for _ in range(n_heads))
    runs, accs = lax.cond(
        i > 0,
        lambda: key_blocks([(i, True), (i - 1, False)], runs, accs),
        lambda: key_blocks([(i, True)], runs, accs))

    def sticks_left(state):
        kb, runs, _ = state
        least = functools.reduce(jnp.minimum, runs)
        return (kb >= 0) & (jnp.min(least) < UNDERFLOW_LOG2)

    def earlier_block(state):
        kb, runs, accs = state
        runs, accs = key_blocks([(kb, False)], runs, accs)
        return kb - 1, runs, accs

    _, _, accs = lax.while_loop(sticks_left, earlier_block, (i - 2, runs, accs))
    yT = jnp.concatenate(accs, axis=0)
    o_ref[0] = (yT.T * gz_ref[0]).astype(BF16)


def _attention(q, k, vT, gz):
    bsz, seq, d_b = q.shape
    grid = (bsz, seq // TQ)
    q_rows = pl.BlockSpec((1, TQ, d_b), lambda b, i: (b, i, 0))
    all_k = pl.BlockSpec((1, seq, d_b), lambda b, i: (b, 0, 0))
    all_vT = pl.BlockSpec((1, seq // TK, d_b, TK), lambda b, i: (b, 0, 0, 0))
    return pl.pallas_call(
        _attn_kernel,
        grid=grid,
        in_specs=[q_rows, all_k, all_vT, q_rows],
        out_specs=q_rows,
        out_shape=jax.ShapeDtypeStruct((bsz, seq, d_b), BF16),
        compiler_params=pltpu.CompilerParams(
            dimension_semantics=("arbitrary", "arbitrary"), vmem_limit_bytes=VMEM_LIMIT),
        name="stickbreak_attn",
    )(q, k, vT, gz)


def _out_kernel(alpha, ya_ref, yb_ref, wo_ref, x_ref, mod_ref, g_ref, b_ref, o_ref):
    gate = mod_ref[0, 2:3, :]
    tm = x_ref.shape[1]

    def project(c):
        rows = slice(c * OUT_ROWS, (c + 1) * OUT_ROWS)
        return _dot(jnp.concatenate([ya_ref[0, rows, :], yb_ref[0, rows, :]], axis=1), wo_ref[...])

    def normalise(c, y):
        rows = slice(c * OUT_ROWS, (c + 1) * OUT_ROWS)
        r = alpha * x_ref[0, rows, :] + gate * y
        mu = jnp.mean(r, axis=-1, keepdims=True)
        rc = r - mu
        var = jnp.mean(rc * rc, axis=-1, keepdims=True)
        o_ref[0, rows, :] = rc * lax.rsqrt(var + LN_EPS) * g_ref[...] + b_ref[...]

    n_chunks = tm // OUT_ROWS
    y = project(0)
    for c in range(n_chunks):
        y_next = project(c + 1) if c + 1 < n_chunks else None
        normalise(c, y)
        y = y_next


def _outproj(ya, yb, w_out, x, mod3, ln_g, ln_b, alpha):
    bsz, seq, d = x.shape
    d_a = ya.shape[-1]
    d_b = yb.shape[-1]
    grid = (bsz, seq // TM)
    const2 = lambda b, m: (0, 0)
    return pl.pallas_call(
        functools.partial(_out_kernel, alpha),
        grid=grid,
        in_specs=[
            pl.BlockSpec((1, TM, d_a), lambda b, m: (b, m, 0)),
            pl.BlockSpec((1, TM, d_b), lambda b, m: (b, m, 0)),
            pl.BlockSpec(w_out.shape, const2),
            pl.BlockSpec((1, TM, d), lambda b, m: (b, m, 0)),
            pl.BlockSpec((1, 3, d), lambda b, m: (b, 0, 0)),
            pl.BlockSpec((1, d), const2),
            pl.BlockSpec((1, d), const2),
        ],
        out_specs=pl.BlockSpec((1, TM, d), lambda b, m: (b, m, 0)),
        out_shape=jax.ShapeDtypeStruct((bsz, seq, d), x.dtype),
        compiler_params=pltpu.CompilerParams(
            dimension_semantics=("arbitrary", "arbitrary"), vmem_limit_bytes=VMEM_LIMIT),
        name="outproj_deepnorm",
    )(ya, yb, w_out, x, mod3, ln_g, ln_b)


def kernel(x, c, w_ada, b_ada, w_in, sgu_ln_g, sgu_ln_b, w_spatial, b_spatial, w_out, ln_g, ln_b):
    depth = w_ada.shape[0]
    bsz, seq, d = x.shape
    d_a = sgu_ln_g.shape[-1]
    alpha = (2.0 * depth) ** 0.25
    for layer in range(depth):
        mod3 = _modulation(c, w_ada[layer], b_ada[layer]).reshape(bsz, 3, d)
        b_sp_full = jnp.repeat(b_spatial[layer].T, GROUP_DIM, axis=1)
        ya, q, k, vT, gz = _inproj(
            x, mod3, w_in[layer].astype(BF16), w_spatial[layer], b_sp_full,
            sgu_ln_g[layer].reshape(1, d_a), sgu_ln_b[layer].reshape(1, d_a))
        yb = _attention(q, k, vT, gz)
        x = _outproj(ya, yb, w_out[layer].astype(BF16), x, mod3,
                     ln_g[layer].reshape(1, d), ln_b[layer].reshape(1, d), alpha)
    return x
```

```python
import functools
import math

import jax
import jax.numpy as jnp
from jax import lax
from jax.experimental import pallas as pl
from jax.experimental.pallas import tpu as pltpu

F32 = jnp.float32
BF16 = jnp.bfloat16

N_GROUPS = 8
GROUP_DIM = 64
CHUNK = 128
LN_EPS = 1e-5
LANES = 128

TQ = 256
TK = 256
SKEW = (3, 4)
UNDERFLOW_LOG2 = 152.0
VMEM_LIMIT = 48 * 1024 * 1024


def _dot(a, b):
    return jnp.dot(a, b, preferred_element_type=F32)


def _dot_nt(a, b):
    return lax.dot_general(a, b, (((1,), (1,)), ((), ())), preferred_element_type=F32)


def _gelu_tanh(x):
    c = math.sqrt(2.0 / math.pi)
    return 0.5 * x * (1.0 + jnp.tanh(c * (x + 0.044715 * (x * x * x))))


def _silu(x):
    return x / (1.0 + jnp.exp(-x))


def _layer_norm(x, g, b):
    mu = jnp.mean(x, axis=-1, keepdims=True)
    xc = x - mu
    var = jnp.mean(xc * xc, axis=-1, keepdims=True)
    return xc * lax.rsqrt(var + LN_EPS) * g + b


def _mod_kernel(c_ref, w_ref, b_ref, o_ref):
    c = c_ref[...]
    sc = _silu(c).astype(BF16)
    o_ref[...] = _dot(sc, w_ref[...].astype(BF16)) + b_ref[...]


def _modulation(c, w_ada, b_ada):
    bsz, d = c.shape
    n = w_ada.shape[1]
    tn = 1024
    return pl.pallas_call(
        _mod_kernel,
        grid=(n // tn,),
        in_specs=[
            pl.BlockSpec((bsz, d), lambda j: (0, 0)),
            pl.BlockSpec((d, tn), lambda j: (0, j)),
            pl.BlockSpec((1, tn), lambda j: (0, j)),
        ],
        out_specs=pl.BlockSpec((bsz, tn), lambda j: (0, j)),
        out_shape=jax.ShapeDtypeStruct((bsz, n), F32),
        compiler_params=pltpu.CompilerParams(
            dimension_semantics=("arbitrary",), vmem_limit_bytes=VMEM_LIMIT),
        name="adaln_mod",
    )(c, w_ada, b_ada.reshape(1, n))


def _stage_blocks(s, n_total, nb):
    last = n_total - 1
    out = []
    for lag in range(3):
        t = jnp.clip(s - lag, 0, last)
        out.append((lax.div(t, nb), lax.rem(t, nb)))
    return out


def _block_kernel(alpha, n_total,
                  x_in_ref, x_out_ref, mod_in_ref, mod_out_ref, w_ref, wsp_ref, bsp_ref,
                  ag_ref, ab_ref, wo_ref, g_ref, b_ref, o_ref,
                  k_scr, vT_scr, q_ring, gz_ring, ya_ring, yb_ring):
    s = pl.program_id(0)
    d_b = q_ring.shape[-1]
    d_a = ya_ring.shape[-1]
    nb = k_scr.shape[1] // TK
    (b_in, i_in), (b_at, i_at), _ = _stage_blocks(s, n_total, nb)
    par_in, par_at = lax.rem(b_in, 2), lax.rem(b_at, 2)
    n_heads = d_b // GROUP_DIM
    heads_per_tile = LANES // GROUP_DIM
    now2, prev2 = lax.rem(s, 2), lax.rem(s + 1, 2)
    now3, prev3 = lax.rem(s, 3), lax.rem(s + 1, 3)

    @pl.when(s == 0)
    def _():
        q_ring[...] = jnp.zeros_like(q_ring)
        gz_ring[...] = jnp.zeros_like(gz_ring)
        ya_ring[...] = jnp.zeros_like(ya_ring)
        yb_ring[...] = jnp.zeros_like(yb_ring)
        k_scr[0, 0:TK, :] = jnp.zeros((TK, d_b), BF16)
        vT_scr[0, 0] = jnp.zeros((d_b, TK), BF16)

    new = {}

    def item_h():
        shift = mod_in_ref[0, 0:1, :]
        scale = mod_in_ref[0, 1:2, :]
        new["h"] = (x_in_ref[0] * (1.0 + scale) + shift).astype(BF16)

    def proj(lo, width):
        return _dot(new["h"], w_ref[:, lo:lo + width])

    def item_u():
        new["ug"] = _gelu_tanh(proj(0, d_a))

    def item_v():
        vg = _gelu_tanh(proj(d_a, d_a))
        new["vn"] = _layer_norm(vg, ag_ref[...], ab_ref[...]).astype(BF16)

    def item_mix():
        t_idx = lax.broadcasted_iota(jnp.int32, (CHUNK, CHUNK), 0)
        s_idx = lax.broadcasted_iota(jnp.int32, (CHUNK, CHUNK), 1)
        causal = t_idx >= s_idx
        w_sp = [jnp.where(causal, wsp_ref[g], 0.0).astype(BF16) for g in range(N_GROUPS)]
        first_group = lax.broadcasted_iota(jnp.int32, (CHUNK, LANES), 1) < GROUP_DIM
        chunks = []
        for c in range(TQ // CHUNK):
            rows = slice(c * CHUNK, (c + 1) * CHUNK)
            pieces = []
            for p in range(d_a // LANES):
                cols = slice(p * LANES, (p + 1) * LANES)
                vp = new["vn"][rows, cols]
                mixed = jnp.where(first_group, _dot(w_sp[2 * p], vp), _dot(w_sp[2 * p + 1], vp))
                pieces.append(new["ug"][rows, cols] * (mixed + bsp_ref[:, cols]))
            chunks.append(jnp.concatenate(pieces, axis=1))
        new["ya_pre"] = jnp.concatenate(chunks, axis=0)

    def item_za():
        new["ya"] = (_silu(proj(2 * d_a, d_a)) * new["ya_pre"]).astype(BF16)

    o = 3 * d_a
    scale_q = math.log2(math.e) / math.sqrt(GROUP_DIM)

    def item_q():
        new["q"] = (proj(o, d_b) * scale_q).astype(BF16)

    def item_k():
        new["k"] = proj(o + d_b, d_b).astype(BF16)

    def item_vt():
        new["vT"] = proj(o + 2 * d_b, d_b).T.astype(BF16)

    def item_gz():
        new["gz"] = _silu(proj(o + 3 * d_b, d_b))

    def item_out_proj():
        ycat = jnp.concatenate([ya_ring[prev3], yb_ring[prev2]], axis=1)
        new["y"] = _dot(ycat, wo_ref[...])

    def item_out_norm():
        gate = mod_out_ref[0, 2:3, :]
        r = alpha * x_out_ref[0] + gate * new["y"]
        o_ref[0] = _layer_norm(r, g_ref[...], b_ref[...])

    items = [item_out_proj, item_h, item_u, item_out_norm, item_v, item_mix, item_za,
             item_q, item_k, item_vt, item_gz]

    lane = lax.broadcasted_iota(jnp.int32, (TQ, LANES), 1)
    half = TK // 2
    s_idx = lax.broadcasted_iota(jnp.int32, (half, half), 0)
    t_idx = lax.broadcasted_iota(jnp.int32, (half, half), 1)
    strictly_earlier_half = s_idx < t_idx
    j_idx = lax.broadcasted_iota(jnp.int32, (TK, TK), 1)
    r_idx = lax.broadcasted_iota(jnp.int32, (TK, TK), 0)
    strictly_later = jnp.where(j_idx > r_idx, 1.0, 0.0).astype(BF16)

    q_masked = []
    for h in range(n_heads):
        p, hh = divmod(h, heads_per_tile)
        q_pair = q_ring[prev2, :, p * LANES:(p + 1) * LANES]
        in_head = (lane >= hh * GROUP_DIM) & (lane < (hh + 1) * GROUP_DIM)
        q_masked.append(jnp.where(in_head, q_pair, jnp.zeros_like(q_pair)))
    gz_old = gz_ring[prev2]

    lo_half, hi_half = slice(0, half), slice(half, TK)
    diag_parts = [(lo_half, lo_half, "tri"), (lo_half, hi_half, "full"),
                  (hi_half, lo_half, "empty"), (hi_half, hi_half, "tri")]
    full_parts = [(slice(0, TK), slice(0, TQ), "full")]

    def assemble(parts, pieces):
        if len(parts) == 1:
            return pieces[0]
        return jnp.concatenate([jnp.concatenate(pieces[0:2], axis=1),
                                jnp.concatenate(pieces[2:4], axis=1)], axis=0)

    def key_blocks(blocks, runs, accs, extra=()):
        runs, accs = list(runs), list(accs)
        tiles = [dict(kb=kb, h=h, keep=keep, parts=diag_parts if diag else full_parts)
                 for kb, diag, keep in blocks for h in range(n_heads)]

        def scores(t):
            start = pl.multiple_of(t["kb"] * TK, TK)
            p = t["h"] // heads_per_tile
            lanes = slice(p * LANES, (p + 1) * LANES)
            t["z"] = _dot_nt(k_scr[par_at, pl.ds(start, TK), lanes], q_masked[t["h"]])

        def softplus(t):
            sps, t["logb"] = [], []
            for rows, cols, kind in t["parts"]:
                if kind == "empty":
                    sps.append(jnp.zeros((half, half), BF16))
                    t["logb"].append(None)
                    continue
                z = t["z"][rows, cols]
                sp = jnp.maximum(z, 0.0) + jnp.log2(1.0 + jnp.exp2(-jnp.abs(z)))
                t["logb"].append(z - sp)
                if kind == "tri":
                    sp = jnp.where(strictly_earlier_half, sp, 0.0)
                sps.append(sp.astype(BF16))
            t["sp"] = assemble(t["parts"], sps)

        def later_sum(t):
            t["csum"] = _dot(strictly_later, t["sp"])

        def weights(t):
            run = runs[t["h"]]
            probs = []
            for (rows, cols, kind), logb in zip(t["parts"], t["logb"]):
                if kind == "empty":
                    probs.append(jnp.zeros((half, half), BF16))
                    continue
                a = jnp.exp2(logb - (run[:, cols] + t["csum"][rows, cols]))
                if kind == "tri":
                    a = jnp.where(strictly_earlier_half, a, 0.0)
                probs.append(a.astype(BF16))
            t["prob"] = assemble(t["parts"], probs)
            runs[t["h"]] = run + (t["csum"][0:1, :] + t["sp"][0:1, :].astype(F32))

        def values(t):
            h = t["h"]
            v_blk = vT_scr[par_at, t["kb"], h * GROUP_DIM:(h + 1) * GROUP_DIM, :]
            if t["keep"] is not None:
                v_blk = (v_blk.astype(F32) * t["keep"]).astype(BF16)
            accs[h] = accs[h] + _dot(v_blk, t["prob"])

        n = len(tiles)
        lead, lag = SKEW
        steps = range(-lead, n + lag)
        extra = list(extra)
        every = max(1, len(steps) // max(1, len(extra)))
        for count, step in enumerate(steps):
            if extra and count % every == 0:
                extra.pop(0)()
            if 0 <= step + lead < n:
                scores(tiles[step + lead])
            if 0 <= step < n:
                softplus(tiles[step])
                later_sum(tiles[step])
            if 0 <= step - lag < n:
                weights(tiles[step - lag])
                values(tiles[step - lag])
        for item in extra:
            item()
        return tuple(runs), tuple(accs)

    runs = tuple(jnp.zeros((1, TQ), F32) for _ in range(n_heads))
    accs = tuple(jnp.zeros((GROUP_DIM, TQ), F32) for _ in range(n_heads))
    has_earlier = (i_at > 0).astype(F32)
    runs, accs = key_blocks(
        [(i_at, True, None), (jnp.maximum(i_at - 1, 0), False, has_earlier)], runs, accs, items)

    ya_ring[now3] = new["ya"]
    q_ring[now2] = new["q"]
    gz_ring[now2] = new["gz"]
    k_scr[par_in, pl.ds(pl.multiple_of(i_in * TK, TK), TK), :] = new["k"]
    vT_scr[par_in, i_in] = new["vT"]

    def sticks_left(state):
        kb, runs, _ = state
        least = functools.reduce(jnp.minimum, runs)
        return (kb >= 0) & (jnp.min(least) < UNDERFLOW_LOG2)

    def earlier_block(state):
        kb, runs, accs = state
        runs, accs = key_blocks([(kb, False, None)], runs, accs)
        return kb - 1, runs, accs

    _, _, accs = lax.while_loop(sticks_left, earlier_block, (i_at - 2, runs, accs))
    yT = jnp.concatenate(accs, axis=0)
    yb_ring[now2] = (yT.T * gz_old).astype(BF16)


def _fused_block(x, mod3, w, w_sp, b_sp_full, a_g, a_b, w_out, ln_g, ln_b, alpha):
    bsz, seq, d = x.shape
    d_a = a_g.shape[-1]
    d_b = (w.shape[1] - 3 * d_a) // 4
    nb = seq // TQ
    n_total = bsz * nb
    grid = (n_total + 2,)

    def stage(lag):
        return lambda s: _stage_blocks(s, n_total, nb)[lag]

    def rows_of(lag):
        return pl.BlockSpec((1, TQ, d), lambda s: (*stage(lag)(s), 0))

    def mod_of(lag):
        return pl.BlockSpec((1, 3, d), lambda s: (stage(lag)(s)[0], 0, 0))

    def whole(a):
        return pl.BlockSpec(a.shape, lambda s: (0,) * a.ndim)

    return pl.pallas_call(
        functools.partial(_block_kernel, alpha, n_total),
        grid=grid,
        in_specs=[rows_of(0), rows_of(2), mod_of(0), mod_of(2), whole(w), whole(w_sp),
                  whole(b_sp_full), whole(a_g), whole(a_b), whole(w_out), whole(ln_g), whole(ln_b)],
        out_specs=rows_of(2),
        out_shape=jax.ShapeDtypeStruct((bsz, seq, d), x.dtype),
        scratch_shapes=[
            pltpu.VMEM((2, seq, d_b), BF16),
            pltpu.VMEM((2, nb, d_b, TK), BF16),
            pltpu.VMEM((2, TQ, d_b), BF16),
            pltpu.VMEM((2, TQ, d_b), F32),
            pltpu.VMEM((3, TQ, d_a), BF16),
            pltpu.VMEM((2, TQ, d_b), BF16),
        ],
        compiler_params=pltpu.CompilerParams(
            dimension_semantics=("arbitrary",), vmem_limit_bytes=VMEM_LIMIT),
        name="fused_block",
    )(x, x, mod3, mod3, w, w_sp, b_sp_full, a_g, a_b, w_out, ln_g, ln_b)


def kernel(x, c, w_ada, b_ada, w_in, sgu_ln_g, sgu_ln_b, w_spatial, b_spatial, w_out, ln_g, ln_b):
    depth = w_ada.shape[0]
    bsz, seq, d = x.shape
    d_a = sgu_ln_g.shape[-1]
    alpha = (2.0 * depth) ** 0.25
    for layer in range(depth):
        mod3 = _modulation(c, w_ada[layer], b_ada[layer]).reshape(bsz, 3, d)
        b_sp_full = jnp.repeat(b_spatial[layer].T, GROUP_DIM, axis=1)
        x = _fused_block(
            x, mod3, w_in[layer].astype(BF16), w_spatial[layer], b_sp_full,
            sgu_ln_g[layer].reshape(1, d_a), sgu_ln_b[layer].reshape(1, d_a),
            w_out[layer].astype(BF16), ln_g[layer].reshape(1, d), ln_b[layer].reshape(1, d), alpha)
    return x
```

```python
import functools
import math

import jax
import jax.numpy as jnp
from jax import lax
from jax.experimental import pallas as pl
from jax.experimental.pallas import tpu as pltpu

F32 = jnp.float32
BF16 = jnp.bfloat16

N_GROUPS = 8
GROUP_DIM = 64
CHUNK = 128
LN_EPS = 1e-5
LANES = 128

TQ = 256
TK = 256
SKEW = (4, 5)
EXTRA_START = 1
MXU_SLAB = 512
FIN_DELAY = 1
UNDERFLOW_LOG2 = 152.0
VMEM_LIMIT = 48 * 1024 * 1024


def _dot(a, b):
    return jnp.dot(a, b, preferred_element_type=F32)


def _dot_nt(a, b):
    return lax.dot_general(a, b, (((1,), (1,)), ((), ())), preferred_element_type=F32)


def _gelu_tanh(x):
    c = math.sqrt(2.0 / math.pi)
    return 0.5 * x * (1.0 + jnp.tanh(c * (x + 0.044715 * (x * x * x))))


def _silu(x):
    return x / (1.0 + jnp.exp(-x))


def _layer_norm(x, g, b):
    mu = jnp.mean(x, axis=-1, keepdims=True)
    xc = x - mu
    var = jnp.mean(xc * xc, axis=-1, keepdims=True)
    return xc * lax.rsqrt(var + LN_EPS) * g + b


def _mod_kernel(c_ref, w_ref, b_ref, o_ref):
    c = c_ref[...]
    sc = _silu(c).astype(BF16)
    o_ref[...] = _dot(sc, w_ref[...].astype(BF16)) + b_ref[...]


def _modulation(c, w_ada, b_ada):
    bsz, d = c.shape
    n = w_ada.shape[1]
    tn = 1024
    return pl.pallas_call(
        _mod_kernel,
        grid=(n // tn,),
        in_specs=[
            pl.BlockSpec((bsz, d), lambda j: (0, 0)),
            pl.BlockSpec((d, tn), lambda j: (0, j)),
            pl.BlockSpec((1, tn), lambda j: (0, j)),
        ],
        out_specs=pl.BlockSpec((bsz, tn), lambda j: (0, j)),
        out_shape=jax.ShapeDtypeStruct((bsz, n), F32),
        compiler_params=pltpu.CompilerParams(
            dimension_semantics=("arbitrary",), vmem_limit_bytes=VMEM_LIMIT),
        name="adaln_mod",
    )(c, w_ada, b_ada.reshape(1, n))


def _stage_blocks(s, n_total, nb):
    last = n_total - 1
    out = []
    for lag in range(3):
        t = jnp.clip(s - lag, 0, last)
        out.append((lax.div(t, nb), lax.rem(t, nb)))
    return out


def _block_kernel(alpha, n_total,
                  x_in_ref, x_out_ref, mod_in_ref, mod_out_ref, w_ref, wsp_ref, bsp_ref,
                  ag_ref, ab_ref, wo_ref, g_ref, b_ref, o_ref,
                  k_scr, vT_scr, q_ring, gz_ring, ya_ring, yb_ring, proj_scr, y_scr):
    s = pl.program_id(0)
    d_b = q_ring.shape[-1]
    d_a = ya_ring.shape[-1]
    nb = k_scr.shape[1] // TK
    (b_in, i_in), (b_at, i_at), _ = _stage_blocks(s, n_total, nb)
    par_in, par_at = lax.rem(b_in, 2), lax.rem(b_at, 2)
    n_heads = d_b // GROUP_DIM
    heads_per_tile = LANES // GROUP_DIM
    now2, prev2 = lax.rem(s, 2), lax.rem(s + 1, 2)
    now3, prev3 = lax.rem(s, 3), lax.rem(s + 1, 3)

    @pl.when(s == 0)
    def _():
        q_ring[...] = jnp.zeros_like(q_ring)
        gz_ring[...] = jnp.zeros_like(gz_ring)
        ya_ring[...] = jnp.zeros_like(ya_ring)
        yb_ring[...] = jnp.zeros_like(yb_ring)
        k_scr[0, 0:TK, :] = jnp.zeros((TK, d_b), BF16)
        vT_scr[0, 0] = jnp.zeros((d_b, TK), BF16)

    new = {}

    def item_h():
        shift = mod_in_ref[0, 0:1, :]
        scale = mod_in_ref[0, 1:2, :]
        new["h"] = (x_in_ref[0] * (1.0 + scale) + shift).astype(BF16)

    landing = {}

    def slabs(key, lhs, rhs_ref, lo, width, dst_ref, dst_lo):
        landing[key] = (dst_ref, dst_lo)

        def make(j):
            def item():
                cols = slice(lo + j * MXU_SLAB, lo + (j + 1) * MXU_SLAB)
                dst_ref[:, dst_lo + j * MXU_SLAB:dst_lo + (j + 1) * MXU_SLAB] = _dot(lhs(), rhs_ref[:, cols])
            return item
        return [make(j) for j in range(width // MXU_SLAB)]

    def joined(key, width):
        dst_ref, dst_lo = landing[key]
        return dst_ref[:, dst_lo:dst_lo + width]

    def h():
        return new["h"]

    def fin_u():
        new["ug"] = _gelu_tanh(joined("u", d_a))

    def fin_v():
        vg = _gelu_tanh(joined("v", d_a))
        new["vn"] = _layer_norm(vg, ag_ref[...], ab_ref[...]).astype(BF16)

    def item_mix():
        t_idx = lax.broadcasted_iota(jnp.int32, (CHUNK, CHUNK), 0)
        s_idx = lax.broadcasted_iota(jnp.int32, (CHUNK, CHUNK), 1)
        causal = t_idx >= s_idx
        w_sp = [jnp.where(causal, wsp_ref[g], 0.0).astype(BF16) for g in range(N_GROUPS)]
        first_group = lax.broadcasted_iota(jnp.int32, (CHUNK, LANES), 1) < GROUP_DIM
        chunks = []
        for c in range(TQ // CHUNK):
            rows = slice(c * CHUNK, (c + 1) * CHUNK)
            pieces = []
            for p in range(d_a // LANES):
                cols = slice(p * LANES, (p + 1) * LANES)
                vp = new["vn"][rows, cols]
                mixed = jnp.where(first_group, _dot(w_sp[2 * p], vp), _dot(w_sp[2 * p + 1], vp))
                pieces.append(new["ug"][rows, cols] * (mixed + bsp_ref[:, cols]))
            chunks.append(jnp.concatenate(pieces, axis=1))
        new["ya_pre"] = jnp.concatenate(chunks, axis=0)

    def fin_za():
        new["ya"] = (_silu(joined("za", d_a)) * new["ya_pre"]).astype(BF16)

    o = 3 * d_a
    scale_q = math.log2(math.e) / math.sqrt(GROUP_DIM)

    def fin_q():
        new["q"] = (joined("q", d_b) * scale_q).astype(BF16)

    def fin_k():
        new["k"] = joined("k", d_b).astype(BF16)

    def fin_vt():
        new["vT"] = joined("vb", d_b).T.astype(BF16)

    def fin_gz():
        new["gz"] = _silu(joined("zb", d_b))

    def ycat():
        return jnp.concatenate([ya_ring[prev3], yb_ring[prev2]], axis=1)

    def fin_out():
        gate = mod_out_ref[0, 2:3, :]
        r = alpha * x_out_ref[0] + gate * joined("y", wo_ref.shape[1])
        o_ref[0] = _layer_norm(r, g_ref[...], b_ref[...])

    def both(*fs):
        return lambda: [f() for f in fs]

    out_p = slabs("y", ycat, wo_ref, 0, wo_ref.shape[1], y_scr, 0)
    u_p, v_p, za_p = (slabs(key, h, w_ref, lo, d_a, proj_scr, lo)
                      for key, lo in (("u", 0), ("v", d_a), ("za", 2 * d_a)))
    q_p, k_p, vb_p, zb_p = (slabs(key, h, w_ref, o + n * d_b, d_b, proj_scr, o + n * d_b)
                            for n, key in enumerate(("q", "k", "vb", "zb")))
    chain = [(out_p, [fin_out]), (u_p, [fin_u]), (v_p, [fin_v, item_mix]), (za_p, [fin_za]),
             (q_p, [fin_q]), (k_p, [fin_k]), (vb_p, [fin_vt]), (zb_p, [fin_gz])]
    dots, due = [], {}
    for parts, fins in chain:
        dots.extend(parts)
        due.setdefault(len(dots) - 1 + FIN_DELAY, []).extend(fins)
    items = [item_h]
    for n, dot_item in enumerate(dots):
        items.append(both(dot_item, *due.pop(n, [])))
    for n in sorted(due):
        items.append(both(*due[n]))

    lane = lax.broadcasted_iota(jnp.int32, (TQ, LANES), 1)
    half = TK // 2
    s_idx = lax.broadcasted_iota(jnp.int32, (half, half), 0)
    t_idx = lax.broadcasted_iota(jnp.int32, (half, half), 1)
    strictly_earlier_half = s_idx < t_idx
    j_idx = lax.broadcasted_iota(jnp.int32, (TK, TK), 1)
    r_idx = lax.broadcasted_iota(jnp.int32, (TK, TK), 0)
    strictly_later = jnp.where(j_idx > r_idx, 1.0, 0.0).astype(BF16)

    q_masked = []
    for h in range(n_heads):
        p, hh = divmod(h, heads_per_tile)
        q_pair = q_ring[prev2, :, p * LANES:(p + 1) * LANES]
        in_head = (lane >= hh * GROUP_DIM) & (lane < (hh + 1) * GROUP_DIM)
        q_masked.append(jnp.where(in_head, q_pair, jnp.zeros_like(q_pair)))
    gz_old = gz_ring[prev2]

    lo_half, hi_half = slice(0, half), slice(half, TK)
    diag_parts = [(lo_half, lo_half, "tri"), (lo_half, hi_half, "full"),
                  (hi_half, lo_half, "empty"), (hi_half, hi_half, "tri")]
    full_parts = [(slice(0, TK), slice(0, TQ), "full")]

    def assemble(parts, pieces):
        if len(parts) == 1:
            return pieces[0]
        return jnp.concatenate([jnp.concatenate(pieces[0:2], axis=1),
                                jnp.concatenate(pieces[2:4], axis=1)], axis=0)

    def key_blocks(blocks, runs, accs, extra=()):
        runs, accs = list(runs), list(accs)
        tiles = [dict(kb=kb, h=h, keep=keep, parts=diag_parts if diag else full_parts)
                 for kb, diag, keep in blocks for h in range(n_heads)]

        def scores(t):
            start = pl.multiple_of(t["kb"] * TK, TK)
            p = t["h"] // heads_per_tile
            lanes = slice(p * LANES, (p + 1) * LANES)
            t["z"] = _dot_nt(k_scr[par_at, pl.ds(start, TK), lanes], q_masked[t["h"]])

        def softplus(t):
            sps, t["logb"] = [], []
            for rows, cols, kind in t["parts"]:
                if kind == "empty":
                    sps.append(jnp.zeros((half, half), BF16))
                    t["logb"].append(None)
                    continue
                z = t["z"][rows, cols]
                sp = jnp.maximum(z, 0.0) + jnp.log2(1.0 + jnp.exp2(-jnp.abs(z)))
                t["logb"].append(z - sp)
                if kind == "tri":
                    sp = jnp.where(strictly_earlier_half, sp, 0.0)
                sps.append(sp.astype(BF16))
            t["sp"] = assemble(t["parts"], sps)

        def later_sum(t):
            t["csum"] = _dot(strictly_later, t["sp"])

        def weights(t):
            run = runs[t["h"]]
            probs = []
            for (rows, cols, kind), logb in zip(t["parts"], t["logb"]):
                if kind == "empty":
                    probs.append(jnp.zeros((half, half), BF16))
                    continue
                a = jnp.exp2(logb - (run[:, cols] + t["csum"][rows, cols]))
                if kind == "tri":
                    a = jnp.where(strictly_earlier_half, a, 0.0)
                probs.append(a.astype(BF16))
            t["prob"] = assemble(t["parts"], probs)
            runs[t["h"]] = run + (t["csum"][0:1, :] + t["sp"][0:1, :].astype(F32))

        def values(t):
            h = t["h"]
            v_blk = vT_scr[par_at, t["kb"], h * GROUP_DIM:(h + 1) * GROUP_DIM, :]
            if t["keep"] is not None:
                v_blk = (v_blk.astype(F32) * t["keep"]).astype(BF16)
            accs[h] = accs[h] + _dot(v_blk, t["prob"])

        n = len(tiles)
        lead, lag = SKEW
        steps = range(-lead, n + lag)
        extra = list(extra)
        every = max(1, (len(steps) - EXTRA_START) // max(1, len(extra)))
        for count, step in enumerate(steps):
            if extra and count >= EXTRA_START and (count - EXTRA_START) % every == 0:
                extra.pop(0)()
            if 0 <= step + lead < n:
                scores(tiles[step + lead])
            if 0 <= step < n:
                softplus(tiles[step])
                later_sum(tiles[step])
            if 0 <= step - lag < n:
                weights(tiles[step - lag])
                values(tiles[step - lag])
        for item in extra:
            item()
        return tuple(runs), tuple(accs)

    runs = tuple(jnp.zeros((1, TQ), F32) for _ in range(n_heads))
    accs = tuple(jnp.zeros((GROUP_DIM, TQ), F32) for _ in range(n_heads))
    has_earlier = (i_at > 0).astype(F32)
    runs, accs = key_blocks(
        [(i_at, True, None), (jnp.maximum(i_at - 1, 0), False, has_earlier)], runs, accs, items)

    ya_ring[now3] = new["ya"]
    q_ring[now2] = new["q"]
    gz_ring[now2] = new["gz"]
    k_scr[par_in, pl.ds(pl.multiple_of(i_in * TK, TK), TK), :] = new["k"]
    vT_scr[par_in, i_in] = new["vT"]

    def sticks_left(runs):
        return jnp.min(functools.reduce(jnp.minimum, runs)) < UNDERFLOW_LOG2

    def earlier_block(state):
        kb, _, runs, accs = state
        runs, accs = key_blocks([(kb, False, None)], runs, accs)
        return kb - 1, sticks_left(runs), runs, accs

    def gated(accs):
        yT = jnp.concatenate(accs, axis=0)
        return (yT.T * gz_old).astype(BF16)

    yb_ring[now2] = gated(accs)
    first = i_at - 2

    @pl.when((first >= 0) & sticks_left(runs))
    def _():
        _, _, _, more = lax.while_loop(lambda st: (st[0] >= 0) & st[1], earlier_block,
                                       (first, True, runs, accs))
        yb_ring[now2] = gated(more)


def _fused_block(x, mod3, w, w_sp, b_sp_full, a_g, a_b, w_out, ln_g, ln_b, alpha):
    bsz, seq, d = x.shape
    d_a = a_g.shape[-1]
    d_b = (w.shape[1] - 3 * d_a) // 4
    nb = seq // TQ
    n_total = bsz * nb
    grid = (n_total + 2,)

    def stage(lag):
        return lambda s: _stage_blocks(s, n_total, nb)[lag]

    def rows_of(lag):
        return pl.BlockSpec((1, TQ, d), lambda s: (*stage(lag)(s), 0))

    def mod_of(lag):
        return pl.BlockSpec((1, 3, d), lambda s: (stage(lag)(s)[0], 0, 0))

    def whole(a):
        return pl.BlockSpec(a.shape, lambda s: (0,) * a.ndim)

    return pl.pallas_call(
        functools.partial(_block_kernel, alpha, n_total),
        grid=grid,
        in_specs=[rows_of(0), rows_of(2), mod_of(0), mod_of(2), whole(w), whole(w_sp),
                  whole(b_sp_full), whole(a_g), whole(a_b), whole(w_out), whole(ln_g), whole(ln_b)],
        out_specs=rows_of(2),
        out_shape=jax.ShapeDtypeStruct((bsz, seq, d), x.dtype),
        scratch_shapes=[
            pltpu.VMEM((2, seq, d_b), BF16),
            pltpu.VMEM((2, nb, d_b, TK), BF16),
            pltpu.VMEM((2, TQ, d_b), BF16),
            pltpu.VMEM((2, TQ, d_b), F32),
            pltpu.VMEM((3, TQ, d_a), BF16),
            pltpu.VMEM((2, TQ, d_b), BF16),
            pltpu.VMEM((TQ, w.shape[1]), F32),
            pltpu.VMEM((TQ, d), F32),
        ],
        compiler_params=pltpu.CompilerParams(
            dimension_semantics=("arbitrary",), vmem_limit_bytes=VMEM_LIMIT),
        name="fused_block",
    )(x, x, mod3, mod3, w, w_sp, b_sp_full, a_g, a_b, w_out, ln_g, ln_b)


def kernel(x, c, w_ada, b_ada, w_in, sgu_ln_g, sgu_ln_b, w_spatial, b_spatial, w_out, ln_g, ln_b):
    depth = w_ada.shape[0]
    bsz, seq, d = x.shape
    d_a = sgu_ln_g.shape[-1]
    alpha = (2.0 * depth) ** 0.25
    for layer in range(depth):
        mod3 = _modulation(c, w_ada[layer], b_ada[layer]).reshape(bsz, 3, d)
        b_sp_full = jnp.repeat(b_spatial[layer].T, GROUP_DIM, axis=1)
        x = _fused_block(
            x, mod3, w_in[layer].astype(BF16), w_spatial[layer], b_sp_full,
            sgu_ln_g[layer].reshape(1, d_a), sgu_ln_b[layer].reshape(1, d_a),
            w_out[layer].astype(BF16), ln_g[layer].reshape(1, d), ln_b[layer].reshape(1, d), alpha)
    return x
```

```python
import functools
import math

import jax
import jax.numpy as jnp
from jax import lax
from jax.experimental import pallas as pl
from jax.experimental.pallas import tpu as pltpu

F32 = jnp.float32
BF16 = jnp.bfloat16

N_GROUPS = 8
GROUP_DIM = 64
CHUNK = 128
LN_EPS = 1e-5
LANES = 128

PREP_TILE = 512
TQ = 256
TK = 256
Q_PER_STEP = 1
ROWS = Q_PER_STEP * TQ
SKEW = (4, 5)
EXTRA_START = 1
MXU_SLAB = 512
LANDING_SLOTS = 7
FIN_DELAY = 1
UNDERFLOW_LOG2 = 152.0
VMEM_LIMIT = 48 * 1024 * 1024


def _dot(a, b):
    return jnp.dot(a, b, preferred_element_type=F32)


def _dot_nt(a, b):
    return lax.dot_general(a, b, (((1,), (1,)), ((), ())), preferred_element_type=F32)


def _gelu_tanh(x):
    c = math.sqrt(2.0 / math.pi)
    return 0.5 * x * (1.0 + jnp.tanh(c * (x + 0.044715 * (x * x * x))))


def _silu(x):
    return x / (1.0 + jnp.exp(-x))


def _layer_norm(x, g, b):
    mu = jnp.mean(x, axis=-1, keepdims=True)
    xc = x - mu
    var = jnp.mean(xc * xc, axis=-1, keepdims=True)
    return xc * lax.rsqrt(var + LN_EPS) * g + b


def _prep_kernel(c_ref, wa_ref, ba_ref, win_ref, wout_ref, mod_ref, win_bf_ref, wout_bf_ref):
    sc = _silu(c_ref[...]).astype(BF16)
    mod_ref[...] = _dot(sc, wa_ref[...].astype(BF16)) + ba_ref[...]
    win_bf_ref[...] = win_ref[...].astype(BF16)
    wout_bf_ref[...] = wout_ref[...].astype(BF16)


def _prepare(c, w_ada, b_ada, w_in, w_out):
    bsz, d = c.shape
    n_mod = w_ada.shape[1]
    steps = w_in.shape[1] // PREP_TILE
    mod_tiles = n_mod // PREP_TILE
    out_tile = PREP_TILE
    out_tiles = w_out.shape[1] // out_tile
    assert mod_tiles <= steps and out_tiles <= steps
    mod_col = lambda j: (0, jnp.minimum(j, mod_tiles - 1))
    out_col = lambda j: (0, jnp.minimum(j, out_tiles - 1))
    return pl.pallas_call(
        _prep_kernel,
        grid=(steps,),
        in_specs=[
            pl.BlockSpec((bsz, d), lambda j: (0, 0)),
            pl.BlockSpec((d, PREP_TILE), mod_col),
            pl.BlockSpec((1, PREP_TILE), mod_col),
            pl.BlockSpec((w_in.shape[0], PREP_TILE), lambda j: (0, j)),
            pl.BlockSpec((w_out.shape[0], out_tile), out_col),
        ],
        out_specs=[
            pl.BlockSpec((bsz, PREP_TILE), mod_col),
            pl.BlockSpec((w_in.shape[0], PREP_TILE), lambda j: (0, j)),
            pl.BlockSpec((w_out.shape[0], out_tile), out_col),
        ],
        out_shape=[
            jax.ShapeDtypeStruct((bsz, n_mod), F32),
            jax.ShapeDtypeStruct(w_in.shape, BF16),
            jax.ShapeDtypeStruct(w_out.shape, BF16),
        ],
        compiler_params=pltpu.CompilerParams(
            dimension_semantics=("arbitrary",), vmem_limit_bytes=VMEM_LIMIT),
        name="adaln_mod_and_casts",
    )(c, w_ada, b_ada.reshape(1, n_mod), w_in, w_out)


def _stage_slabs(s, n_total, per_seq):
    last = n_total - 1
    out = []
    for lag in range(3):
        t = jnp.clip(s - lag, 0, last)
        out.append((lax.div(t, per_seq), lax.rem(t, per_seq)))
    return out


def _block_kernel(alpha, n_total,
                  x_in_ref, x_out_ref, mod_in_ref, mod_out_ref, w_ref, wsp_ref, bsp_ref,
                  ag_ref, ab_ref, wo_ref, g_ref, b_ref, o_ref,
                  k_scr, vT_scr, q_ring, gz_ring, ya_ring, yb_ring, land_scr, y_scr):
    s = pl.program_id(0)
    d_b = q_ring.shape[-1]
    d_a = ya_ring.shape[-1]
    per_seq = k_scr.shape[1] // ROWS
    (b_in, j_in), (b_at, j_at), _ = _stage_slabs(s, n_total, per_seq)
    par_in, par_at = lax.rem(b_in, 2), lax.rem(b_at, 2)
    n_heads = d_b // GROUP_DIM
    heads_per_tile = LANES // GROUP_DIM
    n_pairs = n_heads // heads_per_tile
    now2, prev2 = lax.rem(s, 2), lax.rem(s + 1, 2)
    now3, prev3 = lax.rem(s, 3), lax.rem(s + 1, 3)

    @pl.when(s == 0)
    def _():
        q_ring[...] = jnp.zeros_like(q_ring)
        gz_ring[...] = jnp.zeros_like(gz_ring)
        ya_ring[...] = jnp.zeros_like(ya_ring)
        yb_ring[...] = jnp.zeros_like(yb_ring)
        k_scr[0, 0:ROWS, :] = jnp.zeros((ROWS, d_b), BF16)
        for n in range(Q_PER_STEP):
            vT_scr[0, n] = jnp.zeros((d_b, TK), BF16)

    new = {}

    def item_h():
        shift = mod_in_ref[0, 0:1, :]
        scale = mod_in_ref[0, 1:2, :]
        new["h"] = (x_in_ref[0] * (1.0 + scale) + shift).astype(BF16)

    def h():
        return new["h"]

    def ycat():
        return jnp.concatenate([ya_ring[prev3], yb_ring[prev2]], axis=1)

    def both(*fs):
        return lambda: [f() for f in fs]

    def fin_out():
        gate = mod_out_ref[0, 2:3, :]
        r = alpha * x_out_ref[0] + gate * y_scr[...]
        o_ref[0] = _layer_norm(r, g_ref[...], b_ref[...])

    def fin_u(p):
        new["ug"] = _gelu_tanh(p())

    def fin_v(p):
        new["vn"] = _layer_norm(_gelu_tanh(p()), ag_ref[...], ab_ref[...]).astype(BF16)
        t_idx = lax.broadcasted_iota(jnp.int32, (CHUNK, CHUNK), 0)
        s_idx = lax.broadcasted_iota(jnp.int32, (CHUNK, CHUNK), 1)
        causal = t_idx >= s_idx
        w_sp = [jnp.where(causal, wsp_ref[g], 0.0).astype(BF16) for g in range(N_GROUPS)]
        first_group = lax.broadcasted_iota(jnp.int32, (CHUNK, LANES), 1) < GROUP_DIM
        chunks = []
        for c in range(ROWS // CHUNK):
            rows = slice(c * CHUNK, (c + 1) * CHUNK)
            pieces = []
            for pr in range(d_a // LANES):
                cols = slice(pr * LANES, (pr + 1) * LANES)
                vp = new["vn"][rows, cols]
                mixed = jnp.where(first_group, _dot(w_sp[2 * pr], vp), _dot(w_sp[2 * pr + 1], vp))
                pieces.append(new["ug"][rows, cols] * (mixed + bsp_ref[:, cols]))
            chunks.append(jnp.concatenate(pieces, axis=1))
        new["ya_pre"] = jnp.concatenate(chunks, axis=0)

    def fin_za(p):
        ya_ring[now3] = (_silu(p()) * new["ya_pre"]).astype(BF16)

    scale_q = math.log2(math.e) / math.sqrt(GROUP_DIM)

    def fin_q(p):
        q_ring[now2] = (p() * scale_q).astype(BF16)

    def fin_k(p):
        k_scr[par_in, pl.ds(pl.multiple_of(j_in * ROWS, ROWS), ROWS), :] = p().astype(BF16)

    def fin_vt(p):
        v = p()
        for n in range(Q_PER_STEP):
            vT_scr[par_in, j_in * Q_PER_STEP + n] = v[n * TK:(n + 1) * TK, :].T.astype(BF16)

    def fin_gz(p):
        gz_ring[now2] = _silu(p())

    dots, due = [], {}

    def out_slab(j):
        cols = slice(j * MXU_SLAB, (j + 1) * MXU_SLAB)

        def item():
            y_scr[:, cols] = _dot(ycat(), wo_ref[:, cols])
        return item

    for j in range(wo_ref.shape[1] // MXU_SLAB):
        dots.append(out_slab(j))
    due.setdefault(len(dots) - 1 + FIN_DELAY, []).append(fin_out)

    o = 3 * d_a
    projections = [(0, fin_u), (d_a, fin_v), (2 * d_a, fin_za), (o, fin_q), (o + d_b, fin_k),
                   (o + 2 * d_b, fin_vt), (o + 3 * d_b, fin_gz)]
    assert d_a == d_b == MXU_SLAB

    def in_slab(n, lo):
        def item():
            land_scr[n % LANDING_SLOTS] = _dot(h(), w_ref[:, lo:lo + MXU_SLAB])
        return item

    def landed(n):
        return lambda: land_scr[n % LANDING_SLOTS]

    for n, (lo, fin) in enumerate(projections):
        dots.append(in_slab(n, lo))
        due.setdefault(len(dots) - 1 + FIN_DELAY, []).append(functools.partial(fin, landed(n)))
    items = [item_h]
    for n, dot_item in enumerate(dots):
        items.append(both(dot_item, *due.pop(n, [])))
    for n in sorted(due):
        items.append(both(*due[n]))

    lane = lax.broadcasted_iota(jnp.int32, (TQ, LANES), 1)
    half = TK // 2
    s_idx = lax.broadcasted_iota(jnp.int32, (half, half), 0)
    t_idx = lax.broadcasted_iota(jnp.int32, (half, half), 1)
    strictly_earlier_half = s_idx < t_idx
    j_idx = lax.broadcasted_iota(jnp.int32, (TK, TK), 1)
    r_idx = lax.broadcasted_iota(jnp.int32, (TK, TK), 0)
    strictly_later = jnp.where(j_idx > r_idx, 1.0, 0.0).astype(BF16)

    q_masked = {}
    for qi in range(Q_PER_STEP):
        for hd in range(n_heads):
            p, hh = divmod(hd, heads_per_tile)
            q_pair = q_ring[prev2, qi * TQ:(qi + 1) * TQ, p * LANES:(p + 1) * LANES]
            in_head = (lane >= hh * GROUP_DIM) & (lane < (hh + 1) * GROUP_DIM)
            q_masked[qi, hd] = jnp.where(in_head, q_pair, jnp.zeros_like(q_pair))

    lo_half, hi_half = slice(0, half), slice(half, TK)
    diag_parts = [(lo_half, lo_half, "tri"), (lo_half, hi_half, "full"),
                  (hi_half, lo_half, "empty"), (hi_half, hi_half, "tri")]
    full_parts = [(slice(0, TK), slice(0, TQ), "full")]

    def assemble(parts, pieces):
        if len(parts) == 1:
            return pieces[0]
        return jnp.concatenate([jnp.concatenate(pieces[0:2], axis=1),
                                jnp.concatenate(pieces[2:4], axis=1)], axis=0)

    def gated_pair(accs, qi, p):
        rows, cols = slice(qi * TQ, (qi + 1) * TQ), slice(p * LANES, (p + 1) * LANES)
        yT = jnp.concatenate([accs[qi, heads_per_tile * p + n] for n in range(heads_per_tile)], axis=0)
        yb_ring[now2, rows, cols] = (yT.T * gz_ring[prev2, rows, cols]).astype(BF16)

    def key_blocks(blocks, runs, accs, extra=(), finish=False):
        runs, accs = dict(runs), dict(accs)
        tiles = [dict(qi=qi, kb=kb, h=hd, keep=keep, parts=diag_parts if diag else full_parts)
                 for qi, kb, diag, keep in blocks for hd in range(n_heads)]
        last_tile = {(t["qi"], t["h"]): n for n, t in enumerate(tiles)}

        def scores(t):
            start = pl.multiple_of(t["kb"] * TK, TK)
            p = t["h"] // heads_per_tile
            lanes = slice(p * LANES, (p + 1) * LANES)
            t["z"] = _dot_nt(k_scr[par_at, pl.ds(start, TK), lanes], q_masked[t["qi"], t["h"]])

        def softplus(t):
            sps, t["logb"] = [], []
            for rows, cols, kind in t["parts"]:
                if kind == "empty":
                    sps.append(jnp.zeros((half, half), BF16))
                    t["logb"].append(None)
                    continue
                z = t["z"][rows, cols]
                sp = jnp.maximum(z, 0.0) + jnp.log2(1.0 + jnp.exp2(-jnp.abs(z)))
                t["logb"].append(z - sp)
                if kind == "tri":
                    sp = jnp.where(strictly_earlier_half, sp, 0.0)
                sps.append(sp.astype(BF16))
            t["sp"] = assemble(t["parts"], sps)

        def later_sum(t):
            t["csum"] = _dot(strictly_later, t["sp"])

        def weights(t):
            key = (t["qi"], t["h"])
            run = runs[key]
            probs = []
            for (rows, cols, kind), logb in zip(t["parts"], t["logb"]):
                if kind == "empty":
                    probs.append(jnp.zeros((half, half), BF16))
                    continue
                a = jnp.exp2(logb - (run[:, cols] + t["csum"][rows, cols]))
                if kind == "tri":
                    a = jnp.where(strictly_earlier_half, a, 0.0)
                probs.append(a.astype(BF16))
            t["prob"] = assemble(t["parts"], probs)
            runs[key] = run + (t["csum"][0:1, :] + t["sp"][0:1, :].astype(F32))

        def values(n, t):
            key = (t["qi"], t["h"])
            hd = t["h"]
            v_blk = vT_scr[par_at, t["kb"], hd * GROUP_DIM:(hd + 1) * GROUP_DIM, :]
            if t["keep"] is not None:
                v_blk = (v_blk.astype(F32) * t["keep"]).astype(BF16)
            accs[key] = accs[key] + _dot(v_blk, t["prob"])
            if finish and last_tile[key] == n and hd % heads_per_tile == heads_per_tile - 1:
                gated_pair(accs, t["qi"], hd // heads_per_tile)

        n_tiles = len(tiles)
        lead, lag = SKEW
        steps = range(-lead, n_tiles + lag)
        extra = list(extra)
        every = max(1, (len(steps) - EXTRA_START) // max(1, len(extra)))
        for count, step in enumerate(steps):
            if extra and count >= EXTRA_START and (count - EXTRA_START) % every == 0:
                extra.pop(0)()
            if 0 <= step + lead < n_tiles:
                scores(tiles[step + lead])
            if 0 <= step < n_tiles:
                softplus(tiles[step])
                later_sum(tiles[step])
            if 0 <= step - lag < n_tiles:
                weights(tiles[step - lag])
                values(step - lag, tiles[step - lag])
        for item in extra:
            item()
        return runs, accs

    runs = {(qi, hd): jnp.zeros((1, TQ), F32) for qi in range(Q_PER_STEP) for hd in range(n_heads)}
    accs = {(qi, hd): jnp.zeros((GROUP_DIM, TQ), F32) for qi in range(Q_PER_STEP) for hd in range(n_heads)}
    first_kb = j_at * Q_PER_STEP
    diag_blocks = [(qi, first_kb + qi, True, None) for qi in range(Q_PER_STEP)]
    has_earlier = (j_at > 0).astype(F32)
    prev_blocks = [(qi, jnp.maximum(first_kb + qi - 1, 0), False, has_earlier if qi == 0 else None)
                   for qi in range(Q_PER_STEP)]
    runs, accs = key_blocks(diag_blocks + prev_blocks, runs, accs, items, finish=True)

    for qi in range(Q_PER_STEP):
        def sticks_left(runs):
            return jnp.min(functools.reduce(jnp.minimum, [runs[qi, hd] for hd in range(n_heads)])) < UNDERFLOW_LOG2

        def earlier_block(state, qi=qi, sticks_left=sticks_left):
            kb, _, runs, accs = state
            runs, accs = key_blocks([(qi, kb, False, None)], runs, accs)
            return kb - 1, sticks_left(runs), runs, accs

        first = first_kb + qi - 2
        mine = lambda d: {k: v for k, v in d.items() if k[0] == qi}

        @pl.when((first >= 0) & sticks_left(runs))
        def _(qi=qi, first=first, earlier_block=earlier_block, mine=mine):
            _, _, _, more = lax.while_loop(lambda st: (st[0] >= 0) & st[1], earlier_block,
                                           (first, True, mine(runs), mine(accs)))
            for p in range(n_pairs):
                gated_pair(more, qi, p)


def _fused_block(x, mod3, w, w_sp, b_sp_full, a_g, a_b, w_out, ln_g, ln_b, alpha):
    bsz, seq, d = x.shape
    d_a = a_g.shape[-1]
    d_b = (w.shape[1] - 3 * d_a) // 4
    per_seq = seq // ROWS
    n_total = bsz * per_seq
    grid = (n_total + 2,)

    def stage(lag):
        return lambda s: _stage_slabs(s, n_total, per_seq)[lag]

    def rows_of(lag):
        return pl.BlockSpec((1, ROWS, d), lambda s: (*stage(lag)(s), 0))

    def mod_of(lag):
        return pl.BlockSpec((1, 3, d), lambda s: (stage(lag)(s)[0], 0, 0))

    def whole(a):
        return pl.BlockSpec(a.shape, lambda s: (0,) * a.ndim)

    return pl.pallas_call(
        functools.partial(_block_kernel, alpha, n_total),
        grid=grid,
        in_specs=[rows_of(0), rows_of(2), mod_of(0), mod_of(2), whole(w), whole(w_sp),
                  whole(b_sp_full), whole(a_g), whole(a_b), whole(w_out), whole(ln_g), whole(ln_b)],
        out_specs=rows_of(2),
        out_shape=jax.ShapeDtypeStruct((bsz, seq, d), x.dtype),
        scratch_shapes=[
            pltpu.VMEM((2, seq, d_b), BF16),
            pltpu.VMEM((2, seq // TK, d_b, TK), BF16),
            pltpu.VMEM((2, ROWS, d_b), BF16),
            pltpu.VMEM((2, ROWS, d_b), F32),
            pltpu.VMEM((3, ROWS, d_a), BF16),
            pltpu.VMEM((2, ROWS, d_b), BF16),
            pltpu.VMEM((LANDING_SLOTS, ROWS, MXU_SLAB), F32),
            pltpu.VMEM((ROWS, d), F32),
        ],
        compiler_params=pltpu.CompilerParams(
            dimension_semantics=("arbitrary",), vmem_limit_bytes=VMEM_LIMIT),
        name="fused_block",
    )(x, x, mod3, mod3, w, w_sp, b_sp_full, a_g, a_b, w_out, ln_g, ln_b)


def kernel(x, c, w_ada, b_ada, w_in, sgu_ln_g, sgu_ln_b, w_spatial, b_spatial, w_out, ln_g, ln_b):
    depth = w_ada.shape[0]
    bsz, seq, d = x.shape
    d_a = sgu_ln_g.shape[-1]
    alpha = (2.0 * depth) ** 0.25
    for layer in range(depth):
        mod, w_in_bf, w_out_bf = _prepare(c, w_ada[layer], b_ada[layer], w_in[layer], w_out[layer])
        b_sp_full = jnp.repeat(b_spatial[layer].T, GROUP_DIM, axis=1)
        x = _fused_block(
            x, mod.reshape(bsz, 3, d), w_in_bf, w_spatial[layer], b_sp_full,
            sgu_ln_g[layer].reshape(1, d_a), sgu_ln_b[layer].reshape(1, d_a),
            w_out_bf, ln_g[layer].reshape(1, d), ln_b[layer].reshape(1, d), alpha)
    return x
```

```python
import functools
import math

import jax
import jax.numpy as jnp
from jax import lax
from jax.experimental import pallas as pl
from jax.experimental.pallas import tpu as pltpu

F32 = jnp.float32
BF16 = jnp.bfloat16

N_GROUPS = 8
GROUP_DIM = 64
CHUNK = 128
LN_EPS = 1e-5
LANES = 128

PREP_TILE = 512
TQ = 256
TK = 256
Q_PER_STEP = 1
ROWS = Q_PER_STEP * TQ
SKEW = (8, 4)
EXTRA_START = 1
MXU_SLAB = 512
LANDING_SLOTS = 7
FIN_DELAY = 1
UNDERFLOW_LOG2 = 152.0
VMEM_LIMIT = 48 * 1024 * 1024


def _dot(a, b):
    return jnp.dot(a, b, preferred_element_type=F32)


def _dot_nt(a, b):
    return lax.dot_general(a, b, (((1,), (1,)), ((), ())), preferred_element_type=F32)


def _gelu_tanh(x):
    c = math.sqrt(2.0 / math.pi)
    return 0.5 * x * (1.0 + jnp.tanh(c * (x + 0.044715 * (x * x * x))))


def _silu(x):
    return x / (1.0 + jnp.exp(-x))


def _layer_norm(x, g, b):
    mu = jnp.mean(x, axis=-1, keepdims=True)
    xc = x - mu
    var = jnp.mean(xc * xc, axis=-1, keepdims=True)
    return xc * lax.rsqrt(var + LN_EPS) * g + b


def _prep_kernel(c_ref, wa_ref, ba_ref, win_ref, wout_ref, mod_ref, win_bf_ref, wout_bf_ref):
    sc = _silu(c_ref[...]).astype(BF16)
    mod_ref[...] = _dot(sc, wa_ref[...].astype(BF16)) + ba_ref[...]
    win_bf_ref[...] = win_ref[...].astype(BF16)
    wout_bf_ref[...] = wout_ref[...].astype(BF16)


def _prepare(c, w_ada, b_ada, w_in, w_out):
    bsz, d = c.shape
    n_mod = w_ada.shape[1]
    steps = w_in.shape[1] // PREP_TILE
    mod_tiles = n_mod // PREP_TILE
    out_tile = PREP_TILE
    out_tiles = w_out.shape[1] // out_tile
    assert mod_tiles <= steps and out_tiles <= steps
    mod_col = lambda j: (0, jnp.minimum(j, mod_tiles - 1))
    out_col = lambda j: (0, jnp.minimum(j, out_tiles - 1))
    return pl.pallas_call(
        _prep_kernel,
        grid=(steps,),
        in_specs=[
            pl.BlockSpec((bsz, d), lambda j: (0, 0)),
            pl.BlockSpec((d, PREP_TILE), mod_col),
            pl.BlockSpec((1, PREP_TILE), mod_col),
            pl.BlockSpec((w_in.shape[0], PREP_TILE), lambda j: (0, j)),
            pl.BlockSpec((w_out.shape[0], out_tile), out_col),
        ],
        out_specs=[
            pl.BlockSpec((bsz, PREP_TILE), mod_col),
            pl.BlockSpec((w_in.shape[0], PREP_TILE), lambda j: (0, j)),
            pl.BlockSpec((w_out.shape[0], out_tile), out_col),
        ],
        out_shape=[
            jax.ShapeDtypeStruct((bsz, n_mod), F32),
            jax.ShapeDtypeStruct(w_in.shape, BF16),
            jax.ShapeDtypeStruct(w_out.shape, BF16),
        ],
        compiler_params=pltpu.CompilerParams(
            dimension_semantics=("arbitrary",), vmem_limit_bytes=VMEM_LIMIT),
        name="adaln_mod_and_casts",
    )(c, w_ada, b_ada.reshape(1, n_mod), w_in, w_out)


def _stage_slabs(s, n_total, per_seq):
    last = n_total - 1
    out = []
    for lag in range(3):
        t = jnp.clip(s - lag, 0, last)
        out.append((lax.div(t, per_seq), lax.rem(t, per_seq)))
    return out


def _block_kernel(alpha, n_total,
                  x_in_ref, x_out_ref, mod_in_ref, mod_out_ref, w_ref, wsp_ref, bsp_ref,
                  ag_ref, ab_ref, wo_ref, g_ref, b_ref, o_ref,
                  k_scr, vT_scr, q_ring, gz_ring, ya_ring, yb_ring, z_ring, land_scr, y_scr):
    s = pl.program_id(0)
    d_b = q_ring.shape[-1]
    d_a = ya_ring.shape[-1]
    per_seq = k_scr.shape[1] // ROWS
    (b_in, j_in), (b_at, j_at), _ = _stage_slabs(s, n_total, per_seq)
    par_in, par_at = lax.rem(b_in, 2), lax.rem(b_at, 2)
    n_heads = d_b // GROUP_DIM
    heads_per_tile = LANES // GROUP_DIM
    n_pairs = n_heads // heads_per_tile
    now2, prev2 = lax.rem(s, 2), lax.rem(s + 1, 2)
    now3, prev3 = lax.rem(s, 3), lax.rem(s + 1, 3)

    @pl.when(s == 0)
    def _():
        q_ring[...] = jnp.zeros_like(q_ring)
        gz_ring[...] = jnp.zeros_like(gz_ring)
        ya_ring[...] = jnp.zeros_like(ya_ring)
        yb_ring[...] = jnp.zeros_like(yb_ring)
        z_ring[...] = jnp.zeros_like(z_ring)
        k_scr[0, 0:ROWS, :] = jnp.zeros((ROWS, d_b), BF16)
        for n in range(Q_PER_STEP):
            vT_scr[0, n] = jnp.zeros((d_b, TK), BF16)

    new = {}

    def item_h():
        shift = mod_in_ref[0, 0:1, :]
        scale = mod_in_ref[0, 1:2, :]
        new["h"] = (x_in_ref[0] * (1.0 + scale) + shift).astype(BF16)

    def h():
        return new["h"]

    def ycat():
        return jnp.concatenate([ya_ring[prev3], yb_ring[prev2]], axis=1)

    def both(*fs):
        return lambda: [f() for f in fs]

    def fin_out():
        gate = mod_out_ref[0, 2:3, :]
        r = alpha * x_out_ref[0] + gate * y_scr[...]
        o_ref[0] = _layer_norm(r, g_ref[...], b_ref[...])

    def fin_u(p):
        new["ug"] = _gelu_tanh(p())

    def fin_v(p):
        new["vn"] = _layer_norm(_gelu_tanh(p()), ag_ref[...], ab_ref[...]).astype(BF16)
        t_idx = lax.broadcasted_iota(jnp.int32, (CHUNK, CHUNK), 0)
        s_idx = lax.broadcasted_iota(jnp.int32, (CHUNK, CHUNK), 1)
        causal = t_idx >= s_idx
        w_sp = [jnp.where(causal, wsp_ref[g], 0.0).astype(BF16) for g in range(N_GROUPS)]
        first_group = lax.broadcasted_iota(jnp.int32, (CHUNK, LANES), 1) < GROUP_DIM
        chunks = []
        for c in range(ROWS // CHUNK):
            rows = slice(c * CHUNK, (c + 1) * CHUNK)
            pieces = []
            for pr in range(d_a // LANES):
                cols = slice(pr * LANES, (pr + 1) * LANES)
                vp = new["vn"][rows, cols]
                mixed = jnp.where(first_group, _dot(w_sp[2 * pr], vp), _dot(w_sp[2 * pr + 1], vp))
                pieces.append(new["ug"][rows, cols] * (mixed + bsp_ref[:, cols]))
            chunks.append(jnp.concatenate(pieces, axis=1))
        new["ya_pre"] = jnp.concatenate(chunks, axis=0)

    def fin_za(p):
        ya_ring[now3] = (_silu(p()) * new["ya_pre"]).astype(BF16)

    scale_q = math.log2(math.e) / math.sqrt(GROUP_DIM)

    def fin_q(p):
        new["q"] = (p() * scale_q).astype(BF16)
        q_ring[now2] = new["q"]

    def fin_k(p):
        new["k"] = p().astype(BF16)
        k_scr[par_in, pl.ds(pl.multiple_of(j_in * ROWS, ROWS), ROWS), :] = new["k"]

    def head_of(q_pair, hh):
        in_head = (lane >= hh * GROUP_DIM) & (lane < (hh + 1) * GROUP_DIM)
        return jnp.where(in_head, q_pair, jnp.zeros_like(q_pair))

    def pre_scores():
        for qi in range(Q_PER_STEP):
            rows = slice(qi * TQ, (qi + 1) * TQ)
            for hd in range(n_heads):
                p, hh = divmod(hd, heads_per_tile)
                lanes = slice(p * LANES, (p + 1) * LANES)
                z_ring[now2, qi * n_heads + hd] = _dot_nt(
                    new["k"][rows, lanes], head_of(new["q"][rows, lanes], hh))

    def fin_vt(p):
        v = p()
        for n in range(Q_PER_STEP):
            vT_scr[par_in, j_in * Q_PER_STEP + n] = v[n * TK:(n + 1) * TK, :].T.astype(BF16)

    def fin_gz(p):
        gz_ring[now2] = _silu(p())

    dots, due = [], {}

    def out_slab(j):
        cols = slice(j * MXU_SLAB, (j + 1) * MXU_SLAB)

        def item():
            y_scr[:, cols] = _dot(ycat(), wo_ref[:, cols])
        return item

    for j in range(wo_ref.shape[1] // MXU_SLAB):
        dots.append(out_slab(j))
    due.setdefault(len(dots) - 1 + FIN_DELAY, []).append(fin_out)

    o = 3 * d_a
    projections = [(0, fin_u), (d_a, fin_v), (2 * d_a, fin_za), (o, fin_q), (o + d_b, fin_k),
                   (o + 2 * d_b, fin_vt), (o + 3 * d_b, fin_gz)]
    assert d_a == d_b == MXU_SLAB

    def in_slab(n, lo):
        def item():
            land_scr[n % LANDING_SLOTS] = _dot(h(), w_ref[:, lo:lo + MXU_SLAB])
        return item

    def landed(n):
        return lambda: land_scr[n % LANDING_SLOTS]

    for n, (lo, fin) in enumerate(projections):
        dots.append(in_slab(n, lo))
        due.setdefault(len(dots) - 1 + FIN_DELAY, []).append(functools.partial(fin, landed(n)))
    items = [item_h]
    for n, dot_item in enumerate(dots):
        items.append(both(dot_item, *due.pop(n, [])))
    items.append(both(*[f for n in sorted(due) for f in due[n]], pre_scores))

    lane = lax.broadcasted_iota(jnp.int32, (TQ, LANES), 1)
    half = TK // 2
    s_idx = lax.broadcasted_iota(jnp.int32, (half, half), 0)
    t_idx = lax.broadcasted_iota(jnp.int32, (half, half), 1)
    strictly_earlier_half = s_idx < t_idx
    j_idx = lax.broadcasted_iota(jnp.int32, (TK, TK), 1)
    r_idx = lax.broadcasted_iota(jnp.int32, (TK, TK), 0)
    strictly_later = jnp.where(j_idx > r_idx, 1.0, 0.0).astype(BF16)

    q_masked = {}
    for qi in range(Q_PER_STEP):
        for hd in range(n_heads):
            p, hh = divmod(hd, heads_per_tile)
            q_pair = q_ring[prev2, qi * TQ:(qi + 1) * TQ, p * LANES:(p + 1) * LANES]
            q_masked[qi, hd] = head_of(q_pair, hh)

    lo_half, hi_half = slice(0, half), slice(half, TK)
    diag_parts = [(lo_half, lo_half, "tri"), (lo_half, hi_half, "full"),
                  (hi_half, lo_half, "empty"), (hi_half, hi_half, "tri")]
    full_parts = [(slice(0, TK), slice(0, TQ), "full")]

    def assemble(parts, pieces):
        if len(parts) == 1:
            return pieces[0]
        return jnp.concatenate([jnp.concatenate(pieces[0:2], axis=1),
                                jnp.concatenate(pieces[2:4], axis=1)], axis=0)

    def gated_pair(accs, qi, p):
        rows, cols = slice(qi * TQ, (qi + 1) * TQ), slice(p * LANES, (p + 1) * LANES)
        yT = jnp.concatenate([accs[qi, heads_per_tile * p + n] for n in range(heads_per_tile)], axis=0)
        yb_ring[now2, rows, cols] = (yT.T * gz_ring[prev2, rows, cols]).astype(BF16)

    def key_blocks(blocks, runs, accs, extra=(), finish=False):
        runs, accs = dict(runs), dict(accs)
        tiles = [dict(qi=qi, kb=kb, h=hd, keep=keep, diag=diag, parts=diag_parts if diag else full_parts)
                 for qi, kb, diag, keep in blocks for hd in range(n_heads)]
        last_tile = {(t["qi"], t["h"]): n for n, t in enumerate(tiles)}

        def scores(t):
            if t["diag"]:
                t["z"] = z_ring[prev2, t["qi"] * n_heads + t["h"]]
                return
            start = pl.multiple_of(t["kb"] * TK, TK)
            p = t["h"] // heads_per_tile
            lanes = slice(p * LANES, (p + 1) * LANES)
            t["z"] = _dot_nt(k_scr[par_at, pl.ds(start, TK), lanes], q_masked[t["qi"], t["h"]])

        def softplus(t):
            sps, t["logb"] = [], []
            for rows, cols, kind in t["parts"]:
                if kind == "empty":
                    sps.append(jnp.zeros((half, half), BF16))
                    t["logb"].append(None)
                    continue
                z = t["z"][rows, cols]
                sp = jnp.maximum(z, 0.0) + jnp.log2(1.0 + jnp.exp2(-jnp.abs(z)))
                t["logb"].append(z - sp)
                if kind == "tri":
                    sp = jnp.where(strictly_earlier_half, sp, 0.0)
                sps.append(sp.astype(BF16))
            t["sp"] = assemble(t["parts"], sps)

        def later_sum(t):
            t["csum"] = _dot(strictly_later, t["sp"])

        def weights(t):
            key = (t["qi"], t["h"])
            run = runs[key]
            probs = []
            for (rows, cols, kind), logb in zip(t["parts"], t["logb"]):
                if kind == "empty":
                    probs.append(jnp.zeros((half, half), BF16))
                    continue
                a = jnp.exp2(logb - (run[:, cols] + t["csum"][rows, cols]))
                if kind == "tri":
                    a = jnp.where(strictly_earlier_half, a, 0.0)
                probs.append(a.astype(BF16))
            t["prob"] = assemble(t["parts"], probs)
            runs[key] = run + (t["csum"][0:1, :] + t["sp"][0:1, :].astype(F32))

        def values(n, t):
            key = (t["qi"], t["h"])
            hd = t["h"]
            v_blk = vT_scr[par_at, t["kb"], hd * GROUP_DIM:(hd + 1) * GROUP_DIM, :]
            if t["keep"] is not None:
                v_blk = (v_blk.astype(F32) * t["keep"]).astype(BF16)
            accs[key] = accs[key] + _dot(v_blk, t["prob"])
            if finish and last_tile[key] == n and hd % heads_per_tile == heads_per_tile - 1:
                gated_pair(accs, t["qi"], hd // heads_per_tile)

        n_tiles = len(tiles)
        lead, lag = SKEW
        ready = next((n for n, t in enumerate(tiles) if not t["diag"]), n_tiles)
        for t in tiles[:ready]:
            scores(t)
        steps = range(min(0, ready - lead), n_tiles + lag)
        extra = list(extra)
        slots = [EXTRA_START + (n * (len(steps) - EXTRA_START)) // max(1, len(extra))
                 for n in range(len(extra))]
        for count, step in enumerate(steps):
            while extra and slots[0] <= count:
                slots.pop(0)
                extra.pop(0)()
            if ready <= step + lead < n_tiles:
                scores(tiles[step + lead])
            if 0 <= step < n_tiles:
                softplus(tiles[step])
                later_sum(tiles[step])
            if 0 <= step - lag < n_tiles:
                weights(tiles[step - lag])
                values(step - lag, tiles[step - lag])
        for item in extra:
            item()
        return runs, accs

    runs = {(qi, hd): jnp.zeros((1, TQ), F32) for qi in range(Q_PER_STEP) for hd in range(n_heads)}
    accs = {(qi, hd): jnp.zeros((GROUP_DIM, TQ), F32) for qi in range(Q_PER_STEP) for hd in range(n_heads)}
    first_kb = j_at * Q_PER_STEP
    diag_blocks = [(qi, first_kb + qi, True, None) for qi in range(Q_PER_STEP)]
    has_earlier = (j_at > 0).astype(F32)
    prev_blocks = [(qi, jnp.maximum(first_kb + qi - 1, 0), False, has_earlier if qi == 0 else None)
                   for qi in range(Q_PER_STEP)]
    runs, accs = key_blocks(diag_blocks + prev_blocks, runs, accs, items, finish=True)

    for qi in range(Q_PER_STEP):
        def sticks_left(runs):
            return jnp.min(functools.reduce(jnp.minimum, [runs[qi, hd] for hd in range(n_heads)])) < UNDERFLOW_LOG2

        def earlier_block(state, qi=qi, sticks_left=sticks_left):
            kb, _, runs, accs = state
            runs, accs = key_blocks([(qi, kb, False, None)], runs, accs)
            return kb - 1, sticks_left(runs), runs, accs

        first = first_kb + qi - 2
        mine = lambda d: {k: v for k, v in d.items() if k[0] == qi}

        @pl.when((first >= 0) & sticks_left(runs))
        def _(qi=qi, first=first, earlier_block=earlier_block, mine=mine):
            _, _, _, more = lax.while_loop(lambda st: (st[0] >= 0) & st[1], earlier_block,
                                           (first, True, mine(runs), mine(accs)))
            for p in range(n_pairs):
                gated_pair(more, qi, p)


def _fused_block(x, mod3, w, w_sp, b_sp_full, a_g, a_b, w_out, ln_g, ln_b, alpha):
    bsz, seq, d = x.shape
    d_a = a_g.shape[-1]
    d_b = (w.shape[1] - 3 * d_a) // 4
    per_seq = seq // ROWS
    n_total = bsz * per_seq
    grid = (n_total + 2,)

    def stage(lag):
        return lambda s: _stage_slabs(s, n_total, per_seq)[lag]

    def rows_of(lag):
        return pl.BlockSpec((1, ROWS, d), lambda s: (*stage(lag)(s), 0))

    def mod_of(lag):
        return pl.BlockSpec((1, 3, d), lambda s: (stage(lag)(s)[0], 0, 0))

    def whole(a):
        return pl.BlockSpec(a.shape, lambda s: (0,) * a.ndim)

    return pl.pallas_call(
        functools.partial(_block_kernel, alpha, n_total),
        grid=grid,
        in_specs=[rows_of(0), rows_of(2), mod_of(0), mod_of(2), whole(w), whole(w_sp),
                  whole(b_sp_full), whole(a_g), whole(a_b), whole(w_out), whole(ln_g), whole(ln_b)],
        out_specs=rows_of(2),
        out_shape=jax.ShapeDtypeStruct((bsz, seq, d), x.dtype),
        scratch_shapes=[
            pltpu.VMEM((2, seq, d_b), BF16),
            pltpu.VMEM((2, seq // TK, d_b, TK), BF16),
            pltpu.VMEM((2, ROWS, d_b), BF16),
            pltpu.VMEM((2, ROWS, d_b), F32),
            pltpu.VMEM((3, ROWS, d_a), BF16),
            pltpu.VMEM((2, ROWS, d_b), BF16),
            pltpu.VMEM((2, Q_PER_STEP * (d_b // GROUP_DIM), TK, TQ), F32),
            pltpu.VMEM((LANDING_SLOTS, ROWS, MXU_SLAB), F32),
            pltpu.VMEM((ROWS, d), F32),
        ],
        compiler_params=pltpu.CompilerParams(
            dimension_semantics=("arbitrary",), vmem_limit_bytes=VMEM_LIMIT),
        name="fused_block",
    )(x, x, mod3, mod3, w, w_sp, b_sp_full, a_g, a_b, w_out, ln_g, ln_b)


def kernel(x, c, w_ada, b_ada, w_in, sgu_ln_g, sgu_ln_b, w_spatial, b_spatial, w_out, ln_g, ln_b):
    depth = w_ada.shape[0]
    bsz, seq, d = x.shape
    d_a = sgu_ln_g.shape[-1]
    alpha = (2.0 * depth) ** 0.25
    for layer in range(depth):
        mod, w_in_bf, w_out_bf = _prepare(c, w_ada[layer], b_ada[layer], w_in[layer], w_out[layer])
        b_sp_full = jnp.repeat(b_spatial[layer].T, GROUP_DIM, axis=1)
        x = _fused_block(
            x, mod.reshape(bsz, 3, d), w_in_bf, w_spatial[layer], b_sp_full,
            sgu_ln_g[layer].reshape(1, d_a), sgu_ln_b[layer].reshape(1, d_a),
            w_out_bf, ln_g[layer].reshape(1, d), ln_b[layer].reshape(1, d), alpha)
    return x
```

```python
import functools
import math

import jax
import jax.numpy as jnp
from jax import lax
from jax.experimental import pallas as pl
from jax.experimental.pallas import tpu as pltpu

F32 = jnp.float32
BF16 = jnp.bfloat16

N_GROUPS = 8
GROUP_DIM = 64
CHUNK = 128
LN_EPS = 1e-5
LANES = 128

PREP_TILE = 512
TQ = 256
TK = 256
Q_PER_STEP = 1
ROWS = Q_PER_STEP * TQ
SKEW = (8, 4)
EXTRA_START = 1
MXU_SLAB = 512
FIN_DELAY = 1
UNDERFLOW_LOG2 = 152.0
VMEM_LIMIT = 48 * 1024 * 1024


def _dot(a, b):
    return jnp.dot(a, b, preferred_element_type=F32)


def _dot_nt(a, b):
    return lax.dot_general(a, b, (((1,), (1,)), ((), ())), preferred_element_type=F32)


def _gelu_tanh(x):
    c = math.sqrt(2.0 / math.pi)
    half_x = 0.5 * x
    return half_x + half_x * jnp.tanh(x * (c + (c * 0.044715) * (x * x)))


def _silu(x):
    half_x = 0.5 * x
    return half_x + half_x * jnp.tanh(half_x)


def _layer_norm(x, g, b):
    mu = jnp.mean(x, axis=-1, keepdims=True)
    xc = x - mu
    var = jnp.mean(xc * xc, axis=-1, keepdims=True)
    return xc * lax.rsqrt(var + LN_EPS) * g + b


def _prep_kernel(c_ref, wa_ref, ba_ref, win_ref, wout_ref, mod_ref, win_bf_ref, wout_bf_ref):
    sc = _silu(c_ref[...]).astype(BF16)
    mod_ref[...] = _dot(sc, wa_ref[...].astype(BF16)) + ba_ref[...]
    win_bf_ref[...] = win_ref[...].astype(BF16)
    wout_bf_ref[...] = wout_ref[...].astype(BF16)


def _prepare(c, w_ada, b_ada, w_in, w_out):
    bsz, d = c.shape
    n_mod = w_ada.shape[1]
    steps = w_in.shape[1] // PREP_TILE
    mod_tiles = n_mod // PREP_TILE
    out_tile = PREP_TILE
    out_tiles = w_out.shape[1] // out_tile
    assert mod_tiles <= steps and out_tiles <= steps
    mod_col = lambda j: (0, jnp.minimum(j, mod_tiles - 1))
    out_col = lambda j: (0, jnp.minimum(j, out_tiles - 1))
    return pl.pallas_call(
        _prep_kernel,
        grid=(steps,),
        in_specs=[
            pl.BlockSpec((bsz, d), lambda j: (0, 0)),
            pl.BlockSpec((d, PREP_TILE), mod_col),
            pl.BlockSpec((1, PREP_TILE), mod_col),
            pl.BlockSpec((w_in.shape[0], PREP_TILE), lambda j: (0, j)),
            pl.BlockSpec((w_out.shape[0], out_tile), out_col),
        ],
        out_specs=[
            pl.BlockSpec((bsz, PREP_TILE), mod_col),
            pl.BlockSpec((w_in.shape[0], PREP_TILE), lambda j: (0, j)),
            pl.BlockSpec((w_out.shape[0], out_tile), out_col),
        ],
        out_shape=[
            jax.ShapeDtypeStruct((bsz, n_mod), F32),
            jax.ShapeDtypeStruct(w_in.shape, BF16),
            jax.ShapeDtypeStruct(w_out.shape, BF16),
        ],
        compiler_params=pltpu.CompilerParams(
            dimension_semantics=("arbitrary",), vmem_limit_bytes=VMEM_LIMIT),
        name="adaln_mod_and_casts",
    )(c, w_ada, b_ada.reshape(1, n_mod), w_in, w_out)


def _stage_slabs(s, n_total, per_seq):
    last = n_total - 1
    out = []
    for lag in range(3):
        t = jnp.clip(s - lag, 0, last)
        out.append((lax.div(t, per_seq), lax.rem(t, per_seq)))
    return out


def _block_kernel(alpha, n_total,
                  x_in_ref, x_out_ref, mod_in_ref, mod_out_ref, w_ref, wsp_ref, bsp_ref,
                  ag_ref, ab_ref, wo_ref, g_ref, b_ref, o_ref,
                  k_scr, vT_scr, q_ring, gz_ring, ya_ring, yb_ring, z_ring, land_scr, y_scr):
    s = pl.program_id(0)
    d_b = q_ring.shape[-1]
    d_a = ya_ring.shape[-1]
    per_seq = k_scr.shape[1] // ROWS
    (b_in, j_in), (b_at, j_at), _ = _stage_slabs(s, n_total, per_seq)
    par_in, par_at = lax.rem(b_in, 2), lax.rem(b_at, 2)
    n_heads = d_b // GROUP_DIM
    heads_per_tile = LANES // GROUP_DIM
    n_pairs = n_heads // heads_per_tile
    now2, prev2 = lax.rem(s, 2), lax.rem(s + 1, 2)
    now3, prev3 = lax.rem(s, 3), lax.rem(s + 1, 3)

    @pl.when(s == 0)
    def _():
        q_ring[...] = jnp.zeros_like(q_ring)
        gz_ring[...] = jnp.zeros_like(gz_ring)
        ya_ring[...] = jnp.zeros_like(ya_ring)
        yb_ring[...] = jnp.zeros_like(yb_ring)
        z_ring[...] = jnp.zeros_like(z_ring)
        k_scr[0, 0:ROWS, :] = jnp.zeros((ROWS, d_b), BF16)
        for n in range(Q_PER_STEP):
            vT_scr[0, n] = jnp.zeros((d_b, TK), BF16)

    new = {}

    def item_h():
        shift = mod_in_ref[0, 0:1, :]
        scale = mod_in_ref[0, 1:2, :]
        new["h"] = (x_in_ref[0] * (1.0 + scale) + shift).astype(BF16)

    def h():
        return new["h"]

    def ycat():
        return jnp.concatenate([ya_ring[prev3], yb_ring[prev2]], axis=1)

    def both(*fs):
        return lambda: [f() for f in fs]

    def fin_out():
        gate = mod_out_ref[0, 2:3, :]
        r = alpha * x_out_ref[0] + gate * y_scr[...]
        o_ref[0] = _layer_norm(r, g_ref[...], b_ref[...])

    def fin_u(p):
        new["ug"] = _gelu_tanh(p())

    def fin_v(p):
        new["vn"] = _layer_norm(_gelu_tanh(p()), ag_ref[...], ab_ref[...]).astype(BF16)
        t_idx = lax.broadcasted_iota(jnp.int32, (CHUNK, CHUNK), 0)
        s_idx = lax.broadcasted_iota(jnp.int32, (CHUNK, CHUNK), 1)
        causal = t_idx >= s_idx
        w_sp = [jnp.where(causal, wsp_ref[g], 0.0).astype(BF16) for g in range(N_GROUPS)]
        first_group = lax.broadcasted_iota(jnp.int32, (CHUNK, LANES), 1) < GROUP_DIM
        chunks = []
        for c in range(ROWS // CHUNK):
            rows = slice(c * CHUNK, (c + 1) * CHUNK)
            pieces = []
            for pr in range(d_a // LANES):
                cols = slice(pr * LANES, (pr + 1) * LANES)
                vp = new["vn"][rows, cols]
                mixed = jnp.where(first_group, _dot(w_sp[2 * pr], vp), _dot(w_sp[2 * pr + 1], vp))
                pieces.append(new["ug"][rows, cols] * (mixed + bsp_ref[:, cols]))
            chunks.append(jnp.concatenate(pieces, axis=1))
        new["ya_pre"] = jnp.concatenate(chunks, axis=0)

    def fin_za(p):
        ya_ring[now3] = (_silu(p()) * new["ya_pre"]).astype(BF16)

    scale_q = math.log2(math.e) / math.sqrt(GROUP_DIM)

    def fin_q(p):
        new["q"] = (p() * scale_q).astype(BF16)
        q_ring[now2] = new["q"]

    def fin_k(p):
        new["k"] = p().astype(BF16)
        k_scr[par_in, pl.ds(pl.multiple_of(j_in * ROWS, ROWS), ROWS), :] = new["k"]

    def head_of(q_pair, hh):
        in_head = (lane >= hh * GROUP_DIM) & (lane < (hh + 1) * GROUP_DIM)
        return jnp.where(in_head, q_pair, jnp.zeros_like(q_pair))

    def pre_scores():
        for qi in range(Q_PER_STEP):
            rows = slice(qi * TQ, (qi + 1) * TQ)
            for hd in range(n_heads):
                p, hh = divmod(hd, heads_per_tile)
                lanes = slice(p * LANES, (p + 1) * LANES)
                z_ring[now2, qi * n_heads + hd] = _dot_nt(
                    new["k"][rows, lanes], head_of(new["q"][rows, lanes], hh))

    def fin_vt(p):
        v = p()
        for n in range(Q_PER_STEP):
            vT_scr[par_in, j_in * Q_PER_STEP + n] = v[n * TK:(n + 1) * TK, :].T.astype(BF16)

    def fin_gz(p):
        gz_ring[now2] = _silu(p())

    dots, due = [], {}

    def out_slab(j):
        cols = slice(j * MXU_SLAB, (j + 1) * MXU_SLAB)

        def item():
            y_scr[:, cols] = _dot(ycat(), wo_ref[:, cols])
        return item

    for j in range(wo_ref.shape[1] // MXU_SLAB):
        dots.append(out_slab(j))
    due.setdefault(len(dots) - 1 + FIN_DELAY, []).append(fin_out)

    o = 3 * d_a
    projections = [(0, fin_u), (d_a, fin_v), (2 * d_a, fin_za), (o, fin_q), (o + d_b, fin_k),
                   (o + 2 * d_b, fin_vt), (o + 3 * d_b, fin_gz)]
    assert d_a == d_b == MXU_SLAB

    assert land_scr.shape[0] == len(projections)

    def in_slab(n, lo):
        def item():
            land_scr[n] = _dot(h(), w_ref[:, lo:lo + MXU_SLAB])
        return item

    def landed(n):
        return lambda: land_scr[n]

    for n, (lo, fin) in enumerate(projections):
        dots.append(in_slab(n, lo))
        due.setdefault(len(dots) - 1 + FIN_DELAY, []).append(functools.partial(fin, landed(n)))
    items = [item_h]
    for n, dot_item in enumerate(dots):
        items.append(both(dot_item, *due.pop(n, [])))
    items.append(both(*[f for n in sorted(due) for f in due[n]], pre_scores))

    lane = lax.broadcasted_iota(jnp.int32, (TQ, LANES), 1)
    half = TK // 2
    s_idx = lax.broadcasted_iota(jnp.int32, (half, half), 0)
    t_idx = lax.broadcasted_iota(jnp.int32, (half, half), 1)
    strictly_earlier_half = s_idx < t_idx
    j_idx = lax.broadcasted_iota(jnp.int32, (TK, TK), 1)
    r_idx = lax.broadcasted_iota(jnp.int32, (TK, TK), 0)
    strictly_later = jnp.where(j_idx > r_idx, 1.0, 0.0).astype(BF16)

    q_masked = {}
    for qi in range(Q_PER_STEP):
        for hd in range(n_heads):
            p, hh = divmod(hd, heads_per_tile)
            q_pair = q_ring[prev2, qi * TQ:(qi + 1) * TQ, p * LANES:(p + 1) * LANES]
            q_masked[qi, hd] = head_of(q_pair, hh)

    lo_half, hi_half = slice(0, half), slice(half, TK)
    diag_parts = [(lo_half, lo_half, "tri"), (lo_half, hi_half, "full"),
                  (hi_half, lo_half, "empty"), (hi_half, hi_half, "tri")]
    full_parts = [(slice(0, TK), slice(0, TQ), "full")]

    def assemble(parts, pieces):
        if len(parts) == 1:
            return pieces[0]
        return jnp.concatenate([jnp.concatenate(pieces[0:2], axis=1),
                                jnp.concatenate(pieces[2:4], axis=1)], axis=0)

    def gated_pair(accs, qi, p):
        rows, cols = slice(qi * TQ, (qi + 1) * TQ), slice(p * LANES, (p + 1) * LANES)
        yT = jnp.concatenate([accs[qi, heads_per_tile * p + n] for n in range(heads_per_tile)], axis=0)
        yb_ring[now2, rows, cols] = (yT.T * gz_ring[prev2, rows, cols]).astype(BF16)

    def key_blocks(blocks, runs, accs, extra=(), finish=False):
        runs, accs = dict(runs), dict(accs)
        tiles = [dict(qi=qi, kb=kb, h=hd, keep=keep, diag=diag, parts=diag_parts if diag else full_parts)
                 for qi, kb, diag, keep in blocks for hd in range(n_heads)]
        last_tile = {(t["qi"], t["h"]): n for n, t in enumerate(tiles)}

        def scores(t):
            if t["diag"]:
                t["z"] = z_ring[prev2, t["qi"] * n_heads + t["h"]]
                return
            start = pl.multiple_of(t["kb"] * TK, TK)
            p = t["h"] // heads_per_tile
            lanes = slice(p * LANES, (p + 1) * LANES)
            t["z"] = _dot_nt(k_scr[par_at, pl.ds(start, TK), lanes], q_masked[t["qi"], t["h"]])

        def softplus(t):
            sps, t["logb"] = [], []
            for rows, cols, kind in t["parts"]:
                if kind == "empty":
                    sps.append(jnp.zeros((half, half), BF16))
                    t["logb"].append(None)
                    continue
                z = t["z"][rows, cols]
                sp = jnp.maximum(z, 0.0) + jnp.log2(1.0 + jnp.exp2(-jnp.abs(z)))
                t["logb"].append(z - sp)
                if kind == "tri":
                    sp = jnp.where(strictly_earlier_half, sp, 0.0)
                sps.append(sp.astype(BF16))
            t["sp"] = assemble(t["parts"], sps)

        def later_sum(t):
            t["csum"] = _dot(strictly_later, t["sp"])

        def weights(t):
            key = (t["qi"], t["h"])
            run = runs[key]
            probs = []
            for (rows, cols, kind), logb in zip(t["parts"], t["logb"]):
                if kind == "empty":
                    probs.append(jnp.zeros((half, half), BF16))
                    continue
                a = jnp.exp2(logb - (run[:, cols] + t["csum"][rows, cols]))
                if kind == "tri":
                    a = jnp.where(strictly_earlier_half, a, 0.0)
                probs.append(a.astype(BF16))
            t["prob"] = assemble(t["parts"], probs)
            runs[key] = run + (t["csum"][0:1, :] + t["sp"][0:1, :].astype(F32))

        def values(n, t):
            key = (t["qi"], t["h"])
            hd = t["h"]
            v_blk = vT_scr[par_at, t["kb"], hd * GROUP_DIM:(hd + 1) * GROUP_DIM, :]
            if t["keep"] is not None:
                v_blk = (v_blk.astype(F32) * t["keep"]).astype(BF16)
            accs[key] = accs[key] + _dot(v_blk, t["prob"])
            if finish and last_tile[key] == n and hd % heads_per_tile == heads_per_tile - 1:
                gated_pair(accs, t["qi"], hd // heads_per_tile)

        n_tiles = len(tiles)
        lead, lag = SKEW
        ready = next((n for n, t in enumerate(tiles) if not t["diag"]), n_tiles)
        for t in tiles[:ready]:
            scores(t)
        steps = range(min(0, ready - lead), n_tiles + lag)
        extra = list(extra)
        slots = [EXTRA_START + (n * (len(steps) - EXTRA_START)) // max(1, len(extra))
                 for n in range(len(extra))]
        for count, step in enumerate(steps):
            while extra and slots[0] <= count:
                slots.pop(0)
                extra.pop(0)()
            if ready <= step + lead < n_tiles:
                scores(tiles[step + lead])
            if 0 <= step < n_tiles:
                softplus(tiles[step])
                later_sum(tiles[step])
            if 0 <= step - lag < n_tiles:
                weights(tiles[step - lag])
                values(step - lag, tiles[step - lag])
        for item in extra:
            item()
        return runs, accs

    runs = {(qi, hd): jnp.zeros((1, TQ), F32) for qi in range(Q_PER_STEP) for hd in range(n_heads)}
    accs = {(qi, hd): jnp.zeros((GROUP_DIM, TQ), F32) for qi in range(Q_PER_STEP) for hd in range(n_heads)}
    first_kb = j_at * Q_PER_STEP
    diag_blocks = [(qi, first_kb + qi, True, None) for qi in range(Q_PER_STEP)]
    has_earlier = (j_at > 0).astype(F32)
    prev_blocks = [(qi, jnp.maximum(first_kb + qi - 1, 0), False, has_earlier if qi == 0 else None)
                   for qi in range(Q_PER_STEP)]
    runs, accs = key_blocks(diag_blocks + prev_blocks, runs, accs, items, finish=True)

    for qi in range(Q_PER_STEP):
        def sticks_left(runs):
            return jnp.min(functools.reduce(jnp.minimum, [runs[qi, hd] for hd in range(n_heads)])) < UNDERFLOW_LOG2

        def earlier_block(state, qi=qi, sticks_left=sticks_left):
            kb, _, runs, accs = state
            runs, accs = key_blocks([(qi, kb, False, None)], runs, accs)
            return kb - 1, sticks_left(runs), runs, accs

        first = first_kb + qi - 2
        mine = lambda d: {k: v for k, v in d.items() if k[0] == qi}

        @pl.when((first >= 0) & sticks_left(runs))
        def _(qi=qi, first=first, earlier_block=earlier_block, mine=mine):
            _, _, _, more = lax.while_loop(lambda st: (st[0] >= 0) & st[1], earlier_block,
                                           (first, True, mine(runs), mine(accs)))
            for p in range(n_pairs):
                gated_pair(more, qi, p)


def _fused_block(x, mod3, w, w_sp, b_sp_full, a_g, a_b, w_out, ln_g, ln_b, alpha):
    bsz, seq, d = x.shape
    d_a = a_g.shape[-1]
    d_b = (w.shape[1] - 3 * d_a) // 4
    per_seq = seq // ROWS
    n_total = bsz * per_seq
    grid = (n_total + 2,)

    def stage(lag):
        return lambda s: _stage_slabs(s, n_total, per_seq)[lag]

    def rows_of(lag):
        return pl.BlockSpec((1, ROWS, d), lambda s: (*stage(lag)(s), 0))

    def mod_of(lag):
        return pl.BlockSpec((1, 3, d), lambda s: (stage(lag)(s)[0], 0, 0))

    def whole(a):
        return pl.BlockSpec(a.shape, lambda s: (0,) * a.ndim)

    return pl.pallas_call(
        functools.partial(_block_kernel, alpha, n_total),
        grid=grid,
        in_specs=[rows_of(0), rows_of(2), mod_of(0), mod_of(2), whole(w), whole(w_sp),
                  whole(b_sp_full), whole(a_g), whole(a_b), whole(w_out), whole(ln_g), whole(ln_b)],
        out_specs=rows_of(2),
        out_shape=jax.ShapeDtypeStruct((bsz, seq, d), x.dtype),
        scratch_shapes=[
            pltpu.VMEM((2, seq, d_b), BF16),
            pltpu.VMEM((2, seq // TK, d_b, TK), BF16),
            pltpu.VMEM((2, ROWS, d_b), BF16),
            pltpu.VMEM((2, ROWS, d_b), F32),
            pltpu.VMEM((3, ROWS, d_a), BF16),
            pltpu.VMEM((2, ROWS, d_b), BF16),
            pltpu.VMEM((2, Q_PER_STEP * (d_b // GROUP_DIM), TK, TQ), F32),
            pltpu.VMEM((w.shape[1] // MXU_SLAB, ROWS, MXU_SLAB), F32),
            pltpu.VMEM((ROWS, d), F32),
        ],
        compiler_params=pltpu.CompilerParams(
            dimension_semantics=("arbitrary",), vmem_limit_bytes=VMEM_LIMIT),
        name="fused_block",
    )(x, x, mod3, mod3, w, w_sp, b_sp_full, a_g, a_b, w_out, ln_g, ln_b)


def kernel(x, c, w_ada, b_ada, w_in, sgu_ln_g, sgu_ln_b, w_spatial, b_spatial, w_out, ln_g, ln_b):
    depth = w_ada.shape[0]
    bsz, seq, d = x.shape
    d_a = sgu_ln_g.shape[-1]
    alpha = (2.0 * depth) ** 0.25
    for layer in range(depth):
        mod, w_in_bf, w_out_bf = _prepare(c, w_ada[layer], b_ada[layer], w_in[layer], w_out[layer])
        b_sp_full = jnp.repeat(b_spatial[layer].T, GROUP_DIM, axis=1)
        x = _fused_block(
            x, mod.reshape(bsz, 3, d), w_in_bf, w_spatial[layer], b_sp_full,
            sgu_ln_g[layer].reshape(1, d_a), sgu_ln_b[layer].reshape(1, d_a),
            w_out_bf, ln_g[layer].reshape(1, d), ln_b[layer].reshape(1, d), alpha)
    return x
```

```python
import functools
import math

import jax
import jax.numpy as jnp
from jax import lax
from jax.experimental import pallas as pl
from jax.experimental.pallas import tpu as pltpu

F32 = jnp.float32
BF16 = jnp.bfloat16

N_GROUPS = 8
GROUP_DIM = 64
CHUNK = 128
LN_EPS = 1e-5
LANES = 128

PREP_TILE = 512
TQ = 256
TK = 256
Q_PER_STEP = 1
ROWS = Q_PER_STEP * TQ
SKEW = (8, 4)
EXTRA_START = 1
MXU_SLAB = 512
FIN_DELAY = 1
UNDERFLOW_LOG2 = 152.0
VMEM_LIMIT = 48 * 1024 * 1024


def _dot(a, b):
    return jnp.dot(a, b, preferred_element_type=F32)


def _dot_nt(a, b):
    return lax.dot_general(a, b, (((1,), (1,)), ((), ())), preferred_element_type=F32)


def _gelu_tanh(x):
    c = math.sqrt(2.0 / math.pi)
    half_x = 0.5 * x
    return half_x + half_x * jnp.tanh(x * (c + (c * 0.044715) * (x * x)))


def _silu(x):
    return x / (1.0 + jnp.exp(-x))


def _layer_norm(x, g, b):
    mu = jnp.mean(x, axis=-1, keepdims=True)
    xc = x - mu
    var = jnp.mean(xc * xc, axis=-1, keepdims=True)
    return xc * lax.rsqrt(var + LN_EPS) * g + b


def _prep_kernel(c_ref, wa_ref, ba_ref, win_ref, wout_ref, mod_ref, win_bf_ref, wout_bf_ref):
    sc = _silu(c_ref[...]).astype(BF16)
    mod_ref[...] = _dot(sc, wa_ref[...].astype(BF16)) + ba_ref[...]
    win_bf_ref[...] = win_ref[...].astype(BF16)
    wout_bf_ref[...] = wout_ref[...].astype(BF16)


def _prepare(c, w_ada, b_ada, w_in, w_out):
    bsz, d = c.shape
    n_mod = w_ada.shape[1]
    steps = w_in.shape[1] // PREP_TILE
    mod_tiles = n_mod // PREP_TILE
    out_tile = PREP_TILE
    out_tiles = w_out.shape[1] // out_tile
    assert mod_tiles <= steps and out_tiles <= steps
    mod_col = lambda j: (0, jnp.minimum(j, mod_tiles - 1))
    out_col = lambda j: (0, jnp.minimum(j, out_tiles - 1))
    return pl.pallas_call(
        _prep_kernel,
        grid=(steps,),
        in_specs=[
            pl.BlockSpec((bsz, d), lambda j: (0, 0)),
            pl.BlockSpec((d, PREP_TILE), mod_col),
            pl.BlockSpec((1, PREP_TILE), mod_col),
            pl.BlockSpec((w_in.shape[0], PREP_TILE), lambda j: (0, j)),
            pl.BlockSpec((w_out.shape[0], out_tile), out_col),
        ],
        out_specs=[
            pl.BlockSpec((bsz, PREP_TILE), mod_col),
            pl.BlockSpec((w_in.shape[0], PREP_TILE), lambda j: (0, j)),
            pl.BlockSpec((w_out.shape[0], out_tile), out_col),
        ],
        out_shape=[
            jax.ShapeDtypeStruct((bsz, n_mod), F32),
            jax.ShapeDtypeStruct(w_in.shape, BF16),
            jax.ShapeDtypeStruct(w_out.shape, BF16),
        ],
        compiler_params=pltpu.CompilerParams(
            dimension_semantics=("arbitrary",), vmem_limit_bytes=VMEM_LIMIT),
        name="adaln_mod_and_casts",
    )(c, w_ada, b_ada.reshape(1, n_mod), w_in, w_out)


def _stage_slabs(s, n_total, per_seq):
    last = n_total - 1
    out = []
    for lag in range(3):
        t = jnp.clip(s - lag, 0, last)
        out.append((lax.div(t, per_seq), lax.rem(t, per_seq)))
    return out


def _block_kernel(alpha, n_total,
                  x_in_ref, x_out_ref, mod_in_ref, mod_out_ref, w_ref, wsp_ref, bsp_ref,
                  ag_ref, ab_ref, wo_ref, g_ref, b_ref, o_ref,
                  k_scr, vT_scr, q_ring, gz_ring, ya_ring, yb_ring, z_ring, land_scr, y_scr):
    s = pl.program_id(0)
    d_b = q_ring.shape[-1]
    d_a = ya_ring.shape[-1]
    per_seq = k_scr.shape[1] // ROWS
    (b_in, j_in), (b_at, j_at), _ = _stage_slabs(s, n_total, per_seq)
    par_in, par_at = lax.rem(b_in, 2), lax.rem(b_at, 2)
    n_heads = d_b // GROUP_DIM
    heads_per_tile = LANES // GROUP_DIM
    n_pairs = n_heads // heads_per_tile
    now2, prev2 = lax.rem(s, 2), lax.rem(s + 1, 2)
    now3, prev3 = lax.rem(s, 3), lax.rem(s + 1, 3)

    @pl.when(s == 0)
    def _():
        q_ring[...] = jnp.zeros_like(q_ring)
        gz_ring[...] = jnp.zeros_like(gz_ring)
        ya_ring[...] = jnp.zeros_like(ya_ring)
        yb_ring[...] = jnp.zeros_like(yb_ring)
        z_ring[...] = jnp.zeros_like(z_ring)
        k_scr[0, 0:ROWS, :] = jnp.zeros((ROWS, d_b), BF16)
        for n in range(Q_PER_STEP):
            vT_scr[0, n] = jnp.zeros((d_b, TK), BF16)

    new = {}

    def item_h():
        shift = mod_in_ref[0, 0:1, :]
        scale = mod_in_ref[0, 1:2, :]
        new["h"] = (x_in_ref[0] * (1.0 + scale) + shift).astype(BF16)

    def h():
        return new["h"]

    def ycat():
        return jnp.concatenate([ya_ring[prev3], yb_ring[prev2]], axis=1)

    def both(*fs):
        return lambda: [f() for f in fs]

    def fin_out():
        gate = mod_out_ref[0, 2:3, :]
        r = alpha * x_out_ref[0] + gate * y_scr[...]
        o_ref[0] = _layer_norm(r, g_ref[...], b_ref[...])

    def fin_u(p):
        new["ug"] = _gelu_tanh(p())

    def fin_v(p):
        new["vn"] = _layer_norm(_gelu_tanh(p()), ag_ref[...], ab_ref[...]).astype(BF16)
        t_idx = lax.broadcasted_iota(jnp.int32, (CHUNK, CHUNK), 0)
        s_idx = lax.broadcasted_iota(jnp.int32, (CHUNK, CHUNK), 1)
        causal = t_idx >= s_idx
        w_sp = [jnp.where(causal, wsp_ref[g], 0.0).astype(BF16) for g in range(N_GROUPS)]
        first_group = lax.broadcasted_iota(jnp.int32, (CHUNK, LANES), 1) < GROUP_DIM
        chunks = []
        for c in range(ROWS // CHUNK):
            rows = slice(c * CHUNK, (c + 1) * CHUNK)
            pieces = []
            for pr in range(d_a // LANES):
                cols = slice(pr * LANES, (pr + 1) * LANES)
                vp = new["vn"][rows, cols]
                mixed = jnp.where(first_group, _dot(w_sp[2 * pr], vp), _dot(w_sp[2 * pr + 1], vp))
                pieces.append(new["ug"][rows, cols] * (mixed + bsp_ref[:, cols]))
            chunks.append(jnp.concatenate(pieces, axis=1))
        new["ya_pre"] = jnp.concatenate(chunks, axis=0)

    def fin_za(p):
        ya_ring[now3] = (_silu(p()) * new["ya_pre"]).astype(BF16)

    scale_q = math.log2(math.e) / math.sqrt(GROUP_DIM)

    def fin_q(p):
        new["q"] = (p() * scale_q).astype(BF16)
        q_ring[now2] = new["q"]

    def fin_k(p):
        new["k"] = p().astype(BF16)
        k_scr[par_in, pl.ds(pl.multiple_of(j_in * ROWS, ROWS), ROWS), :] = new["k"]

    def head_of(q_pair, hh):
        in_head = (lane >= hh * GROUP_DIM) & (lane < (hh + 1) * GROUP_DIM)
        return jnp.where(in_head, q_pair, jnp.zeros_like(q_pair))

    def pre_scores():
        for qi in range(Q_PER_STEP):
            rows = slice(qi * TQ, (qi + 1) * TQ)
            for hd in range(n_heads):
                p, hh = divmod(hd, heads_per_tile)
                lanes = slice(p * LANES, (p + 1) * LANES)
                z_ring[now2, qi * n_heads + hd] = _dot_nt(
                    new["k"][rows, lanes], head_of(new["q"][rows, lanes], hh))

    def fin_vt(p):
        v = p()
        for n in range(Q_PER_STEP):
            vT_scr[par_in, j_in * Q_PER_STEP + n] = v[n * TK:(n + 1) * TK, :].T.astype(BF16)

    def fin_gz(p):
        gz_ring[now2] = _silu(p())

    dots, due = [], {}

    def out_slab(j):
        cols = slice(j * MXU_SLAB, (j + 1) * MXU_SLAB)

        def item():
            y_scr[:, cols] = _dot(ycat(), wo_ref[:, cols])
        return item

    for j in range(wo_ref.shape[1] // MXU_SLAB):
        dots.append(out_slab(j))
    due.setdefault(len(dots) - 1 + FIN_DELAY, []).append(fin_out)

    o = 3 * d_a
    projections = [(0, fin_u), (d_a, fin_v), (2 * d_a, fin_za), (o, fin_q), (o + d_b, fin_k),
                   (o + 2 * d_b, fin_vt), (o + 3 * d_b, fin_gz)]
    assert d_a == d_b == MXU_SLAB

    assert land_scr.shape[0] == len(projections)

    def in_slab(n, lo):
        def item():
            land_scr[n] = _dot(h(), w_ref[:, lo:lo + MXU_SLAB])
        return item

    def landed(n):
        return lambda: land_scr[n]

    for n, (lo, fin) in enumerate(projections):
        dots.append(in_slab(n, lo))
        due.setdefault(len(dots) - 1 + FIN_DELAY, []).append(functools.partial(fin, landed(n)))
    items = [item_h]
    for n, dot_item in enumerate(dots):
        items.append(both(dot_item, *due.pop(n, [])))
    items.append(both(*[f for n in sorted(due) for f in due[n]], pre_scores))

    lane = lax.broadcasted_iota(jnp.int32, (TQ, LANES), 1)
    half = TK // 2
    s_idx = lax.broadcasted_iota(jnp.int32, (half, half), 0)
    t_idx = lax.broadcasted_iota(jnp.int32, (half, half), 1)
    strictly_earlier_half = s_idx < t_idx
    j_idx = lax.broadcasted_iota(jnp.int32, (TK, TK), 1)
    r_idx = lax.broadcasted_iota(jnp.int32, (TK, TK), 0)
    strictly_later = jnp.where(j_idx > r_idx, 1.0, 0.0).astype(BF16)

    q_masked = {}
    for qi in range(Q_PER_STEP):
        for hd in range(n_heads):
            p, hh = divmod(hd, heads_per_tile)
            q_pair = q_ring[prev2, qi * TQ:(qi + 1) * TQ, p * LANES:(p + 1) * LANES]
            q_masked[qi, hd] = head_of(q_pair, hh)

    lo_half, hi_half = slice(0, half), slice(half, TK)
    diag_parts = [(lo_half, lo_half, "tri"), (lo_half, hi_half, "full"),
                  (hi_half, lo_half, "empty"), (hi_half, hi_half, "tri")]
    full_parts = [(slice(0, TK), slice(0, TQ), "full")]

    def assemble(parts, pieces):
        if len(parts) == 1:
            return pieces[0]
        return jnp.concatenate([jnp.concatenate(pieces[0:2], axis=1),
                                jnp.concatenate(pieces[2:4], axis=1)], axis=0)

    def gated_pair(accs, qi, p):
        rows, cols = slice(qi * TQ, (qi + 1) * TQ), slice(p * LANES, (p + 1) * LANES)
        yT = jnp.concatenate([accs[qi, heads_per_tile * p + n] for n in range(heads_per_tile)], axis=0)
        yb_ring[now2, rows, cols] = (yT.T * gz_ring[prev2, rows, cols]).astype(BF16)

    def key_blocks(blocks, runs, accs, extra=(), finish=False):
        runs, accs = dict(runs), dict(accs)
        tiles = [dict(qi=qi, kb=kb, h=hd, keep=keep, diag=diag, parts=diag_parts if diag else full_parts)
                 for qi, kb, diag, keep in blocks for hd in range(n_heads)]
        last_tile = {(t["qi"], t["h"]): n for n, t in enumerate(tiles)}

        def scores(t):
            if t["diag"]:
                t["z"] = z_ring[prev2, t["qi"] * n_heads + t["h"]]
                return
            start = pl.multiple_of(t["kb"] * TK, TK)
            p = t["h"] // heads_per_tile
            lanes = slice(p * LANES, (p + 1) * LANES)
            t["z"] = _dot_nt(k_scr[par_at, pl.ds(start, TK), lanes], q_masked[t["qi"], t["h"]])

        def softplus(t):
            sps, t["logb"] = [], []
            for rows, cols, kind in t["parts"]:
                if kind == "empty":
                    sps.append(jnp.zeros((half, half), BF16))
                    t["logb"].append(None)
                    continue
                z = t["z"][rows, cols]
                sp = jnp.maximum(z, 0.0) + jnp.log2(1.0 + jnp.exp2(-jnp.abs(z)))
                t["logb"].append(z - sp)
                if kind == "tri":
                    sp = jnp.where(strictly_earlier_half, sp, 0.0)
                sps.append(sp.astype(BF16))
            t["sp"] = assemble(t["parts"], sps)

        def later_sum(t):
            t["csum"] = _dot(strictly_later, t["sp"])

        def weights(t):
            key = (t["qi"], t["h"])
            run = runs[key]
            probs = []
            for (rows, cols, kind), logb in zip(t["parts"], t["logb"]):
                if kind == "empty":
                    probs.append(jnp.zeros((half, half), BF16))
                    continue
                a = jnp.exp2(logb - (run[:, cols] + t["csum"][rows, cols]))
                if kind == "tri":
                    a = jnp.where(strictly_earlier_half, a, 0.0)
                probs.append(a.astype(BF16))
            t["prob"] = assemble(t["parts"], probs)
            runs[key] = run + (t["csum"][0:1, :] + t["sp"][0:1, :].astype(F32))

        def values(n, t):
            key = (t["qi"], t["h"])
            hd = t["h"]
            v_blk = vT_scr[par_at, t["kb"], hd * GROUP_DIM:(hd + 1) * GROUP_DIM, :]
            if t["keep"] is not None:
                v_blk = (v_blk.astype(F32) * t["keep"]).astype(BF16)
            accs[key] = accs[key] + _dot(v_blk, t["prob"])
            if finish and last_tile[key] == n and hd % heads_per_tile == heads_per_tile - 1:
                gated_pair(accs, t["qi"], hd // heads_per_tile)

        n_tiles = len(tiles)
        lead, lag = SKEW
        ready = next((n for n, t in enumerate(tiles) if not t["diag"]), n_tiles)
        for t in tiles[:ready]:
            scores(t)
        steps = range(min(0, ready - lead), n_tiles + lag)
        extra = list(extra)
        slots = [EXTRA_START + (n * (len(steps) - EXTRA_START)) // max(1, len(extra))
                 for n in range(len(extra))]
        for count, step in enumerate(steps):
            while extra and slots[0] <= count:
                slots.pop(0)
                extra.pop(0)()
            if ready <= step + lead < n_tiles:
                scores(tiles[step + lead])
            if 0 <= step < n_tiles:
                softplus(tiles[step])
                later_sum(tiles[step])
            if 0 <= step - lag < n_tiles:
                weights(tiles[step - lag])
                values(step - lag, tiles[step - lag])
        for item in extra:
            item()
        return runs, accs

    runs = {(qi, hd): jnp.zeros((1, TQ), F32) for qi in range(Q_PER_STEP) for hd in range(n_heads)}
    accs = {(qi, hd): jnp.zeros((GROUP_DIM, TQ), F32) for qi in range(Q_PER_STEP) for hd in range(n_heads)}
    first_kb = j_at * Q_PER_STEP
    diag_blocks = [(qi, first_kb + qi, True, None) for qi in range(Q_PER_STEP)]
    has_earlier = (j_at > 0).astype(F32)
    prev_blocks = [(qi, jnp.maximum(first_kb + qi - 1, 0), False, has_earlier if qi == 0 else None)
                   for qi in range(Q_PER_STEP)]
    runs, accs = key_blocks(diag_blocks + prev_blocks, runs, accs, items, finish=True)

    for qi in range(Q_PER_STEP):
        def sticks_left(runs):
            return jnp.min(functools.reduce(jnp.minimum, [runs[qi, hd] for hd in range(n_heads)])) < UNDERFLOW_LOG2

        def earlier_block(state, qi=qi, sticks_left=sticks_left):
            kb, _, runs, accs = state
            runs, accs = key_blocks([(qi, kb, False, None)], runs, accs)
            return kb - 1, sticks_left(runs), runs, accs

        first = first_kb + qi - 2
        mine = lambda d: {k: v for k, v in d.items() if k[0] == qi}

        @pl.when((first >= 0) & sticks_left(runs))
        def _(qi=qi, first=first, earlier_block=earlier_block, mine=mine):
            _, _, _, more = lax.while_loop(lambda st: (st[0] >= 0) & st[1], earlier_block,
                                           (first, True, mine(runs), mine(accs)))
            for p in range(n_pairs):
                gated_pair(more, qi, p)


def _fused_block(x, mod3, w, w_sp, b_sp_full, a_g, a_b, w_out, ln_g, ln_b, alpha):
    bsz, seq, d = x.shape
    d_a = a_g.shape[-1]
    d_b = (w.shape[1] - 3 * d_a) // 4
    per_seq = seq // ROWS
    n_total = bsz * per_seq
    grid = (n_total + 2,)

    def stage(lag):
        return lambda s: _stage_slabs(s, n_total, per_seq)[lag]

    def rows_of(lag):
        return pl.BlockSpec((1, ROWS, d), lambda s: (*stage(lag)(s), 0))

    def mod_of(lag):
        return pl.BlockSpec((1, 3, d), lambda s: (stage(lag)(s)[0], 0, 0))

    def whole(a):
        return pl.BlockSpec(a.shape, lambda s: (0,) * a.ndim)

    return pl.pallas_call(
        functools.partial(_block_kernel, alpha, n_total),
        grid=grid,
        in_specs=[rows_of(0), rows_of(2), mod_of(0), mod_of(2), whole(w), whole(w_sp),
                  whole(b_sp_full), whole(a_g), whole(a_b), whole(w_out), whole(ln_g), whole(ln_b)],
        out_specs=rows_of(2),
        out_shape=jax.ShapeDtypeStruct((bsz, seq, d), x.dtype),
        scratch_shapes=[
            pltpu.VMEM((2, seq, d_b), BF16),
            pltpu.VMEM((2, seq // TK, d_b, TK), BF16),
            pltpu.VMEM((2, ROWS, d_b), BF16),
            pltpu.VMEM((2, ROWS, d_b), F32),
            pltpu.VMEM((3, ROWS, d_a), BF16),
            pltpu.VMEM((2, ROWS, d_b), BF16),
            pltpu.VMEM((2, Q_PER_STEP * (d_b // GROUP_DIM), TK, TQ), F32),
            pltpu.VMEM((w.shape[1] // MXU_SLAB, ROWS, MXU_SLAB), F32),
            pltpu.VMEM((ROWS, d), F32),
        ],
        compiler_params=pltpu.CompilerParams(
            dimension_semantics=("arbitrary",), vmem_limit_bytes=VMEM_LIMIT),
        name="fused_block",
    )(x, x, mod3, mod3, w, w_sp, b_sp_full, a_g, a_b, w_out, ln_g, ln_b)


def kernel(x, c, w_ada, b_ada, w_in, sgu_ln_g, sgu_ln_b, w_spatial, b_spatial, w_out, ln_g, ln_b):
    depth = w_ada.shape[0]
    bsz, seq, d = x.shape
    d_a = sgu_ln_g.shape[-1]
    alpha = (2.0 * depth) ** 0.25
    for layer in range(depth):
        mod, w_in_bf, w_out_bf = _prepare(c, w_ada[layer], b_ada[layer], w_in[layer], w_out[layer])
        b_sp_full = jnp.repeat(b_spatial[layer].T, GROUP_DIM, axis=1)
        x = _fused_block(
            x, mod.reshape(bsz, 3, d), w_in_bf, w_spatial[layer], b_sp_full,
            sgu_ln_g[layer].reshape(1, d_a), sgu_ln_b[layer].reshape(1, d_a),
            w_out_bf, ln_g[layer].reshape(1, d), ln_b[layer].reshape(1, d), alpha)
    return x
```

```python
import functools
import math

import jax
import jax.numpy as jnp
from jax import lax
from jax.experimental import pallas as pl
from jax.experimental.pallas import tpu as pltpu

F32 = jnp.float32
BF16 = jnp.bfloat16

N_GROUPS = 8
GROUP_DIM = 64
CHUNK = 128
LN_EPS = 1e-5
LANES = 128

PREP_TILE = 512
TQ = 256
TK = 256
Q_PER_STEP = 1
ROWS = Q_PER_STEP * TQ
SKEW = (8, 4)
EXTRA_START = 1
MXU_SLAB = 512
FIN_DELAY = 1
UNDERFLOW_LOG2 = 152.0
VMEM_LIMIT = 48 * 1024 * 1024


def _dot(a, b):
    return jnp.dot(a, b, preferred_element_type=F32)


def _dot_nt(a, b):
    return lax.dot_general(a, b, (((1,), (1,)), ((), ())), preferred_element_type=F32)


def _gelu_tanh(x):
    k1 = -2.0 * math.sqrt(2.0 / math.pi) * math.log2(math.e)
    return x / (1.0 + jnp.exp2(x * (k1 + (k1 * 0.044715) * (x * x))))


def _silu(x):
    return x / (1.0 + jnp.exp(-x))


def _layer_norm(x, g, b):
    mu = jnp.mean(x, axis=-1, keepdims=True)
    xc = x - mu
    var = jnp.mean(xc * xc, axis=-1, keepdims=True)
    return xc * lax.rsqrt(var + LN_EPS) * g + b


def _prep_kernel(c_ref, wa_ref, ba_ref, win_ref, wout_ref, mod_ref, win_bf_ref, wout_bf_ref):
    sc = _silu(c_ref[...]).astype(BF16)
    mod_ref[...] = _dot(sc, wa_ref[...].astype(BF16)) + ba_ref[...]
    win_bf_ref[...] = win_ref[...].astype(BF16)
    wout_bf_ref[...] = wout_ref[...].astype(BF16)


def _prepare(c, w_ada, b_ada, w_in, w_out):
    bsz, d = c.shape
    n_mod = w_ada.shape[1]
    steps = w_in.shape[1] // PREP_TILE
    mod_tiles = n_mod // PREP_TILE
    out_tile = PREP_TILE
    out_tiles = w_out.shape[1] // out_tile
    assert mod_tiles <= steps and out_tiles <= steps
    mod_col = lambda j: (0, jnp.minimum(j, mod_tiles - 1))
    out_col = lambda j: (0, jnp.minimum(j, out_tiles - 1))
    return pl.pallas_call(
        _prep_kernel,
        grid=(steps,),
        in_specs=[
            pl.BlockSpec((bsz, d), lambda j: (0, 0)),
            pl.BlockSpec((d, PREP_TILE), mod_col),
            pl.BlockSpec((1, PREP_TILE), mod_col),
            pl.BlockSpec((w_in.shape[0], PREP_TILE), lambda j: (0, j)),
            pl.BlockSpec((w_out.shape[0], out_tile), out_col),
        ],
        out_specs=[
            pl.BlockSpec((bsz, PREP_TILE), mod_col),
            pl.BlockSpec((w_in.shape[0], PREP_TILE), lambda j: (0, j)),
            pl.BlockSpec((w_out.shape[0], out_tile), out_col),
        ],
        out_shape=[
            jax.ShapeDtypeStruct((bsz, n_mod), F32),
            jax.ShapeDtypeStruct(w_in.shape, BF16),
            jax.ShapeDtypeStruct(w_out.shape, BF16),
        ],
        compiler_params=pltpu.CompilerParams(
            dimension_semantics=("arbitrary",), vmem_limit_bytes=VMEM_LIMIT),
        name="adaln_mod_and_casts",
    )(c, w_ada, b_ada.reshape(1, n_mod), w_in, w_out)


def _stage_slabs(s, n_total, per_seq):
    last = n_total - 1
    out = []
    for lag in range(3):
        t = jnp.clip(s - lag, 0, last)
        out.append((lax.div(t, per_seq), lax.rem(t, per_seq)))
    return out


def _block_kernel(alpha, n_total,
                  x_in_ref, x_out_ref, mod_in_ref, mod_out_ref, w_ref, wsp_ref, bsp_ref,
                  ag_ref, ab_ref, wo_ref, g_ref, b_ref, o_ref,
                  k_scr, vT_scr, q_ring, gz_ring, ya_ring, yb_ring, z_ring, land_scr, y_scr):
    s = pl.program_id(0)
    d_b = q_ring.shape[-1]
    d_a = ya_ring.shape[-1]
    per_seq = k_scr.shape[1] // ROWS
    (b_in, j_in), (b_at, j_at), _ = _stage_slabs(s, n_total, per_seq)
    par_in, par_at = lax.rem(b_in, 2), lax.rem(b_at, 2)
    n_heads = d_b // GROUP_DIM
    heads_per_tile = LANES // GROUP_DIM
    n_pairs = n_heads // heads_per_tile
    now2, prev2 = lax.rem(s, 2), lax.rem(s + 1, 2)
    now3, prev3 = lax.rem(s, 3), lax.rem(s + 1, 3)

    @pl.when(s == 0)
    def _():
        q_ring[...] = jnp.zeros_like(q_ring)
        gz_ring[...] = jnp.zeros_like(gz_ring)
        ya_ring[...] = jnp.zeros_like(ya_ring)
        yb_ring[...] = jnp.zeros_like(yb_ring)
        z_ring[...] = jnp.zeros_like(z_ring)
        k_scr[0, 0:ROWS, :] = jnp.zeros((ROWS, d_b), BF16)
        for n in range(Q_PER_STEP):
            vT_scr[0, n] = jnp.zeros((d_b, TK), BF16)

    new = {}

    def item_h():
        shift = mod_in_ref[0, 0:1, :]
        scale = mod_in_ref[0, 1:2, :]
        new["h"] = (x_in_ref[0] * (1.0 + scale) + shift).astype(BF16)

    def h():
        return new["h"]

    def ycat():
        return jnp.concatenate([ya_ring[prev3], yb_ring[prev2]], axis=1)

    def both(*fs):
        return lambda: [f() for f in fs]

    def fin_out():
        gate = mod_out_ref[0, 2:3, :]
        r = alpha * x_out_ref[0] + gate * y_scr[...]
        o_ref[0] = _layer_norm(r, g_ref[...], b_ref[...])

    def fin_u(p):
        new["ug"] = _gelu_tanh(p())

    def fin_v(p):
        new["vn"] = _layer_norm(_gelu_tanh(p()), ag_ref[...], ab_ref[...]).astype(BF16)
        t_idx = lax.broadcasted_iota(jnp.int32, (CHUNK, CHUNK), 0)
        s_idx = lax.broadcasted_iota(jnp.int32, (CHUNK, CHUNK), 1)
        causal = t_idx >= s_idx
        w_sp = [jnp.where(causal, wsp_ref[g], 0.0).astype(BF16) for g in range(N_GROUPS)]
        first_group = lax.broadcasted_iota(jnp.int32, (CHUNK, LANES), 1) < GROUP_DIM
        chunks = []
        for c in range(ROWS // CHUNK):
            rows = slice(c * CHUNK, (c + 1) * CHUNK)
            pieces = []
            for pr in range(d_a // LANES):
                cols = slice(pr * LANES, (pr + 1) * LANES)
                vp = new["vn"][rows, cols]
                mixed = jnp.where(first_group, _dot(w_sp[2 * pr], vp), _dot(w_sp[2 * pr + 1], vp))
                pieces.append(new["ug"][rows, cols] * (mixed + bsp_ref[:, cols]))
            chunks.append(jnp.concatenate(pieces, axis=1))
        new["ya_pre"] = jnp.concatenate(chunks, axis=0)

    def fin_za(p):
        ya_ring[now3] = (_silu(p()) * new["ya_pre"]).astype(BF16)

    scale_q = math.log2(math.e) / math.sqrt(GROUP_DIM)

    def fin_q(p):
        new["q"] = (p() * scale_q).astype(BF16)
        q_ring[now2] = new["q"]

    def fin_k(p):
        new["k"] = p().astype(BF16)
        k_scr[par_in, pl.ds(pl.multiple_of(j_in * ROWS, ROWS), ROWS), :] = new["k"]

    def head_of(q_pair, hh):
        in_head = (lane >= hh * GROUP_DIM) & (lane < (hh + 1) * GROUP_DIM)
        return jnp.where(in_head, q_pair, jnp.zeros_like(q_pair))

    def pre_scores():
        for qi in range(Q_PER_STEP):
            rows = slice(qi * TQ, (qi + 1) * TQ)
            for hd in range(n_heads):
                p, hh = divmod(hd, heads_per_tile)
                lanes = slice(p * LANES, (p + 1) * LANES)
                z_ring[now2, qi * n_heads + hd] = _dot_nt(
                    new["k"][rows, lanes], head_of(new["q"][rows, lanes], hh))

    def fin_vt(p):
        v = p()
        for n in range(Q_PER_STEP):
            vT_scr[par_in, j_in * Q_PER_STEP + n] = v[n * TK:(n + 1) * TK, :].T.astype(BF16)

    def fin_gz(p):
        gz_ring[now2] = _silu(p())

    dots, due = [], {}

    def out_slab(j):
        cols = slice(j * MXU_SLAB, (j + 1) * MXU_SLAB)

        def item():
            y_scr[:, cols] = _dot(ycat(), wo_ref[:, cols])
        return item

    for j in range(wo_ref.shape[1] // MXU_SLAB):
        dots.append(out_slab(j))
    due.setdefault(len(dots) - 1 + FIN_DELAY, []).append(fin_out)

    o = 3 * d_a
    projections = [(0, fin_u), (d_a, fin_v), (2 * d_a, fin_za), (o, fin_q), (o + d_b, fin_k),
                   (o + 2 * d_b, fin_vt), (o + 3 * d_b, fin_gz)]
    assert d_a == d_b == MXU_SLAB

    assert land_scr.shape[0] == len(projections)

    def in_slab(n, lo):
        def item():
            land_scr[n] = _dot(h(), w_ref[:, lo:lo + MXU_SLAB])
        return item

    def landed(n):
        return lambda: land_scr[n]

    for n, (lo, fin) in enumerate(projections):
        dots.append(in_slab(n, lo))
        due.setdefault(len(dots) - 1 + FIN_DELAY, []).append(functools.partial(fin, landed(n)))
    items = [item_h]
    for n, dot_item in enumerate(dots):
        items.append(both(dot_item, *due.pop(n, [])))
    items.append(both(*[f for n in sorted(due) for f in due[n]], pre_scores))

    lane = lax.broadcasted_iota(jnp.int32, (TQ, LANES), 1)
    half = TK // 2
    s_idx = lax.broadcasted_iota(jnp.int32, (half, half), 0)
    t_idx = lax.broadcasted_iota(jnp.int32, (half, half), 1)
    strictly_earlier_half = s_idx < t_idx
    j_idx = lax.broadcasted_iota(jnp.int32, (TK, TK), 1)
    r_idx = lax.broadcasted_iota(jnp.int32, (TK, TK), 0)
    strictly_later = jnp.where(j_idx > r_idx, 1.0, 0.0).astype(BF16)

    q_masked = {}
    for qi in range(Q_PER_STEP):
        for hd in range(n_heads):
            p, hh = divmod(hd, heads_per_tile)
            q_pair = q_ring[prev2, qi * TQ:(qi + 1) * TQ, p * LANES:(p + 1) * LANES]
            q_masked[qi, hd] = head_of(q_pair, hh)

    lo_half, hi_half = slice(0, half), slice(half, TK)
    diag_parts = [(lo_half, lo_half, "tri"), (lo_half, hi_half, "full"),
                  (hi_half, lo_half, "empty"), (hi_half, hi_half, "tri")]
    full_parts = [(slice(0, TK), slice(0, TQ), "full")]

    def assemble(parts, pieces):
        if len(parts) == 1:
            return pieces[0]
        return jnp.concatenate([jnp.concatenate(pieces[0:2], axis=1),
                                jnp.concatenate(pieces[2:4], axis=1)], axis=0)

    def gated_pair(accs, qi, p):
        rows, cols = slice(qi * TQ, (qi + 1) * TQ), slice(p * LANES, (p + 1) * LANES)
        yT = jnp.concatenate([accs[qi, heads_per_tile * p + n] for n in range(heads_per_tile)], axis=0)
        yb_ring[now2, rows, cols] = (yT.T * gz_ring[prev2, rows, cols]).astype(BF16)

    def key_blocks(blocks, runs, accs, extra=(), finish=False):
        runs, accs = dict(runs), dict(accs)
        tiles = [dict(qi=qi, kb=kb, h=hd, keep=keep, diag=diag, parts=diag_parts if diag else full_parts)
                 for qi, kb, diag, keep in blocks for hd in range(n_heads)]
        last_tile = {(t["qi"], t["h"]): n for n, t in enumerate(tiles)}

        def scores(t):
            if t["diag"]:
                t["z"] = z_ring[prev2, t["qi"] * n_heads + t["h"]]
                return
            start = pl.multiple_of(t["kb"] * TK, TK)
            p = t["h"] // heads_per_tile
            lanes = slice(p * LANES, (p + 1) * LANES)
            t["z"] = _dot_nt(k_scr[par_at, pl.ds(start, TK), lanes], q_masked[t["qi"], t["h"]])

        def softplus(t):
            sps, t["logb"] = [], []
            for rows, cols, kind in t["parts"]:
                if kind == "empty":
                    sps.append(jnp.zeros((half, half), BF16))
                    t["logb"].append(None)
                    continue
                z = t["z"][rows, cols]
                sp = jnp.maximum(z, 0.0) + jnp.log2(1.0 + jnp.exp2(-jnp.abs(z)))
                t["logb"].append(z - sp)
                if kind == "tri":
                    sp = jnp.where(strictly_earlier_half, sp, 0.0)
                sps.append(sp.astype(BF16))
            t["sp"] = assemble(t["parts"], sps)

        def later_sum(t):
            t["csum"] = _dot(strictly_later, t["sp"])

        def weights(t):
            key = (t["qi"], t["h"])
            run = runs[key]
            probs = []
            for (rows, cols, kind), logb in zip(t["parts"], t["logb"]):
                if kind == "empty":
                    probs.append(jnp.zeros((half, half), BF16))
                    continue
                a = jnp.exp2(logb - (run[:, cols] + t["csum"][rows, cols]))
                if kind == "tri":
                    a = jnp.where(strictly_earlier_half, a, 0.0)
                probs.append(a.astype(BF16))
            t["prob"] = assemble(t["parts"], probs)
            runs[key] = run + (t["csum"][0:1, :] + t["sp"][0:1, :].astype(F32))

        def values(n, t):
            key = (t["qi"], t["h"])
            hd = t["h"]
            v_blk = vT_scr[par_at, t["kb"], hd * GROUP_DIM:(hd + 1) * GROUP_DIM, :]
            if t["keep"] is not None:
                v_blk = (v_blk.astype(F32) * t["keep"]).astype(BF16)
            accs[key] = accs[key] + _dot(v_blk, t["prob"])
            if finish and last_tile[key] == n and hd % heads_per_tile == heads_per_tile - 1:
                gated_pair(accs, t["qi"], hd // heads_per_tile)

        n_tiles = len(tiles)
        lead, lag = SKEW
        ready = next((n for n, t in enumerate(tiles) if not t["diag"]), n_tiles)
        for t in tiles[:ready]:
            scores(t)
        steps = range(min(0, ready - lead), n_tiles + lag)
        extra = list(extra)
        slots = [EXTRA_START + (n * (len(steps) - EXTRA_START)) // max(1, len(extra))
                 for n in range(len(extra))]
        for count, step in enumerate(steps):
            while extra and slots[0] <= count:
                slots.pop(0)
                extra.pop(0)()
            if ready <= step + lead < n_tiles:
                scores(tiles[step + lead])
            if 0 <= step < n_tiles:
                softplus(tiles[step])
                later_sum(tiles[step])
            if 0 <= step - lag < n_tiles:
                weights(tiles[step - lag])
                values(step - lag, tiles[step - lag])
        for item in extra:
            item()
        return runs, accs

    runs = {(qi, hd): jnp.zeros((1, TQ), F32) for qi in range(Q_PER_STEP) for hd in range(n_heads)}
    accs = {(qi, hd): jnp.zeros((GROUP_DIM, TQ), F32) for qi in range(Q_PER_STEP) for hd in range(n_heads)}
    first_kb = j_at * Q_PER_STEP
    diag_blocks = [(qi, first_kb + qi, True, None) for qi in range(Q_PER_STEP)]
    has_earlier = (j_at > 0).astype(F32)
    prev_blocks = [(qi, jnp.maximum(first_kb + qi - 1, 0), False, has_earlier if qi == 0 else None)
                   for qi in range(Q_PER_STEP)]
    runs, accs = key_blocks(diag_blocks + prev_blocks, runs, accs, items, finish=True)

    for qi in range(Q_PER_STEP):
        def sticks_left(runs):
            return jnp.min(functools.reduce(jnp.minimum, [runs[qi, hd] for hd in range(n_heads)])) < UNDERFLOW_LOG2

        def earlier_block(state, qi=qi, sticks_left=sticks_left):
            kb, _, runs, accs = state
            runs, accs = key_blocks([(qi, kb, False, None)], runs, accs)
            return kb - 1, sticks_left(runs), runs, accs

        first = first_kb + qi - 2
        mine = lambda d: {k: v for k, v in d.items() if k[0] == qi}

        @pl.when((first >= 0) & sticks_left(runs))
        def _(qi=qi, first=first, earlier_block=earlier_block, mine=mine):
            _, _, _, more = lax.while_loop(lambda st: (st[0] >= 0) & st[1], earlier_block,
                                           (first, True, mine(runs), mine(accs)))
            for p in range(n_pairs):
                gated_pair(more, qi, p)


def _fused_block(x, mod3, w, w_sp, b_sp_full, a_g, a_b, w_out, ln_g, ln_b, alpha):
    bsz, seq, d = x.shape
    d_a = a_g.shape[-1]
    d_b = (w.shape[1] - 3 * d_a) // 4
    per_seq = seq // ROWS
    n_total = bsz * per_seq
    grid = (n_total + 2,)

    def stage(lag):
        return lambda s: _stage_slabs(s, n_total, per_seq)[lag]

    def rows_of(lag):
        return pl.BlockSpec((1, ROWS, d), lambda s: (*stage(lag)(s), 0))

    def mod_of(lag):
        return pl.BlockSpec((1, 3, d), lambda s: (stage(lag)(s)[0], 0, 0))

    def whole(a):
        return pl.BlockSpec(a.shape, lambda s: (0,) * a.ndim)

    return pl.pallas_call(
        functools.partial(_block_kernel, alpha, n_total),
        grid=grid,
        in_specs=[rows_of(0), rows_of(2), mod_of(0), mod_of(2), whole(w), whole(w_sp),
                  whole(b_sp_full), whole(a_g), whole(a_b), whole(w_out), whole(ln_g), whole(ln_b)],
        out_specs=rows_of(2),
        out_shape=jax.ShapeDtypeStruct((bsz, seq, d), x.dtype),
        scratch_shapes=[
            pltpu.VMEM((2, seq, d_b), BF16),
            pltpu.VMEM((2, seq // TK, d_b, TK), BF16),
            pltpu.VMEM((2, ROWS, d_b), BF16),
            pltpu.VMEM((2, ROWS, d_b), F32),
            pltpu.VMEM((3, ROWS, d_a), BF16),
            pltpu.VMEM((2, ROWS, d_b), BF16),
            pltpu.VMEM((2, Q_PER_STEP * (d_b // GROUP_DIM), TK, TQ), F32),
            pltpu.VMEM((w.shape[1] // MXU_SLAB, ROWS, MXU_SLAB), F32),
            pltpu.VMEM((ROWS, d), F32),
        ],
        compiler_params=pltpu.CompilerParams(
            dimension_semantics=("arbitrary",), vmem_limit_bytes=VMEM_LIMIT),
        name="fused_block",
    )(x, x, mod3, mod3, w, w_sp, b_sp_full, a_g, a_b, w_out, ln_g, ln_b)


def kernel(x, c, w_ada, b_ada, w_in, sgu_ln_g, sgu_ln_b, w_spatial, b_spatial, w_out, ln_g, ln_b):
    depth = w_ada.shape[0]
    bsz, seq, d = x.shape
    d_a = sgu_ln_g.shape[-1]
    alpha = (2.0 * depth) ** 0.25
    for layer in range(depth):
        mod, w_in_bf, w_out_bf = _prepare(c, w_ada[layer], b_ada[layer], w_in[layer], w_out[layer])
        b_sp_full = jnp.repeat(b_spatial[layer].T, GROUP_DIM, axis=1)
        x = _fused_block(
            x, mod.reshape(bsz, 3, d), w_in_bf, w_spatial[layer], b_sp_full,
            sgu_ln_g[layer].reshape(1, d_a), sgu_ln_b[layer].reshape(1, d_a),
            w_out_bf, ln_g[layer].reshape(1, d), ln_b[layer].reshape(1, d), alpha)
    return x
```

```python
import functools
import math

import jax
import jax.numpy as jnp
from jax import lax
from jax.experimental import pallas as pl
from jax.experimental.pallas import tpu as pltpu

F32 = jnp.float32
BF16 = jnp.bfloat16

N_GROUPS = 8
GROUP_DIM = 64
CHUNK = 128
LN_EPS = 1e-5
LANES = 128

PREP_TILE = 512
TQ = 256
TK = 256
Q_PER_STEP = 1
ROWS = Q_PER_STEP * TQ
SKEW = (8, 4)
EXTRA_START = 1
MXU_SLAB = 512
FIN_DELAY = 1
UNDERFLOW_LOG2 = 152.0
VMEM_LIMIT = 60 * 1024 * 1024


def _dot(a, b):
    return jnp.dot(a, b, preferred_element_type=F32)


def _dot_nt(a, b):
    return lax.dot_general(a, b, (((1,), (1,)), ((), ())), preferred_element_type=F32)


def _gelu_tanh(x):
    k1 = -2.0 * math.sqrt(2.0 / math.pi) * math.log2(math.e)
    return x / (1.0 + jnp.exp2(x * (k1 + (k1 * 0.044715) * (x * x))))


def _silu(x):
    return x / (1.0 + jnp.exp(-x))


def _layer_norm(x, g, b):
    mu = jnp.mean(x, axis=-1, keepdims=True)
    xc = x - mu
    var = jnp.mean(xc * xc, axis=-1, keepdims=True)
    return xc * lax.rsqrt(var + LN_EPS) * g + b


def _prep_kernel(c_ref, wa_ref, ba_ref, wout_ref, mod_ref, wout_bf_ref):
    sc = _silu(c_ref[...]).astype(BF16)
    mod_ref[...] = _dot(sc, wa_ref[...].astype(BF16)) + ba_ref[...]
    wout_bf_ref[...] = wout_ref[...].astype(BF16)


def _prepare(c, w_ada, b_ada, w_out):
    bsz, d = c.shape
    n_mod = w_ada.shape[1]
    steps = n_mod // PREP_TILE
    out_tiles = w_out.shape[1] // PREP_TILE
    assert out_tiles <= steps
    out_col = lambda j: (0, jnp.minimum(j, out_tiles - 1))
    return pl.pallas_call(
        _prep_kernel,
        grid=(steps,),
        in_specs=[
            pl.BlockSpec((bsz, d), lambda j: (0, 0)),
            pl.BlockSpec((d, PREP_TILE), lambda j: (0, j)),
            pl.BlockSpec((1, PREP_TILE), lambda j: (0, j)),
            pl.BlockSpec((w_out.shape[0], PREP_TILE), out_col),
        ],
        out_specs=[
            pl.BlockSpec((bsz, PREP_TILE), lambda j: (0, j)),
            pl.BlockSpec((w_out.shape[0], PREP_TILE), out_col),
        ],
        out_shape=[
            jax.ShapeDtypeStruct((bsz, n_mod), F32),
            jax.ShapeDtypeStruct(w_out.shape, BF16),
        ],
        compiler_params=pltpu.CompilerParams(
            dimension_semantics=("arbitrary",), vmem_limit_bytes=VMEM_LIMIT),
        name="adaln_mod_and_casts",
    )(c, w_ada, b_ada.reshape(1, n_mod), w_out)


def _stage_slabs(s, n_total, per_seq):
    last = n_total - 1
    out = []
    for lag in range(3):
        t = jnp.clip(s - lag, 0, last)
        out.append((lax.div(t, per_seq), lax.rem(t, per_seq)))
    return out


def _block_kernel(alpha, n_total,
                  x_in_ref, x_out_ref, mod_in_ref, mod_out_ref, w_ref, wsp_ref, bsp_ref,
                  ag_ref, ab_ref, wo_ref, g_ref, b_ref, o_ref,
                  k_scr, vT_scr, q_ring, gz_ring, ya_ring, yb_ring, z_ring, land_scr, y_scr, wbf_scr):
    s = pl.program_id(0)
    d_b = q_ring.shape[-1]
    d_a = ya_ring.shape[-1]
    per_seq = k_scr.shape[1] // ROWS
    (b_in, j_in), (b_at, j_at), _ = _stage_slabs(s, n_total, per_seq)
    par_in, par_at = lax.rem(b_in, 2), lax.rem(b_at, 2)
    n_heads = d_b // GROUP_DIM
    heads_per_tile = LANES // GROUP_DIM
    n_pairs = n_heads // heads_per_tile
    now2, prev2 = lax.rem(s, 2), lax.rem(s + 1, 2)
    now3, prev3 = lax.rem(s, 3), lax.rem(s + 1, 3)

    @pl.when(s == 0)
    def _():
        q_ring[...] = jnp.zeros_like(q_ring)
        gz_ring[...] = jnp.zeros_like(gz_ring)
        ya_ring[...] = jnp.zeros_like(ya_ring)
        yb_ring[...] = jnp.zeros_like(yb_ring)
        z_ring[...] = jnp.zeros_like(z_ring)
        for lo in range(0, w_ref.shape[1], MXU_SLAB):
            wbf_scr[:, lo:lo + MXU_SLAB] = w_ref[:, lo:lo + MXU_SLAB].astype(BF16)
        k_scr[0, 0:ROWS, :] = jnp.zeros((ROWS, d_b), BF16)
        for n in range(Q_PER_STEP):
            vT_scr[0, n] = jnp.zeros((d_b, TK), BF16)

    new = {}

    def item_h():
        shift = mod_in_ref[0, 0:1, :]
        scale = mod_in_ref[0, 1:2, :]
        new["h"] = (x_in_ref[0] * (1.0 + scale) + shift).astype(BF16)

    def h():
        return new["h"]

    def ycat():
        return jnp.concatenate([ya_ring[prev3], yb_ring[prev2]], axis=1)

    def both(*fs):
        return lambda: [f() for f in fs]

    def fin_out():
        gate = mod_out_ref[0, 2:3, :]
        r = alpha * x_out_ref[0] + gate * y_scr[...]
        o_ref[0] = _layer_norm(r, g_ref[...], b_ref[...])

    def fin_u(p):
        new["ug"] = _gelu_tanh(p())

    def fin_v(p):
        new["vn"] = _layer_norm(_gelu_tanh(p()), ag_ref[...], ab_ref[...]).astype(BF16)
        t_idx = lax.broadcasted_iota(jnp.int32, (CHUNK, CHUNK), 0)
        s_idx = lax.broadcasted_iota(jnp.int32, (CHUNK, CHUNK), 1)
        causal = t_idx >= s_idx
        w_sp = [jnp.where(causal, wsp_ref[g], 0.0).astype(BF16) for g in range(N_GROUPS)]
        first_group = lax.broadcasted_iota(jnp.int32, (CHUNK, LANES), 1) < GROUP_DIM
        chunks = []
        for c in range(ROWS // CHUNK):
            rows = slice(c * CHUNK, (c + 1) * CHUNK)
            pieces = []
            for pr in range(d_a // LANES):
                cols = slice(pr * LANES, (pr + 1) * LANES)
                vp = new["vn"][rows, cols]
                mixed = jnp.where(first_group, _dot(w_sp[2 * pr], vp), _dot(w_sp[2 * pr + 1], vp))
                pieces.append(new["ug"][rows, cols] * (mixed + bsp_ref[:, cols]))
            chunks.append(jnp.concatenate(pieces, axis=1))
        new["ya_pre"] = jnp.concatenate(chunks, axis=0)

    def fin_za(p):
        ya_ring[now3] = (_silu(p()) * new["ya_pre"]).astype(BF16)

    scale_q = math.log2(math.e) / math.sqrt(GROUP_DIM)

    def fin_q(p):
        new["q"] = (p() * scale_q).astype(BF16)
        q_ring[now2] = new["q"]

    def fin_k(p):
        new["k"] = p().astype(BF16)
        k_scr[par_in, pl.ds(pl.multiple_of(j_in * ROWS, ROWS), ROWS), :] = new["k"]

    def head_of(q_pair, hh):
        in_head = (lane >= hh * GROUP_DIM) & (lane < (hh + 1) * GROUP_DIM)
        return jnp.where(in_head, q_pair, jnp.zeros_like(q_pair))

    def pre_scores():
        for qi in range(Q_PER_STEP):
            rows = slice(qi * TQ, (qi + 1) * TQ)
            for hd in range(n_heads):
                p, hh = divmod(hd, heads_per_tile)
                lanes = slice(p * LANES, (p + 1) * LANES)
                z_ring[now2, qi * n_heads + hd] = _dot_nt(
                    new["k"][rows, lanes], head_of(new["q"][rows, lanes], hh))

    def fin_vt(p):
        v = p()
        for n in range(Q_PER_STEP):
            vT_scr[par_in, j_in * Q_PER_STEP + n] = v[n * TK:(n + 1) * TK, :].T.astype(BF16)

    def fin_gz(p):
        gz_ring[now2] = _silu(p())

    dots, due = [], {}

    def out_slab(j):
        cols = slice(j * MXU_SLAB, (j + 1) * MXU_SLAB)

        def item():
            y_scr[:, cols] = _dot(ycat(), wo_ref[:, cols])
        return item

    for j in range(wo_ref.shape[1] // MXU_SLAB):
        dots.append(out_slab(j))
    due.setdefault(len(dots) - 1 + FIN_DELAY, []).append(fin_out)

    o = 3 * d_a
    projections = [(0, fin_u), (d_a, fin_v), (2 * d_a, fin_za), (o, fin_q), (o + d_b, fin_k),
                   (o + 2 * d_b, fin_vt), (o + 3 * d_b, fin_gz)]
    assert d_a == d_b == MXU_SLAB

    assert land_scr.shape[0] == len(projections)

    def in_slab(n, lo):
        def item():
            land_scr[n] = _dot(h(), wbf_scr[:, lo:lo + MXU_SLAB])
        return item

    def landed(n):
        return lambda: land_scr[n]

    for n, (lo, fin) in enumerate(projections):
        dots.append(in_slab(n, lo))
        due.setdefault(len(dots) - 1 + FIN_DELAY, []).append(functools.partial(fin, landed(n)))
    items = [item_h]
    for n, dot_item in enumerate(dots):
        items.append(both(dot_item, *due.pop(n, [])))
    items.append(both(*[f for n in sorted(due) for f in due[n]], pre_scores))

    lane = lax.broadcasted_iota(jnp.int32, (TQ, LANES), 1)
    half = TK // 2
    s_idx = lax.broadcasted_iota(jnp.int32, (half, half), 0)
    t_idx = lax.broadcasted_iota(jnp.int32, (half, half), 1)
    strictly_earlier_half = s_idx < t_idx
    j_idx = lax.broadcasted_iota(jnp.int32, (TK, TK), 1)
    r_idx = lax.broadcasted_iota(jnp.int32, (TK, TK), 0)
    strictly_later = jnp.where(j_idx > r_idx, 1.0, 0.0).astype(BF16)

    q_masked = {}
    for qi in range(Q_PER_STEP):
        for hd in range(n_heads):
            p, hh = divmod(hd, heads_per_tile)
            q_pair = q_ring[prev2, qi * TQ:(qi + 1) * TQ, p * LANES:(p + 1) * LANES]
            q_masked[qi, hd] = head_of(q_pair, hh)

    lo_half, hi_half = slice(0, half), slice(half, TK)
    diag_parts = [(lo_half, lo_half, "tri"), (lo_half, hi_half, "full"),
                  (hi_half, lo_half, "empty"), (hi_half, hi_half, "tri")]
    full_parts = [(slice(0, TK), slice(0, TQ), "full")]

    def assemble(parts, pieces):
        if len(parts) == 1:
            return pieces[0]
        return jnp.concatenate([jnp.concatenate(pieces[0:2], axis=1),
                                jnp.concatenate(pieces[2:4], axis=1)], axis=0)

    def gated_pair(accs, qi, p):
        rows, cols = slice(qi * TQ, (qi + 1) * TQ), slice(p * LANES, (p + 1) * LANES)
        yT = jnp.concatenate([accs[qi, heads_per_tile * p + n] for n in range(heads_per_tile)], axis=0)
        yb_ring[now2, rows, cols] = (yT.T * gz_ring[prev2, rows, cols]).astype(BF16)

    def key_blocks(blocks, runs, accs, extra=(), finish=False):
        runs, accs = dict(runs), dict(accs)
        tiles = [dict(qi=qi, kb=kb, h=hd, keep=keep, diag=diag, parts=diag_parts if diag else full_parts)
                 for qi, kb, diag, keep in blocks for hd in range(n_heads)]
        last_tile = {(t["qi"], t["h"]): n for n, t in enumerate(tiles)}

        def scores(t):
            if t["diag"]:
                t["z"] = z_ring[prev2, t["qi"] * n_heads + t["h"]]
                return
            start = pl.multiple_of(t["kb"] * TK, TK)
            p = t["h"] // heads_per_tile
            lanes = slice(p * LANES, (p + 1) * LANES)
            t["z"] = _dot_nt(k_scr[par_at, pl.ds(start, TK), lanes], q_masked[t["qi"], t["h"]])

        def softplus(t):
            sps, t["logb"] = [], []
            for rows, cols, kind in t["parts"]:
                if kind == "empty":
                    sps.append(jnp.zeros((half, half), BF16))
                    t["logb"].append(None)
                    continue
                z = t["z"][rows, cols]
                sp = jnp.maximum(z, 0.0) + jnp.log2(1.0 + jnp.exp2(-jnp.abs(z)))
                t["logb"].append(z - sp)
                if kind == "tri":
                    sp = jnp.where(strictly_earlier_half, sp, 0.0)
                sps.append(sp.astype(BF16))
            t["sp"] = assemble(t["parts"], sps)

        def later_sum(t):
            t["csum"] = _dot(strictly_later, t["sp"])

        def weights(t):
            key = (t["qi"], t["h"])
            run = runs[key]
            probs = []
            for (rows, cols, kind), logb in zip(t["parts"], t["logb"]):
                if kind == "empty":
                    probs.append(jnp.zeros((half, half), BF16))
                    continue
                a = jnp.exp2(logb - (run[:, cols] + t["csum"][rows, cols]))
                if kind == "tri":
                    a = jnp.where(strictly_earlier_half, a, 0.0)
                probs.append(a.astype(BF16))
            t["prob"] = assemble(t["parts"], probs)
            runs[key] = run + (t["csum"][0:1, :] + t["sp"][0:1, :].astype(F32))

        def values(n, t):
            key = (t["qi"], t["h"])
            hd = t["h"]
            v_blk = vT_scr[par_at, t["kb"], hd * GROUP_DIM:(hd + 1) * GROUP_DIM, :]
            if t["keep"] is not None:
                v_blk = (v_blk.astype(F32) * t["keep"]).astype(BF16)
            accs[key] = accs[key] + _dot(v_blk, t["prob"])
            if finish and last_tile[key] == n and hd % heads_per_tile == heads_per_tile - 1:
                gated_pair(accs, t["qi"], hd // heads_per_tile)

        n_tiles = len(tiles)
        lead, lag = SKEW
        ready = next((n for n, t in enumerate(tiles) if not t["diag"]), n_tiles)
        for t in tiles[:ready]:
            scores(t)
        steps = range(min(0, ready - lead), n_tiles + lag)
        extra = list(extra)
        slots = [EXTRA_START + (n * (len(steps) - EXTRA_START)) // max(1, len(extra))
                 for n in range(len(extra))]
        for count, step in enumerate(steps):
            while extra and slots[0] <= count:
                slots.pop(0)
                extra.pop(0)()
            if ready <= step + lead < n_tiles:
                scores(tiles[step + lead])
            if 0 <= step < n_tiles:
                softplus(tiles[step])
                later_sum(tiles[step])
            if 0 <= step - lag < n_tiles:
                weights(tiles[step - lag])
                values(step - lag, tiles[step - lag])
        for item in extra:
            item()
        return runs, accs

    runs = {(qi, hd): jnp.zeros((1, TQ), F32) for qi in range(Q_PER_STEP) for hd in range(n_heads)}
    accs = {(qi, hd): jnp.zeros((GROUP_DIM, TQ), F32) for qi in range(Q_PER_STEP) for hd in range(n_heads)}
    first_kb = j_at * Q_PER_STEP
    diag_blocks = [(qi, first_kb + qi, True, None) for qi in range(Q_PER_STEP)]
    has_earlier = (j_at > 0).astype(F32)
    prev_blocks = [(qi, jnp.maximum(first_kb + qi - 1, 0), False, has_earlier if qi == 0 else None)
                   for qi in range(Q_PER_STEP)]
    runs, accs = key_blocks(diag_blocks + prev_blocks, runs, accs, items, finish=True)

    for qi in range(Q_PER_STEP):
        def sticks_left(runs):
            return jnp.min(functools.reduce(jnp.minimum, [runs[qi, hd] for hd in range(n_heads)])) < UNDERFLOW_LOG2

        def earlier_block(state, qi=qi, sticks_left=sticks_left):
            kb, _, runs, accs = state
            runs, accs = key_blocks([(qi, kb, False, None)], runs, accs)
            return kb - 1, sticks_left(runs), runs, accs

        first = first_kb + qi - 2
        mine = lambda d: {k: v for k, v in d.items() if k[0] == qi}

        @pl.when((first >= 0) & sticks_left(runs))
        def _(qi=qi, first=first, earlier_block=earlier_block, mine=mine):
            _, _, _, more = lax.while_loop(lambda st: (st[0] >= 0) & st[1], earlier_block,
                                           (first, True, mine(runs), mine(accs)))
            for p in range(n_pairs):
                gated_pair(more, qi, p)


def _fused_block(x, mod3, w, w_sp, b_sp_full, a_g, a_b, w_out, ln_g, ln_b, alpha):
    bsz, seq, d = x.shape
    d_a = a_g.shape[-1]
    d_b = (w.shape[1] - 3 * d_a) // 4
    per_seq = seq // ROWS
    n_total = bsz * per_seq
    grid = (n_total + 2,)

    def stage(lag):
        return lambda s: _stage_slabs(s, n_total, per_seq)[lag]

    def rows_of(lag):
        return pl.BlockSpec((1, ROWS, d), lambda s: (*stage(lag)(s), 0))

    def mod_of(lag):
        return pl.BlockSpec((1, 3, d), lambda s: (stage(lag)(s)[0], 0, 0))

    def whole(a):
        return pl.BlockSpec(a.shape, lambda s: (0,) * a.ndim)

    return pl.pallas_call(
        functools.partial(_block_kernel, alpha, n_total),
        grid=grid,
        in_specs=[rows_of(0), rows_of(2), mod_of(0), mod_of(2), whole(w), whole(w_sp),
                  whole(b_sp_full), whole(a_g), whole(a_b), whole(w_out), whole(ln_g), whole(ln_b)],
        out_specs=rows_of(2),
        out_shape=jax.ShapeDtypeStruct((bsz, seq, d), x.dtype),
        scratch_shapes=[
            pltpu.VMEM((2, seq, d_b), BF16),
            pltpu.VMEM((2, seq // TK, d_b, TK), BF16),
            pltpu.VMEM((2, ROWS, d_b), BF16),
            pltpu.VMEM((2, ROWS, d_b), F32),
            pltpu.VMEM((3, ROWS, d_a), BF16),
            pltpu.VMEM((2, ROWS, d_b), BF16),
            pltpu.VMEM((2, Q_PER_STEP * (d_b // GROUP_DIM), TK, TQ), F32),
            pltpu.VMEM((w.shape[1] // MXU_SLAB, ROWS, MXU_SLAB), F32),
            pltpu.VMEM((ROWS, d), F32),
            pltpu.VMEM(w.shape, BF16),
        ],
        compiler_params=pltpu.CompilerParams(
            dimension_semantics=("arbitrary",), vmem_limit_bytes=VMEM_LIMIT),
        name="fused_block",
    )(x, x, mod3, mod3, w, w_sp, b_sp_full, a_g, a_b, w_out, ln_g, ln_b)


def kernel(x, c, w_ada, b_ada, w_in, sgu_ln_g, sgu_ln_b, w_spatial, b_spatial, w_out, ln_g, ln_b):
    depth = w_ada.shape[0]
    bsz, seq, d = x.shape
    d_a = sgu_ln_g.shape[-1]
    alpha = (2.0 * depth) ** 0.25
    for layer in range(depth):
        mod, w_out_bf = _prepare(c, w_ada[layer], b_ada[layer], w_out[layer])
        b_sp_full = jnp.repeat(b_spatial[layer].T, GROUP_DIM, axis=1)
        x = _fused_block(
            x, mod.reshape(bsz, 3, d), w_in[layer], w_spatial[layer], b_sp_full,
            sgu_ln_g[layer].reshape(1, d_a), sgu_ln_b[layer].reshape(1, d_a),
            w_out_bf, ln_g[layer].reshape(1, d), ln_b[layer].reshape(1, d), alpha)
    return x
```

```python
import functools
import math

import jax
import jax.numpy as jnp
from jax import lax
from jax.experimental import pallas as pl
from jax.experimental.pallas import tpu as pltpu

F32 = jnp.float32
BF16 = jnp.bfloat16

N_GROUPS = 8
GROUP_DIM = 64
CHUNK = 128
LN_EPS = 1e-5
LANES = 128

PREP_TILE = 512
TQ = 256
TK = 256
Q_PER_STEP = 1
ROWS = Q_PER_STEP * TQ
SKEW = (8, 5)
EXTRA_START = 1
MXU_SLAB = 512
FIN_DELAY = 1
UNDERFLOW_LOG2 = 152.0
VMEM_LIMIT = 48 * 1024 * 1024


def _dot(a, b):
    return jnp.dot(a, b, preferred_element_type=F32)


def _dot_nt(a, b):
    return lax.dot_general(a, b, (((1,), (1,)), ((), ())), preferred_element_type=F32)


def _gelu_tanh(x):
    k1 = -2.0 * math.sqrt(2.0 / math.pi) * math.log2(math.e)
    return x / (1.0 + jnp.exp2(x * (k1 + (k1 * 0.044715) * (x * x))))


def _silu(x):
    return x / (1.0 + jnp.exp(-x))


def _layer_norm(x, g, b):
    mu = jnp.mean(x, axis=-1, keepdims=True)
    xc = x - mu
    var = jnp.mean(xc * xc, axis=-1, keepdims=True)
    return xc * lax.rsqrt(var + LN_EPS) * g + b


def _prep_kernel(c_ref, wa_ref, ba_ref, win_ref, wout_ref, mod_ref, win_bf_ref, wout_bf_ref):
    sc = _silu(c_ref[...]).astype(BF16)
    mod_ref[...] = _dot(sc, wa_ref[...].astype(BF16)) + ba_ref[...]
    win_bf_ref[...] = win_ref[...].astype(BF16)
    wout_bf_ref[...] = wout_ref[...].astype(BF16)


def _prepare(c, w_ada, b_ada, w_in, w_out):
    bsz, d = c.shape
    n_mod = w_ada.shape[1]
    steps = w_in.shape[1] // PREP_TILE
    mod_tiles = n_mod // PREP_TILE
    out_tile = PREP_TILE
    out_tiles = w_out.shape[1] // out_tile
    assert mod_tiles <= steps and out_tiles <= steps
    mod_col = lambda j: (0, jnp.minimum(j, mod_tiles - 1))
    out_col = lambda j: (0, jnp.minimum(j, out_tiles - 1))
    return pl.pallas_call(
        _prep_kernel,
        grid=(steps,),
        in_specs=[
            pl.BlockSpec((bsz, d), lambda j: (0, 0)),
            pl.BlockSpec((d, PREP_TILE), mod_col),
            pl.BlockSpec((1, PREP_TILE), mod_col),
            pl.BlockSpec((w_in.shape[0], PREP_TILE), lambda j: (0, j)),
            pl.BlockSpec((w_out.shape[0], out_tile), out_col),
        ],
        out_specs=[
            pl.BlockSpec((bsz, PREP_TILE), mod_col),
            pl.BlockSpec((w_in.shape[0], PREP_TILE), lambda j: (0, j)),
            pl.BlockSpec((w_out.shape[0], out_tile), out_col),
        ],
        out_shape=[
            jax.ShapeDtypeStruct((bsz, n_mod), F32),
            jax.ShapeDtypeStruct(w_in.shape, BF16),
            jax.ShapeDtypeStruct(w_out.shape, BF16),
        ],
        compiler_params=pltpu.CompilerParams(
            dimension_semantics=("arbitrary",), vmem_limit_bytes=VMEM_LIMIT),
        name="adaln_mod_and_casts",
    )(c, w_ada, b_ada.reshape(1, n_mod), w_in, w_out)


def _stage_slabs(s, n_total, per_seq):
    last = n_total - 1
    out = []
    for lag in range(3):
        t = jnp.clip(s - lag, 0, last)
        out.append((lax.div(t, per_seq), lax.rem(t, per_seq)))
    return out


def _block_kernel(alpha, n_total,
                  x_in_ref, x_out_ref, mod_in_ref, mod_out_ref, w_ref, wsp_ref, bsp_ref,
                  ag_ref, ab_ref, wo_ref, g_ref, b_ref, o_ref,
                  k_scr, vT_scr, q_ring, gz_ring, ya_ring, yb_ring, z_ring, land_scr, y_scr):
    s = pl.program_id(0)
    d_b = q_ring.shape[-1]
    d_a = ya_ring.shape[-1]
    per_seq = k_scr.shape[1] // ROWS
    (b_in, j_in), (b_at, j_at), _ = _stage_slabs(s, n_total, per_seq)
    par_in, par_at = lax.rem(b_in, 2), lax.rem(b_at, 2)
    n_heads = d_b // GROUP_DIM
    heads_per_tile = LANES // GROUP_DIM
    n_pairs = n_heads // heads_per_tile
    now2, prev2 = lax.rem(s, 2), lax.rem(s + 1, 2)
    now3, prev3 = lax.rem(s, 3), lax.rem(s + 1, 3)

    @pl.when(s == 0)
    def _():
        q_ring[...] = jnp.zeros_like(q_ring)
        gz_ring[...] = jnp.zeros_like(gz_ring)
        ya_ring[...] = jnp.zeros_like(ya_ring)
        yb_ring[...] = jnp.zeros_like(yb_ring)
        z_ring[...] = jnp.zeros_like(z_ring)
        k_scr[0, 0:ROWS, :] = jnp.zeros((ROWS, d_b), BF16)
        for n in range(Q_PER_STEP):
            vT_scr[0, n] = jnp.zeros((d_b, TK), BF16)

    new = {}

    def item_h():
        shift = mod_in_ref[0, 0:1, :]
        scale = mod_in_ref[0, 1:2, :]
        new["h"] = (x_in_ref[0] * (1.0 + scale) + shift).astype(BF16)

    def h():
        return new["h"]

    def ycat():
        return jnp.concatenate([ya_ring[prev3], yb_ring[prev2]], axis=1)

    def both(*fs):
        return lambda: [f() for f in fs]

    def fin_out():
        gate = mod_out_ref[0, 2:3, :]
        r = alpha * x_out_ref[0] + gate * y_scr[...]
        o_ref[0] = _layer_norm(r, g_ref[...], b_ref[...])

    def fin_u(p):
        new["ug"] = _gelu_tanh(p())

    def fin_v(p):
        new["vn"] = _layer_norm(_gelu_tanh(p()), ag_ref[...], ab_ref[...]).astype(BF16)
        t_idx = lax.broadcasted_iota(jnp.int32, (CHUNK, CHUNK), 0)
        s_idx = lax.broadcasted_iota(jnp.int32, (CHUNK, CHUNK), 1)
        causal = t_idx >= s_idx
        w_sp = [jnp.where(causal, wsp_ref[g], 0.0).astype(BF16) for g in range(N_GROUPS)]
        first_group = lax.broadcasted_iota(jnp.int32, (CHUNK, LANES), 1) < GROUP_DIM
        chunks = []
        for c in range(ROWS // CHUNK):
            rows = slice(c * CHUNK, (c + 1) * CHUNK)
            pieces = []
            for pr in range(d_a // LANES):
                cols = slice(pr * LANES, (pr + 1) * LANES)
                vp = new["vn"][rows, cols]
                mixed = jnp.where(first_group, _dot(w_sp[2 * pr], vp), _dot(w_sp[2 * pr + 1], vp))
                pieces.append(new["ug"][rows, cols] * (mixed + bsp_ref[:, cols]))
            chunks.append(jnp.concatenate(pieces, axis=1))
        new["ya_pre"] = jnp.concatenate(chunks, axis=0)

    def fin_za(p):
        ya_ring[now3] = (_silu(p()) * new["ya_pre"]).astype(BF16)

    scale_q = math.log2(math.e) / math.sqrt(GROUP_DIM)

    def fin_q(p):
        new["q"] = (p() * scale_q).astype(BF16)
        q_ring[now2] = new["q"]

    def fin_k(p):
        new["k"] = p().astype(BF16)
        k_scr[par_in, pl.ds(pl.multiple_of(j_in * ROWS, ROWS), ROWS), :] = new["k"]

    def head_of(q_pair, hh):
        in_head = (lane >= hh * GROUP_DIM) & (lane < (hh + 1) * GROUP_DIM)
        return jnp.where(in_head, q_pair, jnp.zeros_like(q_pair))

    def pre_scores():
        for qi in range(Q_PER_STEP):
            rows = slice(qi * TQ, (qi + 1) * TQ)
            for hd in range(n_heads):
                p, hh = divmod(hd, heads_per_tile)
                lanes = slice(p * LANES, (p + 1) * LANES)
                z_ring[now2, qi * n_heads + hd] = _dot_nt(
                    new["k"][rows, lanes], head_of(new["q"][rows, lanes], hh))

    def fin_vt(p):
        v = p()
        for n in range(Q_PER_STEP):
            vT_scr[par_in, j_in * Q_PER_STEP + n] = v[n * TK:(n + 1) * TK, :].T.astype(BF16)

    def fin_gz(p):
        gz_ring[now2] = _silu(p())

    dots, due = [], {}

    def out_slab(j):
        cols = slice(j * MXU_SLAB, (j + 1) * MXU_SLAB)

        def item():
            y_scr[:, cols] = _dot(ycat(), wo_ref[:, cols])
        return item

    for j in range(wo_ref.shape[1] // MXU_SLAB):
        dots.append(out_slab(j))
    due.setdefault(len(dots) - 1 + FIN_DELAY, []).append(fin_out)

    o = 3 * d_a
    projections = [(0, fin_u), (d_a, fin_v), (2 * d_a, fin_za), (o, fin_q), (o + d_b, fin_k),
                   (o + 2 * d_b, fin_vt), (o + 3 * d_b, fin_gz)]
    assert d_a == d_b == MXU_SLAB

    assert land_scr.shape[0] == len(projections)

    def in_slab(n, lo):
        def item():
            land_scr[n] = _dot(h(), w_ref[:, lo:lo + MXU_SLAB])
        return item

    def landed(n):
        return lambda: land_scr[n]

    for n, (lo, fin) in enumerate(projections):
        dots.append(in_slab(n, lo))
        due.setdefault(len(dots) - 1 + FIN_DELAY, []).append(functools.partial(fin, landed(n)))
    items = [item_h]
    for n, dot_item in enumerate(dots):
        items.append(both(dot_item, *due.pop(n, [])))
    items.append(both(*[f for n in sorted(due) for f in due[n]], pre_scores))

    lane = lax.broadcasted_iota(jnp.int32, (TQ, LANES), 1)
    half = TK // 2
    s_idx = lax.broadcasted_iota(jnp.int32, (half, half), 0)
    t_idx = lax.broadcasted_iota(jnp.int32, (half, half), 1)
    strictly_earlier_half = s_idx < t_idx
    j_idx = lax.broadcasted_iota(jnp.int32, (TK, TK), 1)
    r_idx = lax.broadcasted_iota(jnp.int32, (TK, TK), 0)
    strictly_later = jnp.where(j_idx > r_idx, 1.0, 0.0).astype(BF16)

    q_masked = {}
    for qi in range(Q_PER_STEP):
        for hd in range(n_heads):
            p, hh = divmod(hd, heads_per_tile)
            q_pair = q_ring[prev2, qi * TQ:(qi + 1) * TQ, p * LANES:(p + 1) * LANES]
            q_masked[qi, hd] = head_of(q_pair, hh)

    lo_half, hi_half = slice(0, half), slice(half, TK)
    diag_parts = [(lo_half, lo_half, "tri"), (lo_half, hi_half, "full"),
                  (hi_half, lo_half, "empty"), (hi_half, hi_half, "tri")]
    full_parts = [(slice(0, TK), slice(0, TQ), "full")]

    def assemble(parts, pieces):
        if len(parts) == 1:
            return pieces[0]
        return jnp.concatenate([jnp.concatenate(pieces[0:2], axis=1),
                                jnp.concatenate(pieces[2:4], axis=1)], axis=0)

    def gated_pair(accs, qi, p):
        rows, cols = slice(qi * TQ, (qi + 1) * TQ), slice(p * LANES, (p + 1) * LANES)
        yT = jnp.concatenate([accs[qi, heads_per_tile * p + n] for n in range(heads_per_tile)], axis=0)
        yb_ring[now2, rows, cols] = (yT.T * gz_ring[prev2, rows, cols]).astype(BF16)

    def key_blocks(blocks, runs, accs, extra=(), finish=False):
        runs, accs = dict(runs), dict(accs)
        tiles = [dict(qi=qi, kb=kb, h=hd, keep=keep, diag=diag, parts=diag_parts if diag else full_parts)
                 for qi, kb, diag, keep in blocks for hd in range(n_heads)]
        last_tile = {(t["qi"], t["h"]): n for n, t in enumerate(tiles)}

        def scores(t):
            if t["diag"]:
                t["z"] = z_ring[prev2, t["qi"] * n_heads + t["h"]]
                return
            start = pl.multiple_of(t["kb"] * TK, TK)
            p = t["h"] // heads_per_tile
            lanes = slice(p * LANES, (p + 1) * LANES)
            t["z"] = _dot_nt(k_scr[par_at, pl.ds(start, TK), lanes], q_masked[t["qi"], t["h"]])

        def softplus(t):
            sps, t["logb"] = [], []
            for rows, cols, kind in t["parts"]:
                if kind == "empty":
                    sps.append(jnp.zeros((half, half), BF16))
                    t["logb"].append(None)
                    continue
                z = t["z"][rows, cols]
                sp = jnp.maximum(z, 0.0) + jnp.log2(1.0 + jnp.exp2(-jnp.abs(z)))
                t["logb"].append(z - sp)
                if kind == "tri":
                    sp = jnp.where(strictly_earlier_half, sp, 0.0)
                sps.append(sp.astype(BF16))
            t["sp"] = assemble(t["parts"], sps)

        def later_sum(t):
            t["csum"] = _dot(strictly_later, t["sp"])

        def weights(t):
            key = (t["qi"], t["h"])
            run = runs[key]
            probs = []
            for (rows, cols, kind), logb in zip(t["parts"], t["logb"]):
                if kind == "empty":
                    probs.append(jnp.zeros((half, half), BF16))
                    continue
                a = jnp.exp2(logb - (run[:, cols] + t["csum"][rows, cols]))
                if kind == "tri":
                    a = jnp.where(strictly_earlier_half, a, 0.0)
                probs.append(a.astype(BF16))
            t["prob"] = assemble(t["parts"], probs)
            runs[key] = run + (t["csum"][0:1, :] + t["sp"][0:1, :].astype(F32))

        def values(n, t):
            key = (t["qi"], t["h"])
            hd = t["h"]
            v_blk = vT_scr[par_at, t["kb"], hd * GROUP_DIM:(hd + 1) * GROUP_DIM, :]
            if t["keep"] is not None:
                v_blk = (v_blk.astype(F32) * t["keep"]).astype(BF16)
            accs[key] = accs[key] + _dot(v_blk, t["prob"])
            if finish and last_tile[key] == n and hd % heads_per_tile == heads_per_tile - 1:
                gated_pair(accs, t["qi"], hd // heads_per_tile)

        n_tiles = len(tiles)
        lead, lag = SKEW
        ready = next((n for n, t in enumerate(tiles) if not t["diag"]), n_tiles)
        for t in tiles[:ready]:
            scores(t)
        steps = range(min(0, ready - lead), n_tiles + lag)
        extra = list(extra)
        slots = [EXTRA_START + (n * (len(steps) - EXTRA_START)) // max(1, len(extra))
                 for n in range(len(extra))]
        for count, step in enumerate(steps):
            while extra and slots[0] <= count:
                slots.pop(0)
                extra.pop(0)()
            if ready <= step + lead < n_tiles:
                scores(tiles[step + lead])
            if 0 <= step < n_tiles:
                softplus(tiles[step])
                later_sum(tiles[step])
            if 0 <= step - lag < n_tiles:
                weights(tiles[step - lag])
                values(step - lag, tiles[step - lag])
        for item in extra:
            item()
        return runs, accs

    runs = {(qi, hd): jnp.zeros((1, TQ), F32) for qi in range(Q_PER_STEP) for hd in range(n_heads)}
    accs = {(qi, hd): jnp.zeros((GROUP_DIM, TQ), F32) for qi in range(Q_PER_STEP) for hd in range(n_heads)}
    first_kb = j_at * Q_PER_STEP
    diag_blocks = [(qi, first_kb + qi, True, None) for qi in range(Q_PER_STEP)]
    has_earlier = (j_at > 0).astype(F32)
    prev_blocks = [(qi, jnp.maximum(first_kb + qi - 1, 0), False, has_earlier if qi == 0 else None)
                   for qi in range(Q_PER_STEP)]
    runs, accs = key_blocks(diag_blocks + prev_blocks, runs, accs, items, finish=True)

    for qi in range(Q_PER_STEP):
        def sticks_left(runs):
            return jnp.min(functools.reduce(jnp.minimum, [runs[qi, hd] for hd in range(n_heads)])) < UNDERFLOW_LOG2

        def earlier_block(state, qi=qi, sticks_left=sticks_left):
            kb, _, runs, accs = state
            runs, accs = key_blocks([(qi, kb, False, None)], runs, accs)
            return kb - 1, sticks_left(runs), runs, accs

        first = first_kb + qi - 2
        mine = lambda d: {k: v for k, v in d.items() if k[0] == qi}

        @pl.when((first >= 0) & sticks_left(runs))
        def _(qi=qi, first=first, earlier_block=earlier_block, mine=mine):
            _, _, _, more = lax.while_loop(lambda st: (st[0] >= 0) & st[1], earlier_block,
                                           (first, True, mine(runs), mine(accs)))
            for p in range(n_pairs):
                gated_pair(more, qi, p)


def _fused_block(x, mod3, w, w_sp, b_sp_full, a_g, a_b, w_out, ln_g, ln_b, alpha):
    bsz, seq, d = x.shape
    d_a = a_g.shape[-1]
    d_b = (w.shape[1] - 3 * d_a) // 4
    per_seq = seq // ROWS
    n_total = bsz * per_seq
    grid = (n_total + 2,)

    def stage(lag):
        return lambda s: _stage_slabs(s, n_total, per_seq)[lag]

    def rows_of(lag):
        return pl.BlockSpec((1, ROWS, d), lambda s: (*stage(lag)(s), 0))

    def mod_of(lag):
        return pl.BlockSpec((1, 3, d), lambda s: (stage(lag)(s)[0], 0, 0))

    def whole(a):
        return pl.BlockSpec(a.shape, lambda s: (0,) * a.ndim)

    return pl.pallas_call(
        functools.partial(_block_kernel, alpha, n_total),
        grid=grid,
        in_specs=[rows_of(0), rows_of(2), mod_of(0), mod_of(2), whole(w), whole(w_sp),
                  whole(b_sp_full), whole(a_g), whole(a_b), whole(w_out), whole(ln_g), whole(ln_b)],
        out_specs=rows_of(2),
        out_shape=jax.ShapeDtypeStruct((bsz, seq, d), x.dtype),
        scratch_shapes=[
            pltpu.VMEM((2, seq, d_b), BF16),
            pltpu.VMEM((2, seq // TK, d_b, TK), BF16),
            pltpu.VMEM((2, ROWS, d_b), BF16),
            pltpu.VMEM((2, ROWS, d_b), F32),
            pltpu.VMEM((3, ROWS, d_a), BF16),
            pltpu.VMEM((2, ROWS, d_b), BF16),
            pltpu.VMEM((2, Q_PER_STEP * (d_b // GROUP_DIM), TK, TQ), F32),
            pltpu.VMEM((w.shape[1] // MXU_SLAB, ROWS, MXU_SLAB), F32),
            pltpu.VMEM((ROWS, d), F32),
        ],
        compiler_params=pltpu.CompilerParams(
            dimension_semantics=("arbitrary",), vmem_limit_bytes=VMEM_LIMIT),
        name="fused_block",
    )(x, x, mod3, mod3, w, w_sp, b_sp_full, a_g, a_b, w_out, ln_g, ln_b)


def kernel(x, c, w_ada, b_ada, w_in, sgu_ln_g, sgu_ln_b, w_spatial, b_spatial, w_out, ln_g, ln_b):
    depth = w_ada.shape[0]
    bsz, seq, d = x.shape
    d_a = sgu_ln_g.shape[-1]
    alpha = (2.0 * depth) ** 0.25
    for layer in range(depth):
        mod, w_in_bf, w_out_bf = _prepare(c, w_ada[layer], b_ada[layer], w_in[layer], w_out[layer])
        b_sp_full = jnp.repeat(b_spatial[layer].T, GROUP_DIM, axis=1)
        x = _fused_block(
            x, mod.reshape(bsz, 3, d), w_in_bf, w_spatial[layer], b_sp_full,
            sgu_ln_g[layer].reshape(1, d_a), sgu_ln_b[layer].reshape(1, d_a),
            w_out_bf, ln_g[layer].reshape(1, d), ln_b[layer].reshape(1, d), alpha)
    return x
```

```python
import functools
import math

import jax
import jax.numpy as jnp
from jax import lax
from jax.experimental import pallas as pl
from jax.experimental.pallas import tpu as pltpu

F32 = jnp.float32
BF16 = jnp.bfloat16

N_GROUPS = 8
GROUP_DIM = 64
CHUNK = 128
LN_EPS = 1e-5
LANES = 128

PREP_TILE = 512
TQ = 256
TK = 256
Q_PER_STEP = 1
ROWS = Q_PER_STEP * TQ
SKEW = (8, 3)
EXTRA_START = 1
MXU_SLAB = 512
FIN_DELAY = 1
UNDERFLOW_LOG2 = 152.0
VMEM_LIMIT = 48 * 1024 * 1024


def _dot(a, b):
    return jnp.dot(a, b, preferred_element_type=F32)


def _dot_nt(a, b):
    return lax.dot_general(a, b, (((1,), (1,)), ((), ())), preferred_element_type=F32)


def _gelu_tanh(x):
    k1 = -2.0 * math.sqrt(2.0 / math.pi) * math.log2(math.e)
    return x / (1.0 + jnp.exp2(x * (k1 + (k1 * 0.044715) * (x * x))))


def _silu(x):
    return x / (1.0 + jnp.exp(-x))


def _layer_norm(x, g, b):
    mu = jnp.mean(x, axis=-1, keepdims=True)
    xc = x - mu
    var = jnp.mean(xc * xc, axis=-1, keepdims=True)
    return xc * lax.rsqrt(var + LN_EPS) * g + b


def _prep_kernel(c_ref, wa_ref, ba_ref, win_ref, wout_ref, mod_ref, win_bf_ref, wout_bf_ref):
    sc = _silu(c_ref[...]).astype(BF16)
    mod_ref[...] = _dot(sc, wa_ref[...].astype(BF16)) + ba_ref[...]
    win_bf_ref[...] = win_ref[...].astype(BF16)
    wout_bf_ref[...] = wout_ref[...].astype(BF16)


def _prepare(c, w_ada, b_ada, w_in, w_out):
    bsz, d = c.shape
    n_mod = w_ada.shape[1]
    steps = w_in.shape[1] // PREP_TILE
    mod_tiles = n_mod // PREP_TILE
    out_tile = PREP_TILE
    out_tiles = w_out.shape[1] // out_tile
    assert mod_tiles <= steps and out_tiles <= steps
    mod_col = lambda j: (0, jnp.minimum(j, mod_tiles - 1))
    out_col = lambda j: (0, jnp.minimum(j, out_tiles - 1))
    return pl.pallas_call(
        _prep_kernel,
        grid=(steps,),
        in_specs=[
            pl.BlockSpec((bsz, d), lambda j: (0, 0)),
            pl.BlockSpec((d, PREP_TILE), mod_col),
            pl.BlockSpec((1, PREP_TILE), mod_col),
            pl.BlockSpec((w_in.shape[0], PREP_TILE), lambda j: (0, j)),
            pl.BlockSpec((w_out.shape[0], out_tile), out_col),
        ],
        out_specs=[
            pl.BlockSpec((bsz, PREP_TILE), mod_col),
            pl.BlockSpec((w_in.shape[0], PREP_TILE), lambda j: (0, j)),
            pl.BlockSpec((w_out.shape[0], out_tile), out_col),
        ],
        out_shape=[
            jax.ShapeDtypeStruct((bsz, n_mod), F32),
            jax.ShapeDtypeStruct(w_in.shape, BF16),
            jax.ShapeDtypeStruct(w_out.shape, BF16),
        ],
        compiler_params=pltpu.CompilerParams(
            dimension_semantics=("arbitrary",), vmem_limit_bytes=VMEM_LIMIT),
        name="adaln_mod_and_casts",
    )(c, w_ada, b_ada.reshape(1, n_mod), w_in, w_out)


def _stage_slabs(s, n_total, per_seq):
    last = n_total - 1
    out = []
    for lag in range(3):
        t = jnp.clip(s - lag, 0, last)
        out.append((lax.div(t, per_seq), lax.rem(t, per_seq)))
    return out


def _block_kernel(alpha, n_total,
                  x_in_ref, x_out_ref, mod_in_ref, mod_out_ref, w_ref, wsp_ref, bsp_ref,
                  ag_ref, ab_ref, wo_ref, g_ref, b_ref, o_ref,
                  k_scr, vT_scr, q_ring, gz_ring, ya_ring, yb_ring, z_ring, land_scr, y_scr):
    s = pl.program_id(0)
    d_b = q_ring.shape[-1]
    d_a = ya_ring.shape[-1]
    per_seq = k_scr.shape[1] // ROWS
    (b_in, j_in), (b_at, j_at), _ = _stage_slabs(s, n_total, per_seq)
    par_in, par_at = lax.rem(b_in, 2), lax.rem(b_at, 2)
    n_heads = d_b // GROUP_DIM
    heads_per_tile = LANES // GROUP_DIM
    n_pairs = n_heads // heads_per_tile
    now2, prev2 = lax.rem(s, 2), lax.rem(s + 1, 2)
    now3, prev3 = lax.rem(s, 3), lax.rem(s + 1, 3)

    @pl.when(s == 0)
    def _():
        q_ring[...] = jnp.zeros_like(q_ring)
        gz_ring[...] = jnp.zeros_like(gz_ring)
        ya_ring[...] = jnp.zeros_like(ya_ring)
        yb_ring[...] = jnp.zeros_like(yb_ring)
        z_ring[...] = jnp.zeros_like(z_ring)
        k_scr[0, 0:ROWS, :] = jnp.zeros((ROWS, d_b), BF16)
        for n in range(Q_PER_STEP):
            vT_scr[0, n] = jnp.zeros((d_b, TK), BF16)

    new = {}

    def item_h():
        shift = mod_in_ref[0, 0:1, :]
        scale = mod_in_ref[0, 1:2, :]
        new["h"] = (x_in_ref[0] * (1.0 + scale) + shift).astype(BF16)

    def h():
        return new["h"]

    def ycat():
        return jnp.concatenate([ya_ring[prev3], yb_ring[prev2]], axis=1)

    def both(*fs):
        return lambda: [f() for f in fs]

    def fin_out():
        gate = mod_out_ref[0, 2:3, :]
        r = alpha * x_out_ref[0] + gate * y_scr[...]
        o_ref[0] = _layer_norm(r, g_ref[...], b_ref[...])

    def fin_u(p):
        new["ug"] = _gelu_tanh(p())

    def fin_v(p):
        new["vn"] = _layer_norm(_gelu_tanh(p()), ag_ref[...], ab_ref[...]).astype(BF16)
        t_idx = lax.broadcasted_iota(jnp.int32, (CHUNK, CHUNK), 0)
        s_idx = lax.broadcasted_iota(jnp.int32, (CHUNK, CHUNK), 1)
        causal = t_idx >= s_idx
        w_sp = [jnp.where(causal, wsp_ref[g], 0.0).astype(BF16) for g in range(N_GROUPS)]
        first_group = lax.broadcasted_iota(jnp.int32, (CHUNK, LANES), 1) < GROUP_DIM
        chunks = []
        for c in range(ROWS // CHUNK):
            rows = slice(c * CHUNK, (c + 1) * CHUNK)
            pieces = []
            for pr in range(d_a // LANES):
                cols = slice(pr * LANES, (pr + 1) * LANES)
                vp = new["vn"][rows, cols]
                mixed = jnp.where(first_group, _dot(w_sp[2 * pr], vp), _dot(w_sp[2 * pr + 1], vp))
                pieces.append(new["ug"][rows, cols] * (mixed + bsp_ref[:, cols]))
            chunks.append(jnp.concatenate(pieces, axis=1))
        new["ya_pre"] = jnp.concatenate(chunks, axis=0)

    def fin_za(p):
        ya_ring[now3] = (_silu(p()) * new["ya_pre"]).astype(BF16)

    scale_q = math.log2(math.e) / math.sqrt(GROUP_DIM)

    def fin_q(p):
        new["q"] = (p() * scale_q).astype(BF16)
        q_ring[now2] = new["q"]

    def fin_k(p):
        new["k"] = p().astype(BF16)
        k_scr[par_in, pl.ds(pl.multiple_of(j_in * ROWS, ROWS), ROWS), :] = new["k"]

    def head_of(q_pair, hh):
        in_head = (lane >= hh * GROUP_DIM) & (lane < (hh + 1) * GROUP_DIM)
        return jnp.where(in_head, q_pair, jnp.zeros_like(q_pair))

    def pre_scores():
        for qi in range(Q_PER_STEP):
            rows = slice(qi * TQ, (qi + 1) * TQ)
            for hd in range(n_heads):
                p, hh = divmod(hd, heads_per_tile)
                lanes = slice(p * LANES, (p + 1) * LANES)
                z_ring[now2, qi * n_heads + hd] = _dot_nt(
                    new["k"][rows, lanes], head_of(new["q"][rows, lanes], hh))

    def fin_vt(p):
        v = p()
        for n in range(Q_PER_STEP):
            vT_scr[par_in, j_in * Q_PER_STEP + n] = v[n * TK:(n + 1) * TK, :].T.astype(BF16)

    def fin_gz(p):
        gz_ring[now2] = _silu(p())

    dots, due = [], {}

    def out_slab(j):
        cols = slice(j * MXU_SLAB, (j + 1) * MXU_SLAB)

        def item():
            y_scr[:, cols] = _dot(ycat(), wo_ref[:, cols])
        return item

    for j in range(wo_ref.shape[1] // MXU_SLAB):
        dots.append(out_slab(j))
    due.setdefault(len(dots) - 1 + FIN_DELAY, []).append(fin_out)

    o = 3 * d_a
    projections = [(0, fin_u), (d_a, fin_v), (2 * d_a, fin_za), (o, fin_q), (o + d_b, fin_k),
                   (o + 2 * d_b, fin_vt), (o + 3 * d_b, fin_gz)]
    assert d_a == d_b == MXU_SLAB

    assert land_scr.shape[0] == len(projections)

    def in_slab(n, lo):
        def item():
            land_scr[n] = _dot(h(), w_ref[:, lo:lo + MXU_SLAB])
        return item

    def landed(n):
        return lambda: land_scr[n]

    for n, (lo, fin) in enumerate(projections):
        dots.append(in_slab(n, lo))
        due.setdefault(len(dots) - 1 + FIN_DELAY, []).append(functools.partial(fin, landed(n)))
    items = [item_h]
    for n, dot_item in enumerate(dots):
        items.append(both(dot_item, *due.pop(n, [])))
    items.append(both(*[f for n in sorted(due) for f in due[n]], pre_scores))

    lane = lax.broadcasted_iota(jnp.int32, (TQ, LANES), 1)
    half = TK // 2
    s_idx = lax.broadcasted_iota(jnp.int32, (half, half), 0)
    t_idx = lax.broadcasted_iota(jnp.int32, (half, half), 1)
    strictly_earlier_half = s_idx < t_idx
    j_idx = lax.broadcasted_iota(jnp.int32, (TK, TK), 1)
    r_idx = lax.broadcasted_iota(jnp.int32, (TK, TK), 0)
    strictly_later = jnp.where(j_idx > r_idx, 1.0, 0.0).astype(BF16)

    q_masked = {}
    for qi in range(Q_PER_STEP):
        for hd in range(n_heads):
            p, hh = divmod(hd, heads_per_tile)
            q_pair = q_ring[prev2, qi * TQ:(qi + 1) * TQ, p * LANES:(p + 1) * LANES]
            q_masked[qi, hd] = head_of(q_pair, hh)

    lo_half, hi_half = slice(0, half), slice(half, TK)
    diag_parts = [(lo_half, lo_half, "tri"), (lo_half, hi_half, "full"),
                  (hi_half, lo_half, "empty"), (hi_half, hi_half, "tri")]
    full_parts = [(slice(0, TK), slice(0, TQ), "full")]

    def assemble(parts, pieces):
        if len(parts) == 1:
            return pieces[0]
        return jnp.concatenate([jnp.concatenate(pieces[0:2], axis=1),
                                jnp.concatenate(pieces[2:4], axis=1)], axis=0)

    def gated_pair(accs, qi, p):
        rows, cols = slice(qi * TQ, (qi + 1) * TQ), slice(p * LANES, (p + 1) * LANES)
        yT = jnp.concatenate([accs[qi, heads_per_tile * p + n] for n in range(heads_per_tile)], axis=0)
        yb_ring[now2, rows, cols] = (yT.T * gz_ring[prev2, rows, cols]).astype(BF16)

    def key_blocks(blocks, runs, accs, extra=(), finish=False):
        runs, accs = dict(runs), dict(accs)
        tiles = [dict(qi=qi, kb=kb, h=hd, keep=keep, diag=diag, parts=diag_parts if diag else full_parts)
                 for qi, kb, diag, keep in blocks for hd in range(n_heads)]
        last_tile = {(t["qi"], t["h"]): n for n, t in enumerate(tiles)}

        def scores(t):
            if t["diag"]:
                t["z"] = z_ring[prev2, t["qi"] * n_heads + t["h"]]
                return
            start = pl.multiple_of(t["kb"] * TK, TK)
            p = t["h"] // heads_per_tile
            lanes = slice(p * LANES, (p + 1) * LANES)
            t["z"] = _dot_nt(k_scr[par_at, pl.ds(start, TK), lanes], q_masked[t["qi"], t["h"]])

        def softplus(t):
            sps, t["logb"] = [], []
            for rows, cols, kind in t["parts"]:
                if kind == "empty":
                    sps.append(jnp.zeros((half, half), BF16))
                    t["logb"].append(None)
                    continue
                z = t["z"][rows, cols]
                sp = jnp.maximum(z, 0.0) + jnp.log2(1.0 + jnp.exp2(-jnp.abs(z)))
                t["logb"].append(z - sp)
                if kind == "tri":
                    sp = jnp.where(strictly_earlier_half, sp, 0.0)
                sps.append(sp.astype(BF16))
            t["sp"] = assemble(t["parts"], sps)

        def later_sum(t):
            t["csum"] = _dot(strictly_later, t["sp"])

        def weights(t):
            key = (t["qi"], t["h"])
            run = runs[key]
            probs = []
            for (rows, cols, kind), logb in zip(t["parts"], t["logb"]):
                if kind == "empty":
                    probs.append(jnp.zeros((half, half), BF16))
                    continue
                a = jnp.exp2(logb - (run[:, cols] + t["csum"][rows, cols]))
                if kind == "tri":
                    a = jnp.where(strictly_earlier_half, a, 0.0)
                probs.append(a.astype(BF16))
            t["prob"] = assemble(t["parts"], probs)
            runs[key] = run + (t["csum"][0:1, :] + t["sp"][0:1, :].astype(F32))

        def values(n, t):
            key = (t["qi"], t["h"])
            hd = t["h"]
            v_blk = vT_scr[par_at, t["kb"], hd * GROUP_DIM:(hd + 1) * GROUP_DIM, :]
            if t["keep"] is not None:
                v_blk = (v_blk.astype(F32) * t["keep"]).astype(BF16)
            accs[key] = accs[key] + _dot(v_blk, t["prob"])
            if finish and last_tile[key] == n and hd % heads_per_tile == heads_per_tile - 1:
                gated_pair(accs, t["qi"], hd // heads_per_tile)

        n_tiles = len(tiles)
        lead, lag = SKEW
        ready = next((n for n, t in enumerate(tiles) if not t["diag"]), n_tiles)
        for t in tiles[:ready]:
            scores(t)
        steps = range(min(0, ready - lead), n_tiles + lag)
        extra = list(extra)
        slots = [EXTRA_START + (n * (len(steps) - EXTRA_START)) // max(1, len(extra))
                 for n in range(len(extra))]
        for count, step in enumerate(steps):
            while extra and slots[0] <= count:
                slots.pop(0)
                extra.pop(0)()
            if ready <= step + lead < n_tiles:
                scores(tiles[step + lead])
            if 0 <= step < n_tiles:
                softplus(tiles[step])
                later_sum(tiles[step])
            if 0 <= step - lag < n_tiles:
                weights(tiles[step - lag])
                values(step - lag, tiles[step - lag])
        for item in extra:
            item()
        return runs, accs

    runs = {(qi, hd): jnp.zeros((1, TQ), F32) for qi in range(Q_PER_STEP) for hd in range(n_heads)}
    accs = {(qi, hd): jnp.zeros((GROUP_DIM, TQ), F32) for qi in range(Q_PER_STEP) for hd in range(n_heads)}
    first_kb = j_at * Q_PER_STEP
    diag_blocks = [(qi, first_kb + qi, True, None) for qi in range(Q_PER_STEP)]
    has_earlier = (j_at > 0).astype(F32)
    prev_blocks = [(qi, jnp.maximum(first_kb + qi - 1, 0), False, has_earlier if qi == 0 else None)
                   for qi in range(Q_PER_STEP)]
    runs, accs = key_blocks(diag_blocks + prev_blocks, runs, accs, items, finish=True)

    for qi in range(Q_PER_STEP):
        def sticks_left(runs):
            return jnp.min(functools.reduce(jnp.minimum, [runs[qi, hd] for hd in range(n_heads)])) < UNDERFLOW_LOG2

        def earlier_block(state, qi=qi, sticks_left=sticks_left):
            kb, _, runs, accs = state
            runs, accs = key_blocks([(qi, kb, False, None)], runs, accs)
            return kb - 1, sticks_left(runs), runs, accs

        first = first_kb + qi - 2
        mine = lambda d: {k: v for k, v in d.items() if k[0] == qi}

        @pl.when((first >= 0) & sticks_left(runs))
        def _(qi=qi, first=first, earlier_block=earlier_block, mine=mine):
            _, _, _, more = lax.while_loop(lambda st: (st[0] >= 0) & st[1], earlier_block,
                                           (first, True, mine(runs), mine(accs)))
            for p in range(n_pairs):
                gated_pair(more, qi, p)


def _fused_block(x, mod3, w, w_sp, b_sp_full, a_g, a_b, w_out, ln_g, ln_b, alpha):
    bsz, seq, d = x.shape
    d_a = a_g.shape[-1]
    d_b = (w.shape[1] - 3 * d_a) // 4
    per_seq = seq // ROWS
    n_total = bsz * per_seq
    grid = (n_total + 2,)

    def stage(lag):
        return lambda s: _stage_slabs(s, n_total, per_seq)[lag]

    def rows_of(lag):
        return pl.BlockSpec((1, ROWS, d), lambda s: (*stage(lag)(s), 0))

    def mod_of(lag):
        return pl.BlockSpec((1, 3, d), lambda s: (stage(lag)(s)[0], 0, 0))

    def whole(a):
        return pl.BlockSpec(a.shape, lambda s: (0,) * a.ndim)

    return pl.pallas_call(
        functools.partial(_block_kernel, alpha, n_total),
        grid=grid,
        in_specs=[rows_of(0), rows_of(2), mod_of(0), mod_of(2), whole(w), whole(w_sp),
                  whole(b_sp_full), whole(a_g), whole(a_b), whole(w_out), whole(ln_g), whole(ln_b)],
        out_specs=rows_of(2),
        out_shape=jax.ShapeDtypeStruct((bsz, seq, d), x.dtype),
        scratch_shapes=[
            pltpu.VMEM((2, seq, d_b), BF16),
            pltpu.VMEM((2, seq // TK, d_b, TK), BF16),
            pltpu.VMEM((2, ROWS, d_b), BF16),
            pltpu.VMEM((2, ROWS, d_b), F32),
            pltpu.VMEM((3, ROWS, d_a), BF16),
            pltpu.VMEM((2, ROWS, d_b), BF16),
            pltpu.VMEM((2, Q_PER_STEP * (d_b // GROUP_DIM), TK, TQ), F32),
            pltpu.VMEM((w.shape[1] // MXU_SLAB, ROWS, MXU_SLAB), F32),
            pltpu.VMEM((ROWS, d), F32),
        ],
        compiler_params=pltpu.CompilerParams(
            dimension_semantics=("arbitrary",), vmem_limit_bytes=VMEM_LIMIT),
        name="fused_block",
    )(x, x, mod3, mod3, w, w_sp, b_sp_full, a_g, a_b, w_out, ln_g, ln_b)


def kernel(x, c, w_ada, b_ada, w_in, sgu_ln_g, sgu_ln_b, w_spatial, b_spatial, w_out, ln_g, ln_b):
    depth = w_ada.shape[0]
    bsz, seq, d = x.shape
    d_a = sgu_ln_g.shape[-1]
    alpha = (2.0 * depth) ** 0.25
    for layer in range(depth):
        mod, w_in_bf, w_out_bf = _prepare(c, w_ada[layer], b_ada[layer], w_in[layer], w_out[layer])
        b_sp_full = jnp.repeat(b_spatial[layer].T, GROUP_DIM, axis=1)
        x = _fused_block(
            x, mod.reshape(bsz, 3, d), w_in_bf, w_spatial[layer], b_sp_full,
            sgu_ln_g[layer].reshape(1, d_a), sgu_ln_b[layer].reshape(1, d_a),
            w_out_bf, ln_g[layer].reshape(1, d), ln_b[layer].reshape(1, d), alpha)
    return x
```

```python
import functools
import math

import jax
import jax.numpy as jnp
from jax import lax
from jax.experimental import pallas as pl
from jax.experimental.pallas import tpu as pltpu

F32 = jnp.float32
BF16 = jnp.bfloat16

N_GROUPS = 8
GROUP_DIM = 64
CHUNK = 128
LN_EPS = 1e-5
LANES = 128

PREP_TILE = 512
TQ = 256
TK = 256
Q_PER_STEP = 1
ROWS = Q_PER_STEP * TQ
SKEW = (8, 2)
EXTRA_START = 1
MXU_SLAB = 512
FIN_DELAY = 1
UNDERFLOW_LOG2 = 152.0
VMEM_LIMIT = 48 * 1024 * 1024


def _dot(a, b):
    return jnp.dot(a, b, preferred_element_type=F32)


def _dot_nt(a, b):
    return lax.dot_general(a, b, (((1,), (1,)), ((), ())), preferred_element_type=F32)


def _gelu_tanh(x):
    k1 = -2.0 * math.sqrt(2.0 / math.pi) * math.log2(math.e)
    return x / (1.0 + jnp.exp2(x * (k1 + (k1 * 0.044715) * (x * x))))


def _silu(x):
    return x / (1.0 + jnp.exp(-x))


def _layer_norm(x, g, b):
    mu = jnp.mean(x, axis=-1, keepdims=True)
    xc = x - mu
    var = jnp.mean(xc * xc, axis=-1, keepdims=True)
    return xc * lax.rsqrt(var + LN_EPS) * g + b


def _prep_kernel(c_ref, wa_ref, ba_ref, win_ref, wout_ref, mod_ref, win_bf_ref, wout_bf_ref):
    sc = _silu(c_ref[...]).astype(BF16)
    mod_ref[...] = _dot(sc, wa_ref[...].astype(BF16)) + ba_ref[...]
    win_bf_ref[...] = win_ref[...].astype(BF16)
    wout_bf_ref[...] = wout_ref[...].astype(BF16)


def _prepare(c, w_ada, b_ada, w_in, w_out):
    bsz, d = c.shape
    n_mod = w_ada.shape[1]
    steps = w_in.shape[1] // PREP_TILE
    mod_tiles = n_mod // PREP_TILE
    out_tile = PREP_TILE
    out_tiles = w_out.shape[1] // out_tile
    assert mod_tiles <= steps and out_tiles <= steps
    mod_col = lambda j: (0, jnp.minimum(j, mod_tiles - 1))
    out_col = lambda j: (0, jnp.minimum(j, out_tiles - 1))
    return pl.pallas_call(
        _prep_kernel,
        grid=(steps,),
        in_specs=[
            pl.BlockSpec((bsz, d), lambda j: (0, 0)),
            pl.BlockSpec((d, PREP_TILE), mod_col),
            pl.BlockSpec((1, PREP_TILE), mod_col),
            pl.BlockSpec((w_in.shape[0], PREP_TILE), lambda j: (0, j)),
            pl.BlockSpec((w_out.shape[0], out_tile), out_col),
        ],
        out_specs=[
            pl.BlockSpec((bsz, PREP_TILE), mod_col),
            pl.BlockSpec((w_in.shape[0], PREP_TILE), lambda j: (0, j)),
            pl.BlockSpec((w_out.shape[0], out_tile), out_col),
        ],
        out_shape=[
            jax.ShapeDtypeStruct((bsz, n_mod), F32),
            jax.ShapeDtypeStruct(w_in.shape, BF16),
            jax.ShapeDtypeStruct(w_out.shape, BF16),
        ],
        compiler_params=pltpu.CompilerParams(
            dimension_semantics=("arbitrary",), vmem_limit_bytes=VMEM_LIMIT),
        name="adaln_mod_and_casts",
    )(c, w_ada, b_ada.reshape(1, n_mod), w_in, w_out)


def _stage_slabs(s, n_total, per_seq):
    last = n_total - 1
    out = []
    for lag in range(3):
        t = jnp.clip(s - lag, 0, last)
        out.append((lax.div(t, per_seq), lax.rem(t, per_seq)))
    return out


def _block_kernel(alpha, n_total,
                  x_in_ref, x_out_ref, mod_in_ref, mod_out_ref, w_ref, wsp_ref, bsp_ref,
                  ag_ref, ab_ref, wo_ref, g_ref, b_ref, o_ref,
                  k_scr, vT_scr, q_ring, gz_ring, ya_ring, yb_ring, z_ring, land_scr, y_scr):
    s = pl.program_id(0)
    d_b = q_ring.shape[-1]
    d_a = ya_ring.shape[-1]
    per_seq = k_scr.shape[1] // ROWS
    (b_in, j_in), (b_at, j_at), _ = _stage_slabs(s, n_total, per_seq)
    par_in, par_at = lax.rem(b_in, 2), lax.rem(b_at, 2)
    n_heads = d_b // GROUP_DIM
    heads_per_tile = LANES // GROUP_DIM
    n_pairs = n_heads // heads_per_tile
    now2, prev2 = lax.rem(s, 2), lax.rem(s + 1, 2)
    now3, prev3 = lax.rem(s, 3), lax.rem(s + 1, 3)

    @pl.when(s == 0)
    def _():
        q_ring[...] = jnp.zeros_like(q_ring)
        gz_ring[...] = jnp.zeros_like(gz_ring)
        ya_ring[...] = jnp.zeros_like(ya_ring)
        yb_ring[...] = jnp.zeros_like(yb_ring)
        z_ring[...] = jnp.zeros_like(z_ring)
        k_scr[0, 0:ROWS, :] = jnp.zeros((ROWS, d_b), BF16)
        for n in range(Q_PER_STEP):
            vT_scr[0, n] = jnp.zeros((d_b, TK), BF16)

    new = {}

    def item_h():
        shift = mod_in_ref[0, 0:1, :]
        scale = mod_in_ref[0, 1:2, :]
        new["h"] = (x_in_ref[0] * (1.0 + scale) + shift).astype(BF16)

    def h():
        return new["h"]

    def ycat():
        return jnp.concatenate([ya_ring[prev3], yb_ring[prev2]], axis=1)

    def both(*fs):
        return lambda: [f() for f in fs]

    def fin_out():
        gate = mod_out_ref[0, 2:3, :]
        r = alpha * x_out_ref[0] + gate * y_scr[...]
        o_ref[0] = _layer_norm(r, g_ref[...], b_ref[...])

    def fin_u(p):
        new["ug"] = _gelu_tanh(p())

    def fin_v(p):
        new["vn"] = _layer_norm(_gelu_tanh(p()), ag_ref[...], ab_ref[...]).astype(BF16)
        t_idx = lax.broadcasted_iota(jnp.int32, (CHUNK, CHUNK), 0)
        s_idx = lax.broadcasted_iota(jnp.int32, (CHUNK, CHUNK), 1)
        causal = t_idx >= s_idx
        w_sp = [jnp.where(causal, wsp_ref[g], 0.0).astype(BF16) for g in range(N_GROUPS)]
        first_group = lax.broadcasted_iota(jnp.int32, (CHUNK, LANES), 1) < GROUP_DIM
        chunks = []
        for c in range(ROWS // CHUNK):
            rows = slice(c * CHUNK, (c + 1) * CHUNK)
            pieces = []
            for pr in range(d_a // LANES):
                cols = slice(pr * LANES, (pr + 1) * LANES)
                vp = new["vn"][rows, cols]
                mixed = jnp.where(first_group, _dot(w_sp[2 * pr], vp), _dot(w_sp[2 * pr + 1], vp))
                pieces.append(new["ug"][rows, cols] * (mixed + bsp_ref[:, cols]))
            chunks.append(jnp.concatenate(pieces, axis=1))
        new["ya_pre"] = jnp.concatenate(chunks, axis=0)

    def fin_za(p):
        ya_ring[now3] = (_silu(p()) * new["ya_pre"]).astype(BF16)

    scale_q = math.log2(math.e) / math.sqrt(GROUP_DIM)

    def fin_q(p):
        new["q"] = (p() * scale_q).astype(BF16)
        q_ring[now2] = new["q"]

    def fin_k(p):
        new["k"] = p().astype(BF16)
        k_scr[par_in, pl.ds(pl.multiple_of(j_in * ROWS, ROWS), ROWS), :] = new["k"]

    def head_of(q_pair, hh):
        in_head = (lane >= hh * GROUP_DIM) & (lane < (hh + 1) * GROUP_DIM)
        return jnp.where(in_head, q_pair, jnp.zeros_like(q_pair))

    def pre_scores():
        for qi in range(Q_PER_STEP):
            rows = slice(qi * TQ, (qi + 1) * TQ)
            for hd in range(n_heads):
                p, hh = divmod(hd, heads_per_tile)
                lanes = slice(p * LANES, (p + 1) * LANES)
                z_ring[now2, qi * n_heads + hd] = _dot_nt(
                    new["k"][rows, lanes], head_of(new["q"][rows, lanes], hh))

    def fin_vt(p):
        v = p()
        for n in range(Q_PER_STEP):
            vT_scr[par_in, j_in * Q_PER_STEP + n] = v[n * TK:(n + 1) * TK, :].T.astype(BF16)

    def fin_gz(p):
        gz_ring[now2] = _silu(p())

    dots, due = [], {}

    def out_slab(j):
        cols = slice(j * MXU_SLAB, (j + 1) * MXU_SLAB)

        def item():
            y_scr[:, cols] = _dot(ycat(), wo_ref[:, cols])
        return item

    for j in range(wo_ref.shape[1] // MXU_SLAB):
        dots.append(out_slab(j))
    due.setdefault(len(dots) - 1 + FIN_DELAY, []).append(fin_out)

    o = 3 * d_a
    projections = [(0, fin_u), (d_a, fin_v), (2 * d_a, fin_za), (o, fin_q), (o + d_b, fin_k),
                   (o + 2 * d_b, fin_vt), (o + 3 * d_b, fin_gz)]
    assert d_a == d_b == MXU_SLAB

    assert land_scr.shape[0] == len(projections)

    def in_slab(n, lo):
        def item():
            land_scr[n] = _dot(h(), w_ref[:, lo:lo + MXU_SLAB])
        return item

    def landed(n):
        return lambda: land_scr[n]

    for n, (lo, fin) in enumerate(projections):
        dots.append(in_slab(n, lo))
        due.setdefault(len(dots) - 1 + FIN_DELAY, []).append(functools.partial(fin, landed(n)))
    items = [item_h]
    for n, dot_item in enumerate(dots):
        items.append(both(dot_item, *due.pop(n, [])))
    items.append(both(*[f for n in sorted(due) for f in due[n]], pre_scores))

    lane = lax.broadcasted_iota(jnp.int32, (TQ, LANES), 1)
    half = TK // 2
    s_idx = lax.broadcasted_iota(jnp.int32, (half, half), 0)
    t_idx = lax.broadcasted_iota(jnp.int32, (half, half), 1)
    strictly_earlier_half = s_idx < t_idx
    j_idx = lax.broadcasted_iota(jnp.int32, (TK, TK), 1)
    r_idx = lax.broadcasted_iota(jnp.int32, (TK, TK), 0)
    strictly_later = jnp.where(j_idx > r_idx, 1.0, 0.0).astype(BF16)

    q_masked = {}
    for qi in range(Q_PER_STEP):
        for hd in range(n_heads):
            p, hh = divmod(hd, heads_per_tile)
            q_pair = q_ring[prev2, qi * TQ:(qi + 1) * TQ, p * LANES:(p + 1) * LANES]
            q_masked[qi, hd] = head_of(q_pair, hh)

    lo_half, hi_half = slice(0, half), slice(half, TK)
    diag_parts = [(lo_half, lo_half, "tri"), (lo_half, hi_half, "full"),
                  (hi_half, lo_half, "empty"), (hi_half, hi_half, "tri")]
    full_parts = [(slice(0, TK), slice(0, TQ), "full")]

    def assemble(parts, pieces):
        if len(parts) == 1:
            return pieces[0]
        return jnp.concatenate([jnp.concatenate(pieces[0:2], axis=1),
                                jnp.concatenate(pieces[2:4], axis=1)], axis=0)

    def gated_pair(accs, qi, p):
        rows, cols = slice(qi * TQ, (qi + 1) * TQ), slice(p * LANES, (p + 1) * LANES)
        yT = jnp.concatenate([accs[qi, heads_per_tile * p + n] for n in range(heads_per_tile)], axis=0)
        yb_ring[now2, rows, cols] = (yT.T * gz_ring[prev2, rows, cols]).astype(BF16)

    def key_blocks(blocks, runs, accs, extra=(), finish=False):
        runs, accs = dict(runs), dict(accs)
        tiles = [dict(qi=qi, kb=kb, h=hd, keep=keep, diag=diag, parts=diag_parts if diag else full_parts)
                 for qi, kb, diag, keep in blocks for hd in range(n_heads)]
        last_tile = {(t["qi"], t["h"]): n for n, t in enumerate(tiles)}

        def scores(t):
            if t["diag"]:
                t["z"] = z_ring[prev2, t["qi"] * n_heads + t["h"]]
                return
            start = pl.multiple_of(t["kb"] * TK, TK)
            p = t["h"] // heads_per_tile
            lanes = slice(p * LANES, (p + 1) * LANES)
            t["z"] = _dot_nt(k_scr[par_at, pl.ds(start, TK), lanes], q_masked[t["qi"], t["h"]])

        def softplus(t):
            sps, t["logb"] = [], []
            for rows, cols, kind in t["parts"]:
                if kind == "empty":
                    sps.append(jnp.zeros((half, half), BF16))
                    t["logb"].append(None)
                    continue
                z = t["z"][rows, cols]
                sp = jnp.maximum(z, 0.0) + jnp.log2(1.0 + jnp.exp2(-jnp.abs(z)))
                t["logb"].append(z - sp)
                if kind == "tri":
                    sp = jnp.where(strictly_earlier_half, sp, 0.0)
                sps.append(sp.astype(BF16))
            t["sp"] = assemble(t["parts"], sps)

        def later_sum(t):
            t["csum"] = _dot(strictly_later, t["sp"])

        def weights(t):
            key = (t["qi"], t["h"])
            run = runs[key]
            probs = []
            for (rows, cols, kind), logb in zip(t["parts"], t["logb"]):
                if kind == "empty":
                    probs.append(jnp.zeros((half, half), BF16))
                    continue
                a = jnp.exp2(logb - (run[:, cols] + t["csum"][rows, cols]))
                if kind == "tri":
                    a = jnp.where(strictly_earlier_half, a, 0.0)
                probs.append(a.astype(BF16))
            t["prob"] = assemble(t["parts"], probs)
            runs[key] = run + (t["csum"][0:1, :] + t["sp"][0:1, :].astype(F32))

        def values(n, t):
            key = (t["qi"], t["h"])
            hd = t["h"]
            v_blk = vT_scr[par_at, t["kb"], hd * GROUP_DIM:(hd + 1) * GROUP_DIM, :]
            if t["keep"] is not None:
                v_blk = (v_blk.astype(F32) * t["keep"]).astype(BF16)
            accs[key] = accs[key] + _dot(v_blk, t["prob"])
            if finish and last_tile[key] == n and hd % heads_per_tile == heads_per_tile - 1:
                gated_pair(accs, t["qi"], hd // heads_per_tile)

        n_tiles = len(tiles)
        lead, lag = SKEW
        ready = next((n for n, t in enumerate(tiles) if not t["diag"]), n_tiles)
        for t in tiles[:ready]:
            scores(t)
        steps = range(min(0, ready - lead), n_tiles + lag)
        extra = list(extra)
        slots = [EXTRA_START + (n * (len(steps) - EXTRA_START)) // max(1, len(extra))
                 for n in range(len(extra))]
        for count, step in enumerate(steps):
            while extra and slots[0] <= count:
                slots.pop(0)
                extra.pop(0)()
            if ready <= step + lead < n_tiles:
                scores(tiles[step + lead])
            if 0 <= step < n_tiles:
                softplus(tiles[step])
                later_sum(tiles[step])
            if 0 <= step - lag < n_tiles:
                weights(tiles[step - lag])
                values(step - lag, tiles[step - lag])
        for item in extra:
            item()
        return runs, accs

    runs = {(qi, hd): jnp.zeros((1, TQ), F32) for qi in range(Q_PER_STEP) for hd in range(n_heads)}
    accs = {(qi, hd): jnp.zeros((GROUP_DIM, TQ), F32) for qi in range(Q_PER_STEP) for hd in range(n_heads)}
    first_kb = j_at * Q_PER_STEP
    diag_blocks = [(qi, first_kb + qi, True, None) for qi in range(Q_PER_STEP)]
    has_earlier = (j_at > 0).astype(F32)
    prev_blocks = [(qi, jnp.maximum(first_kb + qi - 1, 0), False, has_earlier if qi == 0 else None)
                   for qi in range(Q_PER_STEP)]
    runs, accs = key_blocks(diag_blocks + prev_blocks, runs, accs, items, finish=True)

    for qi in range(Q_PER_STEP):
        def sticks_left(runs):
            return jnp.min(functools.reduce(jnp.minimum, [runs[qi, hd] for hd in range(n_heads)])) < UNDERFLOW_LOG2

        def earlier_block(state, qi=qi, sticks_left=sticks_left):
            kb, _, runs, accs = state
            runs, accs = key_blocks([(qi, kb, False, None)], runs, accs)
            return kb - 1, sticks_left(runs), runs, accs

        first = first_kb + qi - 2
        mine = lambda d: {k: v for k, v in d.items() if k[0] == qi}

        @pl.when((first >= 0) & sticks_left(runs))
        def _(qi=qi, first=first, earlier_block=earlier_block, mine=mine):
            _, _, _, more = lax.while_loop(lambda st: (st[0] >= 0) & st[1], earlier_block,
                                           (first, True, mine(runs), mine(accs)))
            for p in range(n_pairs):
                gated_pair(more, qi, p)


def _fused_block(x, mod3, w, w_sp, b_sp_full, a_g, a_b, w_out, ln_g, ln_b, alpha):
    bsz, seq, d = x.shape
    d_a = a_g.shape[-1]
    d_b = (w.shape[1] - 3 * d_a) // 4
    per_seq = seq // ROWS
    n_total = bsz * per_seq
    grid = (n_total + 2,)

    def stage(lag):
        return lambda s: _stage_slabs(s, n_total, per_seq)[lag]

    def rows_of(lag):
        return pl.BlockSpec((1, ROWS, d), lambda s: (*stage(lag)(s), 0))

    def mod_of(lag):
        return pl.BlockSpec((1, 3, d), lambda s: (stage(lag)(s)[0], 0, 0))

    def whole(a):
        return pl.BlockSpec(a.shape, lambda s: (0,) * a.ndim)

    return pl.pallas_call(
        functools.partial(_block_kernel, alpha, n_total),
        grid=grid,
        in_specs=[rows_of(0), rows_of(2), mod_of(0), mod_of(2), whole(w), whole(w_sp),
                  whole(b_sp_full), whole(a_g), whole(a_b), whole(w_out), whole(ln_g), whole(ln_b)],
        out_specs=rows_of(2),
        out_shape=jax.ShapeDtypeStruct((bsz, seq, d), x.dtype),
        scratch_shapes=[
            pltpu.VMEM((2, seq, d_b), BF16),
            pltpu.VMEM((2, seq // TK, d_b, TK), BF16),
            pltpu.VMEM((2, ROWS, d_b), BF16),
            pltpu.VMEM((2, ROWS, d_b), F32),
            pltpu.VMEM((3, ROWS, d_a), BF16),
            pltpu.VMEM((2, ROWS, d_b), BF16),
            pltpu.VMEM((2, Q_PER_STEP * (d_b // GROUP_DIM), TK, TQ), F32),
            pltpu.VMEM((w.shape[1] // MXU_SLAB, ROWS, MXU_SLAB), F32),
            pltpu.VMEM((ROWS, d), F32),
        ],
        compiler_params=pltpu.CompilerParams(
            dimension_semantics=("arbitrary",), vmem_limit_bytes=VMEM_LIMIT),
        name="fused_block",
    )(x, x, mod3, mod3, w, w_sp, b_sp_full, a_g, a_b, w_out, ln_g, ln_b)


def kernel(x, c, w_ada, b_ada, w_in, sgu_ln_g, sgu_ln_b, w_spatial, b_spatial, w_out, ln_g, ln_b):
    depth = w_ada.shape[0]
    bsz, seq, d = x.shape
    d_a = sgu_ln_g.shape[-1]
    alpha = (2.0 * depth) ** 0.25
    for layer in range(depth):
        mod, w_in_bf, w_out_bf = _prepare(c, w_ada[layer], b_ada[layer], w_in[layer], w_out[layer])
        b_sp_full = jnp.repeat(b_spatial[layer].T, GROUP_DIM, axis=1)
        x = _fused_block(
            x, mod.reshape(bsz, 3, d), w_in_bf, w_spatial[layer], b_sp_full,
            sgu_ln_g[layer].reshape(1, d_a), sgu_ln_b[layer].reshape(1, d_a),
            w_out_bf, ln_g[layer].reshape(1, d), ln_b[layer].reshape(1, d), alpha)
    return x
```

```python
import functools
import math

import jax
import jax.numpy as jnp
from jax import lax
from jax.experimental import pallas as pl
from jax.experimental.pallas import tpu as pltpu

F32 = jnp.float32
BF16 = jnp.bfloat16

N_GROUPS = 8
GROUP_DIM = 64
CHUNK = 128
LN_EPS = 1e-5
LANES = 128

PREP_TILE = 512
TQ = 256
TK = 256
Q_PER_STEP = 1
ROWS = Q_PER_STEP * TQ
SKEW = (6, 3)
EXTRA_START = 1
MXU_SLAB = 512
FIN_DELAY = 1
UNDERFLOW_LOG2 = 152.0
VMEM_LIMIT = 48 * 1024 * 1024


def _dot(a, b):
    return jnp.dot(a, b, preferred_element_type=F32)


def _dot_nt(a, b):
    return lax.dot_general(a, b, (((1,), (1,)), ((), ())), preferred_element_type=F32)


def _gelu_tanh(x):
    k1 = -2.0 * math.sqrt(2.0 / math.pi) * math.log2(math.e)
    return x / (1.0 + jnp.exp2(x * (k1 + (k1 * 0.044715) * (x * x))))


def _silu(x):
    return x / (1.0 + jnp.exp(-x))


def _layer_norm(x, g, b):
    mu = jnp.mean(x, axis=-1, keepdims=True)
    xc = x - mu
    var = jnp.mean(xc * xc, axis=-1, keepdims=True)
    return xc * lax.rsqrt(var + LN_EPS) * g + b


def _prep_kernel(c_ref, wa_ref, ba_ref, win_ref, wout_ref, mod_ref, win_bf_ref, wout_bf_ref):
    sc = _silu(c_ref[...]).astype(BF16)
    mod_ref[...] = _dot(sc, wa_ref[...].astype(BF16)) + ba_ref[...]
    win_bf_ref[...] = win_ref[...].astype(BF16)
    wout_bf_ref[...] = wout_ref[...].astype(BF16)


def _prepare(c, w_ada, b_ada, w_in, w_out):
    bsz, d = c.shape
    n_mod = w_ada.shape[1]
    steps = w_in.shape[1] // PREP_TILE
    mod_tiles = n_mod // PREP_TILE
    out_tile = PREP_TILE
    out_tiles = w_out.shape[1] // out_tile
    assert mod_tiles <= steps and out_tiles <= steps
    mod_col = lambda j: (0, jnp.minimum(j, mod_tiles - 1))
    out_col = lambda j: (0, jnp.minimum(j, out_tiles - 1))
    return pl.pallas_call(
        _prep_kernel,
        grid=(steps,),
        in_specs=[
            pl.BlockSpec((bsz, d), lambda j: (0, 0)),
            pl.BlockSpec((d, PREP_TILE), mod_col),
            pl.BlockSpec((1, PREP_TILE), mod_col),
            pl.BlockSpec((w_in.shape[0], PREP_TILE), lambda j: (0, j)),
            pl.BlockSpec((w_out.shape[0], out_tile), out_col),
        ],
        out_specs=[
            pl.BlockSpec((bsz, PREP_TILE), mod_col),
            pl.BlockSpec((w_in.shape[0], PREP_TILE), lambda j: (0, j)),
            pl.BlockSpec((w_out.shape[0], out_tile), out_col),
        ],
        out_shape=[
            jax.ShapeDtypeStruct((bsz, n_mod), F32),
            jax.ShapeDtypeStruct(w_in.shape, BF16),
            jax.ShapeDtypeStruct(w_out.shape, BF16),
        ],
        compiler_params=pltpu.CompilerParams(
            dimension_semantics=("arbitrary",), vmem_limit_bytes=VMEM_LIMIT),
        name="adaln_mod_and_casts",
    )(c, w_ada, b_ada.reshape(1, n_mod), w_in, w_out)


def _stage_slabs(s, n_total, per_seq):
    last = n_total - 1
    out = []
    for lag in range(3):
        t = jnp.clip(s - lag, 0, last)
        out.append((lax.div(t, per_seq), lax.rem(t, per_seq)))
    return out


def _block_kernel(alpha, n_total,
                  x_in_ref, x_out_ref, mod_in_ref, mod_out_ref, w_ref, wsp_ref, bsp_ref,
                  ag_ref, ab_ref, wo_ref, g_ref, b_ref, o_ref,
                  k_scr, vT_scr, q_ring, gz_ring, ya_ring, yb_ring, z_ring, land_scr, y_scr):
    s = pl.program_id(0)
    d_b = q_ring.shape[-1]
    d_a = ya_ring.shape[-1]
    per_seq = k_scr.shape[1] // ROWS
    (b_in, j_in), (b_at, j_at), _ = _stage_slabs(s, n_total, per_seq)
    par_in, par_at = lax.rem(b_in, 2), lax.rem(b_at, 2)
    n_heads = d_b // GROUP_DIM
    heads_per_tile = LANES // GROUP_DIM
    n_pairs = n_heads // heads_per_tile
    now2, prev2 = lax.rem(s, 2), lax.rem(s + 1, 2)
    now3, prev3 = lax.rem(s, 3), lax.rem(s + 1, 3)

    @pl.when(s == 0)
    def _():
        q_ring[...] = jnp.zeros_like(q_ring)
        gz_ring[...] = jnp.zeros_like(gz_ring)
        ya_ring[...] = jnp.zeros_like(ya_ring)
        yb_ring[...] = jnp.zeros_like(yb_ring)
        z_ring[...] = jnp.zeros_like(z_ring)
        k_scr[0, 0:ROWS, :] = jnp.zeros((ROWS, d_b), BF16)
        for n in range(Q_PER_STEP):
            vT_scr[0, n] = jnp.zeros((d_b, TK), BF16)

    new = {}

    def item_h():
        shift = mod_in_ref[0, 0:1, :]
        scale = mod_in_ref[0, 1:2, :]
        new["h"] = (x_in_ref[0] * (1.0 + scale) + shift).astype(BF16)

    def h():
        return new["h"]

    def ycat():
        return jnp.concatenate([ya_ring[prev3], yb_ring[prev2]], axis=1)

    def both(*fs):
        return lambda: [f() for f in fs]

    def fin_out():
        gate = mod_out_ref[0, 2:3, :]
        r = alpha * x_out_ref[0] + gate * y_scr[...]
        o_ref[0] = _layer_norm(r, g_ref[...], b_ref[...])

    def fin_u(p):
        new["ug"] = _gelu_tanh(p())

    def fin_v(p):
        new["vn"] = _layer_norm(_gelu_tanh(p()), ag_ref[...], ab_ref[...]).astype(BF16)
        t_idx = lax.broadcasted_iota(jnp.int32, (CHUNK, CHUNK), 0)
        s_idx = lax.broadcasted_iota(jnp.int32, (CHUNK, CHUNK), 1)
        causal = t_idx >= s_idx
        w_sp = [jnp.where(causal, wsp_ref[g], 0.0).astype(BF16) for g in range(N_GROUPS)]
        first_group = lax.broadcasted_iota(jnp.int32, (CHUNK, LANES), 1) < GROUP_DIM
        chunks = []
        for c in range(ROWS // CHUNK):
            rows = slice(c * CHUNK, (c + 1) * CHUNK)
            pieces = []
            for pr in range(d_a // LANES):
                cols = slice(pr * LANES, (pr + 1) * LANES)
                vp = new["vn"][rows, cols]
                mixed = jnp.where(first_group, _dot(w_sp[2 * pr], vp), _dot(w_sp[2 * pr + 1], vp))
                pieces.append(new["ug"][rows, cols] * (mixed + bsp_ref[:, cols]))
            chunks.append(jnp.concatenate(pieces, axis=1))
        new["ya_pre"] = jnp.concatenate(chunks, axis=0)

    def fin_za(p):
        ya_ring[now3] = (_silu(p()) * new["ya_pre"]).astype(BF16)

    scale_q = math.log2(math.e) / math.sqrt(GROUP_DIM)

    def fin_q(p):
        new["q"] = (p() * scale_q).astype(BF16)
        q_ring[now2] = new["q"]

    def fin_k(p):
        new["k"] = p().astype(BF16)
        k_scr[par_in, pl.ds(pl.multiple_of(j_in * ROWS, ROWS), ROWS), :] = new["k"]

    def head_of(q_pair, hh):
        in_head = (lane >= hh * GROUP_DIM) & (lane < (hh + 1) * GROUP_DIM)
        return jnp.where(in_head, q_pair, jnp.zeros_like(q_pair))

    def pre_scores():
        for qi in range(Q_PER_STEP):
            rows = slice(qi * TQ, (qi + 1) * TQ)
            for hd in range(n_heads):
                p, hh = divmod(hd, heads_per_tile)
                lanes = slice(p * LANES, (p + 1) * LANES)
                z_ring[now2, qi * n_heads + hd] = _dot_nt(
                    new["k"][rows, lanes], head_of(new["q"][rows, lanes], hh))

    def fin_vt(p):
        v = p()
        for n in range(Q_PER_STEP):
            vT_scr[par_in, j_in * Q_PER_STEP + n] = v[n * TK:(n + 1) * TK, :].T.astype(BF16)

    def fin_gz(p):
        gz_ring[now2] = _silu(p())

    dots, due = [], {}

    def out_slab(j):
        cols = slice(j * MXU_SLAB, (j + 1) * MXU_SLAB)

        def item():
            y_scr[:, cols] = _dot(ycat(), wo_ref[:, cols])
        return item

    for j in range(wo_ref.shape[1] // MXU_SLAB):
        dots.append(out_slab(j))
    due.setdefault(len(dots) - 1 + FIN_DELAY, []).append(fin_out)

    o = 3 * d_a
    projections = [(0, fin_u), (d_a, fin_v), (2 * d_a, fin_za), (o, fin_q), (o + d_b, fin_k),
                   (o + 2 * d_b, fin_vt), (o + 3 * d_b, fin_gz)]
    assert d_a == d_b == MXU_SLAB

    assert land_scr.shape[0] == len(projections)

    def in_slab(n, lo):
        def item():
            land_scr[n] = _dot(h(), w_ref[:, lo:lo + MXU_SLAB])
        return item

    def landed(n):
        return lambda: land_scr[n]

    for n, (lo, fin) in enumerate(projections):
        dots.append(in_slab(n, lo))
        due.setdefault(len(dots) - 1 + FIN_DELAY, []).append(functools.partial(fin, landed(n)))
    items = [item_h]
    for n, dot_item in enumerate(dots):
        items.append(both(dot_item, *due.pop(n, [])))
    items.append(both(*[f for n in sorted(due) for f in due[n]], pre_scores))

    lane = lax.broadcasted_iota(jnp.int32, (TQ, LANES), 1)
    half = TK // 2
    s_idx = lax.broadcasted_iota(jnp.int32, (half, half), 0)
    t_idx = lax.broadcasted_iota(jnp.int32, (half, half), 1)
    strictly_earlier_half = s_idx < t_idx
    j_idx = lax.broadcasted_iota(jnp.int32, (TK, TK), 1)
    r_idx = lax.broadcasted_iota(jnp.int32, (TK, TK), 0)
    strictly_later = jnp.where(j_idx > r_idx, 1.0, 0.0).astype(BF16)

    q_masked = {}
    for qi in range(Q_PER_STEP):
        for hd in range(n_heads):
            p, hh = divmod(hd, heads_per_tile)
            q_pair = q_ring[prev2, qi * TQ:(qi + 1) * TQ, p * LANES:(p + 1) * LANES]
            q_masked[qi, hd] = head_of(q_pair, hh)

    lo_half, hi_half = slice(0, half), slice(half, TK)
    diag_parts = [(lo_half, lo_half, "tri"), (lo_half, hi_half, "full"),
                  (hi_half, lo_half, "empty"), (hi_half, hi_half, "tri")]
    full_parts = [(slice(0, TK), slice(0, TQ), "full")]

    def assemble(parts, pieces):
        if len(parts) == 1:
            return pieces[0]
        return jnp.concatenate([jnp.concatenate(pieces[0:2], axis=1),
                                jnp.concatenate(pieces[2:4], axis=1)], axis=0)

    def gated_pair(accs, qi, p):
        rows, cols = slice(qi * TQ, (qi + 1) * TQ), slice(p * LANES, (p + 1) * LANES)
        yT = jnp.concatenate([accs[qi, heads_per_tile * p + n] for n in range(heads_per_tile)], axis=0)
        yb_ring[now2, rows, cols] = (yT.T * gz_ring[prev2, rows, cols]).astype(BF16)

    def key_blocks(blocks, runs, accs, extra=(), finish=False):
        runs, accs = dict(runs), dict(accs)
        tiles = [dict(qi=qi, kb=kb, h=hd, keep=keep, diag=diag, parts=diag_parts if diag else full_parts)
                 for qi, kb, diag, keep in blocks for hd in range(n_heads)]
        last_tile = {(t["qi"], t["h"]): n for n, t in enumerate(tiles)}

        def scores(t):
            if t["diag"]:
                t["z"] = z_ring[prev2, t["qi"] * n_heads + t["h"]]
                return
            start = pl.multiple_of(t["kb"] * TK, TK)
            p = t["h"] // heads_per_tile
            lanes = slice(p * LANES, (p + 1) * LANES)
            t["z"] = _dot_nt(k_scr[par_at, pl.ds(start, TK), lanes], q_masked[t["qi"], t["h"]])

        def softplus(t):
            sps, t["logb"] = [], []
            for rows, cols, kind in t["parts"]:
                if kind == "empty":
                    sps.append(jnp.zeros((half, half), BF16))
                    t["logb"].append(None)
                    continue
                z = t["z"][rows, cols]
                sp = jnp.maximum(z, 0.0) + jnp.log2(1.0 + jnp.exp2(-jnp.abs(z)))
                t["logb"].append(z - sp)
                if kind == "tri":
                    sp = jnp.where(strictly_earlier_half, sp, 0.0)
                sps.append(sp.astype(BF16))
            t["sp"] = assemble(t["parts"], sps)

        def later_sum(t):
            t["csum"] = _dot(strictly_later, t["sp"])

        def weights(t):
            key = (t["qi"], t["h"])
            run = runs[key]
            probs = []
            for (rows, cols, kind), logb in zip(t["parts"], t["logb"]):
                if kind == "empty":
                    probs.append(jnp.zeros((half, half), BF16))
                    continue
                a = jnp.exp2(logb - (run[:, cols] + t["csum"][rows, cols]))
                if kind == "tri":
                    a = jnp.where(strictly_earlier_half, a, 0.0)
                probs.append(a.astype(BF16))
            t["prob"] = assemble(t["parts"], probs)
            runs[key] = run + (t["csum"][0:1, :] + t["sp"][0:1, :].astype(F32))

        def values(n, t):
            key = (t["qi"], t["h"])
            hd = t["h"]
            v_blk = vT_scr[par_at, t["kb"], hd * GROUP_DIM:(hd + 1) * GROUP_DIM, :]
            if t["keep"] is not None:
                v_blk = (v_blk.astype(F32) * t["keep"]).astype(BF16)
            accs[key] = accs[key] + _dot(v_blk, t["prob"])
            if finish and last_tile[key] == n and hd % heads_per_tile == heads_per_tile - 1:
                gated_pair(accs, t["qi"], hd // heads_per_tile)

        n_tiles = len(tiles)
        lead, lag = SKEW
        ready = next((n for n, t in enumerate(tiles) if not t["diag"]), n_tiles)
        for t in tiles[:ready]:
            scores(t)
        steps = range(min(0, ready - lead), n_tiles + lag)
        extra = list(extra)
        slots = [EXTRA_START + (n * (len(steps) - EXTRA_START)) // max(1, len(extra))
                 for n in range(len(extra))]
        for count, step in enumerate(steps):
            while extra and slots[0] <= count:
                slots.pop(0)
                extra.pop(0)()
            if ready <= step + lead < n_tiles:
                scores(tiles[step + lead])
            if 0 <= step < n_tiles:
                softplus(tiles[step])
                later_sum(tiles[step])
            if 0 <= step - lag < n_tiles:
                weights(tiles[step - lag])
                values(step - lag, tiles[step - lag])
        for item in extra:
            item()
        return runs, accs

    runs = {(qi, hd): jnp.zeros((1, TQ), F32) for qi in range(Q_PER_STEP) for hd in range(n_heads)}
    accs = {(qi, hd): jnp.zeros((GROUP_DIM, TQ), F32) for qi in range(Q_PER_STEP) for hd in range(n_heads)}
    first_kb = j_at * Q_PER_STEP
    diag_blocks = [(qi, first_kb + qi, True, None) for qi in range(Q_PER_STEP)]
    has_earlier = (j_at > 0).astype(F32)
    prev_blocks = [(qi, jnp.maximum(first_kb + qi - 1, 0), False, has_earlier if qi == 0 else None)
                   for qi in range(Q_PER_STEP)]
    runs, accs = key_blocks(diag_blocks + prev_blocks, runs, accs, items, finish=True)

    for qi in range(Q_PER_STEP):
        def sticks_left(runs):
            return jnp.min(functools.reduce(jnp.minimum, [runs[qi, hd] for hd in range(n_heads)])) < UNDERFLOW_LOG2

        def earlier_block(state, qi=qi, sticks_left=sticks_left):
            kb, _, runs, accs = state
            runs, accs = key_blocks([(qi, kb, False, None)], runs, accs)
            return kb - 1, sticks_left(runs), runs, accs

        first = first_kb + qi - 2
        mine = lambda d: {k: v for k, v in d.items() if k[0] == qi}

        @pl.when((first >= 0) & sticks_left(runs))
        def _(qi=qi, first=first, earlier_block=earlier_block, mine=mine):
            _, _, _, more = lax.while_loop(lambda st: (st[0] >= 0) & st[1], earlier_block,
                                           (first, True, mine(runs), mine(accs)))
            for p in range(n_pairs):
                gated_pair(more, qi, p)


def _fused_block(x, mod3, w, w_sp, b_sp_full, a_g, a_b, w_out, ln_g, ln_b, alpha):
    bsz, seq, d = x.shape
    d_a = a_g.shape[-1]
    d_b = (w.shape[1] - 3 * d_a) // 4
    per_seq = seq // ROWS
    n_total = bsz * per_seq
    grid = (n_total + 2,)

    def stage(lag):
        return lambda s: _stage_slabs(s, n_total, per_seq)[lag]

    def rows_of(lag):
        return pl.BlockSpec((1, ROWS, d), lambda s: (*stage(lag)(s), 0))

    def mod_of(lag):
        return pl.BlockSpec((1, 3, d), lambda s: (stage(lag)(s)[0], 0, 0))

    def whole(a):
        return pl.BlockSpec(a.shape, lambda s: (0,) * a.ndim)

    return pl.pallas_call(
        functools.partial(_block_kernel, alpha, n_total),
        grid=grid,
        in_specs=[rows_of(0), rows_of(2), mod_of(0), mod_of(2), whole(w), whole(w_sp),
                  whole(b_sp_full), whole(a_g), whole(a_b), whole(w_out), whole(ln_g), whole(ln_b)],
        out_specs=rows_of(2),
        out_shape=jax.ShapeDtypeStruct((bsz, seq, d), x.dtype),
        scratch_shapes=[
            pltpu.VMEM((2, seq, d_b), BF16),
            pltpu.VMEM((2, seq // TK, d_b, TK), BF16),
            pltpu.VMEM((2, ROWS, d_b), BF16),
            pltpu.VMEM((2, ROWS, d_b), F32),
            pltpu.VMEM((3, ROWS, d_a), BF16),
            pltpu.VMEM((2, ROWS, d_b), BF16),
            pltpu.VMEM((2, Q_PER_STEP * (d_b // GROUP_DIM), TK, TQ), F32),
            pltpu.VMEM((w.shape[1] // MXU_SLAB, ROWS, MXU_SLAB), F32),
            pltpu.VMEM((ROWS, d), F32),
        ],
        compiler_params=pltpu.CompilerParams(
            dimension_semantics=("arbitrary",), vmem_limit_bytes=VMEM_LIMIT),
        name="fused_block",
    )(x, x, mod3, mod3, w, w_sp, b_sp_full, a_g, a_b, w_out, ln_g, ln_b)


def kernel(x, c, w_ada, b_ada, w_in, sgu_ln_g, sgu_ln_b, w_spatial, b_spatial, w_out, ln_g, ln_b):
    depth = w_ada.shape[0]
    bsz, seq, d = x.shape
    d_a = sgu_ln_g.shape[-1]
    alpha = (2.0 * depth) ** 0.25
    for layer in range(depth):
        mod, w_in_bf, w_out_bf = _prepare(c, w_ada[layer], b_ada[layer], w_in[layer], w_out[layer])
        b_sp_full = jnp.repeat(b_spatial[layer].T, GROUP_DIM, axis=1)
        x = _fused_block(
            x, mod.reshape(bsz, 3, d), w_in_bf, w_spatial[layer], b_sp_full,
            sgu_ln_g[layer].reshape(1, d_a), sgu_ln_b[layer].reshape(1, d_a),
            w_out_bf, ln_g[layer].reshape(1, d), ln_b[layer].reshape(1, d), alpha)
    return x
```

```python
import functools
import math

import jax
import jax.numpy as jnp
from jax import lax
from jax.experimental import pallas as pl
from jax.experimental.pallas import tpu as pltpu

F32 = jnp.float32
BF16 = jnp.bfloat16

N_GROUPS = 8
GROUP_DIM = 64
CHUNK = 128
LN_EPS = 1e-5
LANES = 128

PREP_TILE = 512
TQ = 256
TK = 256
Q_PER_STEP = 1
ROWS = Q_PER_STEP * TQ
SKEW = (8, 3)
EXTRA_START = 1
MXU_SLAB = 512
FIN_DELAY = 0
UNDERFLOW_LOG2 = 152.0
VMEM_LIMIT = 48 * 1024 * 1024


def _dot(a, b):
    return jnp.dot(a, b, preferred_element_type=F32)


def _dot_nt(a, b):
    return lax.dot_general(a, b, (((1,), (1,)), ((), ())), preferred_element_type=F32)


def _gelu_tanh(x):
    k1 = -2.0 * math.sqrt(2.0 / math.pi) * math.log2(math.e)
    return x / (1.0 + jnp.exp2(x * (k1 + (k1 * 0.044715) * (x * x))))


def _silu(x):
    return x / (1.0 + jnp.exp(-x))


def _layer_norm(x, g, b):
    mu = jnp.mean(x, axis=-1, keepdims=True)
    xc = x - mu
    var = jnp.mean(xc * xc, axis=-1, keepdims=True)
    return xc * lax.rsqrt(var + LN_EPS) * g + b


def _prep_kernel(c_ref, wa_ref, ba_ref, win_ref, wout_ref, mod_ref, win_bf_ref, wout_bf_ref):
    sc = _silu(c_ref[...]).astype(BF16)
    mod_ref[...] = _dot(sc, wa_ref[...].astype(BF16)) + ba_ref[...]
    win_bf_ref[...] = win_ref[...].astype(BF16)
    wout_bf_ref[...] = wout_ref[...].astype(BF16)


def _prepare(c, w_ada, b_ada, w_in, w_out):
    bsz, d = c.shape
    n_mod = w_ada.shape[1]
    steps = w_in.shape[1] // PREP_TILE
    mod_tiles = n_mod // PREP_TILE
    out_tile = PREP_TILE
    out_tiles = w_out.shape[1] // out_tile
    assert mod_tiles <= steps and out_tiles <= steps
    mod_col = lambda j: (0, jnp.minimum(j, mod_tiles - 1))
    out_col = lambda j: (0, jnp.minimum(j, out_tiles - 1))
    return pl.pallas_call(
        _prep_kernel,
        grid=(steps,),
        in_specs=[
            pl.BlockSpec((bsz, d), lambda j: (0, 0)),
            pl.BlockSpec((d, PREP_TILE), mod_col),
            pl.BlockSpec((1, PREP_TILE), mod_col),
            pl.BlockSpec((w_in.shape[0], PREP_TILE), lambda j: (0, j)),
            pl.BlockSpec((w_out.shape[0], out_tile), out_col),
        ],
        out_specs=[
            pl.BlockSpec((bsz, PREP_TILE), mod_col),
            pl.BlockSpec((w_in.shape[0], PREP_TILE), lambda j: (0, j)),
            pl.BlockSpec((w_out.shape[0], out_tile), out_col),
        ],
        out_shape=[
            jax.ShapeDtypeStruct((bsz, n_mod), F32),
            jax.ShapeDtypeStruct(w_in.shape, BF16),
            jax.ShapeDtypeStruct(w_out.shape, BF16),
        ],
        compiler_params=pltpu.CompilerParams(
            dimension_semantics=("arbitrary",), vmem_limit_bytes=VMEM_LIMIT),
        name="adaln_mod_and_casts",
    )(c, w_ada, b_ada.reshape(1, n_mod), w_in, w_out)


def _stage_slabs(s, n_total, per_seq):
    last = n_total - 1
    out = []
    for lag in range(3):
        t = jnp.clip(s - lag, 0, last)
        out.append((lax.div(t, per_seq), lax.rem(t, per_seq)))
    return out


def _block_kernel(alpha, n_total,
                  x_in_ref, x_out_ref, mod_in_ref, mod_out_ref, w_ref, wsp_ref, bsp_ref,
                  ag_ref, ab_ref, wo_ref, g_ref, b_ref, o_ref,
                  k_scr, vT_scr, q_ring, gz_ring, ya_ring, yb_ring, z_ring, land_scr, y_scr):
    s = pl.program_id(0)
    d_b = q_ring.shape[-1]
    d_a = ya_ring.shape[-1]
    per_seq = k_scr.shape[1] // ROWS
    (b_in, j_in), (b_at, j_at), _ = _stage_slabs(s, n_total, per_seq)
    par_in, par_at = lax.rem(b_in, 2), lax.rem(b_at, 2)
    n_heads = d_b // GROUP_DIM
    heads_per_tile = LANES // GROUP_DIM
    n_pairs = n_heads // heads_per_tile
    now2, prev2 = lax.rem(s, 2), lax.rem(s + 1, 2)
    now3, prev3 = lax.rem(s, 3), lax.rem(s + 1, 3)

    @pl.when(s == 0)
    def _():
        q_ring[...] = jnp.zeros_like(q_ring)
        gz_ring[...] = jnp.zeros_like(gz_ring)
        ya_ring[...] = jnp.zeros_like(ya_ring)
        yb_ring[...] = jnp.zeros_like(yb_ring)
        z_ring[...] = jnp.zeros_like(z_ring)
        k_scr[0, 0:ROWS, :] = jnp.zeros((ROWS, d_b), BF16)
        for n in range(Q_PER_STEP):
            vT_scr[0, n] = jnp.zeros((d_b, TK), BF16)

    new = {}

    def item_h():
        shift = mod_in_ref[0, 0:1, :]
        scale = mod_in_ref[0, 1:2, :]
        new["h"] = (x_in_ref[0] * (1.0 + scale) + shift).astype(BF16)

    def h():
        return new["h"]

    def ycat():
        return jnp.concatenate([ya_ring[prev3], yb_ring[prev2]], axis=1)

    def both(*fs):
        return lambda: [f() for f in fs]

    def fin_out():
        gate = mod_out_ref[0, 2:3, :]
        r = alpha * x_out_ref[0] + gate * y_scr[...]
        o_ref[0] = _layer_norm(r, g_ref[...], b_ref[...])

    def fin_u(p):
        new["ug"] = _gelu_tanh(p())

    def fin_v(p):
        new["vn"] = _layer_norm(_gelu_tanh(p()), ag_ref[...], ab_ref[...]).astype(BF16)
        t_idx = lax.broadcasted_iota(jnp.int32, (CHUNK, CHUNK), 0)
        s_idx = lax.broadcasted_iota(jnp.int32, (CHUNK, CHUNK), 1)
        causal = t_idx >= s_idx
        w_sp = [jnp.where(causal, wsp_ref[g], 0.0).astype(BF16) for g in range(N_GROUPS)]
        first_group = lax.broadcasted_iota(jnp.int32, (CHUNK, LANES), 1) < GROUP_DIM
        chunks = []
        for c in range(ROWS // CHUNK):
            rows = slice(c * CHUNK, (c + 1) * CHUNK)
            pieces = []
            for pr in range(d_a // LANES):
                cols = slice(pr * LANES, (pr + 1) * LANES)
                vp = new["vn"][rows, cols]
                mixed = jnp.where(first_group, _dot(w_sp[2 * pr], vp), _dot(w_sp[2 * pr + 1], vp))
                pieces.append(new["ug"][rows, cols] * (mixed + bsp_ref[:, cols]))
            chunks.append(jnp.concatenate(pieces, axis=1))
        new["ya_pre"] = jnp.concatenate(chunks, axis=0)

    def fin_za(p):
        ya_ring[now3] = (_silu(p()) * new["ya_pre"]).astype(BF16)

    scale_q = math.log2(math.e) / math.sqrt(GROUP_DIM)

    def fin_q(p):
        new["q"] = (p() * scale_q).astype(BF16)
        q_ring[now2] = new["q"]

    def fin_k(p):
        new["k"] = p().astype(BF16)
        k_scr[par_in, pl.ds(pl.multiple_of(j_in * ROWS, ROWS), ROWS), :] = new["k"]

    def head_of(q_pair, hh):
        in_head = (lane >= hh * GROUP_DIM) & (lane < (hh + 1) * GROUP_DIM)
        return jnp.where(in_head, q_pair, jnp.zeros_like(q_pair))

    def pre_scores():
        for qi in range(Q_PER_STEP):
            rows = slice(qi * TQ, (qi + 1) * TQ)
            for hd in range(n_heads):
                p, hh = divmod(hd, heads_per_tile)
                lanes = slice(p * LANES, (p + 1) * LANES)
                z_ring[now2, qi * n_heads + hd] = _dot_nt(
                    new["k"][rows, lanes], head_of(new["q"][rows, lanes], hh))

    def fin_vt(p):
        v = p()
        for n in range(Q_PER_STEP):
            vT_scr[par_in, j_in * Q_PER_STEP + n] = v[n * TK:(n + 1) * TK, :].T.astype(BF16)

    def fin_gz(p):
        gz_ring[now2] = _silu(p())

    dots, due = [], {}

    def out_slab(j):
        cols = slice(j * MXU_SLAB, (j + 1) * MXU_SLAB)

        def item():
            y_scr[:, cols] = _dot(ycat(), wo_ref[:, cols])
        return item

    for j in range(wo_ref.shape[1] // MXU_SLAB):
        dots.append(out_slab(j))
    due.setdefault(len(dots) - 1 + FIN_DELAY, []).append(fin_out)

    o = 3 * d_a
    projections = [(0, fin_u), (d_a, fin_v), (2 * d_a, fin_za), (o, fin_q), (o + d_b, fin_k),
                   (o + 2 * d_b, fin_vt), (o + 3 * d_b, fin_gz)]
    assert d_a == d_b == MXU_SLAB

    assert land_scr.shape[0] == len(projections)

    def in_slab(n, lo):
        def item():
            land_scr[n] = _dot(h(), w_ref[:, lo:lo + MXU_SLAB])
        return item

    def landed(n):
        return lambda: land_scr[n]

    for n, (lo, fin) in enumerate(projections):
        dots.append(in_slab(n, lo))
        due.setdefault(len(dots) - 1 + FIN_DELAY, []).append(functools.partial(fin, landed(n)))
    items = [item_h]
    for n, dot_item in enumerate(dots):
        items.append(both(dot_item, *due.pop(n, [])))
    items.append(both(*[f for n in sorted(due) for f in due[n]], pre_scores))

    lane = lax.broadcasted_iota(jnp.int32, (TQ, LANES), 1)
    half = TK // 2
    s_idx = lax.broadcasted_iota(jnp.int32, (half, half), 0)
    t_idx = lax.broadcasted_iota(jnp.int32, (half, half), 1)
    strictly_earlier_half = s_idx < t_idx
    j_idx = lax.broadcasted_iota(jnp.int32, (TK, TK), 1)
    r_idx = lax.broadcasted_iota(jnp.int32, (TK, TK), 0)
    strictly_later = jnp.where(j_idx > r_idx, 1.0, 0.0).astype(BF16)

    q_masked = {}
    for qi in range(Q_PER_STEP):
        for hd in range(n_heads):
            p, hh = divmod(hd, heads_per_tile)
            q_pair = q_ring[prev2, qi * TQ:(qi + 1) * TQ, p * LANES:(p + 1) * LANES]
            q_masked[qi, hd] = head_of(q_pair, hh)

    lo_half, hi_half = slice(0, half), slice(half, TK)
    diag_parts = [(lo_half, lo_half, "tri"), (lo_half, hi_half, "full"),
                  (hi_half, lo_half, "empty"), (hi_half, hi_half, "tri")]
    full_parts = [(slice(0, TK), slice(0, TQ), "full")]

    def assemble(parts, pieces):
        if len(parts) == 1:
            return pieces[0]
        return jnp.concatenate([jnp.concatenate(pieces[0:2], axis=1),
                                jnp.concatenate(pieces[2:4], axis=1)], axis=0)

    def gated_pair(accs, qi, p):
        rows, cols = slice(qi * TQ, (qi + 1) * TQ), slice(p * LANES, (p + 1) * LANES)
        yT = jnp.concatenate([accs[qi, heads_per_tile * p + n] for n in range(heads_per_tile)], axis=0)
        yb_ring[now2, rows, cols] = (yT.T * gz_ring[prev2, rows, cols]).astype(BF16)

    def key_blocks(blocks, runs, accs, extra=(), finish=False):
        runs, accs = dict(runs), dict(accs)
        tiles = [dict(qi=qi, kb=kb, h=hd, keep=keep, diag=diag, parts=diag_parts if diag else full_parts)
                 for qi, kb, diag, keep in blocks for hd in range(n_heads)]
        last_tile = {(t["qi"], t["h"]): n for n, t in enumerate(tiles)}

        def scores(t):
            if t["diag"]:
                t["z"] = z_ring[prev2, t["qi"] * n_heads + t["h"]]
                return
            start = pl.multiple_of(t["kb"] * TK, TK)
            p = t["h"] // heads_per_tile
            lanes = slice(p * LANES, (p + 1) * LANES)
            t["z"] = _dot_nt(k_scr[par_at, pl.ds(start, TK), lanes], q_masked[t["qi"], t["h"]])

        def softplus(t):
            sps, t["logb"] = [], []
            for rows, cols, kind in t["parts"]:
                if kind == "empty":
                    sps.append(jnp.zeros((half, half), BF16))
                    t["logb"].append(None)
                    continue
                z = t["z"][rows, cols]
                sp = jnp.maximum(z, 0.0) + jnp.log2(1.0 + jnp.exp2(-jnp.abs(z)))
                t["logb"].append(z - sp)
                if kind == "tri":
                    sp = jnp.where(strictly_earlier_half, sp, 0.0)
                sps.append(sp.astype(BF16))
            t["sp"] = assemble(t["parts"], sps)

        def later_sum(t):
            t["csum"] = _dot(strictly_later, t["sp"])

        def weights(t):
            key = (t["qi"], t["h"])
            run = runs[key]
            probs = []
            for (rows, cols, kind), logb in zip(t["parts"], t["logb"]):
                if kind == "empty":
                    probs.append(jnp.zeros((half, half), BF16))
                    continue
                a = jnp.exp2(logb - (run[:, cols] + t["csum"][rows, cols]))
                if kind == "tri":
                    a = jnp.where(strictly_earlier_half, a, 0.0)
                probs.append(a.astype(BF16))
            t["prob"] = assemble(t["parts"], probs)
            runs[key] = run + (t["csum"][0:1, :] + t["sp"][0:1, :].astype(F32))

        def values(n, t):
            key = (t["qi"], t["h"])
            hd = t["h"]
            v_blk = vT_scr[par_at, t["kb"], hd * GROUP_DIM:(hd + 1) * GROUP_DIM, :]
            if t["keep"] is not None:
                v_blk = (v_blk.astype(F32) * t["keep"]).astype(BF16)
            accs[key] = accs[key] + _dot(v_blk, t["prob"])
            if finish and last_tile[key] == n and hd % heads_per_tile == heads_per_tile - 1:
                gated_pair(accs, t["qi"], hd // heads_per_tile)

        n_tiles = len(tiles)
        lead, lag = SKEW
        ready = next((n for n, t in enumerate(tiles) if not t["diag"]), n_tiles)
        for t in tiles[:ready]:
            scores(t)
        steps = range(min(0, ready - lead), n_tiles + lag)
        extra = list(extra)
        slots = [EXTRA_START + (n * (len(steps) - EXTRA_START)) // max(1, len(extra))
                 for n in range(len(extra))]
        for count, step in enumerate(steps):
            while extra and slots[0] <= count:
                slots.pop(0)
                extra.pop(0)()
            if ready <= step + lead < n_tiles:
                scores(tiles[step + lead])
            if 0 <= step < n_tiles:
                softplus(tiles[step])
                later_sum(tiles[step])
            if 0 <= step - lag < n_tiles:
                weights(tiles[step - lag])
                values(step - lag, tiles[step - lag])
        for item in extra:
            item()
        return runs, accs

    runs = {(qi, hd): jnp.zeros((1, TQ), F32) for qi in range(Q_PER_STEP) for hd in range(n_heads)}
    accs = {(qi, hd): jnp.zeros((GROUP_DIM, TQ), F32) for qi in range(Q_PER_STEP) for hd in range(n_heads)}
    first_kb = j_at * Q_PER_STEP
    diag_blocks = [(qi, first_kb + qi, True, None) for qi in range(Q_PER_STEP)]
    has_earlier = (j_at > 0).astype(F32)
    prev_blocks = [(qi, jnp.maximum(first_kb + qi - 1, 0), False, has_earlier if qi == 0 else None)
                   for qi in range(Q_PER_STEP)]
    runs, accs = key_blocks(diag_blocks + prev_blocks, runs, accs, items, finish=True)

    for qi in range(Q_PER_STEP):
        def sticks_left(runs):
            return jnp.min(functools.reduce(jnp.minimum, [runs[qi, hd] for hd in range(n_heads)])) < UNDERFLOW_LOG2

        def earlier_block(state, qi=qi, sticks_left=sticks_left):
            kb, _, runs, accs = state
            runs, accs = key_blocks([(qi, kb, False, None)], runs, accs)
            return kb - 1, sticks_left(runs), runs, accs

        first = first_kb + qi - 2
        mine = lambda d: {k: v for k, v in d.items() if k[0] == qi}

        @pl.when((first >= 0) & sticks_left(runs))
        def _(qi=qi, first=first, earlier_block=earlier_block, mine=mine):
            _, _, _, more = lax.while_loop(lambda st: (st[0] >= 0) & st[1], earlier_block,
                                           (first, True, mine(runs), mine(accs)))
            for p in range(n_pairs):
                gated_pair(more, qi, p)


def _fused_block(x, mod3, w, w_sp, b_sp_full, a_g, a_b, w_out, ln_g, ln_b, alpha):
    bsz, seq, d = x.shape
    d_a = a_g.shape[-1]
    d_b = (w.shape[1] - 3 * d_a) // 4
    per_seq = seq // ROWS
    n_total = bsz * per_seq
    grid = (n_total + 2,)

    def stage(lag):
        return lambda s: _stage_slabs(s, n_total, per_seq)[lag]

    def rows_of(lag):
        return pl.BlockSpec((1, ROWS, d), lambda s: (*stage(lag)(s), 0))

    def mod_of(lag):
        return pl.BlockSpec((1, 3, d), lambda s: (stage(lag)(s)[0], 0, 0))

    def whole(a):
        return pl.BlockSpec(a.shape, lambda s: (0,) * a.ndim)

    return pl.pallas_call(
        functools.partial(_block_kernel, alpha, n_total),
        grid=grid,
        in_specs=[rows_of(0), rows_of(2), mod_of(0), mod_of(2), whole(w), whole(w_sp),
                  whole(b_sp_full), whole(a_g), whole(a_b), whole(w_out), whole(ln_g), whole(ln_b)],
        out_specs=rows_of(2),
        out_shape=jax.ShapeDtypeStruct((bsz, seq, d), x.dtype),
        scratch_shapes=[
            pltpu.VMEM((2, seq, d_b), BF16),
            pltpu.VMEM((2, seq // TK, d_b, TK), BF16),
            pltpu.VMEM((2, ROWS, d_b), BF16),
            pltpu.VMEM((2, ROWS, d_b), F32),
            pltpu.VMEM((3, ROWS, d_a), BF16),
            pltpu.VMEM((2, ROWS, d_b), BF16),
            pltpu.VMEM((2, Q_PER_STEP * (d_b // GROUP_DIM), TK, TQ), F32),
            pltpu.VMEM((w.shape[1] // MXU_SLAB, ROWS, MXU_SLAB), F32),
            pltpu.VMEM((ROWS, d), F32),
        ],
        compiler_params=pltpu.CompilerParams(
            dimension_semantics=("arbitrary",), vmem_limit_bytes=VMEM_LIMIT),
        name="fused_block",
    )(x, x, mod3, mod3, w, w_sp, b_sp_full, a_g, a_b, w_out, ln_g, ln_b)


def kernel(x, c, w_ada, b_ada, w_in, sgu_ln_g, sgu_ln_b, w_spatial, b_spatial, w_out, ln_g, ln_b):
    depth = w_ada.shape[0]
    bsz, seq, d = x.shape
    d_a = sgu_ln_g.shape[-1]
    alpha = (2.0 * depth) ** 0.25
    for layer in range(depth):
        mod, w_in_bf, w_out_bf = _prepare(c, w_ada[layer], b_ada[layer], w_in[layer], w_out[layer])
        b_sp_full = jnp.repeat(b_spatial[layer].T, GROUP_DIM, axis=1)
        x = _fused_block(
            x, mod.reshape(bsz, 3, d), w_in_bf, w_spatial[layer], b_sp_full,
            sgu_ln_g[layer].reshape(1, d_a), sgu_ln_b[layer].reshape(1, d_a),
            w_out_bf, ln_g[layer].reshape(1, d), ln_b[layer].reshape(1, d), alpha)
    return x
```

```python
import functools
import math

import jax
import jax.numpy as jnp
from jax import lax
from jax.experimental import pallas as pl
from jax.experimental.pallas import tpu as pltpu

F32 = jnp.float32
BF16 = jnp.bfloat16

N_GROUPS = 8
GROUP_DIM = 64
CHUNK = 128
LN_EPS = 1e-5
LANES = 128

PREP_TILE = 512
TQ = 256
TK = 256
Q_PER_STEP = 1
ROWS = Q_PER_STEP * TQ
SKEW = (8, 3)
EXTRA_START = 2
MXU_SLAB = 512
FIN_DELAY = 1
UNDERFLOW_LOG2 = 152.0
VMEM_LIMIT = 48 * 1024 * 1024


def _dot(a, b):
    return jnp.dot(a, b, preferred_element_type=F32)


def _dot_nt(a, b):
    return lax.dot_general(a, b, (((1,), (1,)), ((), ())), preferred_element_type=F32)


def _gelu_tanh(x):
    k1 = -2.0 * math.sqrt(2.0 / math.pi) * math.log2(math.e)
    return x / (1.0 + jnp.exp2(x * (k1 + (k1 * 0.044715) * (x * x))))


def _silu(x):
    return x / (1.0 + jnp.exp(-x))


def _layer_norm(x, g, b):
    mu = jnp.mean(x, axis=-1, keepdims=True)
    xc = x - mu
    var = jnp.mean(xc * xc, axis=-1, keepdims=True)
    return xc * lax.rsqrt(var + LN_EPS) * g + b


def _prep_kernel(c_ref, wa_ref, ba_ref, win_ref, wout_ref, mod_ref, win_bf_ref, wout_bf_ref):
    sc = _silu(c_ref[...]).astype(BF16)
    mod_ref[...] = _dot(sc, wa_ref[...].astype(BF16)) + ba_ref[...]
    win_bf_ref[...] = win_ref[...].astype(BF16)
    wout_bf_ref[...] = wout_ref[...].astype(BF16)


def _prepare(c, w_ada, b_ada, w_in, w_out):
    bsz, d = c.shape
    n_mod = w_ada.shape[1]
    steps = w_in.shape[1] // PREP_TILE
    mod_tiles = n_mod // PREP_TILE
    out_tile = PREP_TILE
    out_tiles = w_out.shape[1] // out_tile
    assert mod_tiles <= steps and out_tiles <= steps
    mod_col = lambda j: (0, jnp.minimum(j, mod_tiles - 1))
    out_col = lambda j: (0, jnp.minimum(j, out_tiles - 1))
    return pl.pallas_call(
        _prep_kernel,
        grid=(steps,),
        in_specs=[
            pl.BlockSpec((bsz, d), lambda j: (0, 0)),
            pl.BlockSpec((d, PREP_TILE), mod_col),
            pl.BlockSpec((1, PREP_TILE), mod_col),
            pl.BlockSpec((w_in.shape[0], PREP_TILE), lambda j: (0, j)),
            pl.BlockSpec((w_out.shape[0], out_tile), out_col),
        ],
        out_specs=[
            pl.BlockSpec((bsz, PREP_TILE), mod_col),
            pl.BlockSpec((w_in.shape[0], PREP_TILE), lambda j: (0, j)),
            pl.BlockSpec((w_out.shape[0], out_tile), out_col),
        ],
        out_shape=[
            jax.ShapeDtypeStruct((bsz, n_mod), F32),
            jax.ShapeDtypeStruct(w_in.shape, BF16),
            jax.ShapeDtypeStruct(w_out.shape, BF16),
        ],
        compiler_params=pltpu.CompilerParams(
            dimension_semantics=("arbitrary",), vmem_limit_bytes=VMEM_LIMIT),
        name="adaln_mod_and_casts",
    )(c, w_ada, b_ada.reshape(1, n_mod), w_in, w_out)


def _stage_slabs(s, n_total, per_seq):
    last = n_total - 1
    out = []
    for lag in range(3):
        t = jnp.clip(s - lag, 0, last)
        out.append((lax.div(t, per_seq), lax.rem(t, per_seq)))
    return out


def _block_kernel(alpha, n_total,
                  x_in_ref, x_out_ref, mod_in_ref, mod_out_ref, w_ref, wsp_ref, bsp_ref,
                  ag_ref, ab_ref, wo_ref, g_ref, b_ref, o_ref,
                  k_scr, vT_scr, q_ring, gz_ring, ya_ring, yb_ring, z_ring, land_scr, y_scr):
    s = pl.program_id(0)
    d_b = q_ring.shape[-1]
    d_a = ya_ring.shape[-1]
    per_seq = k_scr.shape[1] // ROWS
    (b_in, j_in), (b_at, j_at), _ = _stage_slabs(s, n_total, per_seq)
    par_in, par_at = lax.rem(b_in, 2), lax.rem(b_at, 2)
    n_heads = d_b // GROUP_DIM
    heads_per_tile = LANES // GROUP_DIM
    n_pairs = n_heads // heads_per_tile
    now2, prev2 = lax.rem(s, 2), lax.rem(s + 1, 2)
    now3, prev3 = lax.rem(s, 3), lax.rem(s + 1, 3)

    @pl.when(s == 0)
    def _():
        q_ring[...] = jnp.zeros_like(q_ring)
        gz_ring[...] = jnp.zeros_like(gz_ring)
        ya_ring[...] = jnp.zeros_like(ya_ring)
        yb_ring[...] = jnp.zeros_like(yb_ring)
        z_ring[...] = jnp.zeros_like(z_ring)
        k_scr[0, 0:ROWS, :] = jnp.zeros((ROWS, d_b), BF16)
        for n in range(Q_PER_STEP):
            vT_scr[0, n] = jnp.zeros((d_b, TK), BF16)

    new = {}

    def item_h():
        shift = mod_in_ref[0, 0:1, :]
        scale = mod_in_ref[0, 1:2, :]
        new["h"] = (x_in_ref[0] * (1.0 + scale) + shift).astype(BF16)

    def h():
        return new["h"]

    def ycat():
        return jnp.concatenate([ya_ring[prev3], yb_ring[prev2]], axis=1)

    def both(*fs):
        return lambda: [f() for f in fs]

    def fin_out():
        gate = mod_out_ref[0, 2:3, :]
        r = alpha * x_out_ref[0] + gate * y_scr[...]
        o_ref[0] = _layer_norm(r, g_ref[...], b_ref[...])

    def fin_u(p):
        new["ug"] = _gelu_tanh(p())

    def fin_v(p):
        new["vn"] = _layer_norm(_gelu_tanh(p()), ag_ref[...], ab_ref[...]).astype(BF16)
        t_idx = lax.broadcasted_iota(jnp.int32, (CHUNK, CHUNK), 0)
        s_idx = lax.broadcasted_iota(jnp.int32, (CHUNK, CHUNK), 1)
        causal = t_idx >= s_idx
        w_sp = [jnp.where(causal, wsp_ref[g], 0.0).astype(BF16) for g in range(N_GROUPS)]
        first_group = lax.broadcasted_iota(jnp.int32, (CHUNK, LANES), 1) < GROUP_DIM
        chunks = []
        for c in range(ROWS // CHUNK):
            rows = slice(c * CHUNK, (c + 1) * CHUNK)
            pieces = []
            for pr in range(d_a // LANES):
                cols = slice(pr * LANES, (pr + 1) * LANES)
                vp = new["vn"][rows, cols]
                mixed = jnp.where(first_group, _dot(w_sp[2 * pr], vp), _dot(w_sp[2 * pr + 1], vp))
                pieces.append(new["ug"][rows, cols] * (mixed + bsp_ref[:, cols]))
            chunks.append(jnp.concatenate(pieces, axis=1))
        new["ya_pre"] = jnp.concatenate(chunks, axis=0)

    def fin_za(p):
        ya_ring[now3] = (_silu(p()) * new["ya_pre"]).astype(BF16)

    scale_q = math.log2(math.e) / math.sqrt(GROUP_DIM)

    def fin_q(p):
        new["q"] = (p() * scale_q).astype(BF16)
        q_ring[now2] = new["q"]

    def fin_k(p):
        new["k"] = p().astype(BF16)
        k_scr[par_in, pl.ds(pl.multiple_of(j_in * ROWS, ROWS), ROWS), :] = new["k"]

    def head_of(q_pair, hh):
        in_head = (lane >= hh * GROUP_DIM) & (lane < (hh + 1) * GROUP_DIM)
        return jnp.where(in_head, q_pair, jnp.zeros_like(q_pair))

    def pre_scores():
        for qi in range(Q_PER_STEP):
            rows = slice(qi * TQ, (qi + 1) * TQ)
            for hd in range(n_heads):
                p, hh = divmod(hd, heads_per_tile)
                lanes = slice(p * LANES, (p + 1) * LANES)
                z_ring[now2, qi * n_heads + hd] = _dot_nt(
                    new["k"][rows, lanes], head_of(new["q"][rows, lanes], hh))

    def fin_vt(p):
        v = p()
        for n in range(Q_PER_STEP):
            vT_scr[par_in, j_in * Q_PER_STEP + n] = v[n * TK:(n + 1) * TK, :].T.astype(BF16)

    def fin_gz(p):
        gz_ring[now2] = _silu(p())

    dots, due = [], {}

    def out_slab(j):
        cols = slice(j * MXU_SLAB, (j + 1) * MXU_SLAB)

        def item():
            y_scr[:, cols] = _dot(ycat(), wo_ref[:, cols])
        return item

    for j in range(wo_ref.shape[1] // MXU_SLAB):
        dots.append(out_slab(j))
    due.setdefault(len(dots) - 1 + FIN_DELAY, []).append(fin_out)

    o = 3 * d_a
    projections = [(0, fin_u), (d_a, fin_v), (2 * d_a, fin_za), (o, fin_q), (o + d_b, fin_k),
                   (o + 2 * d_b, fin_vt), (o + 3 * d_b, fin_gz)]
    assert d_a == d_b == MXU_SLAB

    assert land_scr.shape[0] == len(projections)

    def in_slab(n, lo):
        def item():
            land_scr[n] = _dot(h(), w_ref[:, lo:lo + MXU_SLAB])
        return item

    def landed(n):
        return lambda: land_scr[n]

    for n, (lo, fin) in enumerate(projections):
        dots.append(in_slab(n, lo))
        due.setdefault(len(dots) - 1 + FIN_DELAY, []).append(functools.partial(fin, landed(n)))
    items = [item_h]
    for n, dot_item in enumerate(dots):
        items.append(both(dot_item, *due.pop(n, [])))
    items.append(both(*[f for n in sorted(due) for f in due[n]], pre_scores))

    lane = lax.broadcasted_iota(jnp.int32, (TQ, LANES), 1)
    half = TK // 2
    s_idx = lax.broadcasted_iota(jnp.int32, (half, half), 0)
    t_idx = lax.broadcasted_iota(jnp.int32, (half, half), 1)
    strictly_earlier_half = s_idx < t_idx
    j_idx = lax.broadcasted_iota(jnp.int32, (TK, TK), 1)
    r_idx = lax.broadcasted_iota(jnp.int32, (TK, TK), 0)
    strictly_later = jnp.where(j_idx > r_idx, 1.0, 0.0).astype(BF16)

    q_masked = {}
    for qi in range(Q_PER_STEP):
        for hd in range(n_heads):
            p, hh = divmod(hd, heads_per_tile)
            q_pair = q_ring[prev2, qi * TQ:(qi + 1) * TQ, p * LANES:(p + 1) * LANES]
            q_masked[qi, hd] = head_of(q_pair, hh)

    lo_half, hi_half = slice(0, half), slice(half, TK)
    diag_parts = [(lo_half, lo_half, "tri"), (lo_half, hi_half, "full"),
                  (hi_half, lo_half, "empty"), (hi_half, hi_half, "tri")]
    full_parts = [(slice(0, TK), slice(0, TQ), "full")]

    def assemble(parts, pieces):
        if len(parts) == 1:
            return pieces[0]
        return jnp.concatenate([jnp.concatenate(pieces[0:2], axis=1),
                                jnp.concatenate(pieces[2:4], axis=1)], axis=0)

    def gated_pair(accs, qi, p):
        rows, cols = slice(qi * TQ, (qi + 1) * TQ), slice(p * LANES, (p + 1) * LANES)
        yT = jnp.concatenate([accs[qi, heads_per_tile * p + n] for n in range(heads_per_tile)], axis=0)
        yb_ring[now2, rows, cols] = (yT.T * gz_ring[prev2, rows, cols]).astype(BF16)

    def key_blocks(blocks, runs, accs, extra=(), finish=False):
        runs, accs = dict(runs), dict(accs)
        tiles = [dict(qi=qi, kb=kb, h=hd, keep=keep, diag=diag, parts=diag_parts if diag else full_parts)
                 for qi, kb, diag, keep in blocks for hd in range(n_heads)]
        last_tile = {(t["qi"], t["h"]): n for n, t in enumerate(tiles)}

        def scores(t):
            if t["diag"]:
                t["z"] = z_ring[prev2, t["qi"] * n_heads + t["h"]]
                return
            start = pl.multiple_of(t["kb"] * TK, TK)
            p = t["h"] // heads_per_tile
            lanes = slice(p * LANES, (p + 1) * LANES)
            t["z"] = _dot_nt(k_scr[par_at, pl.ds(start, TK), lanes], q_masked[t["qi"], t["h"]])

        def softplus(t):
            sps, t["logb"] = [], []
            for rows, cols, kind in t["parts"]:
                if kind == "empty":
                    sps.append(jnp.zeros((half, half), BF16))
                    t["logb"].append(None)
                    continue
                z = t["z"][rows, cols]
                sp = jnp.maximum(z, 0.0) + jnp.log2(1.0 + jnp.exp2(-jnp.abs(z)))
                t["logb"].append(z - sp)
                if kind == "tri":
                    sp = jnp.where(strictly_earlier_half, sp, 0.0)
                sps.append(sp.astype(BF16))
            t["sp"] = assemble(t["parts"], sps)

        def later_sum(t):
            t["csum"] = _dot(strictly_later, t["sp"])

        def weights(t):
            key = (t["qi"], t["h"])
            run = runs[key]
            probs = []
            for (rows, cols, kind), logb in zip(t["parts"], t["logb"]):
                if kind == "empty":
                    probs.append(jnp.zeros((half, half), BF16))
                    continue
                a = jnp.exp2(logb - (run[:, cols] + t["csum"][rows, cols]))
                if kind == "tri":
                    a = jnp.where(strictly_earlier_half, a, 0.0)
                probs.append(a.astype(BF16))
            t["prob"] = assemble(t["parts"], probs)
            runs[key] = run + (t["csum"][0:1, :] + t["sp"][0:1, :].astype(F32))

        def values(n, t):
            key = (t["qi"], t["h"])
            hd = t["h"]
            v_blk = vT_scr[par_at, t["kb"], hd * GROUP_DIM:(hd + 1) * GROUP_DIM, :]
            if t["keep"] is not None:
                v_blk = (v_blk.astype(F32) * t["keep"]).astype(BF16)
            accs[key] = accs[key] + _dot(v_blk, t["prob"])
            if finish and last_tile[key] == n and hd % heads_per_tile == heads_per_tile - 1:
                gated_pair(accs, t["qi"], hd // heads_per_tile)

        n_tiles = len(tiles)
        lead, lag = SKEW
        ready = next((n for n, t in enumerate(tiles) if not t["diag"]), n_tiles)
        for t in tiles[:ready]:
            scores(t)
        steps = range(min(0, ready - lead), n_tiles + lag)
        extra = list(extra)
        slots = [EXTRA_START + (n * (len(steps) - EXTRA_START)) // max(1, len(extra))
                 for n in range(len(extra))]
        for count, step in enumerate(steps):
            while extra and slots[0] <= count:
                slots.pop(0)
                extra.pop(0)()
            if ready <= step + lead < n_tiles:
                scores(tiles[step + lead])
            if 0 <= step < n_tiles:
                softplus(tiles[step])
                later_sum(tiles[step])
            if 0 <= step - lag < n_tiles:
                weights(tiles[step - lag])
                values(step - lag, tiles[step - lag])
        for item in extra:
            item()
        return runs, accs

    runs = {(qi, hd): jnp.zeros((1, TQ), F32) for qi in range(Q_PER_STEP) for hd in range(n_heads)}
    accs = {(qi, hd): jnp.zeros((GROUP_DIM, TQ), F32) for qi in range(Q_PER_STEP) for hd in range(n_heads)}
    first_kb = j_at * Q_PER_STEP
    diag_blocks = [(qi, first_kb + qi, True, None) for qi in range(Q_PER_STEP)]
    has_earlier = (j_at > 0).astype(F32)
    prev_blocks = [(qi, jnp.maximum(first_kb + qi - 1, 0), False, has_earlier if qi == 0 else None)
                   for qi in range(Q_PER_STEP)]
    runs, accs = key_blocks(diag_blocks + prev_blocks, runs, accs, items, finish=True)

    for qi in range(Q_PER_STEP):
        def sticks_left(runs):
            return jnp.min(functools.reduce(jnp.minimum, [runs[qi, hd] for hd in range(n_heads)])) < UNDERFLOW_LOG2

        def earlier_block(state, qi=qi, sticks_left=sticks_left):
            kb, _, runs, accs = state
            runs, accs = key_blocks([(qi, kb, False, None)], runs, accs)
            return kb - 1, sticks_left(runs), runs, accs

        first = first_kb + qi - 2
        mine = lambda d: {k: v for k, v in d.items() if k[0] == qi}

        @pl.when((first >= 0) & sticks_left(runs))
        def _(qi=qi, first=first, earlier_block=earlier_block, mine=mine):
            _, _, _, more = lax.while_loop(lambda st: (st[0] >= 0) & st[1], earlier_block,
                                           (first, True, mine(runs), mine(accs)))
            for p in range(n_pairs):
                gated_pair(more, qi, p)


def _fused_block(x, mod3, w, w_sp, b_sp_full, a_g, a_b, w_out, ln_g, ln_b, alpha):
    bsz, seq, d = x.shape
    d_a = a_g.shape[-1]
    d_b = (w.shape[1] - 3 * d_a) // 4
    per_seq = seq // ROWS
    n_total = bsz * per_seq
    grid = (n_total + 2,)

    def stage(lag):
        return lambda s: _stage_slabs(s, n_total, per_seq)[lag]

    def rows_of(lag):
        return pl.BlockSpec((1, ROWS, d), lambda s: (*stage(lag)(s), 0))

    def mod_of(lag):
        return pl.BlockSpec((1, 3, d), lambda s: (stage(lag)(s)[0], 0, 0))

    def whole(a):
        return pl.BlockSpec(a.shape, lambda s: (0,) * a.ndim)

    return pl.pallas_call(
        functools.partial(_block_kernel, alpha, n_total),
        grid=grid,
        in_specs=[rows_of(0), rows_of(2), mod_of(0), mod_of(2), whole(w), whole(w_sp),
                  whole(b_sp_full), whole(a_g), whole(a_b), whole(w_out), whole(ln_g), whole(ln_b)],
        out_specs=rows_of(2),
        out_shape=jax.ShapeDtypeStruct((bsz, seq, d), x.dtype),
        scratch_shapes=[
            pltpu.VMEM((2, seq, d_b), BF16),
            pltpu.VMEM((2, seq // TK, d_b, TK), BF16),
            pltpu.VMEM((2, ROWS, d_b), BF16),
            pltpu.VMEM((2, ROWS, d_b), F32),
            pltpu.VMEM((3, ROWS, d_a), BF16),
            pltpu.VMEM((2, ROWS, d_b), BF16),
            pltpu.VMEM((2, Q_PER_STEP * (d_b // GROUP_DIM), TK, TQ), F32),
            pltpu.VMEM((w.shape[1] // MXU_SLAB, ROWS, MXU_SLAB), F32),
            pltpu.VMEM((ROWS, d), F32),
        ],
        compiler_params=pltpu.CompilerParams(
            dimension_semantics=("arbitrary",), vmem_limit_bytes=VMEM_LIMIT),
        name="fused_block",
    )(x, x, mod3, mod3, w, w_sp, b_sp_full, a_g, a_b, w_out, ln_g, ln_b)


def kernel(x, c, w_ada, b_ada, w_in, sgu_ln_g, sgu_ln_b, w_spatial, b_spatial, w_out, ln_g, ln_b):
    depth = w_ada.shape[0]
    bsz, seq, d = x.shape
    d_a = sgu_ln_g.shape[-1]
    alpha = (2.0 * depth) ** 0.25
    for layer in range(depth):
        mod, w_in_bf, w_out_bf = _prepare(c, w_ada[layer], b_ada[layer], w_in[layer], w_out[layer])
        b_sp_full = jnp.repeat(b_spatial[layer].T, GROUP_DIM, axis=1)
        x = _fused_block(
            x, mod.reshape(bsz, 3, d), w_in_bf, w_spatial[layer], b_sp_full,
            sgu_ln_g[layer].reshape(1, d_a), sgu_ln_b[layer].reshape(1, d_a),
            w_out_bf, ln_g[layer].reshape(1, d), ln_b[layer].reshape(1, d), alpha)
    return x
```

```python
import functools
import math

import jax
import jax.numpy as jnp
from jax import lax
from jax.experimental import pallas as pl
from jax.experimental.pallas import tpu as pltpu

F32 = jnp.float32
BF16 = jnp.bfloat16

N_GROUPS = 8
GROUP_DIM = 64
CHUNK = 128
LN_EPS = 1e-5
LANES = 128

PREP_TILE = 512
TQ = 256
TK = 256
Q_PER_STEP = 1
ROWS = Q_PER_STEP * TQ
SKEW = (8, 3)
EXTRA_START = 2
MXU_SLAB = 512
FIN_DELAY = 1
UNDERFLOW_LOG2 = 152.0
VMEM_LIMIT = 48 * 1024 * 1024


def _dot(a, b):
    return jnp.dot(a, b, preferred_element_type=F32)


def _dot_nt(a, b):
    return lax.dot_general(a, b, (((1,), (1,)), ((), ())), preferred_element_type=F32)


def _gelu_tanh(x):
    k1 = -2.0 * math.sqrt(2.0 / math.pi) * math.log2(math.e)
    return x / (1.0 + jnp.exp2(x * (k1 + (k1 * 0.044715) * (x * x))))


def _silu(x):
    return x / (1.0 + jnp.exp(-x))


def _layer_norm(x, g, b):
    mu = jnp.mean(x, axis=-1, keepdims=True)
    xc = x - mu
    var = jnp.mean(xc * xc, axis=-1, keepdims=True)
    return xc * lax.rsqrt(var + LN_EPS) * g + b


def _prep_kernel(c_ref, wa_ref, ba_ref, win_ref, wout_ref, mod_ref, win_bf_ref, wout_bf_ref):
    sc = _silu(c_ref[...]).astype(BF16)
    mod_ref[...] = _dot(sc, wa_ref[...].astype(BF16)) + ba_ref[...]
    win_bf_ref[...] = win_ref[...].astype(BF16)
    wout_bf_ref[...] = wout_ref[...].astype(BF16)


def _prepare(c, w_ada, b_ada, w_in, w_out):
    bsz, d = c.shape
    n_mod = w_ada.shape[1]
    steps = w_in.shape[1] // PREP_TILE
    mod_tiles = n_mod // PREP_TILE
    out_tile = PREP_TILE
    out_tiles = w_out.shape[1] // out_tile
    assert mod_tiles <= steps and out_tiles <= steps
    mod_col = lambda j: (0, jnp.minimum(j, mod_tiles - 1))
    out_col = lambda j: (0, jnp.minimum(j, out_tiles - 1))
    return pl.pallas_call(
        _prep_kernel,
        grid=(steps,),
        in_specs=[
            pl.BlockSpec((bsz, d), lambda j: (0, 0)),
            pl.BlockSpec((d, PREP_TILE), mod_col),
            pl.BlockSpec((1, PREP_TILE), mod_col),
            pl.BlockSpec((w_in.shape[0], PREP_TILE), lambda j: (0, j)),
            pl.BlockSpec((w_out.shape[0], out_tile), out_col),
        ],
        out_specs=[
            pl.BlockSpec((bsz, PREP_TILE), mod_col),
            pl.BlockSpec((w_in.shape[0], PREP_TILE), lambda j: (0, j)),
            pl.BlockSpec((w_out.shape[0], out_tile), out_col),
        ],
        out_shape=[
            jax.ShapeDtypeStruct((bsz, n_mod), F32),
            jax.ShapeDtypeStruct(w_in.shape, BF16),
            jax.ShapeDtypeStruct(w_out.shape, BF16),
        ],
        compiler_params=pltpu.CompilerParams(
            dimension_semantics=("arbitrary",), vmem_limit_bytes=VMEM_LIMIT),
        name="adaln_mod_and_casts",
    )(c, w_ada, b_ada.reshape(1, n_mod), w_in, w_out)


def _stage_slabs(s, n_total, per_seq):
    last = n_total - 1
    out = []
    for lag in range(3):
        t = jnp.clip(s - lag, 0, last)
        out.append((lax.div(t, per_seq), lax.rem(t, per_seq)))
    return out


def _block_kernel(alpha, n_total,
                  x_in_ref, x_out_ref, mod_in_ref, mod_out_ref, w_ref, wsp_ref, bsp_ref,
                  ag_ref, ab_ref, wo_ref, g_ref, b_ref, o_ref,
                  k_scr, vT_scr, q_ring, gz_ring, ya_ring, yb_ring, z_ring, land_scr, y_scr):
    s = pl.program_id(0)
    d_b = q_ring.shape[-1]
    d_a = ya_ring.shape[-1]
    per_seq = k_scr.shape[1] // ROWS
    (b_in, j_in), (b_at, j_at), _ = _stage_slabs(s, n_total, per_seq)
    par_in, par_at = lax.rem(b_in, 2), lax.rem(b_at, 2)
    n_heads = d_b // GROUP_DIM
    heads_per_tile = LANES // GROUP_DIM
    n_pairs = n_heads // heads_per_tile
    now2, prev2 = lax.rem(s, 2), lax.rem(s + 1, 2)
    now3, prev3 = lax.rem(s, 3), lax.rem(s + 1, 3)

    @pl.when(s == 0)
    def _():
        q_ring[...] = jnp.zeros_like(q_ring)
        gz_ring[...] = jnp.zeros_like(gz_ring)
        ya_ring[...] = jnp.zeros_like(ya_ring)
        yb_ring[...] = jnp.zeros_like(yb_ring)
        z_ring[...] = jnp.zeros_like(z_ring)
        k_scr[0, 0:ROWS, :] = jnp.zeros((ROWS, d_b), BF16)
        for n in range(Q_PER_STEP):
            vT_scr[0, n] = jnp.zeros((d_b, TK), BF16)

    new = {}

    def item_h():
        shift = mod_in_ref[0, 0:1, :]
        scale = mod_in_ref[0, 1:2, :]
        new["h"] = (x_in_ref[0] * (1.0 + scale) + shift).astype(BF16)

    def h():
        return new["h"]

    def ycat():
        return jnp.concatenate([ya_ring[prev3], yb_ring[prev2]], axis=1)

    def both(*fs):
        return lambda: [f() for f in fs]

    def fin_out():
        gate = mod_out_ref[0, 2:3, :]
        r = alpha * x_out_ref[0] + gate * y_scr[...]
        o_ref[0] = _layer_norm(r, g_ref[...], b_ref[...])

    def fin_u(p):
        new["ug"] = _gelu_tanh(p())

    def fin_v(p):
        new["vn"] = _layer_norm(_gelu_tanh(p()), ag_ref[...], ab_ref[...]).astype(BF16)
        t_idx = lax.broadcasted_iota(jnp.int32, (CHUNK, CHUNK), 0)
        s_idx = lax.broadcasted_iota(jnp.int32, (CHUNK, CHUNK), 1)
        causal = t_idx >= s_idx
        w_sp = [jnp.where(causal, wsp_ref[g], 0.0).astype(BF16) for g in range(N_GROUPS)]
        first_group = lax.broadcasted_iota(jnp.int32, (CHUNK, LANES), 1) < GROUP_DIM
        chunks = []
        for c in range(ROWS // CHUNK):
            rows = slice(c * CHUNK, (c + 1) * CHUNK)
            pieces = []
            for pr in range(d_a // LANES):
                cols = slice(pr * LANES, (pr + 1) * LANES)
                vp = new["vn"][rows, cols]
                mixed = jnp.where(first_group, _dot(w_sp[2 * pr], vp), _dot(w_sp[2 * pr + 1], vp))
                pieces.append(new["ug"][rows, cols] * (mixed + bsp_ref[:, cols]))
            chunks.append(jnp.concatenate(pieces, axis=1))
        new["ya_pre"] = jnp.concatenate(chunks, axis=0)

    def fin_za(p):
        ya_ring[now3] = (_silu(p()) * new["ya_pre"]).astype(BF16)

    scale_q = math.log2(math.e) / math.sqrt(GROUP_DIM)

    def fin_q(p):
        q_ring[now2] = (p() * scale_q).astype(BF16)

    def fin_k(p):
        k_scr[par_in, pl.ds(pl.multiple_of(j_in * ROWS, ROWS), ROWS), :] = p().astype(BF16)

    def head_of(q_pair, hh):
        in_head = (lane >= hh * GROUP_DIM) & (lane < (hh + 1) * GROUP_DIM)
        return jnp.where(in_head, q_pair, jnp.zeros_like(q_pair))

    def pre_scores():
        for qi in range(Q_PER_STEP):
            rows = slice(qi * TQ, (qi + 1) * TQ)
            for hd in range(n_heads):
                p, hh = divmod(hd, heads_per_tile)
                lanes = slice(p * LANES, (p + 1) * LANES)
                keys = k_scr[par_in, pl.ds(pl.multiple_of(j_in * ROWS + qi * TQ, TQ), TQ), lanes]
                z_ring[now2, qi * n_heads + hd] = _dot_nt(keys, head_of(q_ring[now2, rows, lanes], hh))

    def fin_vt(p):
        v = p()
        for n in range(Q_PER_STEP):
            vT_scr[par_in, j_in * Q_PER_STEP + n] = v[n * TK:(n + 1) * TK, :].T.astype(BF16)

    def fin_gz(p):
        gz_ring[now2] = _silu(p())

    dots, due = [], {}

    def out_slab(j):
        cols = slice(j * MXU_SLAB, (j + 1) * MXU_SLAB)

        def item():
            y_scr[:, cols] = _dot(ycat(), wo_ref[:, cols])
        return item

    for j in range(wo_ref.shape[1] // MXU_SLAB):
        dots.append(out_slab(j))
    due.setdefault(len(dots) - 1 + FIN_DELAY, []).append(fin_out)

    o = 3 * d_a
    projections = [(0, fin_u), (d_a, fin_v), (2 * d_a, fin_za), (o, fin_q), (o + d_b, fin_k),
                   (o + 2 * d_b, fin_vt), (o + 3 * d_b, fin_gz)]
    assert d_a == d_b == MXU_SLAB

    assert land_scr.shape[0] == len(projections)

    def in_slab(n, lo):
        def item():
            land_scr[n] = _dot(h(), w_ref[:, lo:lo + MXU_SLAB])
        return item

    def landed(n):
        return lambda: land_scr[n]

    for n, (lo, fin) in enumerate(projections):
        dots.append(in_slab(n, lo))
        due.setdefault(len(dots) - 1 + FIN_DELAY, []).append(functools.partial(fin, landed(n)))
    items = [item_h]
    for n, dot_item in enumerate(dots):
        items.append(both(dot_item, *due.pop(n, [])))
    items.append(both(*[f for n in sorted(due) for f in due[n]], pre_scores))

    lane = lax.broadcasted_iota(jnp.int32, (TQ, LANES), 1)
    half = TK // 2
    s_idx = lax.broadcasted_iota(jnp.int32, (half, half), 0)
    t_idx = lax.broadcasted_iota(jnp.int32, (half, half), 1)
    strictly_earlier_half = s_idx < t_idx
    j_idx = lax.broadcasted_iota(jnp.int32, (TK, TK), 1)
    r_idx = lax.broadcasted_iota(jnp.int32, (TK, TK), 0)
    strictly_later = jnp.where(j_idx > r_idx, 1.0, 0.0).astype(BF16)

    q_masked = {}
    for qi in range(Q_PER_STEP):
        for hd in range(n_heads):
            p, hh = divmod(hd, heads_per_tile)
            q_pair = q_ring[prev2, qi * TQ:(qi + 1) * TQ, p * LANES:(p + 1) * LANES]
            q_masked[qi, hd] = head_of(q_pair, hh)

    lo_half, hi_half = slice(0, half), slice(half, TK)
    diag_parts = [(lo_half, lo_half, "tri"), (lo_half, hi_half, "full"),
                  (hi_half, lo_half, "empty"), (hi_half, hi_half, "tri")]
    full_parts = [(slice(0, TK), slice(0, TQ), "full")]

    def assemble(parts, pieces):
        if len(parts) == 1:
            return pieces[0]
        return jnp.concatenate([jnp.concatenate(pieces[0:2], axis=1),
                                jnp.concatenate(pieces[2:4], axis=1)], axis=0)

    def gated_pair(accs, qi, p):
        rows, cols = slice(qi * TQ, (qi + 1) * TQ), slice(p * LANES, (p + 1) * LANES)
        yT = jnp.concatenate([accs[qi, heads_per_tile * p + n] for n in range(heads_per_tile)], axis=0)
        yb_ring[now2, rows, cols] = (yT.T * gz_ring[prev2, rows, cols]).astype(BF16)

    def key_blocks(blocks, runs, accs, extra=(), finish=False):
        runs, accs = dict(runs), dict(accs)
        tiles = [dict(qi=qi, kb=kb, h=hd, keep=keep, diag=diag, parts=diag_parts if diag else full_parts)
                 for qi, kb, diag, keep in blocks for hd in range(n_heads)]
        last_tile = {(t["qi"], t["h"]): n for n, t in enumerate(tiles)}

        def scores(t):
            if t["diag"]:
                t["z"] = z_ring[prev2, t["qi"] * n_heads + t["h"]]
                return
            start = pl.multiple_of(t["kb"] * TK, TK)
            p = t["h"] // heads_per_tile
            lanes = slice(p * LANES, (p + 1) * LANES)
            t["z"] = _dot_nt(k_scr[par_at, pl.ds(start, TK), lanes], q_masked[t["qi"], t["h"]])

        def softplus(t):
            sps, t["logb"] = [], []
            for rows, cols, kind in t["parts"]:
                if kind == "empty":
                    sps.append(jnp.zeros((half, half), BF16))
                    t["logb"].append(None)
                    continue
                z = t["z"][rows, cols]
                sp = jnp.maximum(z, 0.0) + jnp.log2(1.0 + jnp.exp2(-jnp.abs(z)))
                t["logb"].append(z - sp)
                if kind == "tri":
                    sp = jnp.where(strictly_earlier_half, sp, 0.0)
                sps.append(sp.astype(BF16))
            t["sp"] = assemble(t["parts"], sps)

        def later_sum(t):
            t["csum"] = _dot(strictly_later, t["sp"])

        def weights(t):
            key = (t["qi"], t["h"])
            run = runs[key]
            probs = []
            for (rows, cols, kind), logb in zip(t["parts"], t["logb"]):
                if kind == "empty":
                    probs.append(jnp.zeros((half, half), BF16))
                    continue
                a = jnp.exp2(logb - (run[:, cols] + t["csum"][rows, cols]))
                if kind == "tri":
                    a = jnp.where(strictly_earlier_half, a, 0.0)
                probs.append(a.astype(BF16))
            t["prob"] = assemble(t["parts"], probs)
            runs[key] = run + (t["csum"][0:1, :] + t["sp"][0:1, :].astype(F32))

        def values(n, t):
            key = (t["qi"], t["h"])
            hd = t["h"]
            v_blk = vT_scr[par_at, t["kb"], hd * GROUP_DIM:(hd + 1) * GROUP_DIM, :]
            if t["keep"] is not None:
                v_blk = (v_blk.astype(F32) * t["keep"]).astype(BF16)
            accs[key] = accs[key] + _dot(v_blk, t["prob"])
            if finish and last_tile[key] == n and hd % heads_per_tile == heads_per_tile - 1:
                gated_pair(accs, t["qi"], hd // heads_per_tile)

        n_tiles = len(tiles)
        lead, lag = SKEW
        ready = next((n for n, t in enumerate(tiles) if not t["diag"]), n_tiles)
        for t in tiles[:ready]:
            scores(t)
        steps = range(min(0, ready - lead), n_tiles + lag)
        extra = list(extra)
        slots = [EXTRA_START + (n * (len(steps) - EXTRA_START)) // max(1, len(extra))
                 for n in range(len(extra))]
        for count, step in enumerate(steps):
            while extra and slots[0] <= count:
                slots.pop(0)
                extra.pop(0)()
            if ready <= step + lead < n_tiles:
                scores(tiles[step + lead])
            if 0 <= step < n_tiles:
                softplus(tiles[step])
                later_sum(tiles[step])
            if 0 <= step - lag < n_tiles:
                weights(tiles[step - lag])
                values(step - lag, tiles[step - lag])
        for item in extra:
            item()
        return runs, accs

    runs = {(qi, hd): jnp.zeros((1, TQ), F32) for qi in range(Q_PER_STEP) for hd in range(n_heads)}
    accs = {(qi, hd): jnp.zeros((GROUP_DIM, TQ), F32) for qi in range(Q_PER_STEP) for hd in range(n_heads)}
    first_kb = j_at * Q_PER_STEP
    diag_blocks = [(qi, first_kb + qi, True, None) for qi in range(Q_PER_STEP)]
    has_earlier = (j_at > 0).astype(F32)
    prev_blocks = [(qi, jnp.maximum(first_kb + qi - 1, 0), False, has_earlier if qi == 0 else None)
                   for qi in range(Q_PER_STEP)]
    runs, accs = key_blocks(diag_blocks + prev_blocks, runs, accs, items, finish=True)

    for qi in range(Q_PER_STEP):
        def sticks_left(runs):
            return jnp.min(functools.reduce(jnp.minimum, [runs[qi, hd] for hd in range(n_heads)])) < UNDERFLOW_LOG2

        def earlier_block(state, qi=qi, sticks_left=sticks_left):
            kb, _, runs, accs = state
            runs, accs = key_blocks([(qi, kb, False, None)], runs, accs)
            return kb - 1, sticks_left(runs), runs, accs

        first = first_kb + qi - 2
        mine = lambda d: {k: v for k, v in d.items() if k[0] == qi}

        @pl.when((first >= 0) & sticks_left(runs))
        def _(qi=qi, first=first, earlier_block=earlier_block, mine=mine):
            _, _, _, more = lax.while_loop(lambda st: (st[0] >= 0) & st[1], earlier_block,
                                           (first, True, mine(runs), mine(accs)))
            for p in range(n_pairs):
                gated_pair(more, qi, p)


def _fused_block(x, mod3, w, w_sp, b_sp_full, a_g, a_b, w_out, ln_g, ln_b, alpha):
    bsz, seq, d = x.shape
    d_a = a_g.shape[-1]
    d_b = (w.shape[1] - 3 * d_a) // 4
    per_seq = seq // ROWS
    n_total = bsz * per_seq
    grid = (n_total + 2,)

    def stage(lag):
        return lambda s: _stage_slabs(s, n_total, per_seq)[lag]

    def rows_of(lag):
        return pl.BlockSpec((1, ROWS, d), lambda s: (*stage(lag)(s), 0))

    def mod_of(lag):
        return pl.BlockSpec((1, 3, d), lambda s: (stage(lag)(s)[0], 0, 0))

    def whole(a):
        return pl.BlockSpec(a.shape, lambda s: (0,) * a.ndim)

    return pl.pallas_call(
        functools.partial(_block_kernel, alpha, n_total),
        grid=grid,
        in_specs=[rows_of(0), rows_of(2), mod_of(0), mod_of(2), whole(w), whole(w_sp),
                  whole(b_sp_full), whole(a_g), whole(a_b), whole(w_out), whole(ln_g), whole(ln_b)],
        out_specs=rows_of(2),
        out_shape=jax.ShapeDtypeStruct((bsz, seq, d), x.dtype),
        scratch_shapes=[
            pltpu.VMEM((2, seq, d_b), BF16),
            pltpu.VMEM((2, seq // TK, d_b, TK), BF16),
            pltpu.VMEM((2, ROWS, d_b), BF16),
            pltpu.VMEM((2, ROWS, d_b), F32),
            pltpu.VMEM((3, ROWS, d_a), BF16),
            pltpu.VMEM((2, ROWS, d_b), BF16),
            pltpu.VMEM((2, Q_PER_STEP * (d_b // GROUP_DIM), TK, TQ), F32),
            pltpu.VMEM((w.shape[1] // MXU_SLAB, ROWS, MXU_SLAB), F32),
            pltpu.VMEM((ROWS, d), F32),
        ],
        compiler_params=pltpu.CompilerParams(
            dimension_semantics=("arbitrary",), vmem_limit_bytes=VMEM_LIMIT),
        name="fused_block",
    )(x, x, mod3, mod3, w, w_sp, b_sp_full, a_g, a_b, w_out, ln_g, ln_b)


def kernel(x, c, w_ada, b_ada, w_in, sgu_ln_g, sgu_ln_b, w_spatial, b_spatial, w_out, ln_g, ln_b):
    depth = w_ada.shape[0]
    bsz, seq, d = x.shape
    d_a = sgu_ln_g.shape[-1]
    alpha = (2.0 * depth) ** 0.25
    for layer in range(depth):
        mod, w_in_bf, w_out_bf = _prepare(c, w_ada[layer], b_ada[layer], w_in[layer], w_out[layer])
        b_sp_full = jnp.repeat(b_spatial[layer].T, GROUP_DIM, axis=1)
        x = _fused_block(
            x, mod.reshape(bsz, 3, d), w_in_bf, w_spatial[layer], b_sp_full,
            sgu_ln_g[layer].reshape(1, d_a), sgu_ln_b[layer].reshape(1, d_a),
            w_out_bf, ln_g[layer].reshape(1, d), ln_b[layer].reshape(1, d), alpha)
    return x
```

```python
import functools
import math

import jax
import jax.numpy as jnp
from jax import lax
from jax.experimental import pallas as pl
from jax.experimental.pallas import tpu as pltpu

F32 = jnp.float32
BF16 = jnp.bfloat16

N_GROUPS = 8
GROUP_DIM = 64
CHUNK = 128
LN_EPS = 1e-5
LANES = 128

PREP_TILE = 512
TQ = 256
TK = 256
Q_PER_STEP = 2
ROWS = Q_PER_STEP * TQ
SCORES_AHEAD = False
SKEW = (8, 3)
EXTRA_START = 2
MXU_SLAB = 512
FIN_DELAY = 1
UNDERFLOW_LOG2 = 152.0
VMEM_LIMIT = 60 * 1024 * 1024


def _dot(a, b):
    return jnp.dot(a, b, preferred_element_type=F32)


def _dot_nt(a, b):
    return lax.dot_general(a, b, (((1,), (1,)), ((), ())), preferred_element_type=F32)


def _gelu_tanh(x):
    k1 = -2.0 * math.sqrt(2.0 / math.pi) * math.log2(math.e)
    return x / (1.0 + jnp.exp2(x * (k1 + (k1 * 0.044715) * (x * x))))


def _silu(x):
    return x / (1.0 + jnp.exp(-x))


def _layer_norm(x, g, b):
    mu = jnp.mean(x, axis=-1, keepdims=True)
    xc = x - mu
    var = jnp.mean(xc * xc, axis=-1, keepdims=True)
    return xc * lax.rsqrt(var + LN_EPS) * g + b


def _prep_kernel(c_ref, wa_ref, ba_ref, win_ref, wout_ref, mod_ref, win_bf_ref, wout_bf_ref):
    sc = _silu(c_ref[...]).astype(BF16)
    mod_ref[...] = _dot(sc, wa_ref[...].astype(BF16)) + ba_ref[...]
    win_bf_ref[...] = win_ref[...].astype(BF16)
    wout_bf_ref[...] = wout_ref[...].astype(BF16)


def _prepare(c, w_ada, b_ada, w_in, w_out):
    bsz, d = c.shape
    n_mod = w_ada.shape[1]
    steps = w_in.shape[1] // PREP_TILE
    mod_tiles = n_mod // PREP_TILE
    out_tile = PREP_TILE
    out_tiles = w_out.shape[1] // out_tile
    assert mod_tiles <= steps and out_tiles <= steps
    mod_col = lambda j: (0, jnp.minimum(j, mod_tiles - 1))
    out_col = lambda j: (0, jnp.minimum(j, out_tiles - 1))
    return pl.pallas_call(
        _prep_kernel,
        grid=(steps,),
        in_specs=[
            pl.BlockSpec((bsz, d), lambda j: (0, 0)),
            pl.BlockSpec((d, PREP_TILE), mod_col),
            pl.BlockSpec((1, PREP_TILE), mod_col),
            pl.BlockSpec((w_in.shape[0], PREP_TILE), lambda j: (0, j)),
            pl.BlockSpec((w_out.shape[0], out_tile), out_col),
        ],
        out_specs=[
            pl.BlockSpec((bsz, PREP_TILE), mod_col),
            pl.BlockSpec((w_in.shape[0], PREP_TILE), lambda j: (0, j)),
            pl.BlockSpec((w_out.shape[0], out_tile), out_col),
        ],
        out_shape=[
            jax.ShapeDtypeStruct((bsz, n_mod), F32),
            jax.ShapeDtypeStruct(w_in.shape, BF16),
            jax.ShapeDtypeStruct(w_out.shape, BF16),
        ],
        compiler_params=pltpu.CompilerParams(
            dimension_semantics=("arbitrary",), vmem_limit_bytes=VMEM_LIMIT),
        name="adaln_mod_and_casts",
    )(c, w_ada, b_ada.reshape(1, n_mod), w_in, w_out)


def _stage_slabs(s, n_total, per_seq):
    last = n_total - 1
    out = []
    for lag in range(3):
        t = jnp.clip(s - lag, 0, last)
        out.append((lax.div(t, per_seq), lax.rem(t, per_seq)))
    return out


def _block_kernel(alpha, n_total,
                  x_in_ref, x_out_ref, mod_in_ref, mod_out_ref, w_ref, wsp_ref, bsp_ref,
                  ag_ref, ab_ref, wo_ref, g_ref, b_ref, o_ref,
                  k_scr, vT_scr, q_ring, gz_ring, ya_ring, yb_ring, z_ring, land_scr, y_scr):
    s = pl.program_id(0)
    d_b = q_ring.shape[-1]
    d_a = ya_ring.shape[-1]
    per_seq = k_scr.shape[1] // ROWS
    (b_in, j_in), (b_at, j_at), _ = _stage_slabs(s, n_total, per_seq)
    par_in, par_at = lax.rem(b_in, 2), lax.rem(b_at, 2)
    n_heads = d_b // GROUP_DIM
    heads_per_tile = LANES // GROUP_DIM
    n_pairs = n_heads // heads_per_tile
    now2, prev2 = lax.rem(s, 2), lax.rem(s + 1, 2)
    now3, prev3 = lax.rem(s, 3), lax.rem(s + 1, 3)

    @pl.when(s == 0)
    def _():
        q_ring[...] = jnp.zeros_like(q_ring)
        gz_ring[...] = jnp.zeros_like(gz_ring)
        ya_ring[...] = jnp.zeros_like(ya_ring)
        yb_ring[...] = jnp.zeros_like(yb_ring)
        z_ring[...] = jnp.zeros_like(z_ring)
        k_scr[0, 0:ROWS, :] = jnp.zeros((ROWS, d_b), BF16)
        for n in range(Q_PER_STEP):
            vT_scr[0, n] = jnp.zeros((d_b, TK), BF16)

    new = {}

    def item_h():
        shift = mod_in_ref[0, 0:1, :]
        scale = mod_in_ref[0, 1:2, :]
        new["h"] = (x_in_ref[0] * (1.0 + scale) + shift).astype(BF16)

    def h():
        return new["h"]

    def ycat():
        return jnp.concatenate([ya_ring[prev3], yb_ring[prev2]], axis=1)

    def both(*fs):
        return lambda: [f() for f in fs]

    def fin_out():
        gate = mod_out_ref[0, 2:3, :]
        r = alpha * x_out_ref[0] + gate * y_scr[...]
        o_ref[0] = _layer_norm(r, g_ref[...], b_ref[...])

    def fin_u(p):
        new["ug"] = _gelu_tanh(p())

    def fin_v(p):
        new["vn"] = _layer_norm(_gelu_tanh(p()), ag_ref[...], ab_ref[...]).astype(BF16)
        t_idx = lax.broadcasted_iota(jnp.int32, (CHUNK, CHUNK), 0)
        s_idx = lax.broadcasted_iota(jnp.int32, (CHUNK, CHUNK), 1)
        causal = t_idx >= s_idx
        w_sp = [jnp.where(causal, wsp_ref[g], 0.0).astype(BF16) for g in range(N_GROUPS)]
        first_group = lax.broadcasted_iota(jnp.int32, (CHUNK, LANES), 1) < GROUP_DIM
        chunks = []
        for c in range(ROWS // CHUNK):
            rows = slice(c * CHUNK, (c + 1) * CHUNK)
            pieces = []
            for pr in range(d_a // LANES):
                cols = slice(pr * LANES, (pr + 1) * LANES)
                vp = new["vn"][rows, cols]
                mixed = jnp.where(first_group, _dot(w_sp[2 * pr], vp), _dot(w_sp[2 * pr + 1], vp))
                pieces.append(new["ug"][rows, cols] * (mixed + bsp_ref[:, cols]))
            chunks.append(jnp.concatenate(pieces, axis=1))
        new["ya_pre"] = jnp.concatenate(chunks, axis=0)

    def fin_za(p):
        ya_ring[now3] = (_silu(p()) * new["ya_pre"]).astype(BF16)

    scale_q = math.log2(math.e) / math.sqrt(GROUP_DIM)

    def fin_q(p):
        new["q"] = (p() * scale_q).astype(BF16)
        q_ring[now2] = new["q"]

    def fin_k(p):
        new["k"] = p().astype(BF16)
        k_scr[par_in, pl.ds(pl.multiple_of(j_in * ROWS, ROWS), ROWS), :] = new["k"]

    def head_of(q_pair, hh):
        in_head = (lane >= hh * GROUP_DIM) & (lane < (hh + 1) * GROUP_DIM)
        return jnp.where(in_head, q_pair, jnp.zeros_like(q_pair))

    def pre_scores():
        for qi in range(Q_PER_STEP):
            rows = slice(qi * TQ, (qi + 1) * TQ)
            for hd in range(n_heads):
                p, hh = divmod(hd, heads_per_tile)
                lanes = slice(p * LANES, (p + 1) * LANES)
                z_ring[now2, qi * n_heads + hd] = _dot_nt(
                    new["k"][rows, lanes], head_of(new["q"][rows, lanes], hh))

    def fin_vt(p):
        v = p()
        for n in range(Q_PER_STEP):
            vT_scr[par_in, j_in * Q_PER_STEP + n] = v[n * TK:(n + 1) * TK, :].T.astype(BF16)

    def fin_gz(p):
        gz_ring[now2] = _silu(p())

    dots, due = [], {}

    def out_slab(j):
        cols = slice(j * MXU_SLAB, (j + 1) * MXU_SLAB)

        def item():
            y_scr[:, cols] = _dot(ycat(), wo_ref[:, cols])
        return item

    for j in range(wo_ref.shape[1] // MXU_SLAB):
        dots.append(out_slab(j))
    due.setdefault(len(dots) - 1 + FIN_DELAY, []).append(fin_out)

    o = 3 * d_a
    projections = [(0, fin_u), (d_a, fin_v), (2 * d_a, fin_za), (o, fin_q), (o + d_b, fin_k),
                   (o + 2 * d_b, fin_vt), (o + 3 * d_b, fin_gz)]
    assert d_a == d_b == MXU_SLAB

    assert land_scr.shape[0] == len(projections)

    def in_slab(n, lo):
        def item():
            land_scr[n] = _dot(h(), w_ref[:, lo:lo + MXU_SLAB])
        return item

    def landed(n):
        return lambda: land_scr[n]

    for n, (lo, fin) in enumerate(projections):
        dots.append(in_slab(n, lo))
        due.setdefault(len(dots) - 1 + FIN_DELAY, []).append(functools.partial(fin, landed(n)))
    items = [item_h]
    for n, dot_item in enumerate(dots):
        items.append(both(dot_item, *due.pop(n, [])))
    items.append(both(*[f for n in sorted(due) for f in due[n]], *([pre_scores] if SCORES_AHEAD else [])))

    lane = lax.broadcasted_iota(jnp.int32, (TQ, LANES), 1)
    half = TK // 2
    s_idx = lax.broadcasted_iota(jnp.int32, (half, half), 0)
    t_idx = lax.broadcasted_iota(jnp.int32, (half, half), 1)
    strictly_earlier_half = s_idx < t_idx
    j_idx = lax.broadcasted_iota(jnp.int32, (TK, TK), 1)
    r_idx = lax.broadcasted_iota(jnp.int32, (TK, TK), 0)
    strictly_later = jnp.where(j_idx > r_idx, 1.0, 0.0).astype(BF16)

    q_masked = {}
    for qi in range(Q_PER_STEP):
        for hd in range(n_heads):
            p, hh = divmod(hd, heads_per_tile)
            q_pair = q_ring[prev2, qi * TQ:(qi + 1) * TQ, p * LANES:(p + 1) * LANES]
            q_masked[qi, hd] = head_of(q_pair, hh)

    lo_half, hi_half = slice(0, half), slice(half, TK)
    diag_parts = [(lo_half, lo_half, "tri"), (lo_half, hi_half, "full"),
                  (hi_half, lo_half, "empty"), (hi_half, hi_half, "tri")]
    full_parts = [(slice(0, TK), slice(0, TQ), "full")]

    def assemble(parts, pieces):
        if len(parts) == 1:
            return pieces[0]
        return jnp.concatenate([jnp.concatenate(pieces[0:2], axis=1),
                                jnp.concatenate(pieces[2:4], axis=1)], axis=0)

    def gated_pair(accs, qi, p):
        rows, cols = slice(qi * TQ, (qi + 1) * TQ), slice(p * LANES, (p + 1) * LANES)
        yT = jnp.concatenate([accs[qi, heads_per_tile * p + n] for n in range(heads_per_tile)], axis=0)
        yb_ring[now2, rows, cols] = (yT.T * gz_ring[prev2, rows, cols]).astype(BF16)

    def key_blocks(blocks, runs, accs, extra=(), finish=False):
        runs, accs = dict(runs), dict(accs)
        tiles = [dict(qi=qi, kb=kb, h=hd, keep=keep, diag=diag, parts=diag_parts if diag else full_parts)
                 for qi, kb, diag, keep in blocks for hd in range(n_heads)]
        last_tile = {(t["qi"], t["h"]): n for n, t in enumerate(tiles)}

        def scores(t):
            if SCORES_AHEAD and t["diag"]:
                t["z"] = z_ring[prev2, t["qi"] * n_heads + t["h"]]
                return
            start = pl.multiple_of(t["kb"] * TK, TK)
            p = t["h"] // heads_per_tile
            lanes = slice(p * LANES, (p + 1) * LANES)
            t["z"] = _dot_nt(k_scr[par_at, pl.ds(start, TK), lanes], q_masked[t["qi"], t["h"]])

        def softplus(t):
            sps, t["logb"] = [], []
            for rows, cols, kind in t["parts"]:
                if kind == "empty":
                    sps.append(jnp.zeros((half, half), BF16))
                    t["logb"].append(None)
                    continue
                z = t["z"][rows, cols]
                sp = jnp.maximum(z, 0.0) + jnp.log2(1.0 + jnp.exp2(-jnp.abs(z)))
                t["logb"].append(z - sp)
                if kind == "tri":
                    sp = jnp.where(strictly_earlier_half, sp, 0.0)
                sps.append(sp.astype(BF16))
            t["sp"] = assemble(t["parts"], sps)

        def later_sum(t):
            t["csum"] = _dot(strictly_later, t["sp"])

        def weights(t):
            key = (t["qi"], t["h"])
            run = runs[key]
            probs = []
            for (rows, cols, kind), logb in zip(t["parts"], t["logb"]):
                if kind == "empty":
                    probs.append(jnp.zeros((half, half), BF16))
                    continue
                a = jnp.exp2(logb - (run[:, cols] + t["csum"][rows, cols]))
                if kind == "tri":
                    a = jnp.where(strictly_earlier_half, a, 0.0)
                probs.append(a.astype(BF16))
            t["prob"] = assemble(t["parts"], probs)
            runs[key] = run + (t["csum"][0:1, :] + t["sp"][0:1, :].astype(F32))

        def values(n, t):
            key = (t["qi"], t["h"])
            hd = t["h"]
            v_blk = vT_scr[par_at, t["kb"], hd * GROUP_DIM:(hd + 1) * GROUP_DIM, :]
            if t["keep"] is not None:
                v_blk = (v_blk.astype(F32) * t["keep"]).astype(BF16)
            accs[key] = accs[key] + _dot(v_blk, t["prob"])
            if finish and last_tile[key] == n and hd % heads_per_tile == heads_per_tile - 1:
                gated_pair(accs, t["qi"], hd // heads_per_tile)

        n_tiles = len(tiles)
        lead, lag = SKEW
        ready = next((n for n, t in enumerate(tiles) if not (SCORES_AHEAD and t["diag"])), n_tiles)
        for t in tiles[:ready]:
            scores(t)
        steps = range(min(0, ready - lead), n_tiles + lag)
        extra = list(extra)
        slots = [EXTRA_START + (n * (len(steps) - EXTRA_START)) // max(1, len(extra))
                 for n in range(len(extra))]
        for count, step in enumerate(steps):
            while extra and slots[0] <= count:
                slots.pop(0)
                extra.pop(0)()
            if ready <= step + lead < n_tiles:
                scores(tiles[step + lead])
            if 0 <= step < n_tiles:
                softplus(tiles[step])
                later_sum(tiles[step])
            if 0 <= step - lag < n_tiles:
                weights(tiles[step - lag])
                values(step - lag, tiles[step - lag])
        for item in extra:
            item()
        return runs, accs

    runs = {(qi, hd): jnp.zeros((1, TQ), F32) for qi in range(Q_PER_STEP) for hd in range(n_heads)}
    accs = {(qi, hd): jnp.zeros((GROUP_DIM, TQ), F32) for qi in range(Q_PER_STEP) for hd in range(n_heads)}
    first_kb = j_at * Q_PER_STEP
    diag_blocks = [(qi, first_kb + qi, True, None) for qi in range(Q_PER_STEP)]
    has_earlier = (j_at > 0).astype(F32)
    prev_blocks = [(qi, jnp.maximum(first_kb + qi - 1, 0), False, has_earlier if qi == 0 else None)
                   for qi in range(Q_PER_STEP)]
    runs, accs = key_blocks(diag_blocks + prev_blocks, runs, accs, items, finish=True)

    for qi in range(Q_PER_STEP):
        def sticks_left(runs):
            return jnp.min(functools.reduce(jnp.minimum, [runs[qi, hd] for hd in range(n_heads)])) < UNDERFLOW_LOG2

        def earlier_block(state, qi=qi, sticks_left=sticks_left):
            kb, _, runs, accs = state
            runs, accs = key_blocks([(qi, kb, False, None)], runs, accs)
            return kb - 1, sticks_left(runs), runs, accs

        first = first_kb + qi - 2
        mine = lambda d: {k: v for k, v in d.items() if k[0] == qi}

        @pl.when((first >= 0) & sticks_left(runs))
        def _(qi=qi, first=first, earlier_block=earlier_block, mine=mine):
            _, _, _, more = lax.while_loop(lambda st: (st[0] >= 0) & st[1], earlier_block,
                                           (first, True, mine(runs), mine(accs)))
            for p in range(n_pairs):
                gated_pair(more, qi, p)


def _fused_block(x, mod3, w, w_sp, b_sp_full, a_g, a_b, w_out, ln_g, ln_b, alpha):
    bsz, seq, d = x.shape
    d_a = a_g.shape[-1]
    d_b = (w.shape[1] - 3 * d_a) // 4
    per_seq = seq // ROWS
    n_total = bsz * per_seq
    grid = (n_total + 2,)

    def stage(lag):
        return lambda s: _stage_slabs(s, n_total, per_seq)[lag]

    def rows_of(lag):
        return pl.BlockSpec((1, ROWS, d), lambda s: (*stage(lag)(s), 0))

    def mod_of(lag):
        return pl.BlockSpec((1, 3, d), lambda s: (stage(lag)(s)[0], 0, 0))

    def whole(a):
        return pl.BlockSpec(a.shape, lambda s: (0,) * a.ndim)

    return pl.pallas_call(
        functools.partial(_block_kernel, alpha, n_total),
        grid=grid,
        in_specs=[rows_of(0), rows_of(2), mod_of(0), mod_of(2), whole(w), whole(w_sp),
                  whole(b_sp_full), whole(a_g), whole(a_b), whole(w_out), whole(ln_g), whole(ln_b)],
        out_specs=rows_of(2),
        out_shape=jax.ShapeDtypeStruct((bsz, seq, d), x.dtype),
        scratch_shapes=[
            pltpu.VMEM((2, seq, d_b), BF16),
            pltpu.VMEM((2, seq // TK, d_b, TK), BF16),
            pltpu.VMEM((2, ROWS, d_b), BF16),
            pltpu.VMEM((2, ROWS, d_b), F32),
            pltpu.VMEM((3, ROWS, d_a), BF16),
            pltpu.VMEM((2, ROWS, d_b), BF16),
            pltpu.VMEM((2, Q_PER_STEP * (d_b // GROUP_DIM) if SCORES_AHEAD else 1, TK, TQ), F32),
            pltpu.VMEM((w.shape[1] // MXU_SLAB, ROWS, MXU_SLAB), F32),
            pltpu.VMEM((ROWS, d), F32),
        ],
        compiler_params=pltpu.CompilerParams(
            dimension_semantics=("arbitrary",), vmem_limit_bytes=VMEM_LIMIT),
        name="fused_block",
    )(x, x, mod3, mod3, w, w_sp, b_sp_full, a_g, a_b, w_out, ln_g, ln_b)


def kernel(x, c, w_ada, b_ada, w_in, sgu_ln_g, sgu_ln_b, w_spatial, b_spatial, w_out, ln_g, ln_b):
    depth = w_ada.shape[0]
    bsz, seq, d = x.shape
    d_a = sgu_ln_g.shape[-1]
    alpha = (2.0 * depth) ** 0.25
    for layer in range(depth):
        mod, w_in_bf, w_out_bf = _prepare(c, w_ada[layer], b_ada[layer], w_in[layer], w_out[layer])
        b_sp_full = jnp.repeat(b_spatial[layer].T, GROUP_DIM, axis=1)
        x = _fused_block(
            x, mod.reshape(bsz, 3, d), w_in_bf, w_spatial[layer], b_sp_full,
            sgu_ln_g[layer].reshape(1, d_a), sgu_ln_b[layer].reshape(1, d_a),
            w_out_bf, ln_g[layer].reshape(1, d), ln_b[layer].reshape(1, d), alpha)
    return x
```

```python
import functools
import math

import jax
import jax.numpy as jnp
from jax import lax
from jax.experimental import pallas as pl
from jax.experimental.pallas import tpu as pltpu

F32 = jnp.float32
BF16 = jnp.bfloat16

N_GROUPS = 8
GROUP_DIM = 64
CHUNK = 128
LN_EPS = 1e-5
LANES = 128

PREP_TILE = 512
TQ = 256
TK = 256
Q_PER_STEP = 1
ROWS = Q_PER_STEP * TQ
SKEW = (8, 3)
EXTRA_START = 2
MXU_SLAB = 512
FIN_DELAY = 1
UNDERFLOW_LOG2 = 152.0
VMEM_LIMIT = 48 * 1024 * 1024


def _dot(a, b):
    return jnp.dot(a, b, preferred_element_type=F32)


def _dot_nt(a, b):
    return lax.dot_general(a, b, (((1,), (1,)), ((), ())), preferred_element_type=F32)


def _gelu_tanh(x):
    k1 = -2.0 * math.sqrt(2.0 / math.pi) * math.log2(math.e)
    return x / (1.0 + jnp.exp2(x * (k1 + (k1 * 0.044715) * (x * x))))


def _silu(x):
    return x / (1.0 + jnp.exp(-x))


def _layer_norm(x, g, b):
    mu = jnp.mean(x, axis=-1, keepdims=True)
    xc = x - mu
    var = jnp.mean(xc * xc, axis=-1, keepdims=True)
    return xc * lax.rsqrt(var + LN_EPS) * g + b


def _prep_kernel(c_ref, wa_ref, ba_ref, win_ref, wout_ref, mod_ref, win_bf_ref, wout_bf_ref):
    sc = _silu(c_ref[...]).astype(BF16)
    mod_ref[...] = _dot(sc, wa_ref[...].astype(BF16)) + ba_ref[...]
    win_bf_ref[...] = win_ref[...].astype(BF16)
    wout_bf_ref[...] = wout_ref[...].astype(BF16)


def _prepare(c, w_ada, b_ada, w_in, w_out):
    bsz, d = c.shape
    n_mod = w_ada.shape[1]
    steps = w_in.shape[1] // PREP_TILE
    mod_tiles = n_mod // PREP_TILE
    out_tile = PREP_TILE
    out_tiles = w_out.shape[1] // out_tile
    assert mod_tiles <= steps and out_tiles <= steps
    mod_col = lambda j: (0, jnp.minimum(j, mod_tiles - 1))
    out_col = lambda j: (0, jnp.minimum(j, out_tiles - 1))
    return pl.pallas_call(
        _prep_kernel,
        grid=(steps,),
        in_specs=[
            pl.BlockSpec((bsz, d), lambda j: (0, 0)),
            pl.BlockSpec((d, PREP_TILE), mod_col),
            pl.BlockSpec((1, PREP_TILE), mod_col),
            pl.BlockSpec((w_in.shape[0], PREP_TILE), lambda j: (0, j)),
            pl.BlockSpec((w_out.shape[0], out_tile), out_col),
        ],
        out_specs=[
            pl.BlockSpec((bsz, PREP_TILE), mod_col),
            pl.BlockSpec((w_in.shape[0], PREP_TILE), lambda j: (0, j)),
            pl.BlockSpec((w_out.shape[0], out_tile), out_col),
        ],
        out_shape=[
            jax.ShapeDtypeStruct((bsz, n_mod), F32),
            jax.ShapeDtypeStruct(w_in.shape, BF16),
            jax.ShapeDtypeStruct(w_out.shape, BF16),
        ],
        compiler_params=pltpu.CompilerParams(
            dimension_semantics=("arbitrary",), vmem_limit_bytes=VMEM_LIMIT),
        name="adaln_mod_and_casts",
    )(c, w_ada, b_ada.reshape(1, n_mod), w_in, w_out)


def _stage_slabs(s, n_total, per_seq):
    last = n_total - 1
    out = []
    for lag in range(3):
        t = jnp.clip(s - lag, 0, last)
        out.append((lax.div(t, per_seq), lax.rem(t, per_seq)))
    return out


def _block_kernel(alpha, n_total,
                  x_in_ref, x_out_ref, mod_in_ref, mod_out_ref, w_ref, wsp_ref, bsp_ref,
                  ag_ref, ab_ref, wo_ref, g_ref, b_ref, o_ref,
                  k_scr, vT_scr, q_ring, gz_ring, ya_ring, yb_ring, z_ring, land_scr, y_scr):
    s = pl.program_id(0)
    d_b = q_ring.shape[-1]
    d_a = ya_ring.shape[-1]
    per_seq = k_scr.shape[1] // ROWS
    (b_in, j_in), (b_at, j_at), _ = _stage_slabs(s, n_total, per_seq)
    par_in, par_at = lax.rem(b_in, 2), lax.rem(b_at, 2)
    n_heads = d_b // GROUP_DIM
    heads_per_tile = LANES // GROUP_DIM
    n_pairs = n_heads // heads_per_tile
    now2, prev2 = lax.rem(s, 2), lax.rem(s + 1, 2)
    now3, prev3 = lax.rem(s, 3), lax.rem(s + 1, 3)

    @pl.when(s == 0)
    def _():
        q_ring[...] = jnp.zeros_like(q_ring)
        gz_ring[...] = jnp.zeros_like(gz_ring)
        ya_ring[...] = jnp.zeros_like(ya_ring)
        yb_ring[...] = jnp.zeros_like(yb_ring)
        z_ring[...] = jnp.zeros_like(z_ring)
        k_scr[0, 0:ROWS, :] = jnp.zeros((ROWS, d_b), BF16)
        for n in range(Q_PER_STEP):
            vT_scr[0, n] = jnp.zeros((d_b, TK), BF16)

    new = {}

    def item_h():
        shift = mod_in_ref[0, 0:1, :]
        scale = mod_in_ref[0, 1:2, :]
        new["h"] = (x_in_ref[0] * (1.0 + scale) + shift).astype(BF16)

    def h():
        return new["h"]

    def ycat():
        return jnp.concatenate([ya_ring[prev3], yb_ring[prev2]], axis=1)

    def both(*fs):
        return lambda: [f() for f in fs]

    def fin_out():
        gate = mod_out_ref[0, 2:3, :]
        r = alpha * x_out_ref[0] + gate * y_scr[...]
        o_ref[0] = _layer_norm(r, g_ref[...], b_ref[...])

    def fin_u(p):
        new["ug"] = _gelu_tanh(p())

    def fin_v(p):
        new["vn"] = _layer_norm(_gelu_tanh(p()), ag_ref[...], ab_ref[...]).astype(BF16)
        t_idx = lax.broadcasted_iota(jnp.int32, (CHUNK, CHUNK), 0)
        s_idx = lax.broadcasted_iota(jnp.int32, (CHUNK, CHUNK), 1)
        causal = t_idx >= s_idx
        w_sp = [jnp.where(causal, wsp_ref[g], 0.0).astype(BF16) for g in range(N_GROUPS)]
        first_group = lax.broadcasted_iota(jnp.int32, (CHUNK, LANES), 1) < GROUP_DIM
        chunks = []
        for c in range(ROWS // CHUNK):
            rows = slice(c * CHUNK, (c + 1) * CHUNK)
            pieces = []
            for pr in range(d_a // LANES):
                cols = slice(pr * LANES, (pr + 1) * LANES)
                vp = new["vn"][rows, cols]
                mixed = jnp.where(first_group, _dot(w_sp[2 * pr], vp), _dot(w_sp[2 * pr + 1], vp))
                pieces.append(new["ug"][rows, cols] * (mixed + bsp_ref[:, cols]))
            chunks.append(jnp.concatenate(pieces, axis=1))
        new["ya_pre"] = jnp.concatenate(chunks, axis=0)

    def fin_za(p):
        ya_ring[now3] = (_silu(p()) * new["ya_pre"]).astype(BF16)

    scale_q = math.log2(math.e) / math.sqrt(GROUP_DIM)

    def fin_q(p):
        new["q"] = (p() * scale_q).astype(BF16)
        q_ring[now2] = new["q"]

    def fin_k(p):
        new["k"] = p().astype(BF16)
        k_scr[par_in, pl.ds(pl.multiple_of(j_in * ROWS, ROWS), ROWS), :] = new["k"]

    def head_of(q_pair, hh):
        in_head = (lane >= hh * GROUP_DIM) & (lane < (hh + 1) * GROUP_DIM)
        return jnp.where(in_head, q_pair, jnp.zeros_like(q_pair))

    def pre_scores():
        for qi in range(Q_PER_STEP):
            rows = slice(qi * TQ, (qi + 1) * TQ)
            for hd in range(n_heads):
                p, hh = divmod(hd, heads_per_tile)
                lanes = slice(p * LANES, (p + 1) * LANES)
                z_ring[now2, qi * n_heads + hd] = _dot_nt(
                    new["k"][rows, lanes], head_of(new["q"][rows, lanes], hh))

    def fin_vt(p):
        v = p()
        for n in range(Q_PER_STEP):
            vT_scr[par_in, j_in * Q_PER_STEP + n] = v[n * TK:(n + 1) * TK, :].T.astype(BF16)

    def fin_gz(p):
        gz_ring[now2] = _silu(p())

    dots, due = [], {}

    def out_slab(j):
        cols = slice(j * MXU_SLAB, (j + 1) * MXU_SLAB)

        def item():
            y_scr[:, cols] = _dot(ycat(), wo_ref[:, cols])
        return item

    for j in range(wo_ref.shape[1] // MXU_SLAB):
        dots.append(out_slab(j))
    due.setdefault(len(dots) - 1 + FIN_DELAY, []).append(fin_out)

    o = 3 * d_a
    projections = [(0, fin_u), (d_a, fin_v), (2 * d_a, fin_za), (o, fin_q), (o + d_b, fin_k),
                   (o + 2 * d_b, fin_vt), (o + 3 * d_b, fin_gz)]
    assert d_a == d_b == MXU_SLAB

    assert land_scr.shape[0] == len(projections)

    def in_slab(n, lo):
        def item():
            land_scr[n] = _dot(h(), w_ref[:, lo:lo + MXU_SLAB])
        return item

    def landed(n):
        return lambda: land_scr[n]

    for n, (lo, fin) in enumerate(projections):
        dots.append(in_slab(n, lo))
        due.setdefault(len(dots) - 1 + FIN_DELAY, []).append(functools.partial(fin, landed(n)))
    items = [item_h]
    for n, dot_item in enumerate(dots):
        items.append(both(dot_item, *due.pop(n, [])))
    items.append(both(*[f for n in sorted(due) for f in due[n]], pre_scores))

    lane = lax.broadcasted_iota(jnp.int32, (TQ, LANES), 1)
    half = TK // 2
    s_idx = lax.broadcasted_iota(jnp.int32, (half, half), 0)
    t_idx = lax.broadcasted_iota(jnp.int32, (half, half), 1)
    strictly_earlier_half = s_idx < t_idx
    j_idx = lax.broadcasted_iota(jnp.int32, (TK, TK), 1)
    r_idx = lax.broadcasted_iota(jnp.int32, (TK, TK), 0)
    strictly_later = jnp.where(j_idx > r_idx, 1.0, 0.0).astype(BF16)

    q_masked = {}
    for qi in range(Q_PER_STEP):
        for hd in range(n_heads):
            p, hh = divmod(hd, heads_per_tile)
            q_pair = q_ring[prev2, qi * TQ:(qi + 1) * TQ, p * LANES:(p + 1) * LANES]
            q_masked[qi, hd] = head_of(q_pair, hh)

    lo_half, hi_half = slice(0, half), slice(half, TK)
    diag_parts = [(lo_half, lo_half, "tri"), (lo_half, hi_half, "full"),
                  (hi_half, lo_half, "empty"), (hi_half, hi_half, "tri")]
    full_parts = [(slice(0, TK), slice(0, TQ), "full")]

    def assemble(parts, pieces):
        if len(parts) == 1:
            return pieces[0]
        return jnp.concatenate([jnp.concatenate(pieces[0:2], axis=1),
                                jnp.concatenate(pieces[2:4], axis=1)], axis=0)

    def gated_pair(accs, qi, p):
        rows, cols = slice(qi * TQ, (qi + 1) * TQ), slice(p * LANES, (p + 1) * LANES)
        yT = jnp.concatenate([accs[qi, heads_per_tile * p + n] for n in range(heads_per_tile)], axis=0)
        yb_ring[now2, rows, cols] = (yT.T * gz_ring[prev2, rows, cols]).astype(BF16)

    def key_blocks(blocks, runs, accs, extra=(), finish=False):
        runs, accs = dict(runs), dict(accs)
        tiles = [dict(qi=qi, kb=kb, h=hd, keep=keep, diag=diag, parts=diag_parts if diag else full_parts)
                 for qi, kb, diag, keep in blocks for hd in range(n_heads)]
        last_tile = {(t["qi"], t["h"]): n for n, t in enumerate(tiles)}

        def scores(t):
            if t["diag"]:
                t["z"] = z_ring[prev2, t["qi"] * n_heads + t["h"]]
                return
            start = pl.multiple_of(t["kb"] * TK, TK)
            p = t["h"] // heads_per_tile
            lanes = slice(p * LANES, (p + 1) * LANES)
            t["z"] = _dot_nt(k_scr[par_at, pl.ds(start, TK), lanes], q_masked[t["qi"], t["h"]])

        def softplus(t):
            sps, t["logb"] = [], []
            for rows, cols, kind in t["parts"]:
                if kind == "empty":
                    sps.append(jnp.zeros((half, half), BF16))
                    t["logb"].append(None)
                    continue
                z = t["z"][rows, cols]
                sp = jnp.maximum(z, 0.0) + jnp.log2(1.0 + jnp.exp2(-jnp.abs(z)))
                t["logb"].append(z - sp)
                if kind == "tri":
                    sp = jnp.where(strictly_earlier_half, sp, 0.0)
                sps.append(sp.astype(BF16))
            t["sp"] = assemble(t["parts"], sps)

        def later_sum(t):
            t["csum"] = _dot(strictly_later, t["sp"])

        def weights(t):
            key = (t["qi"], t["h"])
            run = runs[key]
            probs = []
            for (rows, cols, kind), logb in zip(t["parts"], t["logb"]):
                if kind == "empty":
                    probs.append(jnp.zeros((half, half), BF16))
                    continue
                a = jnp.exp2(logb - (run[:, cols] + t["csum"][rows, cols]))
                if kind == "tri":
                    a = jnp.where(strictly_earlier_half, a, 0.0)
                probs.append(a.astype(BF16))
            t["prob"] = assemble(t["parts"], probs)
            runs[key] = run + (t["csum"][0:1, :] + t["sp"][0:1, :].astype(F32))

        def values(n, t):
            key = (t["qi"], t["h"])
            hd = t["h"]
            v_blk = vT_scr[par_at, t["kb"], hd * GROUP_DIM:(hd + 1) * GROUP_DIM, :]
            if t["keep"] is not None:
                v_blk = (v_blk.astype(F32) * t["keep"]).astype(BF16)
            accs[key] = accs[key] + _dot(v_blk, t["prob"])
            if finish and last_tile[key] == n and hd % heads_per_tile == heads_per_tile - 1:
                gated_pair(accs, t["qi"], hd // heads_per_tile)

        n_tiles = len(tiles)
        lead, lag = SKEW
        ready = next((n for n, t in enumerate(tiles) if not t["diag"]), n_tiles)
        for t in tiles[:ready]:
            scores(t)
        steps = range(min(0, ready - lead), n_tiles + lag)
        extra = list(extra)
        slots = [EXTRA_START + (n * (len(steps) - EXTRA_START)) // max(1, len(extra))
                 for n in range(len(extra))]
        for count, step in enumerate(steps):
            while extra and slots[0] <= count:
                slots.pop(0)
                extra.pop(0)()
            if ready <= step + lead < n_tiles:
                scores(tiles[step + lead])
            if 0 <= step < n_tiles:
                softplus(tiles[step])
            if 0 <= step - 1 < n_tiles:
                later_sum(tiles[step - 1])
            if 0 <= step - lag < n_tiles:
                weights(tiles[step - lag])
                values(step - lag, tiles[step - lag])
        for item in extra:
            item()
        return runs, accs

    runs = {(qi, hd): jnp.zeros((1, TQ), F32) for qi in range(Q_PER_STEP) for hd in range(n_heads)}
    accs = {(qi, hd): jnp.zeros((GROUP_DIM, TQ), F32) for qi in range(Q_PER_STEP) for hd in range(n_heads)}
    first_kb = j_at * Q_PER_STEP
    diag_blocks = [(qi, first_kb + qi, True, None) for qi in range(Q_PER_STEP)]
    has_earlier = (j_at > 0).astype(F32)
    prev_blocks = [(qi, jnp.maximum(first_kb + qi - 1, 0), False, has_earlier if qi == 0 else None)
                   for qi in range(Q_PER_STEP)]
    runs, accs = key_blocks(diag_blocks + prev_blocks, runs, accs, items, finish=True)

    for qi in range(Q_PER_STEP):
        def sticks_left(runs):
            return jnp.min(functools.reduce(jnp.minimum, [runs[qi, hd] for hd in range(n_heads)])) < UNDERFLOW_LOG2

        def earlier_block(state, qi=qi, sticks_left=sticks_left):
            kb, _, runs, accs = state
            runs, accs = key_blocks([(qi, kb, False, None)], runs, accs)
            return kb - 1, sticks_left(runs), runs, accs

        first = first_kb + qi - 2
        mine = lambda d: {k: v for k, v in d.items() if k[0] == qi}

        @pl.when((first >= 0) & sticks_left(runs))
        def _(qi=qi, first=first, earlier_block=earlier_block, mine=mine):
            _, _, _, more = lax.while_loop(lambda st: (st[0] >= 0) & st[1], earlier_block,
                                           (first, True, mine(runs), mine(accs)))
            for p in range(n_pairs):
                gated_pair(more, qi, p)


def _fused_block(x, mod3, w, w_sp, b_sp_full, a_g, a_b, w_out, ln_g, ln_b, alpha):
    bsz, seq, d = x.shape
    d_a = a_g.shape[-1]
    d_b = (w.shape[1] - 3 * d_a) // 4
    per_seq = seq // ROWS
    n_total = bsz * per_seq
    grid = (n_total + 2,)

    def stage(lag):
        return lambda s: _stage_slabs(s, n_total, per_seq)[lag]

    def rows_of(lag):
        return pl.BlockSpec((1, ROWS, d), lambda s: (*stage(lag)(s), 0))

    def mod_of(lag):
        return pl.BlockSpec((1, 3, d), lambda s: (stage(lag)(s)[0], 0, 0))

    def whole(a):
        return pl.BlockSpec(a.shape, lambda s: (0,) * a.ndim)

    return pl.pallas_call(
        functools.partial(_block_kernel, alpha, n_total),
        grid=grid,
        in_specs=[rows_of(0), rows_of(2), mod_of(0), mod_of(2), whole(w), whole(w_sp),
                  whole(b_sp_full), whole(a_g), whole(a_b), whole(w_out), whole(ln_g), whole(ln_b)],
        out_specs=rows_of(2),
        out_shape=jax.ShapeDtypeStruct((bsz, seq, d), x.dtype),
        scratch_shapes=[
            pltpu.VMEM((2, seq, d_b), BF16),
            pltpu.VMEM((2, seq // TK, d_b, TK), BF16),
            pltpu.VMEM((2, ROWS, d_b), BF16),
            pltpu.VMEM((2, ROWS, d_b), F32),
            pltpu.VMEM((3, ROWS, d_a), BF16),
            pltpu.VMEM((2, ROWS, d_b), BF16),
            pltpu.VMEM((2, Q_PER_STEP * (d_b // GROUP_DIM), TK, TQ), F32),
            pltpu.VMEM((w.shape[1] // MXU_SLAB, ROWS, MXU_SLAB), F32),
            pltpu.VMEM((ROWS, d), F32),
        ],
        compiler_params=pltpu.CompilerParams(
            dimension_semantics=("arbitrary",), vmem_limit_bytes=VMEM_LIMIT),
        name="fused_block",
    )(x, x, mod3, mod3, w, w_sp, b_sp_full, a_g, a_b, w_out, ln_g, ln_b)


def kernel(x, c, w_ada, b_ada, w_in, sgu_ln_g, sgu_ln_b, w_spatial, b_spatial, w_out, ln_g, ln_b):
    depth = w_ada.shape[0]
    bsz, seq, d = x.shape
    d_a = sgu_ln_g.shape[-1]
    alpha = (2.0 * depth) ** 0.25
    for layer in range(depth):
        mod, w_in_bf, w_out_bf = _prepare(c, w_ada[layer], b_ada[layer], w_in[layer], w_out[layer])
        b_sp_full = jnp.repeat(b_spatial[layer].T, GROUP_DIM, axis=1)
        x = _fused_block(
            x, mod.reshape(bsz, 3, d), w_in_bf, w_spatial[layer], b_sp_full,
            sgu_ln_g[layer].reshape(1, d_a), sgu_ln_b[layer].reshape(1, d_a),
            w_out_bf, ln_g[layer].reshape(1, d), ln_b[layer].reshape(1, d), alpha)
    return x
```

```python
import functools
import math

import jax
import jax.numpy as jnp
from jax import lax
from jax.experimental import pallas as pl
from jax.experimental.pallas import tpu as pltpu

F32 = jnp.float32
BF16 = jnp.bfloat16

N_GROUPS = 8
GROUP_DIM = 64
CHUNK = 128
LN_EPS = 1e-5
LANES = 128

PREP_TILE = 512
TQ = 256
TK = 256
Q_PER_STEP = 1
ROWS = Q_PER_STEP * TQ
SKEW = (7, 3)
EXTRA_START = 2
MXU_SLAB = 512
FIN_DELAY = 1
UNDERFLOW_LOG2 = 152.0
VMEM_LIMIT = 48 * 1024 * 1024


def _dot(a, b):
    return jnp.dot(a, b, preferred_element_type=F32)


def _dot_nt(a, b):
    return lax.dot_general(a, b, (((1,), (1,)), ((), ())), preferred_element_type=F32)


def _gelu_tanh(x):
    k1 = -2.0 * math.sqrt(2.0 / math.pi) * math.log2(math.e)
    return x / (1.0 + jnp.exp2(x * (k1 + (k1 * 0.044715) * (x * x))))


def _silu(x):
    return x / (1.0 + jnp.exp(-x))


def _layer_norm(x, g, b):
    mu = jnp.mean(x, axis=-1, keepdims=True)
    xc = x - mu
    var = jnp.mean(xc * xc, axis=-1, keepdims=True)
    return xc * lax.rsqrt(var + LN_EPS) * g + b


def _prep_kernel(c_ref, wa_ref, ba_ref, win_ref, wout_ref, mod_ref, win_bf_ref, wout_bf_ref):
    sc = _silu(c_ref[...]).astype(BF16)
    mod_ref[...] = _dot(sc, wa_ref[...].astype(BF16)) + ba_ref[...]
    win_bf_ref[...] = win_ref[...].astype(BF16)
    wout_bf_ref[...] = wout_ref[...].astype(BF16)


def _prepare(c, w_ada, b_ada, w_in, w_out):
    bsz, d = c.shape
    n_mod = w_ada.shape[1]
    steps = w_in.shape[1] // PREP_TILE
    mod_tiles = n_mod // PREP_TILE
    out_tile = PREP_TILE
    out_tiles = w_out.shape[1] // out_tile
    assert mod_tiles <= steps and out_tiles <= steps
    mod_col = lambda j: (0, jnp.minimum(j, mod_tiles - 1))
    out_col = lambda j: (0, jnp.minimum(j, out_tiles - 1))
    return pl.pallas_call(
        _prep_kernel,
        grid=(steps,),
        in_specs=[
            pl.BlockSpec((bsz, d), lambda j: (0, 0)),
            pl.BlockSpec((d, PREP_TILE), mod_col),
            pl.BlockSpec((1, PREP_TILE), mod_col),
            pl.BlockSpec((w_in.shape[0], PREP_TILE), lambda j: (0, j)),
            pl.BlockSpec((w_out.shape[0], out_tile), out_col),
        ],
        out_specs=[
            pl.BlockSpec((bsz, PREP_TILE), mod_col),
            pl.BlockSpec((w_in.shape[0], PREP_TILE), lambda j: (0, j)),
            pl.BlockSpec((w_out.shape[0], out_tile), out_col),
        ],
        out_shape=[
            jax.ShapeDtypeStruct((bsz, n_mod), F32),
            jax.ShapeDtypeStruct(w_in.shape, BF16),
            jax.ShapeDtypeStruct(w_out.shape, BF16),
        ],
        compiler_params=pltpu.CompilerParams(
            dimension_semantics=("arbitrary",), vmem_limit_bytes=VMEM_LIMIT),
        name="adaln_mod_and_casts",
    )(c, w_ada, b_ada.reshape(1, n_mod), w_in, w_out)


def _stage_slabs(s, n_total, per_seq):
    last = n_total - 1
    out = []
    for lag in range(3):
        t = jnp.clip(s - lag, 0, last)
        out.append((lax.div(t, per_seq), lax.rem(t, per_seq)))
    return out


def _block_kernel(alpha, n_total,
                  x_in_ref, x_out_ref, mod_in_ref, mod_out_ref, w_ref, wsp_ref, bsp_ref,
                  ag_ref, ab_ref, wo_ref, g_ref, b_ref, o_ref,
                  k_scr, vT_scr, q_ring, gz_ring, ya_ring, yb_ring, z_ring, land_scr, y_scr):
    s = pl.program_id(0)
    d_b = q_ring.shape[-1]
    d_a = ya_ring.shape[-1]
    per_seq = k_scr.shape[1] // ROWS
    (b_in, j_in), (b_at, j_at), _ = _stage_slabs(s, n_total, per_seq)
    par_in, par_at = lax.rem(b_in, 2), lax.rem(b_at, 2)
    n_heads = d_b // GROUP_DIM
    heads_per_tile = LANES // GROUP_DIM
    n_pairs = n_heads // heads_per_tile
    now2, prev2 = lax.rem(s, 2), lax.rem(s + 1, 2)
    now3, prev3 = lax.rem(s, 3), lax.rem(s + 1, 3)

    @pl.when(s == 0)
    def _():
        q_ring[...] = jnp.zeros_like(q_ring)
        gz_ring[...] = jnp.zeros_like(gz_ring)
        ya_ring[...] = jnp.zeros_like(ya_ring)
        yb_ring[...] = jnp.zeros_like(yb_ring)
        z_ring[...] = jnp.zeros_like(z_ring)
        k_scr[0, 0:ROWS, :] = jnp.zeros((ROWS, d_b), BF16)
        for n in range(Q_PER_STEP):
            vT_scr[0, n] = jnp.zeros((d_b, TK), BF16)

    new = {}

    def item_h():
        shift = mod_in_ref[0, 0:1, :]
        scale = mod_in_ref[0, 1:2, :]
        new["h"] = (x_in_ref[0] * (1.0 + scale) + shift).astype(BF16)

    def h():
        return new["h"]

    def ycat():
        return jnp.concatenate([ya_ring[prev3], yb_ring[prev2]], axis=1)

    def both(*fs):
        return lambda: [f() for f in fs]

    def fin_out():
        gate = mod_out_ref[0, 2:3, :]
        r = alpha * x_out_ref[0] + gate * y_scr[...]
        o_ref[0] = _layer_norm(r, g_ref[...], b_ref[...])

    def fin_u(p):
        new["ug"] = _gelu_tanh(p())

    def fin_v(p):
        new["vn"] = _layer_norm(_gelu_tanh(p()), ag_ref[...], ab_ref[...]).astype(BF16)
        t_idx = lax.broadcasted_iota(jnp.int32, (CHUNK, CHUNK), 0)
        s_idx = lax.broadcasted_iota(jnp.int32, (CHUNK, CHUNK), 1)
        causal = t_idx >= s_idx
        w_sp = [jnp.where(causal, wsp_ref[g], 0.0).astype(BF16) for g in range(N_GROUPS)]
        first_group = lax.broadcasted_iota(jnp.int32, (CHUNK, LANES), 1) < GROUP_DIM
        chunks = []
        for c in range(ROWS // CHUNK):
            rows = slice(c * CHUNK, (c + 1) * CHUNK)
            pieces = []
            for pr in range(d_a // LANES):
                cols = slice(pr * LANES, (pr + 1) * LANES)
                vp = new["vn"][rows, cols]
                mixed = jnp.where(first_group, _dot(w_sp[2 * pr], vp), _dot(w_sp[2 * pr + 1], vp))
                pieces.append(new["ug"][rows, cols] * (mixed + bsp_ref[:, cols]))
            chunks.append(jnp.concatenate(pieces, axis=1))
        new["ya_pre"] = jnp.concatenate(chunks, axis=0)

    def fin_za(p):
        ya_ring[now3] = (_silu(p()) * new["ya_pre"]).astype(BF16)

    scale_q = math.log2(math.e) / math.sqrt(GROUP_DIM)

    def fin_q(p):
        new["q"] = (p() * scale_q).astype(BF16)
        q_ring[now2] = new["q"]

    def fin_k(p):
        new["k"] = p().astype(BF16)
        k_scr[par_in, pl.ds(pl.multiple_of(j_in * ROWS, ROWS), ROWS), :] = new["k"]

    def head_of(q_pair, hh):
        in_head = (lane >= hh * GROUP_DIM) & (lane < (hh + 1) * GROUP_DIM)
        return jnp.where(in_head, q_pair, jnp.zeros_like(q_pair))

    def pre_scores():
        for qi in range(Q_PER_STEP):
            rows = slice(qi * TQ, (qi + 1) * TQ)
            for hd in range(n_heads):
                p, hh = divmod(hd, heads_per_tile)
                lanes = slice(p * LANES, (p + 1) * LANES)
                z_ring[now2, qi * n_heads + hd] = _dot_nt(
                    new["k"][rows, lanes], head_of(new["q"][rows, lanes], hh))

    def fin_vt(p):
        v = p()
        for n in range(Q_PER_STEP):
            vT_scr[par_in, j_in * Q_PER_STEP + n] = v[n * TK:(n + 1) * TK, :].T.astype(BF16)

    def fin_gz(p):
        gz_ring[now2] = _silu(p())

    dots, due = [], {}

    def out_slab(j):
        cols = slice(j * MXU_SLAB, (j + 1) * MXU_SLAB)

        def item():
            y_scr[:, cols] = _dot(ycat(), wo_ref[:, cols])
        return item

    for j in range(wo_ref.shape[1] // MXU_SLAB):
        dots.append(out_slab(j))
    due.setdefault(len(dots) - 1 + FIN_DELAY, []).append(fin_out)

    o = 3 * d_a
    projections = [(0, fin_u), (d_a, fin_v), (2 * d_a, fin_za), (o, fin_q), (o + d_b, fin_k),
                   (o + 2 * d_b, fin_vt), (o + 3 * d_b, fin_gz)]
    assert d_a == d_b == MXU_SLAB

    assert land_scr.shape[0] == len(projections)

    def in_slab(n, lo):
        def item():
            land_scr[n] = _dot(h(), w_ref[:, lo:lo + MXU_SLAB])
        return item

    def landed(n):
        return lambda: land_scr[n]

    for n, (lo, fin) in enumerate(projections):
        dots.append(in_slab(n, lo))
        due.setdefault(len(dots) - 1 + FIN_DELAY, []).append(functools.partial(fin, landed(n)))
    items = [item_h]
    for n, dot_item in enumerate(dots):
        items.append(both(dot_item, *due.pop(n, [])))
    items.append(both(*[f for n in sorted(due) for f in due[n]], pre_scores))

    lane = lax.broadcasted_iota(jnp.int32, (TQ, LANES), 1)
    half = TK // 2
    s_idx = lax.broadcasted_iota(jnp.int32, (half, half), 0)
    t_idx = lax.broadcasted_iota(jnp.int32, (half, half), 1)
    strictly_earlier_half = s_idx < t_idx
    j_idx = lax.broadcasted_iota(jnp.int32, (TK, TK), 1)
    r_idx = lax.broadcasted_iota(jnp.int32, (TK, TK), 0)
    strictly_later = jnp.where(j_idx > r_idx, 1.0, 0.0).astype(BF16)

    q_masked = {}
    for qi in range(Q_PER_STEP):
        for hd in range(n_heads):
            p, hh = divmod(hd, heads_per_tile)
            q_pair = q_ring[prev2, qi * TQ:(qi + 1) * TQ, p * LANES:(p + 1) * LANES]
            q_masked[qi, hd] = head_of(q_pair, hh)

    lo_half, hi_half = slice(0, half), slice(half, TK)
    diag_parts = [(lo_half, lo_half, "tri"), (lo_half, hi_half, "full"),
                  (hi_half, lo_half, "empty"), (hi_half, hi_half, "tri")]
    full_parts = [(slice(0, TK), slice(0, TQ), "full")]

    def assemble(parts, pieces):
        if len(parts) == 1:
            return pieces[0]
        return jnp.concatenate([jnp.concatenate(pieces[0:2], axis=1),
                                jnp.concatenate(pieces[2:4], axis=1)], axis=0)

    def gated_pair(accs, qi, p):
        rows, cols = slice(qi * TQ, (qi + 1) * TQ), slice(p * LANES, (p + 1) * LANES)
        yT = jnp.concatenate([accs[qi, heads_per_tile * p + n] for n in range(heads_per_tile)], axis=0)
        yb_ring[now2, rows, cols] = (yT.T * gz_ring[prev2, rows, cols]).astype(BF16)

    def key_blocks(blocks, runs, accs, extra=(), finish=False):
        runs, accs = dict(runs), dict(accs)
        tiles = [dict(qi=qi, kb=kb, h=hd, keep=keep, diag=diag, parts=diag_parts if diag else full_parts)
                 for qi, kb, diag, keep in blocks for hd in range(n_heads)]
        last_tile = {(t["qi"], t["h"]): n for n, t in enumerate(tiles)}

        def scores(t):
            if t["diag"]:
                t["z"] = z_ring[prev2, t["qi"] * n_heads + t["h"]]
                return
            start = pl.multiple_of(t["kb"] * TK, TK)
            p = t["h"] // heads_per_tile
            lanes = slice(p * LANES, (p + 1) * LANES)
            t["z"] = _dot_nt(k_scr[par_at, pl.ds(start, TK), lanes], q_masked[t["qi"], t["h"]])

        def softplus(t):
            sps, t["logb"] = [], []
            for rows, cols, kind in t["parts"]:
                if kind == "empty":
                    sps.append(jnp.zeros((half, half), BF16))
                    t["logb"].append(None)
                    continue
                z = t["z"][rows, cols]
                sp = jnp.maximum(z, 0.0) + jnp.log2(1.0 + jnp.exp2(-jnp.abs(z)))
                t["logb"].append(z - sp)
                if kind == "tri":
                    sp = jnp.where(strictly_earlier_half, sp, 0.0)
                sps.append(sp.astype(BF16))
            t["sp"] = assemble(t["parts"], sps)

        def later_sum(t):
            t["csum"] = _dot(strictly_later, t["sp"])

        def weights(t):
            key = (t["qi"], t["h"])
            run = runs[key]
            probs = []
            for (rows, cols, kind), logb in zip(t["parts"], t["logb"]):
                if kind == "empty":
                    probs.append(jnp.zeros((half, half), BF16))
                    continue
                a = jnp.exp2(logb - (run[:, cols] + t["csum"][rows, cols]))
                if kind == "tri":
                    a = jnp.where(strictly_earlier_half, a, 0.0)
                probs.append(a.astype(BF16))
            t["prob"] = assemble(t["parts"], probs)
            runs[key] = run + (t["csum"][0:1, :] + t["sp"][0:1, :].astype(F32))

        def values(n, t):
            key = (t["qi"], t["h"])
            hd = t["h"]
            v_blk = vT_scr[par_at, t["kb"], hd * GROUP_DIM:(hd + 1) * GROUP_DIM, :]
            if t["keep"] is not None:
                v_blk = (v_blk.astype(F32) * t["keep"]).astype(BF16)
            accs[key] = accs[key] + _dot(v_blk, t["prob"])
            if finish and last_tile[key] == n and hd % heads_per_tile == heads_per_tile - 1:
                gated_pair(accs, t["qi"], hd // heads_per_tile)

        n_tiles = len(tiles)
        lead, lag = SKEW
        ready = next((n for n, t in enumerate(tiles) if not t["diag"]), n_tiles)
        for t in tiles[:ready]:
            scores(t)
        steps = range(min(0, ready - lead), n_tiles + lag)
        extra = list(extra)
        slots = [EXTRA_START + (n * (len(steps) - EXTRA_START)) // max(1, len(extra))
                 for n in range(len(extra))]
        for count, step in enumerate(steps):
            while extra and slots[0] <= count:
                slots.pop(0)
                extra.pop(0)()
            if ready <= step + lead < n_tiles:
                scores(tiles[step + lead])
            if 0 <= step < n_tiles:
                softplus(tiles[step])
                later_sum(tiles[step])
            if 0 <= step - lag < n_tiles:
                weights(tiles[step - lag])
                values(step - lag, tiles[step - lag])
        for item in extra:
            item()
        return runs, accs

    runs = {(qi, hd): jnp.zeros((1, TQ), F32) for qi in range(Q_PER_STEP) for hd in range(n_heads)}
    accs = {(qi, hd): jnp.zeros((GROUP_DIM, TQ), F32) for qi in range(Q_PER_STEP) for hd in range(n_heads)}
    first_kb = j_at * Q_PER_STEP
    diag_blocks = [(qi, first_kb + qi, True, None) for qi in range(Q_PER_STEP)]
    has_earlier = (j_at > 0).astype(F32)
    prev_blocks = [(qi, jnp.maximum(first_kb + qi - 1, 0), False, has_earlier if qi == 0 else None)
                   for qi in range(Q_PER_STEP)]
    runs, accs = key_blocks(diag_blocks + prev_blocks, runs, accs, items, finish=True)

    for qi in range(Q_PER_STEP):
        def sticks_left(runs):
            return jnp.min(functools.reduce(jnp.minimum, [runs[qi, hd] for hd in range(n_heads)])) < UNDERFLOW_LOG2

        def earlier_block(state, qi=qi, sticks_left=sticks_left):
            kb, _, runs, accs = state
            runs, accs = key_blocks([(qi, kb, False, None)], runs, accs)
            return kb - 1, sticks_left(runs), runs, accs

        first = first_kb + qi - 2
        mine = lambda d: {k: v for k, v in d.items() if k[0] == qi}

        @pl.when((first >= 0) & sticks_left(runs))
        def _(qi=qi, first=first, earlier_block=earlier_block, mine=mine):
            _, _, _, more = lax.while_loop(lambda st: (st[0] >= 0) & st[1], earlier_block,
                                           (first, True, mine(runs), mine(accs)))
            for p in range(n_pairs):
                gated_pair(more, qi, p)


def _fused_block(x, mod3, w, w_sp, b_sp_full, a_g, a_b, w_out, ln_g, ln_b, alpha):
    bsz, seq, d = x.shape
    d_a = a_g.shape[-1]
    d_b = (w.shape[1] - 3 * d_a) // 4
    per_seq = seq // ROWS
    n_total = bsz * per_seq
    grid = (n_total + 2,)

    def stage(lag):
        return lambda s: _stage_slabs(s, n_total, per_seq)[lag]

    def rows_of(lag):
        return pl.BlockSpec((1, ROWS, d), lambda s: (*stage(lag)(s), 0))

    def mod_of(lag):
        return pl.BlockSpec((1, 3, d), lambda s: (stage(lag)(s)[0], 0, 0))

    def whole(a):
        return pl.BlockSpec(a.shape, lambda s: (0,) * a.ndim)

    return pl.pallas_call(
        functools.partial(_block_kernel, alpha, n_total),
        grid=grid,
        in_specs=[rows_of(0), rows_of(2), mod_of(0), mod_of(2), whole(w), whole(w_sp),
                  whole(b_sp_full), whole(a_g), whole(a_b), whole(w_out), whole(ln_g), whole(ln_b)],
        out_specs=rows_of(2),
        out_shape=jax.ShapeDtypeStruct((bsz, seq, d), x.dtype),
        scratch_shapes=[
            pltpu.VMEM((2, seq, d_b), BF16),
            pltpu.VMEM((2, seq // TK, d_b, TK), BF16),
            pltpu.VMEM((2, ROWS, d_b), BF16),
            pltpu.VMEM((2, ROWS, d_b), F32),
            pltpu.VMEM((3, ROWS, d_a), BF16),
            pltpu.VMEM((2, ROWS, d_b), BF16),
            pltpu.VMEM((2, Q_PER_STEP * (d_b // GROUP_DIM), TK, TQ), F32),
            pltpu.VMEM((w.shape[1] // MXU_SLAB, ROWS, MXU_SLAB), F32),
            pltpu.VMEM((ROWS, d), F32),
        ],
        compiler_params=pltpu.CompilerParams(
            dimension_semantics=("arbitrary",), vmem_limit_bytes=VMEM_LIMIT),
        name="fused_block",
    )(x, x, mod3, mod3, w, w_sp, b_sp_full, a_g, a_b, w_out, ln_g, ln_b)


def kernel(x, c, w_ada, b_ada, w_in, sgu_ln_g, sgu_ln_b, w_spatial, b_spatial, w_out, ln_g, ln_b):
    depth = w_ada.shape[0]
    bsz, seq, d = x.shape
    d_a = sgu_ln_g.shape[-1]
    alpha = (2.0 * depth) ** 0.25
    for layer in range(depth):
        mod, w_in_bf, w_out_bf = _prepare(c, w_ada[layer], b_ada[layer], w_in[layer], w_out[layer])
        b_sp_full = jnp.repeat(b_spatial[layer].T, GROUP_DIM, axis=1)
        x = _fused_block(
            x, mod.reshape(bsz, 3, d), w_in_bf, w_spatial[layer], b_sp_full,
            sgu_ln_g[layer].reshape(1, d_a), sgu_ln_b[layer].reshape(1, d_a),
            w_out_bf, ln_g[layer].reshape(1, d), ln_b[layer].reshape(1, d), alpha)
    return x
```

```python
import functools
import math

import jax
import jax.numpy as jnp
from jax import lax
from jax.experimental import pallas as pl
from jax.experimental.pallas import tpu as pltpu

F32 = jnp.float32
BF16 = jnp.bfloat16

N_GROUPS = 8
GROUP_DIM = 64
CHUNK = 128
LN_EPS = 1e-5
LANES = 128

PREP_TILE = 512
TQ = 256
TK = 256
Q_PER_STEP = 1
ROWS = Q_PER_STEP * TQ
SKEW = (8, 3)
EXTRA_START = 2
MXU_SLAB = 512
FIN_DELAY = 1
UNDERFLOW_LOG2 = 152.0
VMEM_LIMIT = 48 * 1024 * 1024


def _dot(a, b):
    return jnp.dot(a, b, preferred_element_type=F32)


def _dot_nt(a, b):
    return lax.dot_general(a, b, (((1,), (1,)), ((), ())), preferred_element_type=F32)


def _gelu_tanh(x):
    k1 = -2.0 * math.sqrt(2.0 / math.pi) * math.log2(math.e)
    return x / (1.0 + jnp.exp2(x * (k1 + (k1 * 0.044715) * (x * x))))


def _silu(x):
    return x / (1.0 + jnp.exp(-x))


def _layer_norm(x, g, b):
    mu = jnp.mean(x, axis=-1, keepdims=True)
    xc = x - mu
    var = jnp.mean(xc * xc, axis=-1, keepdims=True)
    return xc * lax.rsqrt(var + LN_EPS) * g + b


def _prep_kernel(c_ref, wa_ref, ba_ref, win_ref, wout_ref, mod_ref, win_bf_ref, wout_bf_ref):
    sc = _silu(c_ref[...]).astype(BF16)
    mod_ref[...] = _dot(sc, wa_ref[...].astype(BF16)) + ba_ref[...]
    win_bf_ref[...] = win_ref[...].astype(BF16)
    wout_bf_ref[...] = wout_ref[...].astype(BF16)


def _prepare(c, w_ada, b_ada, w_in, w_out):
    bsz, d = c.shape
    n_mod = w_ada.shape[1]
    steps = w_in.shape[1] // PREP_TILE
    mod_tiles = n_mod // PREP_TILE
    out_tile = PREP_TILE
    out_tiles = w_out.shape[1] // out_tile
    assert mod_tiles <= steps and out_tiles <= steps
    mod_col = lambda j: (0, jnp.minimum(j, mod_tiles - 1))
    out_col = lambda j: (0, jnp.minimum(j, out_tiles - 1))
    return pl.pallas_call(
        _prep_kernel,
        grid=(steps,),
        in_specs=[
            pl.BlockSpec((bsz, d), lambda j: (0, 0)),
            pl.BlockSpec((d, PREP_TILE), mod_col),
            pl.BlockSpec((1, PREP_TILE), mod_col),
            pl.BlockSpec((w_in.shape[0], PREP_TILE), lambda j: (0, j)),
            pl.BlockSpec((w_out.shape[0], out_tile), out_col),
        ],
        out_specs=[
            pl.BlockSpec((bsz, PREP_TILE), mod_col),
            pl.BlockSpec((w_in.shape[0], PREP_TILE), lambda j: (0, j)),
            pl.BlockSpec((w_out.shape[0], out_tile), out_col),
        ],
        out_shape=[
            jax.ShapeDtypeStruct((bsz, n_mod), F32),
            jax.ShapeDtypeStruct(w_in.shape, BF16),
            jax.ShapeDtypeStruct(w_out.shape, BF16),
        ],
        compiler_params=pltpu.CompilerParams(
            dimension_semantics=("arbitrary",), vmem_limit_bytes=VMEM_LIMIT),
        name="adaln_mod_and_casts",
    )(c, w_ada, b_ada.reshape(1, n_mod), w_in, w_out)


def _stage_slabs(s, n_total, per_seq):
    last = n_total - 1
    out = []
    for lag in range(3):
        t = jnp.clip(s - lag, 0, last)
        out.append((lax.div(t, per_seq), lax.rem(t, per_seq)))
    return out


def _block_kernel(alpha, n_total,
                  x_in_ref, x_out_ref, mod_in_ref, mod_out_ref, w_ref, wsp_ref, bsp_ref,
                  ag_ref, ab_ref, wo_ref, g_ref, b_ref, o_ref,
                  k_scr, vT_scr, q_ring, gz_ring, ya_ring, yb_ring, z_ring, land_scr, y_scr):
    s = pl.program_id(0)
    d_b = q_ring.shape[-1]
    d_a = ya_ring.shape[-1]
    per_seq = k_scr.shape[1] // ROWS
    (b_in, j_in), (b_at, j_at), _ = _stage_slabs(s, n_total, per_seq)
    par_in, par_at = lax.rem(b_in, 2), lax.rem(b_at, 2)
    n_heads = d_b // GROUP_DIM
    heads_per_tile = LANES // GROUP_DIM
    n_pairs = n_heads // heads_per_tile
    now2, prev2 = lax.rem(s, 2), lax.rem(s + 1, 2)
    now3, prev3 = lax.rem(s, 3), lax.rem(s + 1, 3)

    @pl.when(s == 0)
    def _():
        q_ring[...] = jnp.zeros_like(q_ring)
        gz_ring[...] = jnp.zeros_like(gz_ring)
        ya_ring[...] = jnp.zeros_like(ya_ring)
        yb_ring[...] = jnp.zeros_like(yb_ring)
        z_ring[...] = jnp.zeros_like(z_ring)
        k_scr[0, 0:ROWS, :] = jnp.zeros((ROWS, d_b), BF16)
        for n in range(Q_PER_STEP):
            vT_scr[0, n] = jnp.zeros((d_b, TK), BF16)

    new = {}

    def item_h():
        shift = mod_in_ref[0, 0:1, :]
        scale = mod_in_ref[0, 1:2, :]
        new["h"] = (x_in_ref[0] * (1.0 + scale) + shift).astype(BF16)

    def h():
        return new["h"]

    def ycat():
        return jnp.concatenate([ya_ring[prev3], yb_ring[prev2]], axis=1)

    def both(*fs):
        return lambda: [f() for f in fs]

    def fin_out():
        gate = mod_out_ref[0, 2:3, :]
        r = alpha * x_out_ref[0] + gate * y_scr[...]
        o_ref[0] = _layer_norm(r, g_ref[...], b_ref[...])

    def fin_u(p):
        new["ug"] = _gelu_tanh(p())

    def fin_v(p):
        new["vn"] = _layer_norm(_gelu_tanh(p()), ag_ref[...], ab_ref[...]).astype(BF16)
        t_idx = lax.broadcasted_iota(jnp.int32, (CHUNK, CHUNK), 0)
        s_idx = lax.broadcasted_iota(jnp.int32, (CHUNK, CHUNK), 1)
        causal = t_idx >= s_idx
        w_sp = [jnp.where(causal, wsp_ref[g], 0.0).astype(BF16) for g in range(N_GROUPS)]
        first_group = lax.broadcasted_iota(jnp.int32, (CHUNK, LANES), 1) < GROUP_DIM
        chunks = []
        for c in range(ROWS // CHUNK):
            rows = slice(c * CHUNK, (c + 1) * CHUNK)
            pieces = []
            for pr in range(d_a // LANES):
                cols = slice(pr * LANES, (pr + 1) * LANES)
                vp = new["vn"][rows, cols]
                mixed = jnp.where(first_group, _dot(w_sp[2 * pr], vp), _dot(w_sp[2 * pr + 1], vp))
                pieces.append(new["ug"][rows, cols] * (mixed + bsp_ref[:, cols]))
            chunks.append(jnp.concatenate(pieces, axis=1))
        new["ya_pre"] = jnp.concatenate(chunks, axis=0)

    def fin_za(p):
        ya_ring[now3] = (_silu(p()) * new["ya_pre"]).astype(BF16)

    scale_q = math.log2(math.e) / math.sqrt(GROUP_DIM)

    def fin_q(p):
        new["q"] = (p() * scale_q).astype(BF16)
        q_ring[now2] = new["q"]

    def fin_k(p):
        new["k"] = p().astype(BF16)
        k_scr[par_in, pl.ds(pl.multiple_of(j_in * ROWS, ROWS), ROWS), :] = new["k"]

    def head_of(q_pair, hh):
        in_head = (lane >= hh * GROUP_DIM) & (lane < (hh + 1) * GROUP_DIM)
        return jnp.where(in_head, q_pair, jnp.zeros_like(q_pair))

    def pre_scores():
        for qi in range(Q_PER_STEP):
            rows = slice(qi * TQ, (qi + 1) * TQ)
            for hd in range(n_heads):
                p, hh = divmod(hd, heads_per_tile)
                lanes = slice(p * LANES, (p + 1) * LANES)
                z_ring[now2, qi * n_heads + hd] = _dot_nt(
                    new["k"][rows, lanes], head_of(new["q"][rows, lanes], hh))

    def fin_vt(p):
        v = p()
        for n in range(Q_PER_STEP):
            vT_scr[par_in, j_in * Q_PER_STEP + n] = v[n * TK:(n + 1) * TK, :].T.astype(BF16)

    def fin_gz(p):
        gz_ring[now2] = _silu(p())

    dots, due = [], {}

    def out_slab(j):
        cols = slice(j * MXU_SLAB, (j + 1) * MXU_SLAB)

        def item():
            y_scr[:, cols] = _dot(ycat(), wo_ref[:, cols])
        return item

    for j in range(wo_ref.shape[1] // MXU_SLAB):
        dots.append(out_slab(j))
    due.setdefault(len(dots) - 1 + FIN_DELAY, []).append(fin_out)

    o = 3 * d_a
    projections = [(0, fin_u), (d_a, fin_v), (2 * d_a, fin_za), (o, fin_q), (o + d_b, fin_k),
                   (o + 2 * d_b, fin_vt), (o + 3 * d_b, fin_gz)]
    assert d_a == d_b == MXU_SLAB

    assert land_scr.shape[0] == len(projections)

    def in_slab(n, lo):
        def item():
            land_scr[n] = _dot(h(), w_ref[:, lo:lo + MXU_SLAB])
        return item

    def landed(n):
        return lambda: land_scr[n]

    def direct(lo, fin):
        return lambda: fin(lambda: _dot(h(), w_ref[:, lo:lo + MXU_SLAB]))

    for n, (lo, fin) in enumerate(projections):
        if fin in (fin_q, fin_k):
            dots.append(direct(lo, fin))
            continue
        dots.append(in_slab(n, lo))
        due.setdefault(len(dots) - 1 + FIN_DELAY, []).append(functools.partial(fin, landed(n)))
    items = [item_h]
    for n, dot_item in enumerate(dots):
        items.append(both(dot_item, *due.pop(n, [])))
    items.append(both(*[f for n in sorted(due) for f in due[n]], pre_scores))

    lane = lax.broadcasted_iota(jnp.int32, (TQ, LANES), 1)
    half = TK // 2
    s_idx = lax.broadcasted_iota(jnp.int32, (half, half), 0)
    t_idx = lax.broadcasted_iota(jnp.int32, (half, half), 1)
    strictly_earlier_half = s_idx < t_idx
    j_idx = lax.broadcasted_iota(jnp.int32, (TK, TK), 1)
    r_idx = lax.broadcasted_iota(jnp.int32, (TK, TK), 0)
    strictly_later = jnp.where(j_idx > r_idx, 1.0, 0.0).astype(BF16)

    q_masked = {}
    for qi in range(Q_PER_STEP):
        for hd in range(n_heads):
            p, hh = divmod(hd, heads_per_tile)
            q_pair = q_ring[prev2, qi * TQ:(qi + 1) * TQ, p * LANES:(p + 1) * LANES]
            q_masked[qi, hd] = head_of(q_pair, hh)

    lo_half, hi_half = slice(0, half), slice(half, TK)
    diag_parts = [(lo_half, lo_half, "tri"), (lo_half, hi_half, "full"),
                  (hi_half, lo_half, "empty"), (hi_half, hi_half, "tri")]
    full_parts = [(slice(0, TK), slice(0, TQ), "full")]

    def assemble(parts, pieces):
        if len(parts) == 1:
            return pieces[0]
        return jnp.concatenate([jnp.concatenate(pieces[0:2], axis=1),
                                jnp.concatenate(pieces[2:4], axis=1)], axis=0)

    def gated_pair(accs, qi, p):
        rows, cols = slice(qi * TQ, (qi + 1) * TQ), slice(p * LANES, (p + 1) * LANES)
        yT = jnp.concatenate([accs[qi, heads_per_tile * p + n] for n in range(heads_per_tile)], axis=0)
        yb_ring[now2, rows, cols] = (yT.T * gz_ring[prev2, rows, cols]).astype(BF16)

    def key_blocks(blocks, runs, accs, extra=(), finish=False):
        runs, accs = dict(runs), dict(accs)
        tiles = [dict(qi=qi, kb=kb, h=hd, keep=keep, diag=diag, parts=diag_parts if diag else full_parts)
                 for qi, kb, diag, keep in blocks for hd in range(n_heads)]
        last_tile = {(t["qi"], t["h"]): n for n, t in enumerate(tiles)}

        def scores(t):
            if t["diag"]:
                t["z"] = z_ring[prev2, t["qi"] * n_heads + t["h"]]
                return
            start = pl.multiple_of(t["kb"] * TK, TK)
            p = t["h"] // heads_per_tile
            lanes = slice(p * LANES, (p + 1) * LANES)
            t["z"] = _dot_nt(k_scr[par_at, pl.ds(start, TK), lanes], q_masked[t["qi"], t["h"]])

        def softplus(t):
            sps, t["logb"] = [], []
            for rows, cols, kind in t["parts"]:
                if kind == "empty":
                    sps.append(jnp.zeros((half, half), BF16))
                    t["logb"].append(None)
                    continue
                z = t["z"][rows, cols]
                sp = jnp.maximum(z, 0.0) + jnp.log2(1.0 + jnp.exp2(-jnp.abs(z)))
                t["logb"].append(z - sp)
                if kind == "tri":
                    sp = jnp.where(strictly_earlier_half, sp, 0.0)
                sps.append(sp.astype(BF16))
            t["sp"] = assemble(t["parts"], sps)

        def later_sum(t):
            t["csum"] = _dot(strictly_later, t["sp"])

        def weights(t):
            key = (t["qi"], t["h"])
            run = runs[key]
            probs = []
            for (rows, cols, kind), logb in zip(t["parts"], t["logb"]):
                if kind == "empty":
                    probs.append(jnp.zeros((half, half), BF16))
                    continue
                a = jnp.exp2(logb - (run[:, cols] + t["csum"][rows, cols]))
                if kind == "tri":
                    a = jnp.where(strictly_earlier_half, a, 0.0)
                probs.append(a.astype(BF16))
            t["prob"] = assemble(t["parts"], probs)
            runs[key] = run + (t["csum"][0:1, :] + t["sp"][0:1, :].astype(F32))

        def values(n, t):
            key = (t["qi"], t["h"])
            hd = t["h"]
            v_blk = vT_scr[par_at, t["kb"], hd * GROUP_DIM:(hd + 1) * GROUP_DIM, :]
            if t["keep"] is not None:
                v_blk = (v_blk.astype(F32) * t["keep"]).astype(BF16)
            accs[key] = accs[key] + _dot(v_blk, t["prob"])
            if finish and last_tile[key] == n and hd % heads_per_tile == heads_per_tile - 1:
                gated_pair(accs, t["qi"], hd // heads_per_tile)

        n_tiles = len(tiles)
        lead, lag = SKEW
        ready = next((n for n, t in enumerate(tiles) if not t["diag"]), n_tiles)
        for t in tiles[:ready]:
            scores(t)
        steps = range(min(0, ready - lead), n_tiles + lag)
        extra = list(extra)
        slots = [EXTRA_START + (n * (len(steps) - EXTRA_START)) // max(1, len(extra))
                 for n in range(len(extra))]
        for count, step in enumerate(steps):
            while extra and slots[0] <= count:
                slots.pop(0)
                extra.pop(0)()
            if ready <= step + lead < n_tiles:
                scores(tiles[step + lead])
            if 0 <= step < n_tiles:
                softplus(tiles[step])
                later_sum(tiles[step])
            if 0 <= step - lag < n_tiles:
                weights(tiles[step - lag])
                values(step - lag, tiles[step - lag])
        for item in extra:
            item()
        return runs, accs

    runs = {(qi, hd): jnp.zeros((1, TQ), F32) for qi in range(Q_PER_STEP) for hd in range(n_heads)}
    accs = {(qi, hd): jnp.zeros((GROUP_DIM, TQ), F32) for qi in range(Q_PER_STEP) for hd in range(n_heads)}
    first_kb = j_at * Q_PER_STEP
    diag_blocks = [(qi, first_kb + qi, True, None) for qi in range(Q_PER_STEP)]
    has_earlier = (j_at > 0).astype(F32)
    prev_blocks = [(qi, jnp.maximum(first_kb + qi - 1, 0), False, has_earlier if qi == 0 else None)
                   for qi in range(Q_PER_STEP)]
    runs, accs = key_blocks(diag_blocks + prev_blocks, runs, accs, items, finish=True)

    for qi in range(Q_PER_STEP):
        def sticks_left(runs):
            return jnp.min(functools.reduce(jnp.minimum, [runs[qi, hd] for hd in range(n_heads)])) < UNDERFLOW_LOG2

        def earlier_block(state, qi=qi, sticks_left=sticks_left):
            kb, _, runs, accs = state
            runs, accs = key_blocks([(qi, kb, False, None)], runs, accs)
            return kb - 1, sticks_left(runs), runs, accs

        first = first_kb + qi - 2
        mine = lambda d: {k: v for k, v in d.items() if k[0] == qi}

        @pl.when((first >= 0) & sticks_left(runs))
        def _(qi=qi, first=first, earlier_block=earlier_block, mine=mine):
            _, _, _, more = lax.while_loop(lambda st: (st[0] >= 0) & st[1], earlier_block,
                                           (first, True, mine(runs), mine(accs)))
            for p in range(n_pairs):
                gated_pair(more, qi, p)


def _fused_block(x, mod3, w, w_sp, b_sp_full, a_g, a_b, w_out, ln_g, ln_b, alpha):
    bsz, seq, d = x.shape
    d_a = a_g.shape[-1]
    d_b = (w.shape[1] - 3 * d_a) // 4
    per_seq = seq // ROWS
    n_total = bsz * per_seq
    grid = (n_total + 2,)

    def stage(lag):
        return lambda s: _stage_slabs(s, n_total, per_seq)[lag]

    def rows_of(lag):
        return pl.BlockSpec((1, ROWS, d), lambda s: (*stage(lag)(s), 0))

    def mod_of(lag):
        return pl.BlockSpec((1, 3, d), lambda s: (stage(lag)(s)[0], 0, 0))

    def whole(a):
        return pl.BlockSpec(a.shape, lambda s: (0,) * a.ndim)

    return pl.pallas_call(
        functools.partial(_block_kernel, alpha, n_total),
        grid=grid,
        in_specs=[rows_of(0), rows_of(2), mod_of(0), mod_of(2), whole(w), whole(w_sp),
                  whole(b_sp_full), whole(a_g), whole(a_b), whole(w_out), whole(ln_g), whole(ln_b)],
        out_specs=rows_of(2),
        out_shape=jax.ShapeDtypeStruct((bsz, seq, d), x.dtype),
        scratch_shapes=[
            pltpu.VMEM((2, seq, d_b), BF16),
            pltpu.VMEM((2, seq // TK, d_b, TK), BF16),
            pltpu.VMEM((2, ROWS, d_b), BF16),
            pltpu.VMEM((2, ROWS, d_b), F32),
            pltpu.VMEM((3, ROWS, d_a), BF16),
            pltpu.VMEM((2, ROWS, d_b), BF16),
            pltpu.VMEM((2, Q_PER_STEP * (d_b // GROUP_DIM), TK, TQ), F32),
            pltpu.VMEM((w.shape[1] // MXU_SLAB, ROWS, MXU_SLAB), F32),
            pltpu.VMEM((ROWS, d), F32),
        ],
        compiler_params=pltpu.CompilerParams(
            dimension_semantics=("arbitrary",), vmem_limit_bytes=VMEM_LIMIT),
        name="fused_block",
    )(x, x, mod3, mod3, w, w_sp, b_sp_full, a_g, a_b, w_out, ln_g, ln_b)


def kernel(x, c, w_ada, b_ada, w_in, sgu_ln_g, sgu_ln_b, w_spatial, b_spatial, w_out, ln_g, ln_b):
    depth = w_ada.shape[0]
    bsz, seq, d = x.shape
    d_a = sgu_ln_g.shape[-1]
    alpha = (2.0 * depth) ** 0.25
    for layer in range(depth):
        mod, w_in_bf, w_out_bf = _prepare(c, w_ada[layer], b_ada[layer], w_in[layer], w_out[layer])
        b_sp_full = jnp.repeat(b_spatial[layer].T, GROUP_DIM, axis=1)
        x = _fused_block(
            x, mod.reshape(bsz, 3, d), w_in_bf, w_spatial[layer], b_sp_full,
            sgu_ln_g[layer].reshape(1, d_a), sgu_ln_b[layer].reshape(1, d_a),
            w_out_bf, ln_g[layer].reshape(1, d), ln_b[layer].reshape(1, d), alpha)
    return x
```

```python
import functools
import math

import jax
import jax.numpy as jnp
from jax import lax
from jax.experimental import pallas as pl
from jax.experimental.pallas import tpu as pltpu

F32 = jnp.float32
BF16 = jnp.bfloat16

N_GROUPS = 8
GROUP_DIM = 64
CHUNK = 128
LN_EPS = 1e-5
LANES = 128

PREP_TILE = 512
TQ = 256
TK = 256
Q_PER_STEP = 1
ROWS = Q_PER_STEP * TQ
SKEW = (8, 3)
EXTRA_START = 2
MXU_SLAB = 512
FIN_DELAY = 1
UNDERFLOW_LOG2 = 152.0
VMEM_LIMIT = 48 * 1024 * 1024


def _dot(a, b):
    return jnp.dot(a, b, preferred_element_type=F32)


def _dot_nt(a, b):
    return lax.dot_general(a, b, (((1,), (1,)), ((), ())), preferred_element_type=F32)


def _gelu_tanh(x):
    k1 = -2.0 * math.sqrt(2.0 / math.pi) * math.log2(math.e)
    return x / (1.0 + jnp.exp2(x * (k1 + (k1 * 0.044715) * (x * x))))


def _silu(x):
    return x / (1.0 + jnp.exp(-x))


def _layer_norm(x, g, b):
    mu = jnp.mean(x, axis=-1, keepdims=True)
    xc = x - mu
    var = jnp.mean(xc * xc, axis=-1, keepdims=True)
    return xc * lax.rsqrt(var + LN_EPS) * g + b


def _prep_kernel(c_ref, wa_ref, ba_ref, win_ref, wout_ref, mod_ref, win_bf_ref, wout_bf_ref):
    sc = _silu(c_ref[...]).astype(BF16)
    mod_ref[...] = _dot(sc, wa_ref[...].astype(BF16)) + ba_ref[...]
    win_bf_ref[...] = win_ref[...].astype(BF16)
    wout_bf_ref[...] = wout_ref[...].astype(BF16)


def _prepare(c, w_ada, b_ada, w_in, w_out):
    bsz, d = c.shape
    n_mod = w_ada.shape[1]
    steps = w_in.shape[1] // PREP_TILE
    mod_tiles = n_mod // PREP_TILE
    out_tile = PREP_TILE
    out_tiles = w_out.shape[1] // out_tile
    assert mod_tiles <= steps and out_tiles <= steps
    mod_col = lambda j: (0, jnp.minimum(j, mod_tiles - 1))
    out_col = lambda j: (0, jnp.minimum(j, out_tiles - 1))
    return pl.pallas_call(
        _prep_kernel,
        grid=(steps,),
        in_specs=[
            pl.BlockSpec((bsz, d), lambda j: (0, 0)),
            pl.BlockSpec((d, PREP_TILE), mod_col),
            pl.BlockSpec((1, PREP_TILE), mod_col),
            pl.BlockSpec((w_in.shape[0], PREP_TILE), lambda j: (0, j)),
            pl.BlockSpec((w_out.shape[0], out_tile), out_col),
        ],
        out_specs=[
            pl.BlockSpec((bsz, PREP_TILE), mod_col),
            pl.BlockSpec((w_in.shape[0], PREP_TILE), lambda j: (0, j)),
            pl.BlockSpec((w_out.shape[0], out_tile), out_col),
        ],
        out_shape=[
            jax.ShapeDtypeStruct((bsz, n_mod), F32),
            jax.ShapeDtypeStruct(w_in.shape, BF16),
            jax.ShapeDtypeStruct(w_out.shape, BF16),
        ],
        compiler_params=pltpu.CompilerParams(
            dimension_semantics=("arbitrary",), vmem_limit_bytes=VMEM_LIMIT),
        name="adaln_mod_and_casts",
    )(c, w_ada, b_ada.reshape(1, n_mod), w_in, w_out)


def _stage_slabs(s, n_total, per_seq):
    last = n_total - 1
    out = []
    for lag in range(3):
        t = jnp.clip(s - lag, 0, last)
        out.append((lax.div(t, per_seq), lax.rem(t, per_seq)))
    return out


def _block_kernel(alpha, n_total,
                  x_in_ref, x_out_ref, mod_in_ref, mod_out_ref, w_ref, wsp_ref, bsp_ref,
                  ag_ref, ab_ref, wo_ref, g_ref, b_ref, o_ref,
                  k_scr, vT_scr, q_ring, gz_ring, ya_ring, yb_ring, z_ring, land_scr, y_scr):
    s = pl.program_id(0)
    d_b = q_ring.shape[-1]
    d_a = ya_ring.shape[-1]
    per_seq = k_scr.shape[1] // ROWS
    (b_in, j_in), (b_at, j_at), _ = _stage_slabs(s, n_total, per_seq)
    par_in, par_at = lax.rem(b_in, 2), lax.rem(b_at, 2)
    n_heads = d_b // GROUP_DIM
    heads_per_tile = LANES // GROUP_DIM
    n_pairs = n_heads // heads_per_tile
    now2, prev2 = lax.rem(s, 2), lax.rem(s + 1, 2)
    now3, prev3 = lax.rem(s, 3), lax.rem(s + 1, 3)

    @pl.when(s == 0)
    def _():
        q_ring[...] = jnp.zeros_like(q_ring)
        gz_ring[...] = jnp.zeros_like(gz_ring)
        ya_ring[...] = jnp.zeros_like(ya_ring)
        yb_ring[...] = jnp.zeros_like(yb_ring)
        z_ring[...] = jnp.zeros_like(z_ring)
        k_scr[0, 0:ROWS, :] = jnp.zeros((ROWS, d_b), BF16)
        for n in range(Q_PER_STEP):
            vT_scr[0, n] = jnp.zeros((d_b, TK), BF16)

    new = {}

    def item_h():
        shift = mod_in_ref[0, 0:1, :]
        scale = mod_in_ref[0, 1:2, :]
        new["h"] = (x_in_ref[0] * (1.0 + scale) + shift).astype(BF16)

    def h():
        return new["h"]

    def ycat():
        return jnp.concatenate([ya_ring[prev3], yb_ring[prev2]], axis=1)

    def both(*fs):
        return lambda: [f() for f in fs]

    def fin_out():
        gate = mod_out_ref[0, 2:3, :]
        r = alpha * x_out_ref[0] + gate * y_scr[...]
        o_ref[0] = _layer_norm(r, g_ref[...], b_ref[...])

    def fin_u(p):
        new["ug"] = _gelu_tanh(p())

    def fin_v(p):
        new["vn"] = _layer_norm(_gelu_tanh(p()), ag_ref[...], ab_ref[...]).astype(BF16)
        t_idx = lax.broadcasted_iota(jnp.int32, (CHUNK, CHUNK), 0)
        s_idx = lax.broadcasted_iota(jnp.int32, (CHUNK, CHUNK), 1)
        causal = t_idx >= s_idx
        w_sp = [jnp.where(causal, wsp_ref[g], 0.0).astype(BF16) for g in range(N_GROUPS)]
        first_group = lax.broadcasted_iota(jnp.int32, (CHUNK, LANES), 1) < GROUP_DIM
        chunk_rows = [slice(c * CHUNK, (c + 1) * CHUNK) for c in range(ROWS // CHUNK)]
        zero = jnp.zeros((CHUNK, LANES), BF16)
        pieces = [[] for _ in chunk_rows]
        for pr in range(d_a // LANES):
            cols = slice(pr * LANES, (pr + 1) * LANES)
            w_pair = jnp.concatenate([w_sp[2 * pr], w_sp[2 * pr + 1]], axis=1)
            vps = [new["vn"][rows, cols] for rows in chunk_rows]
            stacked = jnp.concatenate(
                [jnp.concatenate([jnp.where(first_group, vp, zero) for vp in vps], axis=1),
                 jnp.concatenate([jnp.where(first_group, zero, vp) for vp in vps], axis=1)], axis=0)
            mixed = _dot(w_pair, stacked)
            for c, rows in enumerate(chunk_rows):
                pieces[c].append(new["ug"][rows, cols] * (mixed[:, c * LANES:(c + 1) * LANES] + bsp_ref[:, cols]))
        new["ya_pre"] = jnp.concatenate([jnp.concatenate(p, axis=1) for p in pieces], axis=0)

    def fin_za(p):
        ya_ring[now3] = (_silu(p()) * new["ya_pre"]).astype(BF16)

    scale_q = math.log2(math.e) / math.sqrt(GROUP_DIM)

    def fin_q(p):
        new["q"] = (p() * scale_q).astype(BF16)
        q_ring[now2] = new["q"]

    def fin_k(p):
        new["k"] = p().astype(BF16)
        k_scr[par_in, pl.ds(pl.multiple_of(j_in * ROWS, ROWS), ROWS), :] = new["k"]

    def head_of(q_pair, hh):
        in_head = (lane >= hh * GROUP_DIM) & (lane < (hh + 1) * GROUP_DIM)
        return jnp.where(in_head, q_pair, jnp.zeros_like(q_pair))

    def pre_scores():
        for qi in range(Q_PER_STEP):
            rows = slice(qi * TQ, (qi + 1) * TQ)
            for hd in range(n_heads):
                p, hh = divmod(hd, heads_per_tile)
                lanes = slice(p * LANES, (p + 1) * LANES)
                z_ring[now2, qi * n_heads + hd] = _dot_nt(
                    new["k"][rows, lanes], head_of(new["q"][rows, lanes], hh))

    def fin_vt(p):
        v = p()
        for n in range(Q_PER_STEP):
            vT_scr[par_in, j_in * Q_PER_STEP + n] = v[n * TK:(n + 1) * TK, :].T.astype(BF16)

    def fin_gz(p):
        gz_ring[now2] = _silu(p())

    dots, due = [], {}

    def out_slab(j):
        cols = slice(j * MXU_SLAB, (j + 1) * MXU_SLAB)

        def item():
            y_scr[:, cols] = _dot(ycat(), wo_ref[:, cols])
        return item

    for j in range(wo_ref.shape[1] // MXU_SLAB):
        dots.append(out_slab(j))
    due.setdefault(len(dots) - 1 + FIN_DELAY, []).append(fin_out)

    o = 3 * d_a
    projections = [(0, fin_u), (d_a, fin_v), (2 * d_a, fin_za), (o, fin_q), (o + d_b, fin_k),
                   (o + 2 * d_b, fin_vt), (o + 3 * d_b, fin_gz)]
    assert d_a == d_b == MXU_SLAB

    assert land_scr.shape[0] == len(projections)

    def in_slab(n, lo):
        def item():
            land_scr[n] = _dot(h(), w_ref[:, lo:lo + MXU_SLAB])
        return item

    def landed(n):
        return lambda: land_scr[n]

    for n, (lo, fin) in enumerate(projections):
        dots.append(in_slab(n, lo))
        due.setdefault(len(dots) - 1 + FIN_DELAY, []).append(functools.partial(fin, landed(n)))
    items = [item_h]
    for n, dot_item in enumerate(dots):
        items.append(both(dot_item, *due.pop(n, [])))
    items.append(both(*[f for n in sorted(due) for f in due[n]], pre_scores))

    lane = lax.broadcasted_iota(jnp.int32, (TQ, LANES), 1)
    half = TK // 2
    s_idx = lax.broadcasted_iota(jnp.int32, (half, half), 0)
    t_idx = lax.broadcasted_iota(jnp.int32, (half, half), 1)
    strictly_earlier_half = s_idx < t_idx
    j_idx = lax.broadcasted_iota(jnp.int32, (TK, TK), 1)
    r_idx = lax.broadcasted_iota(jnp.int32, (TK, TK), 0)
    strictly_later = jnp.where(j_idx > r_idx, 1.0, 0.0).astype(BF16)

    q_masked = {}
    for qi in range(Q_PER_STEP):
        for hd in range(n_heads):
            p, hh = divmod(hd, heads_per_tile)
            q_pair = q_ring[prev2, qi * TQ:(qi + 1) * TQ, p * LANES:(p + 1) * LANES]
            q_masked[qi, hd] = head_of(q_pair, hh)

    lo_half, hi_half = slice(0, half), slice(half, TK)
    diag_parts = [(lo_half, lo_half, "tri"), (lo_half, hi_half, "full"),
                  (hi_half, lo_half, "empty"), (hi_half, hi_half, "tri")]
    full_parts = [(slice(0, TK), slice(0, TQ), "full")]

    def assemble(parts, pieces):
        if len(parts) == 1:
            return pieces[0]
        return jnp.concatenate([jnp.concatenate(pieces[0:2], axis=1),
                                jnp.concatenate(pieces[2:4], axis=1)], axis=0)

    def gated_pair(accs, qi, p):
        rows, cols = slice(qi * TQ, (qi + 1) * TQ), slice(p * LANES, (p + 1) * LANES)
        yT = jnp.concatenate([accs[qi, heads_per_tile * p + n] for n in range(heads_per_tile)], axis=0)
        yb_ring[now2, rows, cols] = (yT.T * gz_ring[prev2, rows, cols]).astype(BF16)

    def key_blocks(blocks, runs, accs, extra=(), finish=False):
        runs, accs = dict(runs), dict(accs)
        tiles = [dict(qi=qi, kb=kb, h=hd, keep=keep, diag=diag, parts=diag_parts if diag else full_parts)
                 for qi, kb, diag, keep in blocks for hd in range(n_heads)]
        last_tile = {(t["qi"], t["h"]): n for n, t in enumerate(tiles)}

        def scores(t):
            if t["diag"]:
                t["z"] = z_ring[prev2, t["qi"] * n_heads + t["h"]]
                return
            start = pl.multiple_of(t["kb"] * TK, TK)
            p = t["h"] // heads_per_tile
            lanes = slice(p * LANES, (p + 1) * LANES)
            t["z"] = _dot_nt(k_scr[par_at, pl.ds(start, TK), lanes], q_masked[t["qi"], t["h"]])

        def softplus(t):
            sps, t["logb"] = [], []
            for rows, cols, kind in t["parts"]:
                if kind == "empty":
                    sps.append(jnp.zeros((half, half), BF16))
                    t["logb"].append(None)
                    continue
                z = t["z"][rows, cols]
                sp = jnp.maximum(z, 0.0) + jnp.log2(1.0 + jnp.exp2(-jnp.abs(z)))
                t["logb"].append(z - sp)
                if kind == "tri":
                    sp = jnp.where(strictly_earlier_half, sp, 0.0)
                sps.append(sp.astype(BF16))
            t["sp"] = assemble(t["parts"], sps)

        def later_sum(t):
            t["csum"] = _dot(strictly_later, t["sp"])

        def weights(t):
            key = (t["qi"], t["h"])
            run = runs[key]
            probs = []
            for (rows, cols, kind), logb in zip(t["parts"], t["logb"]):
                if kind == "empty":
                    probs.append(jnp.zeros((half, half), BF16))
                    continue
                a = jnp.exp2(logb - (run[:, cols] + t["csum"][rows, cols]))
                if kind == "tri":
                    a = jnp.where(strictly_earlier_half, a, 0.0)
                probs.append(a.astype(BF16))
            t["prob"] = assemble(t["parts"], probs)
            runs[key] = run + (t["csum"][0:1, :] + t["sp"][0:1, :].astype(F32))

        def values(n, t):
            key = (t["qi"], t["h"])
            hd = t["h"]
            v_blk = vT_scr[par_at, t["kb"], hd * GROUP_DIM:(hd + 1) * GROUP_DIM, :]
            if t["keep"] is not None:
                v_blk = (v_blk.astype(F32) * t["keep"]).astype(BF16)
            accs[key] = accs[key] + _dot(v_blk, t["prob"])
            if finish and last_tile[key] == n and hd % heads_per_tile == heads_per_tile - 1:
                gated_pair(accs, t["qi"], hd // heads_per_tile)

        n_tiles = len(tiles)
        lead, lag = SKEW
        ready = next((n for n, t in enumerate(tiles) if not t["diag"]), n_tiles)
        for t in tiles[:ready]:
            scores(t)
        steps = range(min(0, ready - lead), n_tiles + lag)
        extra = list(extra)
        slots = [EXTRA_START + (n * (len(steps) - EXTRA_START)) // max(1, len(extra))
                 for n in range(len(extra))]
        for count, step in enumerate(steps):
            while extra and slots[0] <= count:
                slots.pop(0)
                extra.pop(0)()
            if ready <= step + lead < n_tiles:
                scores(tiles[step + lead])
            if 0 <= step < n_tiles:
                softplus(tiles[step])
                later_sum(tiles[step])
            if 0 <= step - lag < n_tiles:
                weights(tiles[step - lag])
                values(step - lag, tiles[step - lag])
        for item in extra:
            item()
        return runs, accs

    runs = {(qi, hd): jnp.zeros((1, TQ), F32) for qi in range(Q_PER_STEP) for hd in range(n_heads)}
    accs = {(qi, hd): jnp.zeros((GROUP_DIM, TQ), F32) for qi in range(Q_PER_STEP) for hd in range(n_heads)}
    first_kb = j_at * Q_PER_STEP
    diag_blocks = [(qi, first_kb + qi, True, None) for qi in range(Q_PER_STEP)]
    has_earlier = (j_at > 0).astype(F32)
    prev_blocks = [(qi, jnp.maximum(first_kb + qi - 1, 0), False, has_earlier if qi == 0 else None)
                   for qi in range(Q_PER_STEP)]
    runs, accs = key_blocks(diag_blocks + prev_blocks, runs, accs, items, finish=True)

    for qi in range(Q_PER_STEP):
        def sticks_left(runs):
            return jnp.min(functools.reduce(jnp.minimum, [runs[qi, hd] for hd in range(n_heads)])) < UNDERFLOW_LOG2

        def earlier_block(state, qi=qi, sticks_left=sticks_left):
            kb, _, runs, accs = state
            runs, accs = key_blocks([(qi, kb, False, None)], runs, accs)
            return kb - 1, sticks_left(runs), runs, accs

        first = first_kb + qi - 2
        mine = lambda d: {k: v for k, v in d.items() if k[0] == qi}

        @pl.when((first >= 0) & sticks_left(runs))
        def _(qi=qi, first=first, earlier_block=earlier_block, mine=mine):
            _, _, _, more = lax.while_loop(lambda st: (st[0] >= 0) & st[1], earlier_block,
                                           (first, True, mine(runs), mine(accs)))
            for p in range(n_pairs):
                gated_pair(more, qi, p)


def _fused_block(x, mod3, w, w_sp, b_sp_full, a_g, a_b, w_out, ln_g, ln_b, alpha):
    bsz, seq, d = x.shape
    d_a = a_g.shape[-1]
    d_b = (w.shape[1] - 3 * d_a) // 4
    per_seq = seq // ROWS
    n_total = bsz * per_seq
    grid = (n_total + 2,)

    def stage(lag):
        return lambda s: _stage_slabs(s, n_total, per_seq)[lag]

    def rows_of(lag):
        return pl.BlockSpec((1, ROWS, d), lambda s: (*stage(lag)(s), 0))

    def mod_of(lag):
        return pl.BlockSpec((1, 3, d), lambda s: (stage(lag)(s)[0], 0, 0))

    def whole(a):
        return pl.BlockSpec(a.shape, lambda s: (0,) * a.ndim)

    return pl.pallas_call(
        functools.partial(_block_kernel, alpha, n_total),
        grid=grid,
        in_specs=[rows_of(0), rows_of(2), mod_of(0), mod_of(2), whole(w), whole(w_sp),
                  whole(b_sp_full), whole(a_g), whole(a_b), whole(w_out), whole(ln_g), whole(ln_b)],
        out_specs=rows_of(2),
        out_shape=jax.ShapeDtypeStruct((bsz, seq, d), x.dtype),
        scratch_shapes=[
            pltpu.VMEM((2, seq, d_b), BF16),
            pltpu.VMEM((2, seq // TK, d_b, TK), BF16),
            pltpu.VMEM((2, ROWS, d_b), BF16),
            pltpu.VMEM((2, ROWS, d_b), F32),
            pltpu.VMEM((3, ROWS, d_a), BF16),
            pltpu.VMEM((2, ROWS, d_b), BF16),
            pltpu.VMEM((2, Q_PER_STEP * (d_b // GROUP_DIM), TK, TQ), F32),
            pltpu.VMEM((w.shape[1] // MXU_SLAB, ROWS, MXU_SLAB), F32),
            pltpu.VMEM((ROWS, d), F32),
        ],
        compiler_params=pltpu.CompilerParams(
            dimension_semantics=("arbitrary",), vmem_limit_bytes=VMEM_LIMIT),
        name="fused_block",
    )(x, x, mod3, mod3, w, w_sp, b_sp_full, a_g, a_b, w_out, ln_g, ln_b)


def kernel(x, c, w_ada, b_ada, w_in, sgu_ln_g, sgu_ln_b, w_spatial, b_spatial, w_out, ln_g, ln_b):
    depth = w_ada.shape[0]
    bsz, seq, d = x.shape
    d_a = sgu_ln_g.shape[-1]
    alpha = (2.0 * depth) ** 0.25
    for layer in range(depth):
        mod, w_in_bf, w_out_bf = _prepare(c, w_ada[layer], b_ada[layer], w_in[layer], w_out[layer])
        b_sp_full = jnp.repeat(b_spatial[layer].T, GROUP_DIM, axis=1)
        x = _fused_block(
            x, mod.reshape(bsz, 3, d), w_in_bf, w_spatial[layer], b_sp_full,
            sgu_ln_g[layer].reshape(1, d_a), sgu_ln_b[layer].reshape(1, d_a),
            w_out_bf, ln_g[layer].reshape(1, d), ln_b[layer].reshape(1, d), alpha)
    return x
```

```python
import functools
import math

import jax
import jax.numpy as jnp
from jax import lax
from jax.experimental import pallas as pl
from jax.experimental.pallas import tpu as pltpu

F32 = jnp.float32
BF16 = jnp.bfloat16

N_GROUPS = 8
GROUP_DIM = 64
CHUNK = 128
LN_EPS = 1e-5
LANES = 128

PREP_TILE = 512
TQ = 256
TK = 256
Q_PER_STEP = 1
ROWS = Q_PER_STEP * TQ
SKEW = (8, 3)
EXTRA_START = 2
MXU_SLAB = 512
FIN_DELAY = 1
UNDERFLOW_LOG2 = 152.0
VMEM_LIMIT = 48 * 1024 * 1024


def _dot(a, b):
    return jnp.dot(a, b, preferred_element_type=F32)


def _dot_nt(a, b):
    return lax.dot_general(a, b, (((1,), (1,)), ((), ())), preferred_element_type=F32)


def _gelu_tanh(x):
    k1 = -2.0 * math.sqrt(2.0 / math.pi) * math.log2(math.e)
    return x / (1.0 + jnp.exp2(x * (k1 + (k1 * 0.044715) * (x * x))))


def _silu(x):
    return x / (1.0 + jnp.exp(-x))


def _layer_norm(x, g, b):
    mu = jnp.mean(x, axis=-1, keepdims=True)
    xc = x - mu
    var = jnp.mean(xc * xc, axis=-1, keepdims=True)
    return xc * lax.rsqrt(var + LN_EPS) * g + b


def _prep_kernel(c_ref, wa_ref, ba_ref, win_ref, wout_ref, mod_ref, win_bf_ref, wout_bf_ref):
    sc = _silu(c_ref[...]).astype(BF16)
    mod_ref[...] = _dot(sc, wa_ref[...].astype(BF16)) + ba_ref[...]
    win_bf_ref[...] = win_ref[...].astype(BF16)
    wout_bf_ref[...] = wout_ref[...].astype(BF16)


def _prepare(c, w_ada, b_ada, w_in, w_out):
    bsz, d = c.shape
    n_mod = w_ada.shape[1]
    steps = w_in.shape[1] // PREP_TILE
    mod_tiles = n_mod // PREP_TILE
    out_tile = PREP_TILE
    out_tiles = w_out.shape[1] // out_tile
    assert mod_tiles <= steps and out_tiles <= steps
    mod_col = lambda j: (0, jnp.minimum(j, mod_tiles - 1))
    out_col = lambda j: (0, jnp.minimum(j, out_tiles - 1))
    return pl.pallas_call(
        _prep_kernel,
        grid=(steps,),
        in_specs=[
            pl.BlockSpec((bsz, d), lambda j: (0, 0)),
            pl.BlockSpec((d, PREP_TILE), mod_col),
            pl.BlockSpec((1, PREP_TILE), mod_col),
            pl.BlockSpec((w_in.shape[0], PREP_TILE), lambda j: (0, j)),
            pl.BlockSpec((w_out.shape[0], out_tile), out_col),
        ],
        out_specs=[
            pl.BlockSpec((bsz, PREP_TILE), mod_col),
            pl.BlockSpec((w_in.shape[0], PREP_TILE), lambda j: (0, j)),
            pl.BlockSpec((w_out.shape[0], out_tile), out_col),
        ],
        out_shape=[
            jax.ShapeDtypeStruct((bsz, n_mod), F32),
            jax.ShapeDtypeStruct(w_in.shape, BF16),
            jax.ShapeDtypeStruct(w_out.shape, BF16),
        ],
        compiler_params=pltpu.CompilerParams(
            dimension_semantics=("arbitrary",), vmem_limit_bytes=VMEM_LIMIT),
        name="adaln_mod_and_casts",
    )(c, w_ada, b_ada.reshape(1, n_mod), w_in, w_out)


def _stage_slabs(s, n_total, per_seq):
    last = n_total - 1
    out = []
    for lag in range(3):
        t = jnp.clip(s - lag, 0, last)
        out.append((lax.div(t, per_seq), lax.rem(t, per_seq)))
    return out


def _block_kernel(alpha, n_total,
                  x_in_ref, x_out_ref, mod_in_ref, mod_out_ref, w_ref, wsp_ref, bsp_ref,
                  ag_ref, ab_ref, wo_ref, g_ref, b_ref, o_ref,
                  k_scr, vT_scr, q_ring, gz_ring, ya_ring, yb_ring, z_ring, land_scr, y_scr):
    s = pl.program_id(0)
    d_b = q_ring.shape[-1]
    d_a = ya_ring.shape[-1]
    per_seq = k_scr.shape[1] // ROWS
    (b_in, j_in), (b_at, j_at), _ = _stage_slabs(s, n_total, per_seq)
    par_in, par_at = lax.rem(b_in, 2), lax.rem(b_at, 2)
    n_heads = d_b // GROUP_DIM
    heads_per_tile = LANES // GROUP_DIM
    n_pairs = n_heads // heads_per_tile
    now2, prev2 = lax.rem(s, 2), lax.rem(s + 1, 2)
    now3, prev3 = lax.rem(s, 3), lax.rem(s + 1, 3)

    @pl.when(s == 0)
    def _():
        q_ring[...] = jnp.zeros_like(q_ring)
        gz_ring[...] = jnp.zeros_like(gz_ring)
        ya_ring[...] = jnp.zeros_like(ya_ring)
        yb_ring[...] = jnp.zeros_like(yb_ring)
        z_ring[...] = jnp.zeros_like(z_ring)
        k_scr[0, 0:ROWS, :] = jnp.zeros((ROWS, d_b), BF16)
        for n in range(Q_PER_STEP):
            vT_scr[0, n] = jnp.zeros((d_b, TK), BF16)

    new = {}

    def item_h():
        shift = mod_in_ref[0, 0:1, :]
        scale = mod_in_ref[0, 1:2, :]
        new["h"] = (x_in_ref[0] * (1.0 + scale) + shift).astype(BF16)

    def h():
        return new["h"]

    def ycat():
        return jnp.concatenate([ya_ring[prev3], yb_ring[prev2]], axis=1)

    def both(*fs):
        return lambda: [f() for f in fs]

    def fin_out():
        gate = mod_out_ref[0, 2:3, :]
        r = alpha * x_out_ref[0] + gate * y_scr[...]
        o_ref[0] = _layer_norm(r, g_ref[...], b_ref[...])

    def fin_u(p):
        new["ug"] = _gelu_tanh(p())

    def fin_v(p):
        new["vn"] = _layer_norm(_gelu_tanh(p()), ag_ref[...], ab_ref[...]).astype(BF16)
        t_idx = lax.broadcasted_iota(jnp.int32, (CHUNK, CHUNK), 0)
        s_idx = lax.broadcasted_iota(jnp.int32, (CHUNK, CHUNK), 1)
        causal = t_idx >= s_idx
        w_sp = [jnp.where(causal, wsp_ref[g], 0.0).astype(BF16) for g in range(N_GROUPS)]
        first_group = lax.broadcasted_iota(jnp.int32, (CHUNK, LANES), 1) < GROUP_DIM
        chunk_rows = [slice(c * CHUNK, (c + 1) * CHUNK) for c in range(ROWS // CHUNK)]
        zero = jnp.zeros((CHUNK, LANES), BF16)
        pieces = [[] for _ in chunk_rows]
        for pr in range(d_a // LANES):
            cols = slice(pr * LANES, (pr + 1) * LANES)
            w_pair = jnp.concatenate([w_sp[2 * pr], w_sp[2 * pr + 1]], axis=1)
            vps = [new["vn"][rows, cols] for rows in chunk_rows]
            stacked = jnp.concatenate(
                [jnp.concatenate([jnp.where(first_group, vp, zero) for vp in vps], axis=1),
                 jnp.concatenate([jnp.where(first_group, zero, vp) for vp in vps], axis=1)], axis=0)
            mixed = _dot(w_pair, stacked)
            for c, rows in enumerate(chunk_rows):
                pieces[c].append(new["ug"][rows, cols] * (mixed[:, c * LANES:(c + 1) * LANES] + bsp_ref[:, cols]))
        new["ya_pre"] = jnp.concatenate([jnp.concatenate(p, axis=1) for p in pieces], axis=0)

    def fin_za(p):
        ya_ring[now3] = (_silu(p()) * new["ya_pre"]).astype(BF16)

    scale_q = math.log2(math.e) / math.sqrt(GROUP_DIM)

    def fin_q(p):
        new["q"] = (p() * scale_q).astype(BF16)
        q_ring[now2] = new["q"]

    def fin_k(p):
        new["k"] = p().astype(BF16)
        k_scr[par_in, pl.ds(pl.multiple_of(j_in * ROWS, ROWS), ROWS), :] = new["k"]

    def head_of(q_pair, hh):
        in_head = (lane >= hh * GROUP_DIM) & (lane < (hh + 1) * GROUP_DIM)
        return jnp.where(in_head, q_pair, jnp.zeros_like(q_pair))

    def pre_scores():
        lo, hi = slice(0, TK // 2), slice(TK // 2, TK)
        for qi in range(Q_PER_STEP):
            rows = slice(qi * TQ, (qi + 1) * TQ)
            for p in range(n_pairs):
                lanes = slice(p * LANES, (p + 1) * LANES)
                k_pair = new["k"][rows, lanes]
                qs = [head_of(new["q"][rows, lanes], hh) for hh in range(heads_per_tile)]
                early = _dot_nt(k_pair[lo], jnp.concatenate(qs, axis=0))
                late = _dot_nt(k_pair[hi], jnp.concatenate([q[hi] for q in qs], axis=0))
                for hh in range(heads_per_tile):
                    tile = qi * n_heads + p * heads_per_tile + hh
                    z_ring[now2, tile, lo, :] = early[:, hh * TQ:(hh + 1) * TQ]
                    z_ring[now2, tile, hi, hi] = late[:, hh * (TQ // 2):(hh + 1) * (TQ // 2)]

    def fin_vt(p):
        v = p()
        for n in range(Q_PER_STEP):
            vT_scr[par_in, j_in * Q_PER_STEP + n] = v[n * TK:(n + 1) * TK, :].T.astype(BF16)

    def fin_gz(p):
        gz_ring[now2] = _silu(p())

    dots, due = [], {}

    def out_slab(j):
        cols = slice(j * MXU_SLAB, (j + 1) * MXU_SLAB)

        def item():
            y_scr[:, cols] = _dot(ycat(), wo_ref[:, cols])
        return item

    for j in range(wo_ref.shape[1] // MXU_SLAB):
        dots.append(out_slab(j))
    due.setdefault(len(dots) - 1 + FIN_DELAY, []).append(fin_out)

    o = 3 * d_a
    projections = [(0, fin_u), (d_a, fin_v), (2 * d_a, fin_za), (o, fin_q), (o + d_b, fin_k),
                   (o + 2 * d_b, fin_vt), (o + 3 * d_b, fin_gz)]
    assert d_a == d_b == MXU_SLAB

    assert land_scr.shape[0] == len(projections)

    def in_slab(n, lo):
        def item():
            land_scr[n] = _dot(h(), w_ref[:, lo:lo + MXU_SLAB])
        return item

    def landed(n):
        return lambda: land_scr[n]

    for n, (lo, fin) in enumerate(projections):
        dots.append(in_slab(n, lo))
        due.setdefault(len(dots) - 1 + FIN_DELAY, []).append(functools.partial(fin, landed(n)))
    items = [item_h]
    for n, dot_item in enumerate(dots):
        items.append(both(dot_item, *due.pop(n, [])))
    items.append(both(*[f for n in sorted(due) for f in due[n]], pre_scores))

    lane = lax.broadcasted_iota(jnp.int32, (TQ, LANES), 1)
    half = TK // 2
    s_idx = lax.broadcasted_iota(jnp.int32, (half, half), 0)
    t_idx = lax.broadcasted_iota(jnp.int32, (half, half), 1)
    strictly_earlier_half = s_idx < t_idx
    j_idx = lax.broadcasted_iota(jnp.int32, (TK, TK), 1)
    r_idx = lax.broadcasted_iota(jnp.int32, (TK, TK), 0)
    strictly_later = jnp.where(j_idx > r_idx, 1.0, 0.0).astype(BF16)

    q_masked = {}
    for qi in range(Q_PER_STEP):
        for hd in range(n_heads):
            p, hh = divmod(hd, heads_per_tile)
            q_pair = q_ring[prev2, qi * TQ:(qi + 1) * TQ, p * LANES:(p + 1) * LANES]
            q_masked[qi, hd] = head_of(q_pair, hh)

    lo_half, hi_half = slice(0, half), slice(half, TK)
    diag_parts = [(lo_half, lo_half, "tri"), (lo_half, hi_half, "full"),
                  (hi_half, lo_half, "empty"), (hi_half, hi_half, "tri")]
    full_parts = [(slice(0, TK), slice(0, TQ), "full")]

    def assemble(parts, pieces):
        if len(parts) == 1:
            return pieces[0]
        return jnp.concatenate([jnp.concatenate(pieces[0:2], axis=1),
                                jnp.concatenate(pieces[2:4], axis=1)], axis=0)

    def gated_pair(accs, qi, p):
        rows, cols = slice(qi * TQ, (qi + 1) * TQ), slice(p * LANES, (p + 1) * LANES)
        yT = jnp.concatenate([accs[qi, heads_per_tile * p + n] for n in range(heads_per_tile)], axis=0)
        yb_ring[now2, rows, cols] = (yT.T * gz_ring[prev2, rows, cols]).astype(BF16)

    def key_blocks(blocks, runs, accs, extra=(), finish=False):
        runs, accs = dict(runs), dict(accs)
        tiles = [dict(qi=qi, kb=kb, h=hd, keep=keep, diag=diag, parts=diag_parts if diag else full_parts)
                 for qi, kb, diag, keep in blocks for hd in range(n_heads)]
        last_tile = {(t["qi"], t["h"]): n for n, t in enumerate(tiles)}

        def scores(t):
            if t["diag"]:
                t["z"] = z_ring[prev2, t["qi"] * n_heads + t["h"]]
                return
            start = pl.multiple_of(t["kb"] * TK, TK)
            p = t["h"] // heads_per_tile
            lanes = slice(p * LANES, (p + 1) * LANES)
            t["z"] = _dot_nt(k_scr[par_at, pl.ds(start, TK), lanes], q_masked[t["qi"], t["h"]])

        def softplus(t):
            sps, t["logb"] = [], []
            for rows, cols, kind in t["parts"]:
                if kind == "empty":
                    sps.append(jnp.zeros((half, half), BF16))
                    t["logb"].append(None)
                    continue
                z = t["z"][rows, cols]
                sp = jnp.maximum(z, 0.0) + jnp.log2(1.0 + jnp.exp2(-jnp.abs(z)))
                t["logb"].append(z - sp)
                if kind == "tri":
                    sp = jnp.where(strictly_earlier_half, sp, 0.0)
                sps.append(sp.astype(BF16))
            t["sp"] = assemble(t["parts"], sps)

        def later_sum(t):
            t["csum"] = _dot(strictly_later, t["sp"])

        def weights(t):
            key = (t["qi"], t["h"])
            run = runs[key]
            probs = []
            for (rows, cols, kind), logb in zip(t["parts"], t["logb"]):
                if kind == "empty":
                    probs.append(jnp.zeros((half, half), BF16))
                    continue
                a = jnp.exp2(logb - (run[:, cols] + t["csum"][rows, cols]))
                if kind == "tri":
                    a = jnp.where(strictly_earlier_half, a, 0.0)
                probs.append(a.astype(BF16))
            t["prob"] = assemble(t["parts"], probs)
            runs[key] = run + (t["csum"][0:1, :] + t["sp"][0:1, :].astype(F32))

        def values(n, t):
            key = (t["qi"], t["h"])
            hd = t["h"]
            v_blk = vT_scr[par_at, t["kb"], hd * GROUP_DIM:(hd + 1) * GROUP_DIM, :]
            if t["keep"] is not None:
                v_blk = (v_blk.astype(F32) * t["keep"]).astype(BF16)
            accs[key] = accs[key] + _dot(v_blk, t["prob"])
            if finish and last_tile[key] == n and hd % heads_per_tile == heads_per_tile - 1:
                gated_pair(accs, t["qi"], hd // heads_per_tile)

        n_tiles = len(tiles)
        lead, lag = SKEW
        ready = next((n for n, t in enumerate(tiles) if not t["diag"]), n_tiles)
        for t in tiles[:ready]:
            scores(t)
        steps = range(min(0, ready - lead), n_tiles + lag)
        extra = list(extra)
        slots = [EXTRA_START + (n * (len(steps) - EXTRA_START)) // max(1, len(extra))
                 for n in range(len(extra))]
        for count, step in enumerate(steps):
            while extra and slots[0] <= count:
                slots.pop(0)
                extra.pop(0)()
            if ready <= step + lead < n_tiles:
                scores(tiles[step + lead])
            if 0 <= step < n_tiles:
                softplus(tiles[step])
                later_sum(tiles[step])
            if 0 <= step - lag < n_tiles:
                weights(tiles[step - lag])
                values(step - lag, tiles[step - lag])
        for item in extra:
            item()
        return runs, accs

    runs = {(qi, hd): jnp.zeros((1, TQ), F32) for qi in range(Q_PER_STEP) for hd in range(n_heads)}
    accs = {(qi, hd): jnp.zeros((GROUP_DIM, TQ), F32) for qi in range(Q_PER_STEP) for hd in range(n_heads)}
    first_kb = j_at * Q_PER_STEP
    diag_blocks = [(qi, first_kb + qi, True, None) for qi in range(Q_PER_STEP)]
    has_earlier = (j_at > 0).astype(F32)
    prev_blocks = [(qi, jnp.maximum(first_kb + qi - 1, 0), False, has_earlier if qi == 0 else None)
                   for qi in range(Q_PER_STEP)]
    runs, accs = key_blocks(diag_blocks + prev_blocks, runs, accs, items, finish=True)

    for qi in range(Q_PER_STEP):
        def sticks_left(runs):
            return jnp.min(functools.reduce(jnp.minimum, [runs[qi, hd] for hd in range(n_heads)])) < UNDERFLOW_LOG2

        def earlier_block(state, qi=qi, sticks_left=sticks_left):
            kb, _, runs, accs = state
            runs, accs = key_blocks([(qi, kb, False, None)], runs, accs)
            return kb - 1, sticks_left(runs), runs, accs

        first = first_kb + qi - 2
        mine = lambda d: {k: v for k, v in d.items() if k[0] == qi}

        @pl.when((first >= 0) & sticks_left(runs))
        def _(qi=qi, first=first, earlier_block=earlier_block, mine=mine):
            _, _, _, more = lax.while_loop(lambda st: (st[0] >= 0) & st[1], earlier_block,
                                           (first, True, mine(runs), mine(accs)))
            for p in range(n_pairs):
                gated_pair(more, qi, p)


def _fused_block(x, mod3, w, w_sp, b_sp_full, a_g, a_b, w_out, ln_g, ln_b, alpha):
    bsz, seq, d = x.shape
    d_a = a_g.shape[-1]
    d_b = (w.shape[1] - 3 * d_a) // 4
    per_seq = seq // ROWS
    n_total = bsz * per_seq
    grid = (n_total + 2,)

    def stage(lag):
        return lambda s: _stage_slabs(s, n_total, per_seq)[lag]

    def rows_of(lag):
        return pl.BlockSpec((1, ROWS, d), lambda s: (*stage(lag)(s), 0))

    def mod_of(lag):
        return pl.BlockSpec((1, 3, d), lambda s: (stage(lag)(s)[0], 0, 0))

    def whole(a):
        return pl.BlockSpec(a.shape, lambda s: (0,) * a.ndim)

    return pl.pallas_call(
        functools.partial(_block_kernel, alpha, n_total),
        grid=grid,
        in_specs=[rows_of(0), rows_of(2), mod_of(0), mod_of(2), whole(w), whole(w_sp),
                  whole(b_sp_full), whole(a_g), whole(a_b), whole(w_out), whole(ln_g), whole(ln_b)],
        out_specs=rows_of(2),
        out_shape=jax.ShapeDtypeStruct((bsz, seq, d), x.dtype),
        scratch_shapes=[
            pltpu.VMEM((2, seq, d_b), BF16),
            pltpu.VMEM((2, seq // TK, d_b, TK), BF16),
            pltpu.VMEM((2, ROWS, d_b), BF16),
            pltpu.VMEM((2, ROWS, d_b), F32),
            pltpu.VMEM((3, ROWS, d_a), BF16),
            pltpu.VMEM((2, ROWS, d_b), BF16),
            pltpu.VMEM((2, Q_PER_STEP * (d_b // GROUP_DIM), TK, TQ), F32),
            pltpu.VMEM((w.shape[1] // MXU_SLAB, ROWS, MXU_SLAB), F32),
            pltpu.VMEM((ROWS, d), F32),
        ],
        compiler_params=pltpu.CompilerParams(
            dimension_semantics=("arbitrary",), vmem_limit_bytes=VMEM_LIMIT),
        name="fused_block",
    )(x, x, mod3, mod3, w, w_sp, b_sp_full, a_g, a_b, w_out, ln_g, ln_b)


def kernel(x, c, w_ada, b_ada, w_in, sgu_ln_g, sgu_ln_b, w_spatial, b_spatial, w_out, ln_g, ln_b):
    depth = w_ada.shape[0]
    bsz, seq, d = x.shape
    d_a = sgu_ln_g.shape[-1]
    alpha = (2.0 * depth) ** 0.25
    for layer in range(depth):
        mod, w_in_bf, w_out_bf = _prepare(c, w_ada[layer], b_ada[layer], w_in[layer], w_out[layer])
        b_sp_full = jnp.repeat(b_spatial[layer].T, GROUP_DIM, axis=1)
        x = _fused_block(
            x, mod.reshape(bsz, 3, d), w_in_bf, w_spatial[layer], b_sp_full,
            sgu_ln_g[layer].reshape(1, d_a), sgu_ln_b[layer].reshape(1, d_a),
            w_out_bf, ln_g[layer].reshape(1, d), ln_b[layer].reshape(1, d), alpha)
    return x
```

```python
import functools
import math

import jax
import jax.numpy as jnp
from jax import lax
from jax.experimental import pallas as pl
from jax.experimental.pallas import tpu as pltpu

F32 = jnp.float32
BF16 = jnp.bfloat16

N_GROUPS = 8
GROUP_DIM = 64
CHUNK = 128
LN_EPS = 1e-5
LANES = 128

PREP_TILE = 512
TQ = 256
TK = 256
Q_PER_STEP = 1
ROWS = Q_PER_STEP * TQ
SKEW = (8, 3)
EXTRA_START = 2
MXU_SLAB = 512
FIN_DELAY = 1
UNDERFLOW_LOG2 = 152.0
VMEM_LIMIT = 48 * 1024 * 1024


def _dot(a, b):
    return jnp.dot(a, b, preferred_element_type=F32)


def _dot_nt(a, b):
    return lax.dot_general(a, b, (((1,), (1,)), ((), ())), preferred_element_type=F32)


def _gelu_tanh(x):
    k1 = -2.0 * math.sqrt(2.0 / math.pi) * math.log2(math.e)
    return x / (1.0 + jnp.exp2(x * (k1 + (k1 * 0.044715) * (x * x))))


def _silu(x):
    return x / (1.0 + jnp.exp(-x))


def _layer_norm(x, g, b):
    mu = jnp.mean(x, axis=-1, keepdims=True)
    xc = x - mu
    var = jnp.mean(xc * xc, axis=-1, keepdims=True)
    return xc * lax.rsqrt(var + LN_EPS) * g + b


def _prep_kernel(c_ref, wa_ref, ba_ref, win_ref, wout_ref, mod_ref, win_bf_ref, wout_bf_ref):
    sc = _silu(c_ref[...]).astype(BF16)
    mod_ref[...] = _dot(sc, wa_ref[...].astype(BF16)) + ba_ref[...]
    win_bf_ref[...] = win_ref[...].astype(BF16)
    wout_bf_ref[...] = wout_ref[...].astype(BF16)


def _prepare(c, w_ada, b_ada, w_in, w_out):
    bsz, d = c.shape
    n_mod = w_ada.shape[1]
    steps = w_in.shape[1] // PREP_TILE
    mod_tiles = n_mod // PREP_TILE
    out_tile = PREP_TILE
    out_tiles = w_out.shape[1] // out_tile
    assert mod_tiles <= steps and out_tiles <= steps
    mod_col = lambda j: (0, jnp.minimum(j, mod_tiles - 1))
    out_col = lambda j: (0, jnp.minimum(j, out_tiles - 1))
    return pl.pallas_call(
        _prep_kernel,
        grid=(steps,),
        in_specs=[
            pl.BlockSpec((bsz, d), lambda j: (0, 0)),
            pl.BlockSpec((d, PREP_TILE), mod_col),
            pl.BlockSpec((1, PREP_TILE), mod_col),
            pl.BlockSpec((w_in.shape[0], PREP_TILE), lambda j: (0, j)),
            pl.BlockSpec((w_out.shape[0], out_tile), out_col),
        ],
        out_specs=[
            pl.BlockSpec((bsz, PREP_TILE), mod_col),
            pl.BlockSpec((w_in.shape[0], PREP_TILE), lambda j: (0, j)),
            pl.BlockSpec((w_out.shape[0], out_tile), out_col),
        ],
        out_shape=[
            jax.ShapeDtypeStruct((bsz, n_mod), F32),
            jax.ShapeDtypeStruct(w_in.shape, BF16),
            jax.ShapeDtypeStruct(w_out.shape, BF16),
        ],
        compiler_params=pltpu.CompilerParams(
            dimension_semantics=("arbitrary",), vmem_limit_bytes=VMEM_LIMIT),
        name="adaln_mod_and_casts",
    )(c, w_ada, b_ada.reshape(1, n_mod), w_in, w_out)


def _stage_slabs(s, n_total, per_seq):
    last = n_total - 1
    out = []
    for lag in range(3):
        t = jnp.clip(s - lag, 0, last)
        out.append((lax.div(t, per_seq), lax.rem(t, per_seq)))
    return out


def _block_kernel(alpha, n_total,
                  x_in_ref, x_out_ref, mod_in_ref, mod_out_ref, w_ref, wsp_ref, bsp_ref,
                  ag_ref, ab_ref, wo_ref, g_ref, b_ref, o_ref,
                  k_scr, vT_scr, q_ring, gz_ring, ya_ring, yb_ring, z_ring, land_scr, y_scr):
    s = pl.program_id(0)
    d_b = q_ring.shape[-1]
    d_a = ya_ring.shape[-1]
    per_seq = k_scr.shape[1] // ROWS
    (b_in, j_in), (b_at, j_at), _ = _stage_slabs(s, n_total, per_seq)
    par_in, par_at = lax.rem(b_in, 2), lax.rem(b_at, 2)
    n_heads = d_b // GROUP_DIM
    heads_per_tile = LANES // GROUP_DIM
    n_pairs = n_heads // heads_per_tile
    now2, prev2 = lax.rem(s, 2), lax.rem(s + 1, 2)
    now3, prev3 = lax.rem(s, 3), lax.rem(s + 1, 3)

    @pl.when(s == 0)
    def _():
        q_ring[...] = jnp.zeros_like(q_ring)
        gz_ring[...] = jnp.zeros_like(gz_ring)
        ya_ring[...] = jnp.zeros_like(ya_ring)
        yb_ring[...] = jnp.zeros_like(yb_ring)
        z_ring[...] = jnp.zeros_like(z_ring)
        k_scr[0, 0:ROWS, :] = jnp.zeros((ROWS, d_b), BF16)
        for n in range(Q_PER_STEP):
            vT_scr[0, n] = jnp.zeros((d_b, TK), BF16)

    new = {}

    def item_h():
        shift = mod_in_ref[0, 0:1, :]
        scale = mod_in_ref[0, 1:2, :]
        new["h"] = (x_in_ref[0] * (1.0 + scale) + shift).astype(BF16)

    def h():
        return new["h"]

    def ycat():
        return jnp.concatenate([ya_ring[prev3], yb_ring[prev2]], axis=1)

    def both(*fs):
        return lambda: [f() for f in fs]

    def fin_out():
        gate = mod_out_ref[0, 2:3, :]
        r = alpha * x_out_ref[0] + gate * y_scr[...]
        o_ref[0] = _layer_norm(r, g_ref[...], b_ref[...])

    def fin_u(p):
        new["ug"] = _gelu_tanh(p())

    def fin_v(p):
        new["vn"] = _layer_norm(_gelu_tanh(p()), ag_ref[...], ab_ref[...]).astype(BF16)
        t_idx = lax.broadcasted_iota(jnp.int32, (CHUNK, CHUNK), 0)
        s_idx = lax.broadcasted_iota(jnp.int32, (CHUNK, CHUNK), 1)
        causal = t_idx >= s_idx
        w_sp = [jnp.where(causal, wsp_ref[g], 0.0).astype(BF16) for g in range(N_GROUPS)]
        first_group = lax.broadcasted_iota(jnp.int32, (CHUNK, LANES), 1) < GROUP_DIM
        chunk_rows = [slice(c * CHUNK, (c + 1) * CHUNK) for c in range(ROWS // CHUNK)]
        zero = jnp.zeros((CHUNK, LANES), BF16)
        pieces = [[] for _ in chunk_rows]
        for pr in range(d_a // LANES):
            cols = slice(pr * LANES, (pr + 1) * LANES)
            w_pair = jnp.concatenate([w_sp[2 * pr], w_sp[2 * pr + 1]], axis=1)
            vps = [new["vn"][rows, cols] for rows in chunk_rows]
            stacked = jnp.concatenate(
                [jnp.concatenate([jnp.where(first_group, vp, zero) for vp in vps], axis=1),
                 jnp.concatenate([jnp.where(first_group, zero, vp) for vp in vps], axis=1)], axis=0)
            mixed = _dot(w_pair, stacked)
            for c, rows in enumerate(chunk_rows):
                pieces[c].append(new["ug"][rows, cols] * (mixed[:, c * LANES:(c + 1) * LANES] + bsp_ref[:, cols]))
        new["ya_pre"] = jnp.concatenate([jnp.concatenate(p, axis=1) for p in pieces], axis=0)

    def fin_za(p):
        ya_ring[now3] = (_silu(p()) * new["ya_pre"]).astype(BF16)

    scale_q = math.log2(math.e) / math.sqrt(GROUP_DIM)

    def fin_q(p):
        new["q"] = (p() * scale_q).astype(BF16)
        q_ring[now2] = new["q"]

    def fin_k(p):
        new["k"] = p().astype(BF16)
        k_scr[par_in, pl.ds(pl.multiple_of(j_in * ROWS, ROWS), ROWS), :] = new["k"]

    def head_of(q_pair, hh):
        in_head = (lane >= hh * GROUP_DIM) & (lane < (hh + 1) * GROUP_DIM)
        return jnp.where(in_head, q_pair, jnp.zeros_like(q_pair))

    def pre_scores(heads):
        for qi in range(Q_PER_STEP):
            rows = slice(qi * TQ, (qi + 1) * TQ)
            for hd in heads:
                p, hh = divmod(hd, heads_per_tile)
                lanes = slice(p * LANES, (p + 1) * LANES)
                z_ring[now2, qi * n_heads + hd] = _dot_nt(
                    new["k"][rows, lanes], head_of(new["q"][rows, lanes], hh))

    def fin_vt(p):
        v = p()
        for n in range(Q_PER_STEP):
            vT_scr[par_in, j_in * Q_PER_STEP + n] = v[n * TK:(n + 1) * TK, :].T.astype(BF16)

    def fin_gz(p):
        gz_ring[now2] = _silu(p())

    dots, due = [], {}

    def out_slab(j):
        cols = slice(j * MXU_SLAB, (j + 1) * MXU_SLAB)

        def item():
            y_scr[:, cols] = _dot(ycat(), wo_ref[:, cols])
        return item

    for j in range(wo_ref.shape[1] // MXU_SLAB):
        dots.append(out_slab(j))
    due.setdefault(len(dots) - 1 + FIN_DELAY, []).append(fin_out)

    o = 3 * d_a
    projections = [(0, fin_u), (d_a, fin_v), (2 * d_a, fin_za), (o, fin_q), (o + d_b, fin_k),
                   (o + 2 * d_b, fin_vt), (o + 3 * d_b, fin_gz)]
    assert d_a == d_b == MXU_SLAB

    assert land_scr.shape[0] == len(projections)

    def in_slab(n, lo):
        def item():
            land_scr[n] = _dot(h(), w_ref[:, lo:lo + MXU_SLAB])
        return item

    def landed(n):
        return lambda: land_scr[n]

    for n, (lo, fin) in enumerate(projections):
        dots.append(in_slab(n, lo))
        due.setdefault(len(dots) - 1 + FIN_DELAY, []).append(functools.partial(fin, landed(n)))
    items = [item_h]
    for n, dot_item in enumerate(dots):
        items.append(both(dot_item, *due.pop(n, [])))
    items[-1] = both(items[-1], functools.partial(pre_scores, range(0, n_heads // 2)))
    items.append(both(*[f for n in sorted(due) for f in due[n]],
                      functools.partial(pre_scores, range(n_heads // 2, n_heads))))

    lane = lax.broadcasted_iota(jnp.int32, (TQ, LANES), 1)
    half = TK // 2
    s_idx = lax.broadcasted_iota(jnp.int32, (half, half), 0)
    t_idx = lax.broadcasted_iota(jnp.int32, (half, half), 1)
    strictly_earlier_half = s_idx < t_idx
    j_idx = lax.broadcasted_iota(jnp.int32, (TK, TK), 1)
    r_idx = lax.broadcasted_iota(jnp.int32, (TK, TK), 0)
    strictly_later = jnp.where(j_idx > r_idx, 1.0, 0.0).astype(BF16)

    q_masked = {}
    for qi in range(Q_PER_STEP):
        for hd in range(n_heads):
            p, hh = divmod(hd, heads_per_tile)
            q_pair = q_ring[prev2, qi * TQ:(qi + 1) * TQ, p * LANES:(p + 1) * LANES]
            q_masked[qi, hd] = head_of(q_pair, hh)

    lo_half, hi_half = slice(0, half), slice(half, TK)
    diag_parts = [(lo_half, lo_half, "tri"), (lo_half, hi_half, "full"),
                  (hi_half, lo_half, "empty"), (hi_half, hi_half, "tri")]
    full_parts = [(slice(0, TK), slice(0, TQ), "full")]

    def assemble(parts, pieces):
        if len(parts) == 1:
            return pieces[0]
        return jnp.concatenate([jnp.concatenate(pieces[0:2], axis=1),
                                jnp.concatenate(pieces[2:4], axis=1)], axis=0)

    def gated_pair(accs, qi, p):
        rows, cols = slice(qi * TQ, (qi + 1) * TQ), slice(p * LANES, (p + 1) * LANES)
        yT = jnp.concatenate([accs[qi, heads_per_tile * p + n] for n in range(heads_per_tile)], axis=0)
        yb_ring[now2, rows, cols] = (yT.T * gz_ring[prev2, rows, cols]).astype(BF16)

    def key_blocks(blocks, runs, accs, extra=(), finish=False):
        runs, accs = dict(runs), dict(accs)
        tiles = [dict(qi=qi, kb=kb, h=hd, keep=keep, diag=diag, parts=diag_parts if diag else full_parts)
                 for qi, kb, diag, keep in blocks for hd in range(n_heads)]
        last_tile = {(t["qi"], t["h"]): n for n, t in enumerate(tiles)}

        def scores(t):
            if t["diag"]:
                t["z"] = z_ring[prev2, t["qi"] * n_heads + t["h"]]
                return
            start = pl.multiple_of(t["kb"] * TK, TK)
            p = t["h"] // heads_per_tile
            lanes = slice(p * LANES, (p + 1) * LANES)
            t["z"] = _dot_nt(k_scr[par_at, pl.ds(start, TK), lanes], q_masked[t["qi"], t["h"]])

        def softplus(t):
            sps, t["logb"] = [], []
            for rows, cols, kind in t["parts"]:
                if kind == "empty":
                    sps.append(jnp.zeros((half, half), BF16))
                    t["logb"].append(None)
                    continue
                z = t["z"][rows, cols]
                sp = jnp.maximum(z, 0.0) + jnp.log2(1.0 + jnp.exp2(-jnp.abs(z)))
                t["logb"].append(z - sp)
                if kind == "tri":
                    sp = jnp.where(strictly_earlier_half, sp, 0.0)
                sps.append(sp.astype(BF16))
            t["sp"] = assemble(t["parts"], sps)

        def later_sum(t):
            t["csum"] = _dot(strictly_later, t["sp"])

        def weights(t):
            key = (t["qi"], t["h"])
            run = runs[key]
            probs = []
            for (rows, cols, kind), logb in zip(t["parts"], t["logb"]):
                if kind == "empty":
                    probs.append(jnp.zeros((half, half), BF16))
                    continue
                a = jnp.exp2(logb - (run[:, cols] + t["csum"][rows, cols]))
                if kind == "tri":
                    a = jnp.where(strictly_earlier_half, a, 0.0)
                probs.append(a.astype(BF16))
            t["prob"] = assemble(t["parts"], probs)
            runs[key] = run + (t["csum"][0:1, :] + t["sp"][0:1, :].astype(F32))

        def values(n, t):
            key = (t["qi"], t["h"])
            hd = t["h"]
            v_blk = vT_scr[par_at, t["kb"], hd * GROUP_DIM:(hd + 1) * GROUP_DIM, :]
            if t["keep"] is not None:
                v_blk = (v_blk.astype(F32) * t["keep"]).astype(BF16)
            accs[key] = accs[key] + _dot(v_blk, t["prob"])
            if finish and last_tile[key] == n and hd % heads_per_tile == heads_per_tile - 1:
                gated_pair(accs, t["qi"], hd // heads_per_tile)

        n_tiles = len(tiles)
        lead, lag = SKEW
        ready = next((n for n, t in enumerate(tiles) if not t["diag"]), n_tiles)
        for t in tiles[:ready]:
            scores(t)
        steps = range(min(0, ready - lead), n_tiles + lag)
        extra = list(extra)
        slots = [EXTRA_START + (n * (len(steps) - EXTRA_START)) // max(1, len(extra))
                 for n in range(len(extra))]
        for count, step in enumerate(steps):
            while extra and slots[0] <= count:
                slots.pop(0)
                extra.pop(0)()
            if ready <= step + lead < n_tiles:
                scores(tiles[step + lead])
            if 0 <= step < n_tiles:
                softplus(tiles[step])
                later_sum(tiles[step])
            if 0 <= step - lag < n_tiles:
                weights(tiles[step - lag])
                values(step - lag, tiles[step - lag])
        for item in extra:
            item()
        return runs, accs

    runs = {(qi, hd): jnp.zeros((1, TQ), F32) for qi in range(Q_PER_STEP) for hd in range(n_heads)}
    accs = {(qi, hd): jnp.zeros((GROUP_DIM, TQ), F32) for qi in range(Q_PER_STEP) for hd in range(n_heads)}
    first_kb = j_at * Q_PER_STEP
    diag_blocks = [(qi, first_kb + qi, True, None) for qi in range(Q_PER_STEP)]
    has_earlier = (j_at > 0).astype(F32)
    prev_blocks = [(qi, jnp.maximum(first_kb + qi - 1, 0), False, has_earlier if qi == 0 else None)
                   for qi in range(Q_PER_STEP)]
    runs, accs = key_blocks(diag_blocks + prev_blocks, runs, accs, items, finish=True)

    for qi in range(Q_PER_STEP):
        def sticks_left(runs):
            return jnp.min(functools.reduce(jnp.minimum, [runs[qi, hd] for hd in range(n_heads)])) < UNDERFLOW_LOG2

        def earlier_block(state, qi=qi, sticks_left=sticks_left):
            kb, _, runs, accs = state
            runs, accs = key_blocks([(qi, kb, False, None)], runs, accs)
            return kb - 1, sticks_left(runs), runs, accs

        first = first_kb + qi - 2
        mine = lambda d: {k: v for k, v in d.items() if k[0] == qi}

        @pl.when((first >= 0) & sticks_left(runs))
        def _(qi=qi, first=first, earlier_block=earlier_block, mine=mine):
            _, _, _, more = lax.while_loop(lambda st: (st[0] >= 0) & st[1], earlier_block,
                                           (first, True, mine(runs), mine(accs)))
            for p in range(n_pairs):
                gated_pair(more, qi, p)


def _fused_block(x, mod3, w, w_sp, b_sp_full, a_g, a_b, w_out, ln_g, ln_b, alpha):
    bsz, seq, d = x.shape
    d_a = a_g.shape[-1]
    d_b = (w.shape[1] - 3 * d_a) // 4
    per_seq = seq // ROWS
    n_total = bsz * per_seq
    grid = (n_total + 2,)

    def stage(lag):
        return lambda s: _stage_slabs(s, n_total, per_seq)[lag]

    def rows_of(lag):
        return pl.BlockSpec((1, ROWS, d), lambda s: (*stage(lag)(s), 0))

    def mod_of(lag):
        return pl.BlockSpec((1, 3, d), lambda s: (stage(lag)(s)[0], 0, 0))

    def whole(a):
        return pl.BlockSpec(a.shape, lambda s: (0,) * a.ndim)

    return pl.pallas_call(
        functools.partial(_block_kernel, alpha, n_total),
        grid=grid,
        in_specs=[rows_of(0), rows_of(2), mod_of(0), mod_of(2), whole(w), whole(w_sp),
                  whole(b_sp_full), whole(a_g), whole(a_b), whole(w_out), whole(ln_g), whole(ln_b)],
        out_specs=rows_of(2),
        out_shape=jax.ShapeDtypeStruct((bsz, seq, d), x.dtype),
        scratch_shapes=[
            pltpu.VMEM((2, seq, d_b), BF16),
            pltpu.VMEM((2, seq // TK, d_b, TK), BF16),
            pltpu.VMEM((2, ROWS, d_b), BF16),
            pltpu.VMEM((2, ROWS, d_b), F32),
            pltpu.VMEM((3, ROWS, d_a), BF16),
            pltpu.VMEM((2, ROWS, d_b), BF16),
            pltpu.VMEM((2, Q_PER_STEP * (d_b // GROUP_DIM), TK, TQ), F32),
            pltpu.VMEM((w.shape[1] // MXU_SLAB, ROWS, MXU_SLAB), F32),
            pltpu.VMEM((ROWS, d), F32),
        ],
        compiler_params=pltpu.CompilerParams(
            dimension_semantics=("arbitrary",), vmem_limit_bytes=VMEM_LIMIT),
        name="fused_block",
    )(x, x, mod3, mod3, w, w_sp, b_sp_full, a_g, a_b, w_out, ln_g, ln_b)


def kernel(x, c, w_ada, b_ada, w_in, sgu_ln_g, sgu_ln_b, w_spatial, b_spatial, w_out, ln_g, ln_b):
    depth = w_ada.shape[0]
    bsz, seq, d = x.shape
    d_a = sgu_ln_g.shape[-1]
    alpha = (2.0 * depth) ** 0.25
    for layer in range(depth):
        mod, w_in_bf, w_out_bf = _prepare(c, w_ada[layer], b_ada[layer], w_in[layer], w_out[layer])
        b_sp_full = jnp.repeat(b_spatial[layer].T, GROUP_DIM, axis=1)
        x = _fused_block(
            x, mod.reshape(bsz, 3, d), w_in_bf, w_spatial[layer], b_sp_full,
            sgu_ln_g[layer].reshape(1, d_a), sgu_ln_b[layer].reshape(1, d_a),
            w_out_bf, ln_g[layer].reshape(1, d), ln_b[layer].reshape(1, d), alpha)
    return x
```

```python
import functools
import math

import jax
import jax.numpy as jnp
from jax import lax
from jax.experimental import pallas as pl
from jax.experimental.pallas import tpu as pltpu

F32 = jnp.float32
BF16 = jnp.bfloat16

N_GROUPS = 8
GROUP_DIM = 64
CHUNK = 128
LN_EPS = 1e-5
LANES = 128

PREP_TILE = 512
TQ = 256
TK = 256
Q_PER_STEP = 1
ROWS = Q_PER_STEP * TQ
SKEW = (8, 3)
EXTRA_START = 2
MXU_SLAB = 512
FIN_DELAY = 1
UNDERFLOW_LOG2 = 152.0
VMEM_LIMIT = 48 * 1024 * 1024


def _dot(a, b):
    return jnp.dot(a, b, preferred_element_type=F32)


def _dot_nt(a, b):
    return lax.dot_general(a, b, (((1,), (1,)), ((), ())), preferred_element_type=F32)


def _gelu_tanh(x):
    k1 = -2.0 * math.sqrt(2.0 / math.pi) * math.log2(math.e)
    return x / (1.0 + jnp.exp2(x * (k1 + (k1 * 0.044715) * (x * x))))


def _silu(x):
    return x / (1.0 + jnp.exp(-x))


def _layer_norm(x, g, b):
    mu = jnp.mean(x, axis=-1, keepdims=True)
    xc = x - mu
    var = jnp.mean(xc * xc, axis=-1, keepdims=True)
    return xc * lax.rsqrt(var + LN_EPS) * g + b


def _prep_kernel(c_ref, wa_ref, ba_ref, win_ref, wout_ref, mod_ref, win_bf_ref, wout_bf_ref):
    sc = _silu(c_ref[...]).astype(BF16)
    mod_ref[...] = _dot(sc, wa_ref[...].astype(BF16)) + ba_ref[...]
    win_bf_ref[...] = win_ref[...].astype(BF16)
    wout_bf_ref[...] = wout_ref[...].astype(BF16)


def _prepare(c, w_ada, b_ada, w_in, w_out):
    bsz, d = c.shape
    n_mod = w_ada.shape[1]
    steps = w_in.shape[1] // PREP_TILE
    mod_tiles = n_mod // PREP_TILE
    out_tile = PREP_TILE
    out_tiles = w_out.shape[1] // out_tile
    assert mod_tiles <= steps and out_tiles <= steps
    mod_col = lambda j: (0, jnp.minimum(j, mod_tiles - 1))
    out_col = lambda j: (0, jnp.minimum(j, out_tiles - 1))
    return pl.pallas_call(
        _prep_kernel,
        grid=(steps,),
        in_specs=[
            pl.BlockSpec((bsz, d), lambda j: (0, 0)),
            pl.BlockSpec((d, PREP_TILE), mod_col),
            pl.BlockSpec((1, PREP_TILE), mod_col),
            pl.BlockSpec((w_in.shape[0], PREP_TILE), lambda j: (0, j)),
            pl.BlockSpec((w_out.shape[0], out_tile), out_col),
        ],
        out_specs=[
            pl.BlockSpec((bsz, PREP_TILE), mod_col),
            pl.BlockSpec((w_in.shape[0], PREP_TILE), lambda j: (0, j)),
            pl.BlockSpec((w_out.shape[0], out_tile), out_col),
        ],
        out_shape=[
            jax.ShapeDtypeStruct((bsz, n_mod), F32),
            jax.ShapeDtypeStruct(w_in.shape, BF16),
            jax.ShapeDtypeStruct(w_out.shape, BF16),
        ],
        compiler_params=pltpu.CompilerParams(
            dimension_semantics=("arbitrary",), vmem_limit_bytes=VMEM_LIMIT),
        name="adaln_mod_and_casts",
    )(c, w_ada, b_ada.reshape(1, n_mod), w_in, w_out)


def _stage_slabs(s, n_total, per_seq):
    last = n_total - 1
    out = []
    for lag in range(3):
        t = jnp.clip(s - lag, 0, last)
        out.append((lax.div(t, per_seq), lax.rem(t, per_seq)))
    return out


def _block_kernel(alpha, n_total,
                  x_in_ref, x_out_ref, mod_in_ref, mod_out_ref, w_ref, wsp_ref, bsp_ref,
                  ag_ref, ab_ref, wo_ref, g_ref, b_ref, o_ref,
                  k_scr, vT_scr, q_ring, gz_ring, ya_ring, yb_ring, z_ring, land_scr, y_scr):
    s = pl.program_id(0)
    d_b = q_ring.shape[-1]
    d_a = ya_ring.shape[-1]
    per_seq = k_scr.shape[1] // ROWS
    (b_in, j_in), (b_at, j_at), _ = _stage_slabs(s, n_total, per_seq)
    par_in, par_at = lax.rem(b_in, 2), lax.rem(b_at, 2)
    n_heads = d_b // GROUP_DIM
    heads_per_tile = LANES // GROUP_DIM
    n_pairs = n_heads // heads_per_tile
    now2, prev2 = lax.rem(s, 2), lax.rem(s + 1, 2)
    now3, prev3 = lax.rem(s, 3), lax.rem(s + 1, 3)

    @pl.when(s == 0)
    def _():
        q_ring[...] = jnp.zeros_like(q_ring)
        gz_ring[...] = jnp.zeros_like(gz_ring)
        ya_ring[...] = jnp.zeros_like(ya_ring)
        yb_ring[...] = jnp.zeros_like(yb_ring)
        z_ring[...] = jnp.zeros_like(z_ring)
        k_scr[0, 0:ROWS, :] = jnp.zeros((ROWS, d_b), BF16)
        for n in range(Q_PER_STEP):
            vT_scr[0, n] = jnp.zeros((d_b, TK), BF16)

    new = {}

    def item_h():
        shift = mod_in_ref[0, 0:1, :]
        scale = mod_in_ref[0, 1:2, :]
        new["h"] = (x_in_ref[0] * (1.0 + scale) + shift).astype(BF16)

    def h():
        return new["h"]

    def ycat():
        return jnp.concatenate([ya_ring[prev3], yb_ring[prev2]], axis=1)

    def both(*fs):
        return lambda: [f() for f in fs]

    def fin_out():
        gate = mod_out_ref[0, 2:3, :]
        r = alpha * x_out_ref[0] + gate * y_scr[...]
        o_ref[0] = _layer_norm(r, g_ref[...], b_ref[...])

    def fin_u(p):
        new["ug"] = _gelu_tanh(p())

    def fin_v(p):
        new["vn"] = _layer_norm(_gelu_tanh(p()), ag_ref[...], ab_ref[...]).astype(BF16)
        t_idx = lax.broadcasted_iota(jnp.int32, (CHUNK, CHUNK), 0)
        s_idx = lax.broadcasted_iota(jnp.int32, (CHUNK, CHUNK), 1)
        causal = t_idx >= s_idx
        w_sp = [jnp.where(causal, wsp_ref[g], 0.0).astype(BF16) for g in range(N_GROUPS)]
        first_group = lax.broadcasted_iota(jnp.int32, (CHUNK, LANES), 1) < GROUP_DIM
        chunk_rows = [slice(c * CHUNK, (c + 1) * CHUNK) for c in range(ROWS // CHUNK)]
        zero = jnp.zeros((CHUNK, LANES), BF16)
        pieces = [[] for _ in chunk_rows]
        for pr in range(d_a // LANES):
            cols = slice(pr * LANES, (pr + 1) * LANES)
            w_pair = jnp.concatenate([w_sp[2 * pr], w_sp[2 * pr + 1]], axis=1)
            vps = [new["vn"][rows, cols] for rows in chunk_rows]
            stacked = jnp.concatenate(
                [jnp.concatenate([jnp.where(first_group, vp, zero) for vp in vps], axis=1),
                 jnp.concatenate([jnp.where(first_group, zero, vp) for vp in vps], axis=1)], axis=0)
            mixed = _dot(w_pair, stacked)
            for c, rows in enumerate(chunk_rows):
                pieces[c].append(new["ug"][rows, cols] * (mixed[:, c * LANES:(c + 1) * LANES] + bsp_ref[:, cols]))
        new["ya_pre"] = jnp.concatenate([jnp.concatenate(p, axis=1) for p in pieces], axis=0)

    def fin_za(p):
        ya_ring[now3] = (_silu(p()) * new["ya_pre"]).astype(BF16)

    scale_q = math.log2(math.e) / math.sqrt(GROUP_DIM)

    def fin_q(p):
        new["q"] = (p() * scale_q).astype(BF16)
        q_ring[now2] = new["q"]

    def fin_k(p):
        new["k"] = p().astype(BF16)
        k_scr[par_in, pl.ds(pl.multiple_of(j_in * ROWS, ROWS), ROWS), :] = new["k"]

    def head_of(q_pair, hh):
        in_head = (lane >= hh * GROUP_DIM) & (lane < (hh + 1) * GROUP_DIM)
        return jnp.where(in_head, q_pair, jnp.zeros_like(q_pair))

    def pre_scores():
        for qi in range(Q_PER_STEP):
            rows = slice(qi * TQ, (qi + 1) * TQ)
            for hd in range(n_heads):
                p, hh = divmod(hd, heads_per_tile)
                lanes = slice(p * LANES, (p + 1) * LANES)
                z_ring[now2, qi * n_heads + hd] = _dot_nt(
                    new["k"][rows, lanes], head_of(new["q"][rows, lanes], hh))

    def fin_vt(p):
        v = p()
        for n in range(Q_PER_STEP):
            vT_scr[par_in, j_in * Q_PER_STEP + n] = v[n * TK:(n + 1) * TK, :].T.astype(BF16)

    def fin_gz(p):
        gz_ring[now2] = _silu(p())

    dots, due = [], {}

    def out_slab(j):
        cols = slice(j * MXU_SLAB, (j + 1) * MXU_SLAB)

        def item():
            y_scr[:, cols] = _dot(ycat(), wo_ref[:, cols])
        return item

    for j in range(wo_ref.shape[1] // MXU_SLAB):
        dots.append(out_slab(j))
    due.setdefault(len(dots) - 1 + FIN_DELAY, []).append(fin_out)

    o = 3 * d_a
    projections = [(0, fin_u), (d_a, fin_v), (2 * d_a, fin_za), (o, fin_q), (o + d_b, fin_k),
                   (o + 2 * d_b, fin_vt), (o + 3 * d_b, fin_gz)]
    assert d_a == d_b == MXU_SLAB

    assert land_scr.shape[0] == len(projections)

    def in_slab(n, lo):
        def item():
            land_scr[n] = _dot(h(), w_ref[:, lo:lo + MXU_SLAB])
        return item

    def landed(n):
        return lambda: land_scr[n]

    for n, (lo, fin) in enumerate(projections):
        dots.append(in_slab(n, lo))
        due.setdefault(len(dots) - 1 + FIN_DELAY, []).append(functools.partial(fin, landed(n)))
    items = [item_h]
    for n, dot_item in enumerate(dots):
        items.append(both(dot_item, *due.pop(n, [])))
    items.append(both(*[f for n in sorted(due) for f in due[n]], pre_scores))

    lane = lax.broadcasted_iota(jnp.int32, (TQ, LANES), 1)
    half = TK // 2
    s_idx = lax.broadcasted_iota(jnp.int32, (half, half), 0)
    t_idx = lax.broadcasted_iota(jnp.int32, (half, half), 1)
    strictly_earlier_half = s_idx < t_idx
    j_idx = lax.broadcasted_iota(jnp.int32, (TK, TK), 1)
    r_idx = lax.broadcasted_iota(jnp.int32, (TK, TK), 0)
    strictly_later = jnp.where(j_idx > r_idx, 1.0, 0.0).astype(BF16)

    q_masked = {}
    for qi in range(Q_PER_STEP):
        for hd in range(n_heads):
            p, hh = divmod(hd, heads_per_tile)
            q_pair = q_ring[prev2, qi * TQ:(qi + 1) * TQ, p * LANES:(p + 1) * LANES]
            q_masked[qi, hd] = head_of(q_pair, hh)

    lo_half, hi_half = slice(0, half), slice(half, TK)
    diag_parts = [(lo_half, lo_half, "tri"), (lo_half, hi_half, "full"),
                  (hi_half, lo_half, "empty"), (hi_half, hi_half, "tri")]
    full_parts = [(slice(0, TK), slice(0, TQ), "full")]

    def assemble(parts, pieces):
        if len(parts) == 1:
            return pieces[0]
        return jnp.concatenate([jnp.concatenate(pieces[0:2], axis=1),
                                jnp.concatenate(pieces[2:4], axis=1)], axis=0)

    def gated_pair(accs, qi, p):
        rows, cols = slice(qi * TQ, (qi + 1) * TQ), slice(p * LANES, (p + 1) * LANES)
        yT = jnp.concatenate([accs[qi, heads_per_tile * p + n] for n in range(heads_per_tile)], axis=0)
        yb_ring[now2, rows, cols] = (yT.T * gz_ring[prev2, rows, cols]).astype(BF16)

    def key_blocks(blocks, runs, accs, extra=(), finish=False):
        runs, accs = dict(runs), dict(accs)
        tiles = [dict(qi=qi, kb=kb, h=hd, keep=keep, diag=diag, parts=diag_parts if diag else full_parts)
                 for qi, kb, diag, keep in blocks for hd in range(n_heads)]
        last_tile = {(t["qi"], t["h"]): n for n, t in enumerate(tiles)}

        def scores(t):
            if t["diag"]:
                t["z"] = z_ring[prev2, t["qi"] * n_heads + t["h"]]
                return
            start = pl.multiple_of(t["kb"] * TK, TK)
            p = t["h"] // heads_per_tile
            lanes = slice(p * LANES, (p + 1) * LANES)
            t["z"] = _dot_nt(k_scr[par_at, pl.ds(start, TK), lanes], q_masked[t["qi"], t["h"]])

        def softplus(t):
            sps, t["logb"] = [], []
            for rows, cols, kind in t["parts"]:
                if kind == "empty":
                    sps.append(jnp.zeros((half, half), BF16))
                    t["logb"].append(None)
                    continue
                z = t["z"][rows, cols]
                sp = jnp.maximum(z, 0.0) + jnp.log2(1.0 + jnp.exp2(-jnp.abs(z)))
                t["logb"].append(z - sp)
                if kind == "tri":
                    sp = jnp.where(strictly_earlier_half, sp, 0.0)
                sps.append(sp.astype(BF16))
            t["sp"] = assemble(t["parts"], sps)

        def later_sum(t):
            t["csum"] = _dot(strictly_later, t["sp"])

        def weights(t):
            key = (t["qi"], t["h"])
            run = runs[key]
            probs = []
            for (rows, cols, kind), logb in zip(t["parts"], t["logb"]):
                if kind == "empty":
                    probs.append(jnp.zeros((half, half), BF16))
                    continue
                a = jnp.exp2(logb - (run[:, cols] + t["csum"][rows, cols]))
                if kind == "tri":
                    a = jnp.where(strictly_earlier_half, a, 0.0)
                probs.append(a.astype(BF16))
            t["prob"] = assemble(t["parts"], probs)
            runs[key] = run + (t["csum"][0:1, :] + t["sp"][0:1, :].astype(F32))

        def values(n, t):
            key = (t["qi"], t["h"])
            hd = t["h"]
            v_blk = vT_scr[par_at, t["kb"], hd * GROUP_DIM:(hd + 1) * GROUP_DIM, :]
            if t["keep"] is not None:
                v_blk = jnp.where(t["keep"], v_blk, jnp.zeros_like(v_blk))
            accs[key] = accs[key] + _dot(v_blk, t["prob"])
            if finish and last_tile[key] == n and hd % heads_per_tile == heads_per_tile - 1:
                gated_pair(accs, t["qi"], hd // heads_per_tile)

        n_tiles = len(tiles)
        lead, lag = SKEW
        ready = next((n for n, t in enumerate(tiles) if not t["diag"]), n_tiles)
        for t in tiles[:ready]:
            scores(t)
        steps = range(min(0, ready - lead), n_tiles + lag)
        extra = list(extra)
        slots = [EXTRA_START + (n * (len(steps) - EXTRA_START)) // max(1, len(extra))
                 for n in range(len(extra))]
        for count, step in enumerate(steps):
            while extra and slots[0] <= count:
                slots.pop(0)
                extra.pop(0)()
            if ready <= step + lead < n_tiles:
                scores(tiles[step + lead])
            if 0 <= step < n_tiles:
                softplus(tiles[step])
                later_sum(tiles[step])
            if 0 <= step - lag < n_tiles:
                weights(tiles[step - lag])
                values(step - lag, tiles[step - lag])
        for item in extra:
            item()
        return runs, accs

    runs = {(qi, hd): jnp.zeros((1, TQ), F32) for qi in range(Q_PER_STEP) for hd in range(n_heads)}
    accs = {(qi, hd): jnp.zeros((GROUP_DIM, TQ), F32) for qi in range(Q_PER_STEP) for hd in range(n_heads)}
    first_kb = j_at * Q_PER_STEP
    diag_blocks = [(qi, first_kb + qi, True, None) for qi in range(Q_PER_STEP)]
    has_earlier = j_at > 0
    prev_blocks = [(qi, jnp.maximum(first_kb + qi - 1, 0), False, has_earlier if qi == 0 else None)
                   for qi in range(Q_PER_STEP)]
    runs, accs = key_blocks(diag_blocks + prev_blocks, runs, accs, items, finish=True)

    for qi in range(Q_PER_STEP):
        def sticks_left(runs):
            return jnp.min(functools.reduce(jnp.minimum, [runs[qi, hd] for hd in range(n_heads)])) < UNDERFLOW_LOG2

        def earlier_block(state, qi=qi, sticks_left=sticks_left):
            kb, _, runs, accs = state
            runs, accs = key_blocks([(qi, kb, False, None)], runs, accs)
            return kb - 1, sticks_left(runs), runs, accs

        first = first_kb + qi - 2
        mine = lambda d: {k: v for k, v in d.items() if k[0] == qi}

        @pl.when((first >= 0) & sticks_left(runs))
        def _(qi=qi, first=first, earlier_block=earlier_block, mine=mine):
            _, _, _, more = lax.while_loop(lambda st: (st[0] >= 0) & st[1], earlier_block,
                                           (first, True, mine(runs), mine(accs)))
            for p in range(n_pairs):
                gated_pair(more, qi, p)


def _fused_block(x, mod3, w, w_sp, b_sp_full, a_g, a_b, w_out, ln_g, ln_b, alpha):
    bsz, seq, d = x.shape
    d_a = a_g.shape[-1]
    d_b = (w.shape[1] - 3 * d_a) // 4
    per_seq = seq // ROWS
    n_total = bsz * per_seq
    grid = (n_total + 2,)

    def stage(lag):
        return lambda s: _stage_slabs(s, n_total, per_seq)[lag]

    def rows_of(lag):
        return pl.BlockSpec((1, ROWS, d), lambda s: (*stage(lag)(s), 0))

    def mod_of(lag):
        return pl.BlockSpec((1, 3, d), lambda s: (stage(lag)(s)[0], 0, 0))

    def whole(a):
        return pl.BlockSpec(a.shape, lambda s: (0,) * a.ndim)

    return pl.pallas_call(
        functools.partial(_block_kernel, alpha, n_total),
        grid=grid,
        in_specs=[rows_of(0), rows_of(2), mod_of(0), mod_of(2), whole(w), whole(w_sp),
                  whole(b_sp_full), whole(a_g), whole(a_b), whole(w_out), whole(ln_g), whole(ln_b)],
        out_specs=rows_of(2),
        out_shape=jax.ShapeDtypeStruct((bsz, seq, d), x.dtype),
        scratch_shapes=[
            pltpu.VMEM((2, seq, d_b), BF16),
            pltpu.VMEM((2, seq // TK, d_b, TK), BF16),
            pltpu.VMEM((2, ROWS, d_b), BF16),
            pltpu.VMEM((2, ROWS, d_b), F32),
            pltpu.VMEM((3, ROWS, d_a), BF16),
            pltpu.VMEM((2, ROWS, d_b), BF16),
            pltpu.VMEM((2, Q_PER_STEP * (d_b // GROUP_DIM), TK, TQ), F32),
            pltpu.VMEM((w.shape[1] // MXU_SLAB, ROWS, MXU_SLAB), F32),
            pltpu.VMEM((ROWS, d), F32),
        ],
        compiler_params=pltpu.CompilerParams(
            dimension_semantics=("arbitrary",), vmem_limit_bytes=VMEM_LIMIT),
        name="fused_block",
    )(x, x, mod3, mod3, w, w_sp, b_sp_full, a_g, a_b, w_out, ln_g, ln_b)


def kernel(x, c, w_ada, b_ada, w_in, sgu_ln_g, sgu_ln_b, w_spatial, b_spatial, w_out, ln_g, ln_b):
    depth = w_ada.shape[0]
    bsz, seq, d = x.shape
    d_a = sgu_ln_g.shape[-1]
    alpha = (2.0 * depth) ** 0.25
    for layer in range(depth):
        mod, w_in_bf, w_out_bf = _prepare(c, w_ada[layer], b_ada[layer], w_in[layer], w_out[layer])
        b_sp_full = jnp.repeat(b_spatial[layer].T, GROUP_DIM, axis=1)
        x = _fused_block(
            x, mod.reshape(bsz, 3, d), w_in_bf, w_spatial[layer], b_sp_full,
            sgu_ln_g[layer].reshape(1, d_a), sgu_ln_b[layer].reshape(1, d_a),
            w_out_bf, ln_g[layer].reshape(1, d), ln_b[layer].reshape(1, d), alpha)
    return x
```

```python
import functools
import math

import jax
import jax.numpy as jnp
from jax import lax
from jax.experimental import pallas as pl
from jax.experimental.pallas import tpu as pltpu

F32 = jnp.float32
BF16 = jnp.bfloat16

N_GROUPS = 8
GROUP_DIM = 64
CHUNK = 128
LN_EPS = 1e-5
LANES = 128

PREP_TILE = 512
TQ = 256
TK = 256
Q_PER_STEP = 1
ROWS = Q_PER_STEP * TQ
SKEW = (8, 3)
EXTRA_START = 3
MXU_SLAB = 512
FIN_DELAY = 1
UNDERFLOW_LOG2 = 152.0
VMEM_LIMIT = 48 * 1024 * 1024


def _dot(a, b):
    return jnp.dot(a, b, preferred_element_type=F32)


def _dot_nt(a, b):
    return lax.dot_general(a, b, (((1,), (1,)), ((), ())), preferred_element_type=F32)


def _gelu_tanh(x):
    k1 = -2.0 * math.sqrt(2.0 / math.pi) * math.log2(math.e)
    return x / (1.0 + jnp.exp2(x * (k1 + (k1 * 0.044715) * (x * x))))


def _silu(x):
    return x / (1.0 + jnp.exp(-x))


def _layer_norm(x, g, b):
    mu = jnp.mean(x, axis=-1, keepdims=True)
    xc = x - mu
    var = jnp.mean(xc * xc, axis=-1, keepdims=True)
    return xc * lax.rsqrt(var + LN_EPS) * g + b


def _prep_kernel(c_ref, wa_ref, ba_ref, win_ref, wout_ref, mod_ref, win_bf_ref, wout_bf_ref):
    sc = _silu(c_ref[...]).astype(BF16)
    mod_ref[...] = _dot(sc, wa_ref[...].astype(BF16)) + ba_ref[...]
    win_bf_ref[...] = win_ref[...].astype(BF16)
    wout_bf_ref[...] = wout_ref[...].astype(BF16)


def _prepare(c, w_ada, b_ada, w_in, w_out):
    bsz, d = c.shape
    n_mod = w_ada.shape[1]
    steps = w_in.shape[1] // PREP_TILE
    mod_tiles = n_mod // PREP_TILE
    out_tile = PREP_TILE
    out_tiles = w_out.shape[1] // out_tile
    assert mod_tiles <= steps and out_tiles <= steps
    mod_col = lambda j: (0, jnp.minimum(j, mod_tiles - 1))
    out_col = lambda j: (0, jnp.minimum(j, out_tiles - 1))
    return pl.pallas_call(
        _prep_kernel,
        grid=(steps,),
        in_specs=[
            pl.BlockSpec((bsz, d), lambda j: (0, 0)),
            pl.BlockSpec((d, PREP_TILE), mod_col),
            pl.BlockSpec((1, PREP_TILE), mod_col),
            pl.BlockSpec((w_in.shape[0], PREP_TILE), lambda j: (0, j)),
            pl.BlockSpec((w_out.shape[0], out_tile), out_col),
        ],
        out_specs=[
            pl.BlockSpec((bsz, PREP_TILE), mod_col),
            pl.BlockSpec((w_in.shape[0], PREP_TILE), lambda j: (0, j)),
            pl.BlockSpec((w_out.shape[0], out_tile), out_col),
        ],
        out_shape=[
            jax.ShapeDtypeStruct((bsz, n_mod), F32),
            jax.ShapeDtypeStruct(w_in.shape, BF16),
            jax.ShapeDtypeStruct(w_out.shape, BF16),
        ],
        compiler_params=pltpu.CompilerParams(
            dimension_semantics=("arbitrary",), vmem_limit_bytes=VMEM_LIMIT),
        name="adaln_mod_and_casts",
    )(c, w_ada, b_ada.reshape(1, n_mod), w_in, w_out)


def _stage_slabs(s, n_total, per_seq):
    last = n_total - 1
    out = []
    for lag in range(3):
        t = jnp.clip(s - lag, 0, last)
        out.append((lax.div(t, per_seq), lax.rem(t, per_seq)))
    return out


def _block_kernel(alpha, n_total,
                  x_in_ref, x_out_ref, mod_in_ref, mod_out_ref, w_ref, wsp_ref, bsp_ref,
                  ag_ref, ab_ref, wo_ref, g_ref, b_ref, o_ref,
                  k_scr, vT_scr, q_ring, gz_ring, ya_ring, yb_ring, z_ring, land_scr, y_scr):
    s = pl.program_id(0)
    d_b = q_ring.shape[-1]
    d_a = ya_ring.shape[-1]
    per_seq = k_scr.shape[1] // ROWS
    (b_in, j_in), (b_at, j_at), _ = _stage_slabs(s, n_total, per_seq)
    par_in, par_at = lax.rem(b_in, 2), lax.rem(b_at, 2)
    n_heads = d_b // GROUP_DIM
    heads_per_tile = LANES // GROUP_DIM
    n_pairs = n_heads // heads_per_tile
    now2, prev2 = lax.rem(s, 2), lax.rem(s + 1, 2)
    now3, prev3 = lax.rem(s, 3), lax.rem(s + 1, 3)

    @pl.when(s == 0)
    def _():
        q_ring[...] = jnp.zeros_like(q_ring)
        gz_ring[...] = jnp.zeros_like(gz_ring)
        ya_ring[...] = jnp.zeros_like(ya_ring)
        yb_ring[...] = jnp.zeros_like(yb_ring)
        z_ring[...] = jnp.zeros_like(z_ring)
        k_scr[0, 0:ROWS, :] = jnp.zeros((ROWS, d_b), BF16)
        for n in range(Q_PER_STEP):
            vT_scr[0, n] = jnp.zeros((d_b, TK), BF16)

    new = {}

    def item_h():
        shift = mod_in_ref[0, 0:1, :]
        scale = mod_in_ref[0, 1:2, :]
        new["h"] = (x_in_ref[0] * (1.0 + scale) + shift).astype(BF16)

    def h():
        return new["h"]

    def ycat():
        return jnp.concatenate([ya_ring[prev3], yb_ring[prev2]], axis=1)

    def both(*fs):
        return lambda: [f() for f in fs]

    def fin_out():
        gate = mod_out_ref[0, 2:3, :]
        r = alpha * x_out_ref[0] + gate * y_scr[...]
        o_ref[0] = _layer_norm(r, g_ref[...], b_ref[...])

    def fin_u(p):
        new["ug"] = _gelu_tanh(p())

    def fin_v(p):
        new["vn"] = _layer_norm(_gelu_tanh(p()), ag_ref[...], ab_ref[...]).astype(BF16)
        t_idx = lax.broadcasted_iota(jnp.int32, (CHUNK, CHUNK), 0)
        s_idx = lax.broadcasted_iota(jnp.int32, (CHUNK, CHUNK), 1)
        causal = t_idx >= s_idx
        w_sp = [jnp.where(causal, wsp_ref[g], 0.0).astype(BF16) for g in range(N_GROUPS)]
        first_group = lax.broadcasted_iota(jnp.int32, (CHUNK, LANES), 1) < GROUP_DIM
        chunk_rows = [slice(c * CHUNK, (c + 1) * CHUNK) for c in range(ROWS // CHUNK)]
        zero = jnp.zeros((CHUNK, LANES), BF16)
        pieces = [[] for _ in chunk_rows]
        for pr in range(d_a // LANES):
            cols = slice(pr * LANES, (pr + 1) * LANES)
            w_pair = jnp.concatenate([w_sp[2 * pr], w_sp[2 * pr + 1]], axis=1)
            vps = [new["vn"][rows, cols] for rows in chunk_rows]
            stacked = jnp.concatenate(
                [jnp.concatenate([jnp.where(first_group, vp, zero) for vp in vps], axis=1),
                 jnp.concatenate([jnp.where(first_group, zero, vp) for vp in vps], axis=1)], axis=0)
            mixed = _dot(w_pair, stacked)
            for c, rows in enumerate(chunk_rows):
                pieces[c].append(new["ug"][rows, cols] * (mixed[:, c * LANES:(c + 1) * LANES] + bsp_ref[:, cols]))
        new["ya_pre"] = jnp.concatenate([jnp.concatenate(p, axis=1) for p in pieces], axis=0)

    def fin_za(p):
        ya_ring[now3] = (_silu(p()) * new["ya_pre"]).astype(BF16)

    scale_q = math.log2(math.e) / math.sqrt(GROUP_DIM)

    def fin_q(p):
        new["q"] = (p() * scale_q).astype(BF16)
        q_ring[now2] = new["q"]

    def fin_k(p):
        new["k"] = p().astype(BF16)
        k_scr[par_in, pl.ds(pl.multiple_of(j_in * ROWS, ROWS), ROWS), :] = new["k"]

    def head_of(q_pair, hh):
        in_head = (lane >= hh * GROUP_DIM) & (lane < (hh + 1) * GROUP_DIM)
        return jnp.where(in_head, q_pair, jnp.zeros_like(q_pair))

    def pre_scores():
        for qi in range(Q_PER_STEP):
            rows = slice(qi * TQ, (qi + 1) * TQ)
            for hd in range(n_heads):
                p, hh = divmod(hd, heads_per_tile)
                lanes = slice(p * LANES, (p + 1) * LANES)
                z_ring[now2, qi * n_heads + hd] = _dot_nt(
                    new["k"][rows, lanes], head_of(new["q"][rows, lanes], hh))

    def fin_vt(p):
        v = p()
        for n in range(Q_PER_STEP):
            vT_scr[par_in, j_in * Q_PER_STEP + n] = v[n * TK:(n + 1) * TK, :].T.astype(BF16)

    def fin_gz(p):
        gz_ring[now2] = _silu(p())

    dots, due = [], {}

    def out_slab(j):
        cols = slice(j * MXU_SLAB, (j + 1) * MXU_SLAB)

        def item():
            y_scr[:, cols] = _dot(ycat(), wo_ref[:, cols])
        return item

    for j in range(wo_ref.shape[1] // MXU_SLAB):
        dots.append(out_slab(j))
    due.setdefault(len(dots) - 1 + FIN_DELAY, []).append(fin_out)

    o = 3 * d_a
    projections = [(0, fin_u), (d_a, fin_v), (2 * d_a, fin_za), (o, fin_q), (o + d_b, fin_k),
                   (o + 2 * d_b, fin_vt), (o + 3 * d_b, fin_gz)]
    assert d_a == d_b == MXU_SLAB

    assert land_scr.shape[0] == len(projections)

    def in_slab(n, lo):
        def item():
            land_scr[n] = _dot(h(), w_ref[:, lo:lo + MXU_SLAB])
        return item

    def landed(n):
        return lambda: land_scr[n]

    for n, (lo, fin) in enumerate(projections):
        dots.append(in_slab(n, lo))
        due.setdefault(len(dots) - 1 + FIN_DELAY, []).append(functools.partial(fin, landed(n)))
    items = [item_h]
    for n, dot_item in enumerate(dots):
        items.append(both(dot_item, *due.pop(n, [])))
    items.append(both(*[f for n in sorted(due) for f in due[n]], pre_scores))

    lane = lax.broadcasted_iota(jnp.int32, (TQ, LANES), 1)
    half = TK // 2
    s_idx = lax.broadcasted_iota(jnp.int32, (half, half), 0)
    t_idx = lax.broadcasted_iota(jnp.int32, (half, half), 1)
    strictly_earlier_half = s_idx < t_idx
    j_idx = lax.broadcasted_iota(jnp.int32, (TK, TK), 1)
    r_idx = lax.broadcasted_iota(jnp.int32, (TK, TK), 0)
    strictly_later = jnp.where(j_idx > r_idx, 1.0, 0.0).astype(BF16)

    q_masked = {}
    for qi in range(Q_PER_STEP):
        for hd in range(n_heads):
            p, hh = divmod(hd, heads_per_tile)
            q_pair = q_ring[prev2, qi * TQ:(qi + 1) * TQ, p * LANES:(p + 1) * LANES]
            q_masked[qi, hd] = head_of(q_pair, hh)

    lo_half, hi_half = slice(0, half), slice(half, TK)
    diag_parts = [(lo_half, lo_half, "tri"), (lo_half, hi_half, "full"),
                  (hi_half, lo_half, "empty"), (hi_half, hi_half, "tri")]
    full_parts = [(slice(0, TK), slice(0, TQ), "full")]

    def assemble(parts, pieces):
        if len(parts) == 1:
            return pieces[0]
        return jnp.concatenate([jnp.concatenate(pieces[0:2], axis=1),
                                jnp.concatenate(pieces[2:4], axis=1)], axis=0)

    def gated_pair(accs, qi, p):
        rows, cols = slice(qi * TQ, (qi + 1) * TQ), slice(p * LANES, (p + 1) * LANES)
        yT = jnp.concatenate([accs[qi, heads_per_tile * p + n] for n in range(heads_per_tile)], axis=0)
        yb_ring[now2, rows, cols] = (yT.T * gz_ring[prev2, rows, cols]).astype(BF16)

    def key_blocks(blocks, runs, accs, extra=(), finish=False):
        runs, accs = dict(runs), dict(accs)
        tiles = [dict(qi=qi, kb=kb, h=hd, keep=keep, diag=diag, parts=diag_parts if diag else full_parts)
                 for qi, kb, diag, keep in blocks for hd in range(n_heads)]
        last_tile = {(t["qi"], t["h"]): n for n, t in enumerate(tiles)}

        def scores(t):
            if t["diag"]:
                t["z"] = z_ring[prev2, t["qi"] * n_heads + t["h"]]
                return
            start = pl.multiple_of(t["kb"] * TK, TK)
            p = t["h"] // heads_per_tile
            lanes = slice(p * LANES, (p + 1) * LANES)
            t["z"] = _dot_nt(k_scr[par_at, pl.ds(start, TK), lanes], q_masked[t["qi"], t["h"]])

        def softplus(t):
            sps, t["logb"] = [], []
            for rows, cols, kind in t["parts"]:
                if kind == "empty":
                    sps.append(jnp.zeros((half, half), BF16))
                    t["logb"].append(None)
                    continue
                z = t["z"][rows, cols]
                sp = jnp.maximum(z, 0.0) + jnp.log2(1.0 + jnp.exp2(-jnp.abs(z)))
                t["logb"].append(z - sp)
                if kind == "tri":
                    sp = jnp.where(strictly_earlier_half, sp, 0.0)
                sps.append(sp.astype(BF16))
            t["sp"] = assemble(t["parts"], sps)

        def later_sum(t):
            t["csum"] = _dot(strictly_later, t["sp"])

        def weights(t):
            key = (t["qi"], t["h"])
            run = runs[key]
            probs = []
            for (rows, cols, kind), logb in zip(t["parts"], t["logb"]):
                if kind == "empty":
                    probs.append(jnp.zeros((half, half), BF16))
                    continue
                a = jnp.exp2(logb - (run[:, cols] + t["csum"][rows, cols]))
                if kind == "tri":
                    a = jnp.where(strictly_earlier_half, a, 0.0)
                probs.append(a.astype(BF16))
            t["prob"] = assemble(t["parts"], probs)
            runs[key] = run + (t["csum"][0:1, :] + t["sp"][0:1, :].astype(F32))

        def values(n, t):
            key = (t["qi"], t["h"])
            hd = t["h"]
            v_blk = vT_scr[par_at, t["kb"], hd * GROUP_DIM:(hd + 1) * GROUP_DIM, :]
            if t["keep"] is not None:
                v_blk = (v_blk.astype(F32) * t["keep"]).astype(BF16)
            accs[key] = accs[key] + _dot(v_blk, t["prob"])
            if finish and last_tile[key] == n and hd % heads_per_tile == heads_per_tile - 1:
                gated_pair(accs, t["qi"], hd // heads_per_tile)

        n_tiles = len(tiles)
        lead, lag = SKEW
        ready = next((n for n, t in enumerate(tiles) if not t["diag"]), n_tiles)
        for t in tiles[:ready]:
            scores(t)
        steps = range(min(0, ready - lead), n_tiles + lag)
        extra = list(extra)
        slots = [EXTRA_START + (n * (len(steps) - EXTRA_START)) // max(1, len(extra))
                 for n in range(len(extra))]
        for count, step in enumerate(steps):
            while extra and slots[0] <= count:
                slots.pop(0)
                extra.pop(0)()
            if ready <= step + lead < n_tiles:
                scores(tiles[step + lead])
            if 0 <= step < n_tiles:
                softplus(tiles[step])
                later_sum(tiles[step])
            if 0 <= step - lag < n_tiles:
                weights(tiles[step - lag])
                values(step - lag, tiles[step - lag])
        for item in extra:
            item()
        return runs, accs

    runs = {(qi, hd): jnp.zeros((1, TQ), F32) for qi in range(Q_PER_STEP) for hd in range(n_heads)}
    accs = {(qi, hd): jnp.zeros((GROUP_DIM, TQ), F32) for qi in range(Q_PER_STEP) for hd in range(n_heads)}
    first_kb = j_at * Q_PER_STEP
    diag_blocks = [(qi, first_kb + qi, True, None) for qi in range(Q_PER_STEP)]
    has_earlier = (j_at > 0).astype(F32)
    prev_blocks = [(qi, jnp.maximum(first_kb + qi - 1, 0), False, has_earlier if qi == 0 else None)
                   for qi in range(Q_PER_STEP)]
    runs, accs = key_blocks(diag_blocks + prev_blocks, runs, accs, items, finish=True)

    for qi in range(Q_PER_STEP):
        def sticks_left(runs):
            return jnp.min(functools.reduce(jnp.minimum, [runs[qi, hd] for hd in range(n_heads)])) < UNDERFLOW_LOG2

        def earlier_block(state, qi=qi, sticks_left=sticks_left):
            kb, _, runs, accs = state
            runs, accs = key_blocks([(qi, kb, False, None)], runs, accs)
            return kb - 1, sticks_left(runs), runs, accs

        first = first_kb + qi - 2
        mine = lambda d: {k: v for k, v in d.items() if k[0] == qi}

        @pl.when((first >= 0) & sticks_left(runs))
        def _(qi=qi, first=first, earlier_block=earlier_block, mine=mine):
            _, _, _, more = lax.while_loop(lambda st: (st[0] >= 0) & st[1], earlier_block,
                                           (first, True, mine(runs), mine(accs)))
            for p in range(n_pairs):
                gated_pair(more, qi, p)


def _fused_block(x, mod3, w, w_sp, b_sp_full, a_g, a_b, w_out, ln_g, ln_b, alpha):
    bsz, seq, d = x.shape
    d_a = a_g.shape[-1]
    d_b = (w.shape[1] - 3 * d_a) // 4
    per_seq = seq // ROWS
    n_total = bsz * per_seq
    grid = (n_total + 2,)

    def stage(lag):
        return lambda s: _stage_slabs(s, n_total, per_seq)[lag]

    def rows_of(lag):
        return pl.BlockSpec((1, ROWS, d), lambda s: (*stage(lag)(s), 0))

    def mod_of(lag):
        return pl.BlockSpec((1, 3, d), lambda s: (stage(lag)(s)[0], 0, 0))

    def whole(a):
        return pl.BlockSpec(a.shape, lambda s: (0,) * a.ndim)

    return pl.pallas_call(
        functools.partial(_block_kernel, alpha, n_total),
        grid=grid,
        in_specs=[rows_of(0), rows_of(2), mod_of(0), mod_of(2), whole(w), whole(w_sp),
                  whole(b_sp_full), whole(a_g), whole(a_b), whole(w_out), whole(ln_g), whole(ln_b)],
        out_specs=rows_of(2),
        out_shape=jax.ShapeDtypeStruct((bsz, seq, d), x.dtype),
        scratch_shapes=[
            pltpu.VMEM((2, seq, d_b), BF16),
            pltpu.VMEM((2, seq // TK, d_b, TK), BF16),
            pltpu.VMEM((2, ROWS, d_b), BF16),
            pltpu.VMEM((2, ROWS, d_b), F32),
            pltpu.VMEM((3, ROWS, d_a), BF16),
            pltpu.VMEM((2, ROWS, d_b), BF16),
            pltpu.VMEM((2, Q_PER_STEP * (d_b // GROUP_DIM), TK, TQ), F32),
            pltpu.VMEM((w.shape[1] // MXU_SLAB, ROWS, MXU_SLAB), F32),
            pltpu.VMEM((ROWS, d), F32),
        ],
        compiler_params=pltpu.CompilerParams(
            dimension_semantics=("arbitrary",), vmem_limit_bytes=VMEM_LIMIT),
        name="fused_block",
    )(x, x, mod3, mod3, w, w_sp, b_sp_full, a_g, a_b, w_out, ln_g, ln_b)


def kernel(x, c, w_ada, b_ada, w_in, sgu_ln_g, sgu_ln_b, w_spatial, b_spatial, w_out, ln_g, ln_b):
    depth = w_ada.shape[0]
    bsz, seq, d = x.shape
    d_a = sgu_ln_g.shape[-1]
    alpha = (2.0 * depth) ** 0.25
    for layer in range(depth):
        mod, w_in_bf, w_out_bf = _prepare(c, w_ada[layer], b_ada[layer], w_in[layer], w_out[layer])
        b_sp_full = jnp.repeat(b_spatial[layer].T, GROUP_DIM, axis=1)
        x = _fused_block(
            x, mod.reshape(bsz, 3, d), w_in_bf, w_spatial[layer], b_sp_full,
            sgu_ln_g[layer].reshape(1, d_a), sgu_ln_b[layer].reshape(1, d_a),
            w_out_bf, ln_g[layer].reshape(1, d), ln_b[layer].reshape(1, d), alpha)
    return x
```

```python
import functools
import math

import jax
import jax.numpy as jnp
from jax import lax
from jax.experimental import pallas as pl
from jax.experimental.pallas import tpu as pltpu

F32 = jnp.float32
BF16 = jnp.bfloat16

N_GROUPS = 8
GROUP_DIM = 64
CHUNK = 128
LN_EPS = 1e-5
LANES = 128

PREP_TILE = 512
TQ = 256
TK = 256
Q_PER_STEP = 1
ROWS = Q_PER_STEP * TQ
SKEW = (8, 3)
EXTRA_START = 2
MXU_SLAB = 512
FIN_DELAY = 1
UNDERFLOW_LOG2 = 152.0
VMEM_LIMIT = 48 * 1024 * 1024


def _dot(a, b):
    return jnp.dot(a, b, preferred_element_type=F32)


def _dot_nt(a, b):
    return lax.dot_general(a, b, (((1,), (1,)), ((), ())), preferred_element_type=F32)


def _gelu_tanh(x):
    k1 = -2.0 * math.sqrt(2.0 / math.pi) * math.log2(math.e)
    return x / (1.0 + jnp.exp2(x * (k1 + (k1 * 0.044715) * (x * x))))


def _silu(x):
    return x / (1.0 + jnp.exp(-x))


def _layer_norm(x, g, b):
    mu = jnp.mean(x, axis=-1, keepdims=True)
    xc = x - mu
    var = jnp.mean(xc * xc, axis=-1, keepdims=True)
    return xc * lax.rsqrt(var + LN_EPS) * g + b


def _prep_kernel(c_ref, wa_ref, ba_ref, win_ref, wout_ref, mod_ref, win_bf_ref, wout_bf_ref):
    sc = _silu(c_ref[...]).astype(BF16)
    mod_ref[...] = _dot(sc, wa_ref[...].astype(BF16)) + ba_ref[...]
    win_bf_ref[...] = win_ref[...].astype(BF16)
    wout_bf_ref[...] = wout_ref[...].astype(BF16)


def _prepare(c, w_ada, b_ada, w_in, w_out):
    bsz, d = c.shape
    n_mod = w_ada.shape[1]
    steps = w_in.shape[1] // PREP_TILE
    mod_tiles = n_mod // PREP_TILE
    out_tile = PREP_TILE
    out_tiles = w_out.shape[1] // out_tile
    assert mod_tiles <= steps and out_tiles <= steps
    mod_col = lambda j: (0, jnp.minimum(j, mod_tiles - 1))
    out_col = lambda j: (0, jnp.minimum(j, out_tiles - 1))
    return pl.pallas_call(
        _prep_kernel,
        grid=(steps,),
        in_specs=[
            pl.BlockSpec((bsz, d), lambda j: (0, 0)),
            pl.BlockSpec((d, PREP_TILE), mod_col),
            pl.BlockSpec((1, PREP_TILE), mod_col),
            pl.BlockSpec((w_in.shape[0], PREP_TILE), lambda j: (0, j)),
            pl.BlockSpec((w_out.shape[0], out_tile), out_col),
        ],
        out_specs=[
            pl.BlockSpec((bsz, PREP_TILE), mod_col),
            pl.BlockSpec((w_in.shape[0], PREP_TILE), lambda j: (0, j)),
            pl.BlockSpec((w_out.shape[0], out_tile), out_col),
        ],
        out_shape=[
            jax.ShapeDtypeStruct((bsz, n_mod), F32),
            jax.ShapeDtypeStruct(w_in.shape, BF16),
            jax.ShapeDtypeStruct(w_out.shape, BF16),
        ],
        compiler_params=pltpu.CompilerParams(
            dimension_semantics=("arbitrary",), vmem_limit_bytes=VMEM_LIMIT),
        name="adaln_mod_and_casts",
    )(c, w_ada, b_ada.reshape(1, n_mod), w_in, w_out)


def _stage_slabs(s, n_total, per_seq):
    last = n_total - 1
    out = []
    for lag in range(3):
        t = jnp.clip(s - lag, 0, last)
        out.append((lax.div(t, per_seq), lax.rem(t, per_seq)))
    return out


def _block_kernel(alpha, n_total,
                  x_in_ref, x_out_ref, mod_in_ref, mod_out_ref, w_ref, wsp_ref, bsp_ref,
                  ag_ref, ab_ref, wo_ref, g_ref, b_ref, o_ref,
                  k_scr, vT_scr, q_ring, gz_ring, ya_ring, yb_ring, z_ring, land_scr, y_scr):
    s = pl.program_id(0)
    d_b = q_ring.shape[-1]
    d_a = ya_ring.shape[-1]
    per_seq = k_scr.shape[1] // ROWS
    (b_in, j_in), (b_at, j_at), _ = _stage_slabs(s, n_total, per_seq)
    par_in, par_at = lax.rem(b_in, 2), lax.rem(b_at, 2)
    n_heads = d_b // GROUP_DIM
    heads_per_tile = LANES // GROUP_DIM
    n_pairs = n_heads // heads_per_tile
    now2, prev2 = lax.rem(s, 2), lax.rem(s + 1, 2)
    now3, prev3 = lax.rem(s, 3), lax.rem(s + 1, 3)

    @pl.when(s == 0)
    def _():
        q_ring[...] = jnp.zeros_like(q_ring)
        gz_ring[...] = jnp.zeros_like(gz_ring)
        ya_ring[...] = jnp.zeros_like(ya_ring)
        yb_ring[...] = jnp.zeros_like(yb_ring)
        z_ring[...] = jnp.zeros_like(z_ring)
        k_scr[0, 0:ROWS, :] = jnp.zeros((ROWS, d_b), BF16)
        for n in range(Q_PER_STEP):
            vT_scr[0, n] = jnp.zeros((d_b, TK), BF16)

    new = {}

    def item_h():
        shift = mod_in_ref[0, 0:1, :]
        scale = mod_in_ref[0, 1:2, :]
        new["h"] = (x_in_ref[0] * (1.0 + scale) + shift).astype(BF16)

    def h():
        return new["h"]

    def ycat():
        return jnp.concatenate([ya_ring[prev3], yb_ring[prev2]], axis=1)

    def both(*fs):
        return lambda: [f() for f in fs]

    def fin_out():
        gate = mod_out_ref[0, 2:3, :]
        r = alpha * x_out_ref[0] + gate * y_scr[...]
        o_ref[0] = _layer_norm(r, g_ref[...], b_ref[...])

    def fin_u(p):
        new["ug"] = _gelu_tanh(p())

    def fin_v(p):
        new["vn"] = _layer_norm(_gelu_tanh(p()), ag_ref[...], ab_ref[...]).astype(BF16)
        t_idx = lax.broadcasted_iota(jnp.int32, (CHUNK, CHUNK), 0)
        s_idx = lax.broadcasted_iota(jnp.int32, (CHUNK, CHUNK), 1)
        causal = t_idx >= s_idx
        w_sp = [jnp.where(causal, wsp_ref[g], 0.0).astype(BF16) for g in range(N_GROUPS)]
        first_group = lax.broadcasted_iota(jnp.int32, (CHUNK, LANES), 1) < GROUP_DIM
        chunk_rows = [slice(c * CHUNK, (c + 1) * CHUNK) for c in range(ROWS // CHUNK)]
        zero = jnp.zeros((CHUNK, LANES), BF16)
        pieces = [[] for _ in chunk_rows]
        for pr in range(d_a // LANES):
            cols = slice(pr * LANES, (pr + 1) * LANES)
            w_pair = jnp.concatenate([w_sp[2 * pr], w_sp[2 * pr + 1]], axis=1)
            vps = [new["vn"][rows, cols] for rows in chunk_rows]
            stacked = jnp.concatenate(
                [jnp.concatenate([jnp.where(first_group, vp, zero) for vp in vps], axis=1),
                 jnp.concatenate([jnp.where(first_group, zero, vp) for vp in vps], axis=1)], axis=0)
            mixed = _dot(w_pair, stacked)
            for c, rows in enumerate(chunk_rows):
                pieces[c].append(new["ug"][rows, cols] * (mixed[:, c * LANES:(c + 1) * LANES] + bsp_ref[:, cols]))
        new["ya_pre"] = jnp.concatenate([jnp.concatenate(p, axis=1) for p in pieces], axis=0)

    def fin_za(p):
        ya_ring[now3] = (_silu(p()) * new["ya_pre"]).astype(BF16)

    scale_q = math.log2(math.e) / math.sqrt(GROUP_DIM)

    def fin_q(p):
        new["q"] = (p() * scale_q).astype(BF16)
        q_ring[now2] = new["q"]

    def fin_k(p):
        new["k"] = p().astype(BF16)
        k_scr[par_in, pl.ds(pl.multiple_of(j_in * ROWS, ROWS), ROWS), :] = new["k"]

    def head_of(q_pair, hh):
        in_head = (lane >= hh * GROUP_DIM) & (lane < (hh + 1) * GROUP_DIM)
        return jnp.where(in_head, q_pair, jnp.zeros_like(q_pair))

    def pre_scores():
        for qi in range(Q_PER_STEP):
            rows = slice(qi * TQ, (qi + 1) * TQ)
            for hd in range(n_heads):
                p, hh = divmod(hd, heads_per_tile)
                lanes = slice(p * LANES, (p + 1) * LANES)
                z_ring[now2, qi * n_heads + hd] = _dot_nt(
                    new["k"][rows, lanes], head_of(new["q"][rows, lanes], hh))

    def fin_vt(p):
        v = p()
        for n in range(Q_PER_STEP):
            vT_scr[par_in, j_in * Q_PER_STEP + n] = v[n * TK:(n + 1) * TK, :].T.astype(BF16)

    def fin_gz(p):
        gz_ring[now2] = _silu(p())

    dots, due = [], {}

    def out_slab(j):
        cols = slice(j * 2 * MXU_SLAB, (j + 1) * 2 * MXU_SLAB)

        def item():
            y_scr[:, cols] = _dot(ycat(), wo_ref[:, cols])
        return item

    for j in range(wo_ref.shape[1] // (2 * MXU_SLAB)):
        dots.append(out_slab(j))
    due.setdefault(len(dots) - 1 + FIN_DELAY, []).append(fin_out)

    o = 3 * d_a
    projections = [(0, fin_u), (d_a, fin_v), (2 * d_a, fin_za), (o, fin_q), (o + d_b, fin_k),
                   (o + 2 * d_b, fin_vt), (o + 3 * d_b, fin_gz)]
    assert d_a == d_b == MXU_SLAB

    assert land_scr.shape[0] == len(projections)

    def in_slab(n, lo):
        def item():
            land_scr[n] = _dot(h(), w_ref[:, lo:lo + MXU_SLAB])
        return item

    def landed(n):
        return lambda: land_scr[n]

    for n, (lo, fin) in enumerate(projections):
        dots.append(in_slab(n, lo))
        due.setdefault(len(dots) - 1 + FIN_DELAY, []).append(functools.partial(fin, landed(n)))
    items = [item_h]
    for n, dot_item in enumerate(dots):
        items.append(both(dot_item, *due.pop(n, [])))
    items.append(both(*[f for n in sorted(due) for f in due[n]], pre_scores))

    lane = lax.broadcasted_iota(jnp.int32, (TQ, LANES), 1)
    half = TK // 2
    s_idx = lax.broadcasted_iota(jnp.int32, (half, half), 0)
    t_idx = lax.broadcasted_iota(jnp.int32, (half, half), 1)
    strictly_earlier_half = s_idx < t_idx
    j_idx = lax.broadcasted_iota(jnp.int32, (TK, TK), 1)
    r_idx = lax.broadcasted_iota(jnp.int32, (TK, TK), 0)
    strictly_later = jnp.where(j_idx > r_idx, 1.0, 0.0).astype(BF16)

    q_masked = {}
    for qi in range(Q_PER_STEP):
        for hd in range(n_heads):
            p, hh = divmod(hd, heads_per_tile)
            q_pair = q_ring[prev2, qi * TQ:(qi + 1) * TQ, p * LANES:(p + 1) * LANES]
            q_masked[qi, hd] = head_of(q_pair, hh)

    lo_half, hi_half = slice(0, half), slice(half, TK)
    diag_parts = [(lo_half, lo_half, "tri"), (lo_half, hi_half, "full"),
                  (hi_half, lo_half, "empty"), (hi_half, hi_half, "tri")]
    full_parts = [(slice(0, TK), slice(0, TQ), "full")]

    def assemble(parts, pieces):
        if len(parts) == 1:
            return pieces[0]
        return jnp.concatenate([jnp.concatenate(pieces[0:2], axis=1),
                                jnp.concatenate(pieces[2:4], axis=1)], axis=0)

    def gated_pair(accs, qi, p):
        rows, cols = slice(qi * TQ, (qi + 1) * TQ), slice(p * LANES, (p + 1) * LANES)
        yT = jnp.concatenate([accs[qi, heads_per_tile * p + n] for n in range(heads_per_tile)], axis=0)
        yb_ring[now2, rows, cols] = (yT.T * gz_ring[prev2, rows, cols]).astype(BF16)

    def key_blocks(blocks, runs, accs, extra=(), finish=False):
        runs, accs = dict(runs), dict(accs)
        tiles = [dict(qi=qi, kb=kb, h=hd, keep=keep, diag=diag, parts=diag_parts if diag else full_parts)
                 for qi, kb, diag, keep in blocks for hd in range(n_heads)]
        last_tile = {(t["qi"], t["h"]): n for n, t in enumerate(tiles)}

        def scores(t):
            if t["diag"]:
                t["z"] = z_ring[prev2, t["qi"] * n_heads + t["h"]]
                return
            start = pl.multiple_of(t["kb"] * TK, TK)
            p = t["h"] // heads_per_tile
            lanes = slice(p * LANES, (p + 1) * LANES)
            t["z"] = _dot_nt(k_scr[par_at, pl.ds(start, TK), lanes], q_masked[t["qi"], t["h"]])

        def softplus(t):
            sps, t["logb"] = [], []
            for rows, cols, kind in t["parts"]:
                if kind == "empty":
                    sps.append(jnp.zeros((half, half), BF16))
                    t["logb"].append(None)
                    continue
                z = t["z"][rows, cols]
                sp = jnp.maximum(z, 0.0) + jnp.log2(1.0 + jnp.exp2(-jnp.abs(z)))
                t["logb"].append(z - sp)
                if kind == "tri":
                    sp = jnp.where(strictly_earlier_half, sp, 0.0)
                sps.append(sp.astype(BF16))
            t["sp"] = assemble(t["parts"], sps)

        def later_sum(t):
            t["csum"] = _dot(strictly_later, t["sp"])

        def weights(t):
            key = (t["qi"], t["h"])
            run = runs[key]
            probs = []
            for (rows, cols, kind), logb in zip(t["parts"], t["logb"]):
                if kind == "empty":
                    probs.append(jnp.zeros((half, half), BF16))
                    continue
                a = jnp.exp2(logb - (run[:, cols] + t["csum"][rows, cols]))
                if kind == "tri":
                    a = jnp.where(strictly_earlier_half, a, 0.0)
                probs.append(a.astype(BF16))
            t["prob"] = assemble(t["parts"], probs)
            runs[key] = run + (t["csum"][0:1, :] + t["sp"][0:1, :].astype(F32))

        def values(n, t):
            key = (t["qi"], t["h"])
            hd = t["h"]
            v_blk = vT_scr[par_at, t["kb"], hd * GROUP_DIM:(hd + 1) * GROUP_DIM, :]
            if t["keep"] is not None:
                v_blk = (v_blk.astype(F32) * t["keep"]).astype(BF16)
            accs[key] = accs[key] + _dot(v_blk, t["prob"])
            if finish and last_tile[key] == n and hd % heads_per_tile == heads_per_tile - 1:
                gated_pair(accs, t["qi"], hd // heads_per_tile)

        n_tiles = len(tiles)
        lead, lag = SKEW
        ready = next((n for n, t in enumerate(tiles) if not t["diag"]), n_tiles)
        for t in tiles[:ready]:
            scores(t)
        steps = range(min(0, ready - lead), n_tiles + lag)
        extra = list(extra)
        slots = [EXTRA_START + (n * (len(steps) - EXTRA_START)) // max(1, len(extra))
                 for n in range(len(extra))]
        for count, step in enumerate(steps):
            while extra and slots[0] <= count:
                slots.pop(0)
                extra.pop(0)()
            if ready <= step + lead < n_tiles:
                scores(tiles[step + lead])
            if 0 <= step < n_tiles:
                softplus(tiles[step])
                later_sum(tiles[step])
            if 0 <= step - lag < n_tiles:
                weights(tiles[step - lag])
                values(step - lag, tiles[step - lag])
        for item in extra:
            item()
        return runs, accs

    runs = {(qi, hd): jnp.zeros((1, TQ), F32) for qi in range(Q_PER_STEP) for hd in range(n_heads)}
    accs = {(qi, hd): jnp.zeros((GROUP_DIM, TQ), F32) for qi in range(Q_PER_STEP) for hd in range(n_heads)}
    first_kb = j_at * Q_PER_STEP
    diag_blocks = [(qi, first_kb + qi, True, None) for qi in range(Q_PER_STEP)]
    has_earlier = (j_at > 0).astype(F32)
    prev_blocks = [(qi, jnp.maximum(first_kb + qi - 1, 0), False, has_earlier if qi == 0 else None)
                   for qi in range(Q_PER_STEP)]
    runs, accs = key_blocks(diag_blocks + prev_blocks, runs, accs, items, finish=True)

    for qi in range(Q_PER_STEP):
        def sticks_left(runs):
            return jnp.min(functools.reduce(jnp.minimum, [runs[qi, hd] for hd in range(n_heads)])) < UNDERFLOW_LOG2

        def earlier_block(state, qi=qi, sticks_left=sticks_left):
            kb, _, runs, accs = state
            runs, accs = key_blocks([(qi, kb, False, None)], runs, accs)
            return kb - 1, sticks_left(runs), runs, accs

        first = first_kb + qi - 2
        mine = lambda d: {k: v for k, v in d.items() if k[0] == qi}

        @pl.when((first >= 0) & sticks_left(runs))
        def _(qi=qi, first=first, earlier_block=earlier_block, mine=mine):
            _, _, _, more = lax.while_loop(lambda st: (st[0] >= 0) & st[1], earlier_block,
                                           (first, True, mine(runs), mine(accs)))
            for p in range(n_pairs):
                gated_pair(more, qi, p)


def _fused_block(x, mod3, w, w_sp, b_sp_full, a_g, a_b, w_out, ln_g, ln_b, alpha):
    bsz, seq, d = x.shape
    d_a = a_g.shape[-1]
    d_b = (w.shape[1] - 3 * d_a) // 4
    per_seq = seq // ROWS
    n_total = bsz * per_seq
    grid = (n_total + 2,)

    def stage(lag):
        return lambda s: _stage_slabs(s, n_total, per_seq)[lag]

    def rows_of(lag):
        return pl.BlockSpec((1, ROWS, d), lambda s: (*stage(lag)(s), 0))

    def mod_of(lag):
        return pl.BlockSpec((1, 3, d), lambda s: (stage(lag)(s)[0], 0, 0))

    def whole(a):
        return pl.BlockSpec(a.shape, lambda s: (0,) * a.ndim)

    return pl.pallas_call(
        functools.partial(_block_kernel, alpha, n_total),
        grid=grid,
        in_specs=[rows_of(0), rows_of(2), mod_of(0), mod_of(2), whole(w), whole(w_sp),
                  whole(b_sp_full), whole(a_g), whole(a_b), whole(w_out), whole(ln_g), whole(ln_b)],
        out_specs=rows_of(2),
        out_shape=jax.ShapeDtypeStruct((bsz, seq, d), x.dtype),
        scratch_shapes=[
            pltpu.VMEM((2, seq, d_b), BF16),
            pltpu.VMEM((2, seq // TK, d_b, TK), BF16),
            pltpu.VMEM((2, ROWS, d_b), BF16),
            pltpu.VMEM((2, ROWS, d_b), F32),
            pltpu.VMEM((3, ROWS, d_a), BF16),
            pltpu.VMEM((2, ROWS, d_b), BF16),
            pltpu.VMEM((2, Q_PER_STEP * (d_b // GROUP_DIM), TK, TQ), F32),
            pltpu.VMEM((w.shape[1] // MXU_SLAB, ROWS, MXU_SLAB), F32),
            pltpu.VMEM((ROWS, d), F32),
        ],
        compiler_params=pltpu.CompilerParams(
            dimension_semantics=("arbitrary",), vmem_limit_bytes=VMEM_LIMIT),
        name="fused_block",
    )(x, x, mod3, mod3, w, w_sp, b_sp_full, a_g, a_b, w_out, ln_g, ln_b)


def kernel(x, c, w_ada, b_ada, w_in, sgu_ln_g, sgu_ln_b, w_spatial, b_spatial, w_out, ln_g, ln_b):
    depth = w_ada.shape[0]
    bsz, seq, d = x.shape
    d_a = sgu_ln_g.shape[-1]
    alpha = (2.0 * depth) ** 0.25
    for layer in range(depth):
        mod, w_in_bf, w_out_bf = _prepare(c, w_ada[layer], b_ada[layer], w_in[layer], w_out[layer])
        b_sp_full = jnp.repeat(b_spatial[layer].T, GROUP_DIM, axis=1)
        x = _fused_block(
            x, mod.reshape(bsz, 3, d), w_in_bf, w_spatial[layer], b_sp_full,
            sgu_ln_g[layer].reshape(1, d_a), sgu_ln_b[layer].reshape(1, d_a),
            w_out_bf, ln_g[layer].reshape(1, d), ln_b[layer].reshape(1, d), alpha)
    return x
```

```python
import functools
import math

import jax
import jax.numpy as jnp
from jax import lax
from jax.experimental import pallas as pl
from jax.experimental.pallas import tpu as pltpu

F32 = jnp.float32
BF16 = jnp.bfloat16

N_GROUPS = 8
GROUP_DIM = 64
CHUNK = 128
LN_EPS = 1e-5
LANES = 128

PREP_TILE = 512
TQ = 256
TK = 256
Q_PER_STEP = 1
ROWS = Q_PER_STEP * TQ
SKEW = (8, 3)
EXTRA_START = 2
MXU_SLAB = 512
FIN_DELAY = 1
UNDERFLOW_LOG2 = 152.0
VMEM_LIMIT = 48 * 1024 * 1024


def _dot(a, b):
    return jnp.dot(a, b, preferred_element_type=F32)


def _dot_nt(a, b):
    return lax.dot_general(a, b, (((1,), (1,)), ((), ())), preferred_element_type=F32)


def _gelu_tanh(x):
    k1 = -2.0 * math.sqrt(2.0 / math.pi) * math.log2(math.e)
    return x / (1.0 + jnp.exp2(x * (k1 + (k1 * 0.044715) * (x * x))))


def _silu(x):
    return x / (1.0 + jnp.exp(-x))


def _layer_norm(x, g, b):
    mu = jnp.mean(x, axis=-1, keepdims=True)
    xc = x - mu
    var = jnp.mean(xc * xc, axis=-1, keepdims=True)
    return xc * lax.rsqrt(var + LN_EPS) * g + b


def _prep_kernel(c_ref, wa_ref, ba_ref, win_ref, wout_ref, mod_ref, win_bf_ref, wout_bf_ref):
    sc = _silu(c_ref[...]).astype(BF16)
    mod_ref[...] = _dot(sc, wa_ref[...].astype(BF16)) + ba_ref[...]
    win_bf_ref[...] = win_ref[...].astype(BF16)
    wout_bf_ref[...] = wout_ref[...].astype(BF16)


def _prepare(c, w_ada, b_ada, w_in, w_out):
    bsz, d = c.shape
    n_mod = w_ada.shape[1]
    steps = w_in.shape[1] // PREP_TILE
    mod_tiles = n_mod // PREP_TILE
    out_tile = PREP_TILE
    out_tiles = w_out.shape[1] // out_tile
    assert mod_tiles <= steps and out_tiles <= steps
    mod_col = lambda j: (0, jnp.minimum(j, mod_tiles - 1))
    out_col = lambda j: (0, jnp.minimum(j, out_tiles - 1))
    return pl.pallas_call(
        _prep_kernel,
        grid=(steps,),
        in_specs=[
            pl.BlockSpec((bsz, d), lambda j: (0, 0)),
            pl.BlockSpec((d, PREP_TILE), mod_col),
            pl.BlockSpec((1, PREP_TILE), mod_col),
            pl.BlockSpec((w_in.shape[0], PREP_TILE), lambda j: (0, j)),
            pl.BlockSpec((w_out.shape[0], out_tile), out_col),
        ],
        out_specs=[
            pl.BlockSpec((bsz, PREP_TILE), mod_col),
            pl.BlockSpec((w_in.shape[0], PREP_TILE), lambda j: (0, j)),
            pl.BlockSpec((w_out.shape[0], out_tile), out_col),
        ],
        out_shape=[
            jax.ShapeDtypeStruct((bsz, n_mod), F32),
            jax.ShapeDtypeStruct(w_in.shape, BF16),
            jax.ShapeDtypeStruct(w_out.shape, BF16),
        ],
        compiler_params=pltpu.CompilerParams(
            dimension_semantics=("arbitrary",), vmem_limit_bytes=VMEM_LIMIT),
        name="adaln_mod_and_casts",
    )(c, w_ada, b_ada.reshape(1, n_mod), w_in, w_out)


def _stage_slabs(s, n_total, per_seq):
    last = n_total - 1
    out = []
    for lag in range(3):
        t = jnp.clip(s - lag, 0, last)
        out.append((lax.div(t, per_seq), lax.rem(t, per_seq)))
    return out


def _block_kernel(alpha, n_total,
                  x_in_ref, x_out_ref, mod_in_ref, mod_out_ref, w_ref, wsp_ref, bsp_ref,
                  ag_ref, ab_ref, wo_ref, g_ref, b_ref, o_ref,
                  k_scr, vT_scr, q_ring, gz_ring, ya_ring, yb_ring, z_ring, land_scr, y_scr):
    s = pl.program_id(0)
    d_b = q_ring.shape[-1]
    d_a = ya_ring.shape[-1]
    per_seq = k_scr.shape[1] // ROWS
    (b_in, j_in), (b_at, j_at), _ = _stage_slabs(s, n_total, per_seq)
    par_in, par_at = lax.rem(b_in, 2), lax.rem(b_at, 2)
    n_heads = d_b // GROUP_DIM
    heads_per_tile = LANES // GROUP_DIM
    n_pairs = n_heads // heads_per_tile
    now2, prev2 = lax.rem(s, 2), lax.rem(s + 1, 2)
    now3, prev3 = lax.rem(s, 3), lax.rem(s + 1, 3)

    @pl.when(s == 0)
    def _():
        q_ring[...] = jnp.zeros_like(q_ring)
        gz_ring[...] = jnp.zeros_like(gz_ring)
        ya_ring[...] = jnp.zeros_like(ya_ring)
        yb_ring[...] = jnp.zeros_like(yb_ring)
        z_ring[...] = jnp.zeros_like(z_ring)
        k_scr[0, 0:ROWS, :] = jnp.zeros((ROWS, d_b), BF16)
        for n in range(Q_PER_STEP):
            vT_scr[0, n] = jnp.zeros((d_b, TK), BF16)

    new = {}

    def item_h():
        shift = mod_in_ref[0, 0:1, :]
        scale = mod_in_ref[0, 1:2, :]
        new["h"] = (x_in_ref[0] * (1.0 + scale) + shift).astype(BF16)

    def h():
        return new["h"]

    def ycat():
        return jnp.concatenate([ya_ring[prev3], yb_ring[prev2]], axis=1)

    def both(*fs):
        return lambda: [f() for f in fs]

    def fin_out():
        gate = mod_out_ref[0, 2:3, :]
        r = alpha * x_out_ref[0] + gate * y_scr[...]
        o_ref[0] = _layer_norm(r, g_ref[...], b_ref[...])

    def fin_u(p):
        new["ug"] = _gelu_tanh(p())

    def fin_v(p):
        new["vn"] = _layer_norm(_gelu_tanh(p()), ag_ref[...], ab_ref[...]).astype(BF16)
        t_idx = lax.broadcasted_iota(jnp.int32, (CHUNK, CHUNK), 0)
        s_idx = lax.broadcasted_iota(jnp.int32, (CHUNK, CHUNK), 1)
        causal = t_idx >= s_idx
        w_sp = [jnp.where(causal, wsp_ref[g], 0.0).astype(BF16) for g in range(N_GROUPS)]
        first_group = lax.broadcasted_iota(jnp.int32, (CHUNK, LANES), 1) < GROUP_DIM
        chunk_rows = [slice(c * CHUNK, (c + 1) * CHUNK) for c in range(ROWS // CHUNK)]
        zero = jnp.zeros((CHUNK, LANES), BF16)
        pieces = [[] for _ in chunk_rows]
        for pr in range(d_a // LANES):
            cols = slice(pr * LANES, (pr + 1) * LANES)
            w_pair = jnp.concatenate([w_sp[2 * pr], w_sp[2 * pr + 1]], axis=1)
            vps = [new["vn"][rows, cols] for rows in chunk_rows]
            stacked = jnp.concatenate(
                [jnp.concatenate([jnp.where(first_group, vp, zero) for vp in vps], axis=1),
                 jnp.concatenate([jnp.where(first_group, zero, vp) for vp in vps], axis=1)], axis=0)
            mixed = _dot(w_pair, stacked)
            for c, rows in enumerate(chunk_rows):
                pieces[c].append(new["ug"][rows, cols] * (mixed[:, c * LANES:(c + 1) * LANES] + bsp_ref[:, cols]))
        new["ya_pre"] = jnp.concatenate([jnp.concatenate(p, axis=1) for p in pieces], axis=0)

    def fin_za(p):
        ya_ring[now3] = (_silu(p()) * new["ya_pre"]).astype(BF16)

    scale_q = math.log2(math.e) / math.sqrt(GROUP_DIM)

    def fin_q(p):
        new["q"] = (p() * scale_q).astype(BF16)
        q_ring[now2] = new["q"]

    def fin_k(p):
        new["k"] = p().astype(BF16)
        k_scr[par_in, pl.ds(pl.multiple_of(j_in * ROWS, ROWS), ROWS), :] = new["k"]

    def head_of(q_pair, hh):
        in_head = (lane >= hh * GROUP_DIM) & (lane < (hh + 1) * GROUP_DIM)
        return jnp.where(in_head, q_pair, jnp.zeros_like(q_pair))

    def pre_scores():
        for qi in range(Q_PER_STEP):
            rows = slice(qi * TQ, (qi + 1) * TQ)
            for hd in range(n_heads):
                p, hh = divmod(hd, heads_per_tile)
                lanes = slice(p * LANES, (p + 1) * LANES)
                z_ring[now2, qi * n_heads + hd] = _dot_nt(
                    new["k"][rows, lanes], head_of(new["q"][rows, lanes], hh))

    def fin_vt(p):
        v = p()
        for n in range(Q_PER_STEP):
            vT_scr[par_in, j_in * Q_PER_STEP + n] = v[n * TK:(n + 1) * TK, :].T.astype(BF16)

    def fin_gz(p):
        gz_ring[now2] = _silu(p())

    dots, due = [], {}

    def out_slab(j):
        cols = slice(j * 2 * MXU_SLAB, (j + 1) * 2 * MXU_SLAB)

        def item():
            y_scr[:, cols] = _dot(ycat(), wo_ref[:, cols])
        return item

    for j in range(wo_ref.shape[1] // (2 * MXU_SLAB)):
        dots.append(out_slab(j))
    due.setdefault(len(dots) - 1 + FIN_DELAY, []).append(fin_out)

    o = 3 * d_a
    projections = [(0, fin_u), (d_a, fin_v), (2 * d_a, fin_za), (o, fin_q), (o + d_b, fin_k),
                   (o + 2 * d_b, fin_vt), (o + 3 * d_b, fin_gz)]
    assert d_a == d_b == MXU_SLAB

    assert land_scr.shape[0] == len(projections)

    def in_slab(ns, lo):
        def item():
            r = _dot(h(), w_ref[:, lo:lo + len(ns) * MXU_SLAB])
            for i, n in enumerate(ns):
                land_scr[n] = r[:, i * MXU_SLAB:(i + 1) * MXU_SLAB]
        return item

    def landed(n):
        return lambda: land_scr[n]

    for ns in ([0], [1], [2], [3, 4], [5], [6]):
        n0 = ns[0]
        dots.append(in_slab(ns, projections[n0][0]))
        for n in ns:
            due.setdefault(len(dots) - 1 + FIN_DELAY, []).append(functools.partial(projections[n][1], landed(n)))
    items = [item_h]
    for n, dot_item in enumerate(dots):
        items.append(both(dot_item, *due.pop(n, [])))
    items.append(both(*[f for n in sorted(due) for f in due[n]], pre_scores))

    lane = lax.broadcasted_iota(jnp.int32, (TQ, LANES), 1)
    half = TK // 2
    s_idx = lax.broadcasted_iota(jnp.int32, (half, half), 0)
    t_idx = lax.broadcasted_iota(jnp.int32, (half, half), 1)
    strictly_earlier_half = s_idx < t_idx
    j_idx = lax.broadcasted_iota(jnp.int32, (TK, TK), 1)
    r_idx = lax.broadcasted_iota(jnp.int32, (TK, TK), 0)
    strictly_later = jnp.where(j_idx > r_idx, 1.0, 0.0).astype(BF16)

    q_masked = {}
    for qi in range(Q_PER_STEP):
        for hd in range(n_heads):
            p, hh = divmod(hd, heads_per_tile)
            q_pair = q_ring[prev2, qi * TQ:(qi + 1) * TQ, p * LANES:(p + 1) * LANES]
            q_masked[qi, hd] = head_of(q_pair, hh)

    lo_half, hi_half = slice(0, half), slice(half, TK)
    diag_parts = [(lo_half, lo_half, "tri"), (lo_half, hi_half, "full"),
                  (hi_half, lo_half, "empty"), (hi_half, hi_half, "tri")]
    full_parts = [(slice(0, TK), slice(0, TQ), "full")]

    def assemble(parts, pieces):
        if len(parts) == 1:
            return pieces[0]
        return jnp.concatenate([jnp.concatenate(pieces[0:2], axis=1),
                                jnp.concatenate(pieces[2:4], axis=1)], axis=0)

    def gated_pair(accs, qi, p):
        rows, cols = slice(qi * TQ, (qi + 1) * TQ), slice(p * LANES, (p + 1) * LANES)
        yT = jnp.concatenate([accs[qi, heads_per_tile * p + n] for n in range(heads_per_tile)], axis=0)
        yb_ring[now2, rows, cols] = (yT.T * gz_ring[prev2, rows, cols]).astype(BF16)

    def key_blocks(blocks, runs, accs, extra=(), finish=False):
        runs, accs = dict(runs), dict(accs)
        tiles = [dict(qi=qi, kb=kb, h=hd, keep=keep, diag=diag, parts=diag_parts if diag else full_parts)
                 for qi, kb, diag, keep in blocks for hd in range(n_heads)]
        last_tile = {(t["qi"], t["h"]): n for n, t in enumerate(tiles)}

        def scores(t):
            if t["diag"]:
                t["z"] = z_ring[prev2, t["qi"] * n_heads + t["h"]]
                return
            start = pl.multiple_of(t["kb"] * TK, TK)
            p = t["h"] // heads_per_tile
            lanes = slice(p * LANES, (p + 1) * LANES)
            t["z"] = _dot_nt(k_scr[par_at, pl.ds(start, TK), lanes], q_masked[t["qi"], t["h"]])

        def softplus(t):
            sps, t["logb"] = [], []
            for rows, cols, kind in t["parts"]:
                if kind == "empty":
                    sps.append(jnp.zeros((half, half), BF16))
                    t["logb"].append(None)
                    continue
                z = t["z"][rows, cols]
                sp = jnp.maximum(z, 0.0) + jnp.log2(1.0 + jnp.exp2(-jnp.abs(z)))
                t["logb"].append(z - sp)
                if kind == "tri":
                    sp = jnp.where(strictly_earlier_half, sp, 0.0)
                sps.append(sp.astype(BF16))
            t["sp"] = assemble(t["parts"], sps)

        def later_sum(t):
            t["csum"] = _dot(strictly_later, t["sp"])

        def weights(t):
            key = (t["qi"], t["h"])
            run = runs[key]
            probs = []
            for (rows, cols, kind), logb in zip(t["parts"], t["logb"]):
                if kind == "empty":
                    probs.append(jnp.zeros((half, half), BF16))
                    continue
                a = jnp.exp2(logb - (run[:, cols] + t["csum"][rows, cols]))
                if kind == "tri":
                    a = jnp.where(strictly_earlier_half, a, 0.0)
                probs.append(a.astype(BF16))
            t["prob"] = assemble(t["parts"], probs)
            runs[key] = run + (t["csum"][0:1, :] + t["sp"][0:1, :].astype(F32))

        def values(n, t):
            key = (t["qi"], t["h"])
            hd = t["h"]
            v_blk = vT_scr[par_at, t["kb"], hd * GROUP_DIM:(hd + 1) * GROUP_DIM, :]
            if t["keep"] is not None:
                v_blk = (v_blk.astype(F32) * t["keep"]).astype(BF16)
            accs[key] = accs[key] + _dot(v_blk, t["prob"])
            if finish and last_tile[key] == n and hd % heads_per_tile == heads_per_tile - 1:
                gated_pair(accs, t["qi"], hd // heads_per_tile)

        n_tiles = len(tiles)
        lead, lag = SKEW
        ready = next((n for n, t in enumerate(tiles) if not t["diag"]), n_tiles)
        for t in tiles[:ready]:
            scores(t)
        steps = range(min(0, ready - lead), n_tiles + lag)
        extra = list(extra)
        slots = [EXTRA_START + (n * (len(steps) - EXTRA_START)) // max(1, len(extra))
                 for n in range(len(extra))]
        for count, step in enumerate(steps):
            while extra and slots[0] <= count:
                slots.pop(0)
                extra.pop(0)()
            if ready <= step + lead < n_tiles:
                scores(tiles[step + lead])
            if 0 <= step < n_tiles:
                softplus(tiles[step])
                later_sum(tiles[step])
            if 0 <= step - lag < n_tiles:
                weights(tiles[step - lag])
                values(step - lag, tiles[step - lag])
        for item in extra:
            item()
        return runs, accs

    runs = {(qi, hd): jnp.zeros((1, TQ), F32) for qi in range(Q_PER_STEP) for hd in range(n_heads)}
    accs = {(qi, hd): jnp.zeros((GROUP_DIM, TQ), F32) for qi in range(Q_PER_STEP) for hd in range(n_heads)}
    first_kb = j_at * Q_PER_STEP
    diag_blocks = [(qi, first_kb + qi, True, None) for qi in range(Q_PER_STEP)]
    has_earlier = (j_at > 0).astype(F32)
    prev_blocks = [(qi, jnp.maximum(first_kb + qi - 1, 0), False, has_earlier if qi == 0 else None)
                   for qi in range(Q_PER_STEP)]
    runs, accs = key_blocks(diag_blocks + prev_blocks, runs, accs, items, finish=True)

    for qi in range(Q_PER_STEP):
        def sticks_left(runs):
            return jnp.min(functools.reduce(jnp.minimum, [runs[qi, hd] for hd in range(n_heads)])) < UNDERFLOW_LOG2

        def earlier_block(state, qi=qi, sticks_left=sticks_left):
            kb, _, runs, accs = state
            runs, accs = key_blocks([(qi, kb, False, None)], runs, accs)
            return kb - 1, sticks_left(runs), runs, accs

        first = first_kb + qi - 2
        mine = lambda d: {k: v for k, v in d.items() if k[0] == qi}

        @pl.when((first >= 0) & sticks_left(runs))
        def _(qi=qi, first=first, earlier_block=earlier_block, mine=mine):
            _, _, _, more = lax.while_loop(lambda st: (st[0] >= 0) & st[1], earlier_block,
                                           (first, True, mine(runs), mine(accs)))
            for p in range(n_pairs):
                gated_pair(more, qi, p)


def _fused_block(x, mod3, w, w_sp, b_sp_full, a_g, a_b, w_out, ln_g, ln_b, alpha):
    bsz, seq, d = x.shape
    d_a = a_g.shape[-1]
    d_b = (w.shape[1] - 3 * d_a) // 4
    per_seq = seq // ROWS
    n_total = bsz * per_seq
    grid = (n_total + 2,)

    def stage(lag):
        return lambda s: _stage_slabs(s, n_total, per_seq)[lag]

    def rows_of(lag):
        return pl.BlockSpec((1, ROWS, d), lambda s: (*stage(lag)(s), 0))

    def mod_of(lag):
        return pl.BlockSpec((1, 3, d), lambda s: (stage(lag)(s)[0], 0, 0))

    def whole(a):
        return pl.BlockSpec(a.shape, lambda s: (0,) * a.ndim)

    return pl.pallas_call(
        functools.partial(_block_kernel, alpha, n_total),
        grid=grid,
        in_specs=[rows_of(0), rows_of(2), mod_of(0), mod_of(2), whole(w), whole(w_sp),
                  whole(b_sp_full), whole(a_g), whole(a_b), whole(w_out), whole(ln_g), whole(ln_b)],
        out_specs=rows_of(2),
        out_shape=jax.ShapeDtypeStruct((bsz, seq, d), x.dtype),
        scratch_shapes=[
            pltpu.VMEM((2, seq, d_b), BF16),
            pltpu.VMEM((2, seq // TK, d_b, TK), BF16),
            pltpu.VMEM((2, ROWS, d_b), BF16),
            pltpu.VMEM((2, ROWS, d_b), F32),
            pltpu.VMEM((3, ROWS, d_a), BF16),
            pltpu.VMEM((2, ROWS, d_b), BF16),
            pltpu.VMEM((2, Q_PER_STEP * (d_b // GROUP_DIM), TK, TQ), F32),
            pltpu.VMEM((w.shape[1] // MXU_SLAB, ROWS, MXU_SLAB), F32),
            pltpu.VMEM((ROWS, d), F32),
        ],
        compiler_params=pltpu.CompilerParams(
            dimension_semantics=("arbitrary",), vmem_limit_bytes=VMEM_LIMIT),
        name="fused_block",
    )(x, x, mod3, mod3, w, w_sp, b_sp_full, a_g, a_b, w_out, ln_g, ln_b)


def kernel(x, c, w_ada, b_ada, w_in, sgu_ln_g, sgu_ln_b, w_spatial, b_spatial, w_out, ln_g, ln_b):
    depth = w_ada.shape[0]
    bsz, seq, d = x.shape
    d_a = sgu_ln_g.shape[-1]
    alpha = (2.0 * depth) ** 0.25
    for layer in range(depth):
        mod, w_in_bf, w_out_bf = _prepare(c, w_ada[layer], b_ada[layer], w_in[layer], w_out[layer])
        b_sp_full = jnp.repeat(b_spatial[layer].T, GROUP_DIM, axis=1)
        x = _fused_block(
            x, mod.reshape(bsz, 3, d), w_in_bf, w_spatial[layer], b_sp_full,
            sgu_ln_g[layer].reshape(1, d_a), sgu_ln_b[layer].reshape(1, d_a),
            w_out_bf, ln_g[layer].reshape(1, d), ln_b[layer].reshape(1, d), alpha)
    return x
```
